```python
import math
import jax, jax.numpy as jnp
from jax import lax
import numpy as np

D_MODEL = 1024
BATCH = 8
SEQ = 2048
DEPTH = 2
DEC_BATCH = 128
DEC_SEQ = 8
PAST_LEN = 16384
PAGE_SIZE = 128

MIX_WIDTH = D_MODEL
S5_WIDTH = MIX_WIDTH // 4
GLA_WIDTH = MIX_WIDTH // 4
CONV_DIM = MIX_WIDTH // 4
GMLP_WIDTH = MIX_WIDTH - S5_WIDTH - GLA_WIDTH - CONV_DIM

S5_GROUP = 16
S5_GROUPS = S5_WIDTH // S5_GROUP
S5_STATE = 64
S5_DT_MIN = 0.001
S5_DT_MAX = 0.1

GLA_HEADS = 4
GLA_DV = GLA_WIDTH // GLA_HEADS
GLA_DK = GLA_DV // 2
GLA_KEY_WIDTH = GLA_HEADS * GLA_DK
GLA_GATE_RANK = 16
GLA_TAU = 16.0
GLA_CHUNK = 64

CONV_WIDTH = 31

GMLP_HEADS = 4
GMLP_HEAD_DIM = GMLP_WIDTH // GMLP_HEADS
GMLP_CHUNK = 128

D_FF = 2816
N_EXPERTS = 8
TOP_K = 2
EXPERT_FF = 2816
N_DENSE = (DEPTH + 1) // 2
N_MOE = DEPTH // 2
EPS = 1e-6

PROJ_SIZES = (S5_WIDTH, GLA_KEY_WIDTH, GLA_KEY_WIDTH, GLA_WIDTH, GLA_WIDTH, GLA_GATE_RANK,
              CONV_DIM, CONV_DIM, GMLP_WIDTH, GMLP_WIDTH)
PROJ_WIDTH = S5_WIDTH + 2 * GLA_KEY_WIDTH + 2 * GLA_WIDTH + GLA_GATE_RANK + 2 * CONV_DIM + 2 * GMLP_WIDTH
GROUP_SIZES = (S5_WIDTH, GLA_WIDTH, CONV_DIM, GMLP_WIDTH)

kernel_name = 'hymba_s5_gla_conv_gmlp_adaln_moe_step'


def split_points(sizes):
    return np.cumsum(np.array(sizes))[:-1].tolist()


def rmsnorm(x, g):
    xf = x.astype(jnp.float32)
    y = xf * lax.rsqrt(jnp.mean(xf * xf, axis=-1, keepdims=True) + EPS)
    return (y * g.astype(jnp.float32)).astype(x.dtype)


def layernorm(x, g, b):
    xf = x.astype(jnp.float32)
    mu = jnp.mean(xf, axis=-1, keepdims=True)
    var = jnp.mean(jnp.square(xf - mu), axis=-1, keepdims=True)
    y = (xf - mu) * lax.rsqrt(var + EPS) * g.astype(jnp.float32) + b.astype(jnp.float32)
    return y.astype(x.dtype)


def ada_modulate(c, w, b):
    m = jax.nn.silu(c) @ w + b
    return jnp.split(m[:, None, :], 6, axis=-1)


def s5_mixer(u, h0_re, h0_im, a_re, a_im, log_dt, b_re, b_im, c_re, c_im, d_skip, w_glu, b_glu):
    bsz, t, _ = u.shape
    uf = u.astype(jnp.float32)
    ug = uf.reshape(bsz, t, S5_GROUPS, S5_GROUP)
    lr = a_re.astype(jnp.float32)
    li = a_im.astype(jnp.float32)
    dt = jnp.exp(log_dt.astype(jnp.float32))[:, None]
    mag = jnp.exp(lr * dt)
    ang = li * dt
    ab_re = mag * jnp.cos(ang)
    ab_im = mag * jnp.sin(ang)
    den = lr * lr + li * li
    nr = ab_re - 1.0
    f_re = (nr * lr + ab_im * li) / den
    f_im = (ab_im * lr - nr * li) / den
    br = b_re.astype(jnp.float32)
    bi = b_im.astype(jnp.float32)
    bb_re = f_re[..., None] * br - f_im[..., None] * bi
    bb_im = f_re[..., None] * bi + f_im[..., None] * br
    x_re = jnp.einsum('btgn,gpn->btgp', ug, bb_re)
    x_im = jnp.einsum('btgn,gpn->btgp', ug, bb_im)
    h0r = h0_re.astype(jnp.float32)
    h0i = h0_im.astype(jnp.float32)
    x_re = x_re.at[:, 0].add(ab_re * h0r - ab_im * h0i)
    x_im = x_im.at[:, 0].add(ab_re * h0i + ab_im * h0r)
    a_re_t = jnp.broadcast_to(ab_re, x_re.shape)
    a_im_t = jnp.broadcast_to(ab_im, x_im.shape)

    def combine(e1, e2):
        a1r, a1i, b1r, b1i = e1
        a2r, a2i, b2r, b2i = e2
        return (a2r * a1r - a2i * a1i, a2r * a1i + a2i * a1r,
                a2r * b1r - a2i * b1i + b2r, a2r * b1i + a2i * b1r + b2i)

    _, _, h_re, h_im = lax.associative_scan(combine, (a_re_t, a_im_t, x_re, x_im), axis=1)
    y = (jnp.einsum('btgp,gnp->btgn', h_re, c_re.astype(jnp.float32))
         - jnp.einsum('btgp,gnp->btgn', h_im, c_im.astype(jnp.float32)))
    y = y.reshape(bsz, t, S5_WIDTH) + d_skip.astype(jnp.float32) * uf
    y = jax.nn.gelu(y)
    y = y * jax.nn.sigmoid(y @ w_glu.astype(jnp.float32) + b_glu.astype(jnp.float32))
    return y.astype(u.dtype), h_re[:, -1], h_im[:, -1]


def gla_mixer(q, k, v, r, g_lr, s0, w_gate2, b_gate2, onorm_g):
    bsz, t, _ = q.shape
    chunk = GLA_CHUNK if t >= GLA_CHUNK else t
    n_chunks = -(-t // chunk)
    pad = n_chunks * chunk - t
    log_a = jax.nn.log_sigmoid(g_lr.astype(jnp.float32) @ w_gate2.astype(jnp.float32)
                               + b_gate2.astype(jnp.float32)) / GLA_TAU

    def heads(z, d):
        z = z.astype(jnp.float32).reshape(bsz, t, GLA_HEADS, d)
        z = jnp.pad(z, ((0, 0), (0, pad), (0, 0), (0, 0)))
        return z.reshape(bsz, n_chunks, chunk, GLA_HEADS, d).transpose(1, 0, 3, 2, 4)

    qc = heads(q, GLA_DK) * (GLA_DK ** -0.5)
    kc = heads(k, GLA_DK)
    vc = heads(v, GLA_DV)
    gc = heads(log_a, GLA_DK)
    causal = jnp.tril(jnp.ones((chunk, chunk), dtype=bool))[None, None, :, :, None]

    def step(s, inp):
        qi, ki, vi, gi = inp
        b = jnp.cumsum(gi, axis=2)
        o_inter = jnp.einsum('bhld,bhdv->bhlv', qi * jnp.exp(b), s)
        rel = b[:, :, :, None, :] - b[:, :, None, :, :]
        decay = jnp.exp(jnp.where(causal, rel, -jnp.inf))
        att = jnp.einsum('bhid,bhjd,bhijd->bhij', qi, ki, decay)
        o = o_inter + jnp.einsum('bhij,bhjv->bhiv', att, vi)
        b_last = b[:, :, -1:, :]
        s_new = (jnp.exp(b_last[:, :, 0, :])[..., None] * s
                 + jnp.einsum('bhjd,bhjv->bhdv', ki * jnp.exp(b_last - b), vi))
        return s_new, o

    s_final, o = lax.scan(step, s0.astype(jnp.float32), (qc, kc, vc, gc))
    o = o.transpose(1, 0, 3, 2, 4).reshape(bsz, n_chunks * chunk, GLA_HEADS, GLA_DV)[:, :t]
    o = rmsnorm(o, onorm_g).reshape(bsz, t, GLA_WIDTH)
    o = o * jax.nn.silu(r.astype(jnp.float32))
    return o.astype(q.dtype), s_final


def conv_mixer(a, gate_in, buf, w_dw, b_dw, ln_g, ln_b, w_pw, b_pw):
    z = a * jax.nn.sigmoid(gate_in)
    zc = jnp.concatenate([buf.astype(z.dtype), z], axis=1)
    y = lax.conv_general_dilated(zc, w_dw.astype(zc.dtype)[:, None, :], window_strides=(1,), padding='VALID',
                                 dimension_numbers=('NWC', 'WIO', 'NWC'), feature_group_count=CONV_DIM)
    y = y + b_dw
    y = jax.nn.silu(layernorm(y, ln_g, ln_b))
    y = y @ w_pw + b_pw
    return y, zc[:, -(CONV_WIDTH - 1):]


def gmlp_mixer(u, v, ln_g, ln_b, w_s, b_s):
    bsz, t, _ = u.shape
    n_chunks = -(-t // GMLP_CHUNK)
    pad = n_chunks * GMLP_CHUNK - t
    vn = layernorm(v, ln_g, ln_b)
    vc = jnp.pad(vn, ((0, 0), (0, pad), (0, 0))).reshape(bsz, n_chunks, GMLP_CHUNK, GMLP_HEADS, GMLP_HEAD_DIM)
    ws = jnp.where(jnp.tril(jnp.ones((GMLP_CHUNK, GMLP_CHUNK), dtype=bool))[None], w_s, 0.0).astype(vc.dtype)
    mixed = jnp.einsum('hij,bnjhd->bnihd', ws, vc) + b_s.T.astype(vc.dtype)[None, None, :, :, None]
    mixed = mixed.reshape(bsz, n_chunks * GMLP_CHUNK, GMLP_WIDTH)[:, :t]
    return u * mixed, vn


def swiglu(h, wg, wu, wd):
    return (jax.nn.silu(h @ wg) * (h @ wu)) @ wd


def moe_ffn(h, router, wg, wu, wd):
    logits = (h @ router).astype(jnp.float32)
    top_v, top_i = lax.top_k(logits, TOP_K)
    w = jax.nn.softmax(top_v, axis=-1)
    gates = jnp.sum(jax.nn.one_hot(top_i, N_EXPERTS, dtype=jnp.float32) * w[..., None], axis=-2)
    out = jnp.zeros_like(h)
    for e in range(N_EXPERTS):
        out = out + gates[..., e:e + 1].astype(h.dtype) * swiglu(h, wg[e], wu[e], wd[e])
    return out


def setup_inputs(seed: int = 0) -> dict:
    key = jax.random.key(seed)
    ks = iter(jax.random.split(key, 64))
    nrm = lambda shape, s: jax.random.normal(next(ks), shape, jnp.float32) * s
    gain = lambda shape: 1.0 + nrm(shape, 0.02)
    a_im_base = jnp.pi * jnp.arange(S5_STATE, dtype=jnp.float32)
    return {
        'x_prompt': nrm((BATCH, SEQ, D_MODEL), 1.0),
        'x_sample': nrm((DEC_BATCH, DEC_SEQ, D_MODEL), 1.0),
        'c_prompt': nrm((BATCH, D_MODEL), 1.0),
        'c_sample': nrm((DEC_BATCH, D_MODEL), 1.0),
        'state_s5_re': nrm((DEPTH, DEC_BATCH, S5_GROUPS, S5_STATE), 0.5),
        'state_s5_im': nrm((DEPTH, DEC_BATCH, S5_GROUPS, S5_STATE), 0.5),
        'state_gla': nrm((DEPTH, DEC_BATCH, GLA_HEADS, GLA_DK, GLA_DV), 0.5),
        'cache_conv': nrm((DEPTH, DEC_BATCH, CONV_WIDTH - 1, CONV_DIM), 0.5),
        'ada_w': nrm((DEPTH, D_MODEL, 6 * D_MODEL), 0.5 * D_MODEL ** -0.5),
        'ada_b': nrm((DEPTH, 6 * D_MODEL), 0.02),
        'norm_mix_g': gain((DEPTH, D_MODEL)),
        'norm_ffn_g': gain((DEPTH, D_MODEL)),
        'w_in': nrm((DEPTH, D_MODEL, PROJ_WIDTH), D_MODEL ** -0.5),
        'b_in': nrm((DEPTH, PROJ_WIDTH), 0.02),
        's5_a_re': -0.5 + nrm((DEPTH, S5_GROUPS, S5_STATE), 0.01),
        's5_a_im': a_im_base + nrm((DEPTH, S5_GROUPS, S5_STATE), 0.01),
        's5_log_dt': jax.random.uniform(next(ks), (DEPTH, S5_GROUPS), jnp.float32,
                                        math.log(S5_DT_MIN), math.log(S5_DT_MAX)),
        's5_b_re': nrm((DEPTH, S5_GROUPS, S5_STATE, S5_GROUP), S5_GROUP ** -0.5),
        's5_b_im': nrm((DEPTH, S5_GROUPS, S5_STATE, S5_GROUP), S5_GROUP ** -0.5),
        's5_c_re': nrm((DEPTH, S5_GROUPS, S5_GROUP, S5_STATE), S5_STATE ** -0.5),
        's5_c_im': nrm((DEPTH, S5_GROUPS, S5_GROUP, S5_STATE), S5_STATE ** -0.5),
        's5_d': nrm((DEPTH, S5_WIDTH), 1.0),
        's5_w_glu': nrm((DEPTH, S5_WIDTH, S5_WIDTH), S5_WIDTH ** -0.5),
        's5_b_glu': nrm((DEPTH, S5_WIDTH), 0.02),
        'gla_w_gate2': nrm((DEPTH, GLA_GATE_RANK, GLA_KEY_WIDTH), GLA_GATE_RANK ** -0.5),
        'gla_b_gate2': nrm((DEPTH, GLA_KEY_WIDTH), 0.02),
        'gla_onorm_g': gain((DEPTH, GLA_HEADS, GLA_DV)),
        'conv_w_dw': nrm((DEPTH, CONV_WIDTH, CONV_DIM), CONV_WIDTH ** -0.5),
        'conv_b_dw': nrm((DEPTH, CONV_DIM), 0.02),
        'conv_ln_g': gain((DEPTH, CONV_DIM)),
        'conv_ln_b': nrm((DEPTH, CONV_DIM), 0.02),
        'conv_w_pw': nrm((DEPTH, CONV_DIM, CONV_DIM), CONV_DIM ** -0.5),
        'conv_b_pw': nrm((DEPTH, CONV_DIM), 0.02),
        'gmlp_ln_g': gain((DEPTH, GMLP_WIDTH)),
        'gmlp_ln_b': nrm((DEPTH, GMLP_WIDTH), 0.02),
        'gmlp_w_s': nrm((DEPTH, GMLP_HEADS, GMLP_CHUNK, GMLP_CHUNK), GMLP_CHUNK ** -0.5),
        'gmlp_b_s': gain((DEPTH, GMLP_HEADS, GMLP_CHUNK)),
        'merge_g': gain((DEPTH, MIX_WIDTH)),
        'w_out': nrm((DEPTH, MIX_WIDTH, D_MODEL), MIX_WIDTH ** -0.5),
        'ffn_w_gate': nrm((N_DENSE, D_MODEL, D_FF), D_MODEL ** -0.5),
        'ffn_w_up': nrm((N_DENSE, D_MODEL, D_FF), D_MODEL ** -0.5),
        'ffn_w_down': nrm((N_DENSE, D_FF, D_MODEL), D_FF ** -0.5),
        'moe_router': nrm((N_MOE, D_MODEL, N_EXPERTS), D_MODEL ** -0.5),
        'moe_w_gate': nrm((N_MOE, N_EXPERTS, D_MODEL, EXPERT_FF), D_MODEL ** -0.5),
        'moe_w_up': nrm((N_MOE, N_EXPERTS, D_MODEL, EXPERT_FF), D_MODEL ** -0.5),
        'moe_w_down': nrm((N_MOE, N_EXPERTS, EXPERT_FF, D_MODEL), EXPERT_FF ** -0.5),
        'final_norm_g': gain((D_MODEL,)),
    }


def reference(x_prompt, x_sample, c_prompt, c_sample, state_s5_re, state_s5_im, state_gla, cache_conv,
              ada_w, ada_b, norm_mix_g, norm_ffn_g, w_in, b_in,
              s5_a_re, s5_a_im, s5_log_dt, s5_b_re, s5_b_im, s5_c_re, s5_c_im, s5_d, s5_w_glu, s5_b_glu,
              gla_w_gate2, gla_b_gate2, gla_onorm_g,
              conv_w_dw, conv_b_dw, conv_ln_g, conv_ln_b, conv_w_pw, conv_b_pw,
              gmlp_ln_g, gmlp_ln_b, gmlp_w_s, gmlp_b_s,
              merge_g, w_out,
              ffn_w_gate, ffn_w_up, ffn_w_down,
              moe_router, moe_w_gate, moe_w_up, moe_w_down,
              final_norm_g):
    proj_idx = split_points(PROJ_SIZES)
    group_idx = split_points(GROUP_SIZES)

    def run(x, c, s5_re0, s5_im0, gla0, conv0):
        new_re, new_im, new_gla, new_conv, new_v = [], [], [], [], []
        for i in range(DEPTH):
            sh1, sc1, g1, sh2, sc2, g2 = ada_modulate(c, ada_w[i], ada_b[i])
            h = rmsnorm(x, norm_mix_g[i]) * (1.0 + sc1) + sh1
            p = h @ w_in[i] + b_in[i]
            u_s5, q, k, v, r, g_lr, conv_a, conv_g, u_mlp, v_mlp = jnp.split(p, proj_idx, axis=-1)
            o_s5, h_re, h_im = s5_mixer(u_s5, s5_re0[i], s5_im0[i], s5_a_re[i], s5_a_im[i], s5_log_dt[i],
                                        s5_b_re[i], s5_b_im[i], s5_c_re[i], s5_c_im[i], s5_d[i],
                                        s5_w_glu[i], s5_b_glu[i])
            o_gla, s_gla = gla_mixer(q, k, v, r, g_lr, gla0[i], gla_w_gate2[i], gla_b_gate2[i], gla_onorm_g[i])
            o_conv, buf = conv_mixer(conv_a, conv_g, conv0[i], conv_w_dw[i], conv_b_dw[i], conv_ln_g[i],
                                     conv_ln_b[i], conv_w_pw[i], conv_b_pw[i])
            o_mlp, v_rows = gmlp_mixer(u_mlp, v_mlp, gmlp_ln_g[i], gmlp_ln_b[i], gmlp_w_s[i], gmlp_b_s[i])
            mg = jnp.split(merge_g[i], group_idx)
            mix = jnp.concatenate([rmsnorm(o_s5, mg[0]), rmsnorm(o_gla, mg[1]),
                                   rmsnorm(o_conv, mg[2]), rmsnorm(o_mlp, mg[3])], axis=-1)
            x = x + g1 * (mix @ w_out[i])
            h2 = rmsnorm(x, norm_ffn_g[i]) * (1.0 + sc2) + sh2
            if i % 2 == 0:
                j = i // 2
                f = swiglu(h2, ffn_w_gate[j], ffn_w_up[j], ffn_w_down[j])
            else:
                j = i // 2
                f = moe_ffn(h2, moe_router[j], moe_w_gate[j], moe_w_up[j], moe_w_down[j])
            x = x + g2 * f
            new_re.append(h_re)
            new_im.append(h_im)
            new_gla.append(s_gla)
            new_conv.append(buf)
            new_v.append(v_rows)
        y = rmsnorm(x, final_norm_g)
        return (y, jnp.stack(new_re), jnp.stack(new_im), jnp.stack(new_gla),
                jnp.stack(new_conv), jnp.stack(new_v))

    bp = x_prompt.shape[0]
    z_re = jnp.zeros((DEPTH, bp, S5_GROUPS, S5_STATE), jnp.float32)
    z_gla = jnp.zeros((DEPTH, bp, GLA_HEADS, GLA_DK, GLA_DV), jnp.float32)
    z_conv = jnp.zeros((DEPTH, bp, CONV_WIDTH - 1, CONV_DIM), x_prompt.dtype)
    y_prompt, p_s5_re, p_s5_im, p_gla, p_conv, _ = run(x_prompt, c_prompt, z_re, z_re, z_gla, z_conv)
    y_sample, s_s5_re, s_s5_im, s_gla, s_conv, s_gmlp_v = run(x_sample, c_sample, state_s5_re, state_s5_im,
                                                              state_gla, cache_conv)
    return (y_prompt, y_sample, p_s5_re, p_s5_im, p_gla, p_conv, s_s5_re, s_s5_im, s_gla, s_conv, s_gmlp_v)
```

```python
import functools
import math

import jax
import jax.numpy as jnp
from jax import lax
from jax.experimental import pallas as pl
from jax.experimental.pallas import tpu as pltpu

D_MODEL = 1024
S5_WIDTH = 256
S5_GROUP = 16
S5_GROUPS = 16
S5_STATE = 64
S5_LANES = S5_GROUPS * S5_STATE
GLA_HEADS = 4
GLA_DV = 64
GLA_DK = 32
GLA_WIDTH = 256
GLA_KEY_WIDTH = 128
GLA_GATE_RANK = 16
GLA_TAU = 16.0
GLA_CHUNK = 64
GLA_STATE_LANES = GLA_HEADS * GLA_DK * GLA_DV
CONV_DIM = 256
CONV_WIDTH = 31
CONV_HIST = CONV_WIDTH - 1
GMLP_WIDTH = 256
GMLP_HEADS = 4
GMLP_HEAD_DIM = 64
GMLP_CHUNK = 128
D_FF = 2816
N_EXPERTS = 8
EPS = 1e-6

LANE = 128
PW_S5 = 256
PW_GLA = 128 + 128 + 256 + 256 + LANE
PW_CONV = 512
PW_MLP = 512
PW_TOTAL = PW_S5 + PW_GLA + PW_CONV + PW_MLP
VMEM_LIMIT = 56 * 1024 * 1024

F32 = jnp.float32
BF16 = jnp.bfloat16
HI = lax.Precision.HIGHEST


def _cparams(sem):
    return pltpu.CompilerParams(dimension_semantics=sem, vmem_limit_bytes=VMEM_LIMIT)


def _rms(x):
    return x * lax.rsqrt(jnp.mean(x * x, axis=-1, keepdims=True) + EPS)


def _layernorm(x, g, b):
    mu = jnp.mean(x, axis=-1, keepdims=True)
    xc = x - mu
    var = jnp.mean(xc * xc, axis=-1, keepdims=True)
    return xc * lax.rsqrt(var + EPS) * g + b


def _silu(x):
    return x * jax.nn.sigmoid(x)


def _gelu_tanh(x):
    return 0.5 * x * (1.0 + jnp.tanh(math.sqrt(2.0 / math.pi) * (x + 0.044715 * (x * x * x))))


def _log_sigmoid(x):
    return jnp.minimum(x, 0.0) - jnp.log(1.0 + jnp.exp(-jnp.abs(x)))


def _same_block(shape, row_block, col_block):
    r = lax.broadcasted_iota(jnp.int32, shape, 0) >> (row_block.bit_length() - 1)
    c = lax.broadcasted_iota(jnp.int32, shape, 1) >> (col_block.bit_length() - 1)
    return r == c


def _modulate(y, sc, sh):
    rm = sc.shape[0]
    if rm == 1:
        return y * (1.0 + sc) + sh
    rows, d = y.shape
    y3 = y.reshape(rows // rm, rm, d)
    return (y3 * (1.0 + sc)[None] + sh[None]).reshape(rows, d)


def _gate(y, g):
    rm = g.shape[0]
    if rm == 1:
        return y * g
    rows, d = y.shape
    return (y.reshape(rows // rm, rm, d) * g[None]).reshape(rows, d)


def _ada_kernel(c_ref, w_ref, b_ref, o_ref):
    c = c_ref[...]
    s = _silu(c).astype(BF16)
    o_ref[0] = jnp.dot(s, w_ref[0].astype(BF16), preferred_element_type=F32) + b_ref[0]


def ada_modulation(c_all, ada_w, ada_b):
    depth, d, n6 = ada_w.shape
    rows = c_all.shape[0]
    tn = 1536
    return pl.pallas_call(
        _ada_kernel,
        out_shape=jax.ShapeDtypeStruct((depth, rows, n6), F32),
        grid=(depth, n6 // tn),
        in_specs=[pl.BlockSpec((rows, d), lambda l, j: (0, 0)),
                  pl.BlockSpec((1, d, tn), lambda l, j: (l, 0, j)),
                  pl.BlockSpec((1, 1, tn), lambda l, j: (l, 0, j))],
        out_specs=pl.BlockSpec((1, rows, tn), lambda l, j: (l, 0, j)),
        compiler_params=_cparams(("arbitrary", "arbitrary")),
        name="ada_modulation",
    )(c_all, ada_w, ada_b.reshape(depth, 1, n6))


def _pre_kernel(x_ref, sh_ref, sc_ref, g_ref, w_ref, b_ref, o_s5, o_gla, o_conv, o_mlp):
    x = x_ref[0]
    y = _modulate(_rms(x) * g_ref[...], sc_ref[0], sh_ref[0])
    p = jnp.dot(y.astype(BF16), w_ref[...], preferred_element_type=F32) + b_ref[...]
    o_s5[...] = p[:, 0:PW_S5]
    o_gla[...] = p[:, PW_S5:PW_S5 + PW_GLA]
    o_conv[...] = p[:, PW_S5 + PW_GLA:PW_S5 + PW_GLA + PW_CONV]
    o_mlp[...] = p[:, PW_S5 + PW_GLA + PW_CONV:PW_TOTAL]


def pre_mixer(x, sh, sc, g, w, b, tm):
    s, r, d = x.shape
    rm = sh.shape[1]
    widths = (PW_S5, PW_GLA, PW_CONV, PW_MLP)
    return pl.pallas_call(
        _pre_kernel,
        out_shape=[jax.ShapeDtypeStruct((r, s * w_), F32) for w_ in widths],
        grid=(s, r // tm),
        in_specs=[pl.BlockSpec((1, tm, d), lambda b_, i: (b_, i, 0)),
                  pl.BlockSpec((1, rm, d), lambda b_, i: (b_, 0, 0)),
                  pl.BlockSpec((1, rm, d), lambda b_, i: (b_, 0, 0)),
                  pl.BlockSpec((1, d), lambda b_, i: (0, 0)),
                  pl.BlockSpec((d, PW_TOTAL), lambda b_, i: (0, 0)),
                  pl.BlockSpec((1, PW_TOTAL), lambda b_, i: (0, 0))],
        out_specs=[pl.BlockSpec((tm, w_), lambda b_, i: (i, b_)) for w_ in widths],
        compiler_params=_cparams(("arbitrary", "arbitrary")),
        name="pre_mixer",
    )(x, sh, sc, g, w, b)


def _s5_disc_kernel(lr_ref, li_ref, ldt_ref, br_ref, bi_ref, abr_ref, abi_ref, bbr_ref, bbi_ref):
    lr = lr_ref[...]
    li = li_ref[...]
    dt = jnp.exp(ldt_ref[...])
    mag = jnp.exp(lr * dt)
    ang = li * dt
    ab_re = mag * jnp.cos(ang)
    ab_im = mag * jnp.sin(ang)
    den = lr * lr + li * li
    nr = ab_re - 1.0
    f_re = (nr * lr + ab_im * li) / den
    f_im = (ab_im * lr - nr * li) / den
    br = br_ref[...]
    bi = bi_ref[...]
    abr_ref[...] = ab_re
    abi_ref[...] = ab_im
    bbr_ref[...] = f_re * br - f_im * bi
    bbi_ref[...] = f_re * bi + f_im * br


def s5_discretise(a_re, a_im, log_dt, b_re, b_im):
    n = b_re.shape[-1]
    gp = a_re.size
    bc = lambda a: jnp.broadcast_to(a.reshape(gp, 1), (gp, n))
    ldt = jnp.broadcast_to(log_dt[:, None], a_re.shape)
    outs = pl.pallas_call(
        _s5_disc_kernel,
        out_shape=[jax.ShapeDtypeStruct((gp, n), F32)] * 4,
        name="s5_discretise",
    )(bc(a_re), bc(a_im), bc(ldt), b_re.reshape(gp, n), b_im.reshape(gp, n))
    ab_re, ab_im, bb_re, bb_im = outs
    return ab_re[:, 0], ab_im[:, 0], bb_re, bb_im


def _block_diag_in(bb):
    g, p, n = S5_GROUPS, S5_STATE, S5_GROUP
    b3 = bb.reshape(g, p, n)
    eye = jnp.eye(g, dtype=bb.dtype)
    return jnp.einsum('gpn,gh->gnhp', b3, eye).reshape(g * n, g * p)


def _block_diag_out(c):
    g, p, n = S5_GROUPS, S5_STATE, S5_GROUP
    eye = jnp.eye(g, dtype=c.dtype)
    return jnp.einsum('gnp,gh->gphn', c, eye).reshape(g * p, g * n)


def _s5_kernel(u_ref, h0_ref, bblk_ref, cre_ref, cim_ref, ar_ref, ai_ref, d_ref, wglu_ref, bglu_ref, mg_ref,
               o_ref, hT_ref, xs_ref, hs_ref, *, nb, tc):
    i = pl.program_id(0)

    @pl.when(i == 0)
    def _():
        hs_ref[...] = h0_ref[...]

    u = u_ref[...]
    xs_ref[...] = jnp.dot(u.astype(BF16), bblk_ref[...], preferred_element_type=F32)
    ar = jnp.broadcast_to(ar_ref[...], (nb, S5_LANES))
    ai = jnp.broadcast_to(ai_ref[...], (nb, S5_LANES))

    def step(t, carry):
        hr, hi = carry
        row = pl.multiple_of(t * nb, nb)
        xr = xs_ref[pl.ds(row, nb), 0:S5_LANES]
        xi = xs_ref[pl.ds(row, nb), S5_LANES:2 * S5_LANES]
        nr = ar * hr - ai * hi + xr
        ni = ar * hi + ai * hr + xi
        xs_ref[pl.ds(row, nb), 0:S5_LANES] = nr
        xs_ref[pl.ds(row, nb), S5_LANES:2 * S5_LANES] = ni
        return nr, ni

    hr, hi = lax.fori_loop(0, tc, step, (hs_ref[:, 0:S5_LANES], hs_ref[:, S5_LANES:2 * S5_LANES]),
                           unroll=True if tc <= 8 else 4)
    hs_ref[:, 0:S5_LANES] = hr
    hs_ref[:, S5_LANES:2 * S5_LANES] = hi

    y = (jnp.dot(xs_ref[:, 0:S5_LANES].astype(BF16), cre_ref[...], preferred_element_type=F32)
         - jnp.dot(xs_ref[:, S5_LANES:2 * S5_LANES].astype(BF16), cim_ref[...], preferred_element_type=F32))
    y = y + d_ref[...] * u
    y = _gelu_tanh(y)
    y = y * jax.nn.sigmoid(jnp.dot(y.astype(BF16), wglu_ref[...], preferred_element_type=F32) + bglu_ref[...])
    o_ref[...] = (_rms(y) * mg_ref[...]).astype(o_ref.dtype)

    @pl.when(i == pl.num_programs(0) - 1)
    def _():
        hT_ref[...] = hs_ref[...]


def s5_mixer(u, h0, bblk, cre, cim, ar, ai, d, wglu, bglu, mg, nb, tc):
    rows = u.shape[0]
    rc = nb * tc
    full = lambda shape: pl.BlockSpec(shape, lambda i: (0,) * len(shape))
    return pl.pallas_call(
        functools.partial(_s5_kernel, nb=nb, tc=tc),
        out_shape=[jax.ShapeDtypeStruct((rows, S5_WIDTH), BF16),
                   jax.ShapeDtypeStruct((nb, 2 * S5_LANES), F32)],
        grid=(rows // rc,),
        in_specs=[pl.BlockSpec((rc, S5_WIDTH), lambda i: (i, 0)),
                  full((nb, 2 * S5_LANES)),
                  full((S5_WIDTH, 2 * S5_LANES)),
                  full((S5_LANES, S5_WIDTH)), full((S5_LANES, S5_WIDTH)),
                  full((1, S5_LANES)), full((1, S5_LANES)),
                  full((1, S5_WIDTH)), full((S5_WIDTH, S5_WIDTH)), full((1, S5_WIDTH)), full((1, S5_WIDTH))],
        out_specs=[pl.BlockSpec((rc, S5_WIDTH), lambda i: (i, 0)),
                   full((nb, 2 * S5_LANES))],
        scratch_shapes=[pltpu.VMEM((rc, 2 * S5_LANES), F32), pltpu.VMEM((nb, 2 * S5_LANES), F32)],
        compiler_params=_cparams(("arbitrary",)),
        name="s5_mixer",
    )(u, h0, bblk, cre, cim, ar, ai, d, wglu, bglu, mg)


CONV_ROWS = 64


def _conv_kernel(ag_ref, c0_ref, wdw_ref, bdw_ref, lng_ref, lnb_ref, wpw_ref, bpw_ref, mg_ref,
                 o_ref, buf_ref, zc_ref, y_ref, *, nb, tc):
    i = pl.program_id(0)
    hist = CONV_HIST * nb
    rc = nb * tc

    @pl.when(i == 0)
    def _():
        zc_ref[0:hist, :] = c0_ref[...]

    @pl.when(i > 0)
    def _():
        zc_ref[0:hist, :] = zc_ref[rc:rc + hist, :]

    a = ag_ref[:, 0:CONV_DIM]
    g = ag_ref[:, CONV_DIM:2 * CONV_DIM]
    zc_ref[hist:hist + rc, :] = a * jax.nn.sigmoid(g)

    w = wdw_ref[...]

    def tile(j, carry):
        r0 = pl.multiple_of(j * CONV_ROWS, CONV_ROWS)
        acc = jnp.zeros((CONV_ROWS, CONV_DIM), F32)
        for k in range(CONV_WIDTH):
            acc = acc + w[k:k + 1, :] * zc_ref[pl.ds(r0 + k * nb, CONV_ROWS), :]
        y_ref[pl.ds(r0, CONV_ROWS), :] = acc
        return carry

    lax.fori_loop(0, rc // CONV_ROWS, tile, 0)
    y = y_ref[...] + bdw_ref[...]
    y = _silu(_layernorm(y, lng_ref[...], lnb_ref[...]))
    y = jnp.dot(y.astype(BF16), wpw_ref[...], preferred_element_type=F32) + bpw_ref[...]
    o_ref[...] = (_rms(y) * mg_ref[...]).astype(o_ref.dtype)

    @pl.when(i == pl.num_programs(0) - 1)
    def _():
        buf_ref[...] = zc_ref[rc:rc + hist, :]


def conv_mixer(ag, c0, wdw, bdw, lng, lnb, wpw, bpw, mg, nb, tc):
    rows = ag.shape[0]
    rc = nb * tc
    hist = CONV_HIST * nb
    assert rows == rc or tc >= CONV_HIST
    full = lambda shape: pl.BlockSpec(shape, lambda i: (0,) * len(shape))
    return pl.pallas_call(
        functools.partial(_conv_kernel, nb=nb, tc=tc),
        out_shape=[jax.ShapeDtypeStruct((rows, CONV_DIM), BF16),
                   jax.ShapeDtypeStruct((hist, CONV_DIM), F32)],
        grid=(rows // rc,),
        in_specs=[pl.BlockSpec((rc, 2 * CONV_DIM), lambda i: (i, 0)),
                  full((hist, CONV_DIM)), full((CONV_WIDTH, CONV_DIM)),
                  full((1, CONV_DIM)), full((1, CONV_DIM)), full((1, CONV_DIM)),
                  full((CONV_DIM, CONV_DIM)), full((1, CONV_DIM)), full((1, CONV_DIM))],
        out_specs=[pl.BlockSpec((rc, CONV_DIM), lambda i: (i, 0)), full((hist, CONV_DIM))],
        scratch_shapes=[pltpu.VMEM((hist + rc, CONV_DIM), F32), pltpu.VMEM((rc, CONV_DIM), F32)],
        compiler_params=_cparams(("arbitrary",)),
        name="conv_mixer",
    )(ag, c0, wdw, bdw, lng, lnb, wpw, bpw, mg)


def _gmlp_seq_kernel(uv_ref, lng_ref, lnb_ref, wcat_ref, bias_ref, mg_ref, o_ref, *, tt):
    n_chunks = tt // GMLP_CHUNK
    kc = GMLP_HEADS * GMLP_CHUNK
    rowi = lax.broadcasted_iota(jnp.int32, (GMLP_CHUNK, kc), 0)
    coli = lax.broadcasted_iota(jnp.int32, (GMLP_CHUNK, kc), 1)
    wcat = jnp.where((coli & (GMLP_CHUNK - 1)) <= rowi, wcat_ref[...], 0.0).astype(BF16)
    sel = _same_block((kc, GMLP_WIDTH), GMLP_CHUNK, GMLP_HEAD_DIM)
    for c in range(n_chunks):
        rows = slice(c * GMLP_CHUNK, (c + 1) * GMLP_CHUNK)
        u = uv_ref[rows, 0:GMLP_WIDTH]
        v = uv_ref[rows, GMLP_WIDTH:2 * GMLP_WIDTH]
        vn = _layernorm(v, lng_ref[...], lnb_ref[...])
        vbd = jnp.where(sel, jnp.concatenate([vn] * GMLP_HEADS, axis=0), 0.0).astype(BF16)
        mixed = jnp.dot(wcat, vbd, preferred_element_type=F32) + bias_ref[...]
        o_ref[rows, :] = (_rms(u * mixed) * mg_ref[...]).astype(o_ref.dtype)


def gmlp_seq(uv, nseq, lng, lnb, wcat, bias, mg, tt):
    t = uv.shape[0]
    full = lambda shape: pl.BlockSpec(shape, lambda b_, i: (0,) * len(shape))
    return pl.pallas_call(
        functools.partial(_gmlp_seq_kernel, tt=tt),
        out_shape=jax.ShapeDtypeStruct((t, nseq * GMLP_WIDTH), BF16),
        grid=(nseq, t // tt),
        in_specs=[pl.BlockSpec((tt, 2 * GMLP_WIDTH), lambda b_, i: (i, b_)),
                  full((1, GMLP_WIDTH)), full((1, GMLP_WIDTH)),
                  full((GMLP_CHUNK, GMLP_HEADS * GMLP_CHUNK)), full((GMLP_CHUNK, GMLP_WIDTH)),
                  full((1, GMLP_WIDTH))],
        out_specs=pl.BlockSpec((tt, GMLP_WIDTH), lambda b_, i: (i, b_)),
        compiler_params=_cparams(("arbitrary", "arbitrary")),
        name="gmlp_seq",
    )(uv, lng, lnb, wcat, bias, mg)


def _gmlp_short_kernel(uv_ref, lng_ref, lnb_ref, wrow_ref, brow_ref, mg_ref, o_ref, vn_ref, *, nb, t_len):
    u = uv_ref[:, 0:GMLP_WIDTH]
    v = uv_ref[:, GMLP_WIDTH:2 * GMLP_WIDTH]
    vn = _layernorm(v, lng_ref[...], lnb_ref[...])
    vn_ref[...] = vn
    wrow = wrow_ref[...]
    brow = brow_ref[...]
    for t in range(t_len):
        mixed = jnp.zeros((nb, GMLP_WIDTH), F32) + brow[t:t + 1, :]
        for j in range(t + 1):
            mixed = mixed + wrow[t * t_len + j:t * t_len + j + 1, :] * vn[j * nb:(j + 1) * nb, :]
        o = u[t * nb:(t + 1) * nb, :] * mixed
        o_ref[t * nb:(t + 1) * nb, :] = (_rms(o) * mg_ref[...]).astype(o_ref.dtype)


def gmlp_short(uv, lng, lnb, wrow, brow, mg, nb, t_len):
    rows = uv.shape[0]
    return pl.pallas_call(
        functools.partial(_gmlp_short_kernel, nb=nb, t_len=t_len),
        out_shape=[jax.ShapeDtypeStruct((rows, GMLP_WIDTH), BF16),
                   jax.ShapeDtypeStruct((rows, GMLP_WIDTH), F32)],
        compiler_params=pltpu.CompilerParams(vmem_limit_bytes=VMEM_LIMIT),
        name="gmlp_short",
    )(uv, lng, lnb, wrow, brow, mg)


def _gla_tail(o, r, gmean, onorm, mg):
    ms = jnp.dot(o * o, gmean, preferred_element_type=F32, precision=HI)
    o = o * lax.rsqrt(ms + EPS) * onorm
    o = o * _silu(r)
    return _rms(o) * mg


def _head_mean_matrix():
    return jnp.where(_same_block((GLA_WIDTH, GLA_WIDTH), GLA_DV, GLA_DV), 1.0 / GLA_DV, 0.0).astype(F32)


def _gla_seq_kernel(x_ref, s0_ref, wg_ref, bg_ref, onorm_ref, mg_ref, o_ref, sT_ref, s_ref, *, tt):
    i = pl.program_id(1)
    L = GLA_CHUNK
    kw, vw = GLA_KEY_WIDTH, GLA_WIDTH

    @pl.when(i == 0)
    def _():
        s_ref[...] = s0_ref[0]

    tri = (lax.broadcasted_iota(jnp.int32, (L, L), 1) <= lax.broadcasted_iota(jnp.int32, (L, L), 0)).astype(F32)
    kbd_sel = _same_block((GLA_HEADS * L, kw), L, GLA_DK)
    vbd_sel = _same_block((GLA_HEADS * L, vw), L, GLA_DV)
    causal = ((lax.broadcasted_iota(jnp.int32, (L, GLA_HEADS * L), 1) & (L - 1))
              <= lax.broadcasted_iota(jnp.int32, (L, GLA_HEADS * L), 0))
    s_sel = _same_block((kw, vw), GLA_DK, GLA_DV)
    eye = (lax.broadcasted_iota(jnp.int32, (kw, kw), 0) == lax.broadcasted_iota(jnp.int32, (kw, kw), 1))
    gmean = _head_mean_matrix()
    scale = GLA_DK ** -0.5

    def chunk(c, carry):
        r0 = pl.multiple_of(c * L, L)
        q = x_ref[pl.ds(r0, L), 0:kw] * scale
        k = x_ref[pl.ds(r0, L), kw:2 * kw]
        v = x_ref[pl.ds(r0, L), 2 * kw:2 * kw + vw]
        r = x_ref[pl.ds(r0, L), 2 * kw + vw:2 * kw + 2 * vw]
        gl = x_ref[pl.ds(r0, L), 2 * kw + 2 * vw:2 * kw + 2 * vw + LANE]
        la = _log_sigmoid(jnp.dot(gl.astype(BF16), wg_ref[...], preferred_element_type=F32) + bg_ref[...])
        la = la / GLA_TAU
        bc = jnp.dot(tri, la, preferred_element_type=F32, precision=HI)
        b_last = bc[L - 1:L, :]
        qt = (q * jnp.exp(bc)).astype(BF16)
        kt = k * jnp.exp(-bc)
        kbd = jnp.where(kbd_sel, jnp.concatenate([kt] * GLA_HEADS, axis=0), 0.0).astype(BF16)
        att = lax.dot_general(qt, kbd, (((1,), (1,)), ((), ())), preferred_element_type=F32)
        att = jnp.where(causal, att, 0.0).astype(BF16)
        vb = v.astype(BF16)
        vbd = jnp.where(vbd_sel, jnp.concatenate([vb] * GLA_HEADS, axis=0), jnp.zeros((), BF16))
        s = s_ref[...]
        o = (jnp.dot(att, vbd, preferred_element_type=F32)
             + jnp.dot(qt, s.astype(BF16), preferred_element_type=F32))
        kdec = (k * jnp.exp(b_last - bc)).astype(BF16)
        upd = lax.dot_general(kdec, vb, (((0,), (0,)), ((), ())), preferred_element_type=F32)
        dmat = jnp.where(eye, jnp.broadcast_to(jnp.exp(b_last), (kw, kw)), 0.0)
        s_ref[...] = (jnp.dot(dmat, s, preferred_element_type=F32, precision=HI)
                      + jnp.where(s_sel, upd, 0.0))
        o_ref[pl.ds(r0, L), :] = _gla_tail(o, r, gmean, onorm_ref[...], mg_ref[...]).astype(o_ref.dtype)
        return carry

    lax.fori_loop(0, tt // L, chunk, 0)

    @pl.when(i == pl.num_programs(1) - 1)
    def _():
        sT_ref[0] = s_ref[...]


def gla_seq(x, nseq, s0, wg, bg, onorm, mg, tt):
    t = x.shape[0]
    full = lambda shape: pl.BlockSpec(shape, lambda b_, i: (0,) * len(shape))
    return pl.pallas_call(
        functools.partial(_gla_seq_kernel, tt=tt),
        out_shape=[jax.ShapeDtypeStruct((t, nseq * GLA_WIDTH), BF16),
                   jax.ShapeDtypeStruct((nseq, GLA_KEY_WIDTH, GLA_WIDTH), F32)],
        grid=(nseq, t // tt),
        in_specs=[pl.BlockSpec((tt, PW_GLA), lambda b_, i: (i, b_)),
                  pl.BlockSpec((1, GLA_KEY_WIDTH, GLA_WIDTH), lambda b_, i: (b_, 0, 0)),
                  full((LANE, GLA_KEY_WIDTH)), full((1, GLA_KEY_WIDTH)),
                  full((1, GLA_WIDTH)), full((1, GLA_WIDTH))],
        out_specs=[pl.BlockSpec((tt, GLA_WIDTH), lambda b_, i: (i, b_)),
                   pl.BlockSpec((1, GLA_KEY_WIDTH, GLA_WIDTH), lambda b_, i: (b_, 0, 0))],
        scratch_shapes=[pltpu.VMEM((GLA_KEY_WIDTH, GLA_WIDTH), F32)],
        compiler_params=_cparams(("arbitrary", "arbitrary")),
        name="gla_seq",
    )(x, s0, wg, bg, onorm, mg)


def _split3(x):
    a = x.astype(BF16)
    r1 = x - a.astype(F32)
    b = r1.astype(BF16)
    c = (r1 - b.astype(F32)).astype(BF16)
    return a, b, c


def _gla_rec_kernel(x_ref, s0_ref, ek_ref, ev_ref, wg_ref, bg_ref, onorm_ref, mg_ref, o_ref, sT_ref,
                    *, nb, t_len):
    kw, vw = GLA_KEY_WIDTH, GLA_WIDTH
    hl = GLA_DK * GLA_DV
    sT_ref[...] = s0_ref[...]
    gmean = _head_mean_matrix()
    scale = GLA_DK ** -0.5

    def step(t, carry):
        r0 = pl.multiple_of(t * nb, nb)
        q = x_ref[pl.ds(r0, nb), 0:kw] * scale
        k = x_ref[pl.ds(r0, nb), kw:2 * kw]
        v = x_ref[pl.ds(r0, nb), 2 * kw:2 * kw + vw]
        r = x_ref[pl.ds(r0, nb), 2 * kw + vw:2 * kw + 2 * vw]
        gl = x_ref[pl.ds(r0, nb), 2 * kw + 2 * vw:2 * kw + 2 * vw + LANE]
        la = _log_sigmoid(jnp.dot(gl.astype(BF16), wg_ref[...], preferred_element_type=F32) + bg_ref[...])
        a = jnp.exp(la / GLA_TAU)
        a3 = _split3(a)
        qb = q.astype(BF16)
        kb = k.astype(BF16)
        vb = v.astype(BF16)
        outs = []
        for h in range(GLA_HEADS):
            lanes = slice(h * hl, (h + 1) * hl)
            ek = ek_ref[:, lanes]
            a_e = (jnp.dot(a3[0], ek, preferred_element_type=F32)
                   + jnp.dot(a3[1], ek, preferred_element_type=F32)
                   + jnp.dot(a3[2], ek, preferred_element_type=F32))
            k_e = jnp.dot(kb, ek, preferred_element_type=F32)
            q_e = jnp.dot(qb, ek, preferred_element_type=F32)
            v_e = jnp.dot(vb, ev_ref[:, lanes], preferred_element_type=F32)
            s_new = a_e * sT_ref[:, lanes] + k_e * v_e
            sT_ref[:, lanes] = s_new
            prod = q_e * s_new
            acc = prod[:, 0:LANE]
            for j in range(1, hl // LANE):
                acc = acc + prod[:, j * LANE:(j + 1) * LANE]
            outs.append(acc[:, 0:GLA_DV] + acc[:, GLA_DV:2 * GLA_DV])
        o = jnp.concatenate(outs, axis=1)
        o_ref[pl.ds(r0, nb), :] = _gla_tail(o, r, gmean, onorm_ref[...], mg_ref[...]).astype(o_ref.dtype)
        return carry

    lax.fori_loop(0, t_len, step, 0)


def gla_recurrent(x, s0, ek, ev, wg, bg, onorm, mg, nb, t_len):
    rows = x.shape[0]
    return pl.pallas_call(
        functools.partial(_gla_rec_kernel, nb=nb, t_len=t_len),
        out_shape=[jax.ShapeDtypeStruct((rows, GLA_WIDTH), BF16),
                   jax.ShapeDtypeStruct((nb, GLA_STATE_LANES), F32)],
        compiler_params=pltpu.CompilerParams(vmem_limit_bytes=VMEM_LIMIT),
        name="gla_recurrent",
    )(x, s0, ek, ev, wg, bg, onorm, mg)


def _gla_expanders():
    lane = jnp.arange(GLA_STATE_LANES)
    h = lane // (GLA_DK * GLA_DV)
    dk = (lane // GLA_DV) % GLA_DK
    dv = lane % GLA_DV
    ek = (jnp.arange(GLA_KEY_WIDTH)[:, None] == (h * GLA_DK + dk)[None, :]).astype(BF16)
    ev = (jnp.arange(GLA_WIDTH)[:, None] == (h * GLA_DV + dv)[None, :]).astype(BF16)
    return ek, ev


def _post_kernel(x_ref, m0_ref, m1_ref, m2_ref, m3_ref, g1_ref, sh2_ref, sc2_ref, g2_ref, ng_ref, wout_ref,
                 router_ref, wg_ref, wu_ref, wd_ref, fg_ref, o_ref, h2_ref, acc_ref, gates_ref,
                 *, n_exp, final_norm):
    e = pl.program_id(2)
    c = pl.program_id(3)
    first = jnp.logical_and(e == 0, c == 0)
    last = jnp.logical_and(e == pl.num_programs(2) - 1, c == pl.num_programs(3) - 1)

    @pl.when(first)
    def _():
        mix = jnp.concatenate([m0_ref[...], m1_ref[...], m2_ref[...], m3_ref[...]], axis=1)
        proj = jnp.dot(mix, wout_ref[...], preferred_element_type=F32)
        x1 = x_ref[0] + _gate(proj, g1_ref[0])
        o_ref[0] = x1
        h = _modulate(_rms(x1) * ng_ref[...], sc2_ref[0], sh2_ref[0])
        h2_ref[...] = h.astype(BF16)
        acc_ref[...] = jnp.zeros_like(acc_ref)
        if n_exp > 1:
            logits = jnp.dot(h, router_ref[...], preferred_element_type=F32, precision=HI)
            lane = lax.broadcasted_iota(jnp.int32, logits.shape, 1).astype(F32)
            neg = jnp.float32(-jnp.inf)
            logits = jnp.where(lane < n_exp, logits, neg)
            m1 = jnp.max(logits, axis=1, keepdims=True)
            i1 = jnp.min(jnp.where(logits == m1, lane, float(LANE)), axis=1, keepdims=True)
            rest = jnp.where(lane == i1, neg, logits)
            m2 = jnp.max(rest, axis=1, keepdims=True)
            i2 = jnp.min(jnp.where(rest == m2, lane, float(LANE)), axis=1, keepdims=True)
            e2 = jnp.exp(m2 - m1)
            den = 1.0 + e2
            gates_ref[...] = jnp.where(lane == i1, 1.0 / den, 0.0) + jnp.where(lane == i2, e2 / den, 0.0)

    h2 = h2_ref[...]
    t = (_silu(jnp.dot(h2, wg_ref[0], preferred_element_type=F32))
         * jnp.dot(h2, wu_ref[0], preferred_element_type=F32))
    if n_exp > 1:
        lane = lax.broadcasted_iota(jnp.int32, gates_ref.shape, 1)
        ge = jnp.sum(jnp.where(lane == e, gates_ref[...], 0.0), axis=1, keepdims=True)
        t = t * ge
    acc_ref[...] += jnp.dot(t.astype(BF16), wd_ref[0], preferred_element_type=F32)

    @pl.when(last)
    def _():
        x2 = o_ref[0] + _gate(acc_ref[...], g2_ref[0])
        if final_norm:
            x2 = _rms(x2) * fg_ref[...]
        o_ref[0] = x2


def post_mixer(x, mixes, g1, sh2, sc2, g2, ng, wout, router, wg, wu, wd, fg, tm, tf, final_norm):
    s, r, d = x.shape
    rm = g1.shape[1]
    n_exp, _, ff = wg.shape
    mod = pl.BlockSpec((1, rm, d), lambda b_, i, e, c: (b_, 0, 0))
    const = lambda shape: pl.BlockSpec(shape, lambda b_, i, e, c: (0,) * len(shape))
    mixspec = pl.BlockSpec((tm, 256), lambda b_, i, e, c: (i, b_))
    return pl.pallas_call(
        functools.partial(_post_kernel, n_exp=n_exp, final_norm=final_norm),
        out_shape=jax.ShapeDtypeStruct((s, r, d), F32),
        grid=(s, r // tm, n_exp, ff // tf),
        in_specs=[pl.BlockSpec((1, tm, d), lambda b_, i, e, c: (b_, i, 0)),
                  mixspec, mixspec, mixspec, mixspec,
                  mod, mod, mod, mod,
                  const((1, d)), const((d, d)), const((d, LANE)),
                  pl.BlockSpec((1, d, tf), lambda b_, i, e, c: (e, 0, c)),
                  pl.BlockSpec((1, d, tf), lambda b_, i, e, c: (e, 0, c)),
                  pl.BlockSpec((1, tf, d), lambda b_, i, e, c: (e, c, 0)),
                  const((1, d))],
        out_specs=pl.BlockSpec((1, tm, d), lambda b_, i, e, c: (b_, i, 0)),
        scratch_shapes=[pltpu.VMEM((tm, d), BF16), pltpu.VMEM((tm, d), F32), pltpu.VMEM((tm, LANE), F32)],
        compiler_params=_cparams(("arbitrary", "arbitrary", "arbitrary", "arbitrary")),
        name="post_mixer",
    )(x, *mixes, g1, sh2, sc2, g2, ng, wout, router, wg, wu, wd, fg)


def _reorder_w_in(w_in, b_in):
    cut = PW_S5 + 128 + 128 + 256 + 256 + GLA_GATE_RANK
    pad = LANE - GLA_GATE_RANK
    w = jnp.concatenate([w_in[:, :cut], jnp.zeros((w_in.shape[0], pad), w_in.dtype), w_in[:, cut:]], axis=1)
    b = jnp.concatenate([b_in[:cut], jnp.zeros((pad,), b_in.dtype), b_in[cut:]])
    return w.astype(BF16), b.reshape(1, PW_TOTAL)


def _run_branch(x, mods, states, weights, *, nseq, nb, t_len, seq_form, tm_pre, tm_post, tc_s5, tc_conv, tt_seq):
    W = weights
    depth = W['w_in'].shape[0]
    s5_re0, s5_im0, gla0, conv0 = states
    new_re, new_im, new_gla, new_conv, new_v = [], [], [], [], []
    row = lambda a: a.reshape(1, -1)
    for i in range(depth):
        sh1, sc1, g1, sh2, sc2, g2 = mods[i]
        w_in, b_in = _reorder_w_in(W['w_in'][i], W['b_in'][i])
        p_s5, p_gla, p_conv, p_mlp = pre_mixer(x, sh1, sc1, row(W['norm_mix_g'][i]), w_in, b_in, tm_pre)
        tmaj = lambda a, w_: a.reshape(t_len * nb, w_)
        mg = W['merge_g'][i]

        ab_re, ab_im, bb_re, bb_im = s5_discretise(W['s5_a_re'][i], W['s5_a_im'][i], W['s5_log_dt'][i],
                                                   W['s5_b_re'][i], W['s5_b_im'][i])
        bblk = jnp.concatenate([_block_diag_in(bb_re), _block_diag_in(bb_im)], axis=1).astype(BF16)
        cre = _block_diag_out(W['s5_c_re'][i]).astype(BF16)
        cim = _block_diag_out(W['s5_c_im'][i]).astype(BF16)
        h0 = jnp.concatenate([s5_re0[i].reshape(nb, S5_LANES), s5_im0[i].reshape(nb, S5_LANES)], axis=1)
        o_s5, h_t = s5_mixer(tmaj(p_s5, PW_S5), h0, bblk, cre, cim, row(ab_re), row(ab_im), row(W['s5_d'][i]),
                             W['s5_w_glu'][i].astype(BF16), row(W['s5_b_glu'][i]), row(mg[0:256]), nb, tc_s5)
        new_re.append(h_t[:, :S5_LANES].reshape(nb, S5_GROUPS, S5_STATE))
        new_im.append(h_t[:, S5_LANES:].reshape(nb, S5_GROUPS, S5_STATE))

        wg2 = jnp.zeros((LANE, GLA_KEY_WIDTH), F32).at[:GLA_GATE_RANK].set(W['gla_w_gate2'][i]).astype(BF16)
        bg2 = row(W['gla_b_gate2'][i])
        onorm = row(W['gla_onorm_g'][i])
        if seq_form:
            eye = jnp.eye(GLA_HEADS, dtype=F32)
            s0 = jnp.einsum('bhkv,hg->bhkgv', gla0[i], eye).reshape(nseq, GLA_KEY_WIDTH, GLA_WIDTH)
            o_gla, s_t = gla_seq(p_gla, nseq, s0, wg2, bg2, onorm, row(mg[256:512]), tt_seq)
            s4 = s_t.reshape(nseq, GLA_HEADS, GLA_DK, GLA_HEADS, GLA_DV)
            new_gla.append(jnp.stack([s4[:, h, :, h, :] for h in range(GLA_HEADS)], axis=1))
        else:
            ek, ev = _gla_expanders()
            o_gla, s_t = gla_recurrent(p_gla, gla0[i].reshape(nb, GLA_STATE_LANES), ek, ev, wg2, bg2, onorm,
                                       row(mg[256:512]), nb, t_len)
            new_gla.append(s_t.reshape(nb, GLA_HEADS, GLA_DK, GLA_DV))

        c0 = jnp.transpose(conv0[i], (1, 0, 2)).reshape(CONV_HIST * nb, CONV_DIM)
        o_conv, buf = conv_mixer(tmaj(p_conv, PW_CONV), c0, W['conv_w_dw'][i], row(W['conv_b_dw'][i]),
                                 row(W['conv_ln_g'][i]), row(W['conv_ln_b'][i]), W['conv_w_pw'][i].astype(BF16),
                                 row(W['conv_b_pw'][i]), row(mg[512:768]), nb, tc_conv)
        new_conv.append(jnp.transpose(buf.reshape(CONV_HIST, nb, CONV_DIM), (1, 0, 2)))

        ws = W['gmlp_w_s'][i]
        bs = W['gmlp_b_s'][i]
        lng, lnb = row(W['gmlp_ln_g'][i]), row(W['gmlp_ln_b'][i])
        if seq_form:
            wcat = jnp.transpose(ws, (1, 0, 2)).reshape(GMLP_CHUNK, GMLP_HEADS * GMLP_CHUNK)
            bias = jnp.repeat(bs.T, GMLP_HEAD_DIM, axis=1)
            o_mlp = gmlp_seq(p_mlp, nseq, lng, lnb, wcat, bias, row(mg[768:1024]), tt_seq)
            new_v.append(None)
        else:
            tri = jnp.tril(jnp.ones((t_len, t_len), F32))
            wrow = jnp.repeat(jnp.transpose(ws[:, :t_len, :t_len] * tri[None], (1, 2, 0)).reshape(t_len * t_len, GMLP_HEADS),
                              GMLP_HEAD_DIM, axis=1)
            brow = jnp.repeat(bs[:, :t_len].T, GMLP_HEAD_DIM, axis=1)
            o_mlp, vn = gmlp_short(p_mlp, lng, lnb, wrow, brow, row(mg[768:1024]), nb, t_len)
            new_v.append(vn)

        per_seq = lambda a: a.reshape(t_len, nb * 256) if seq_form else a
        mixes = [per_seq(o_s5), per_seq(o_gla), per_seq(o_conv), per_seq(o_mlp)]
        last = i == depth - 1
        if i % 2 == 0:
            j = i // 2
            wg_, wu_, wd_ = (W['ffn_w_gate'][j][None].astype(BF16), W['ffn_w_up'][j][None].astype(BF16),
                             W['ffn_w_down'][j][None].astype(BF16))
            router = jnp.zeros((D_MODEL, LANE), F32)
        else:
            j = i // 2
            wg_, wu_, wd_ = (W['moe_w_gate'][j].astype(BF16), W['moe_w_up'][j].astype(BF16),
                             W['moe_w_down'][j].astype(BF16))
            router = jnp.zeros((D_MODEL, LANE), F32).at[:, :N_EXPERTS].set(W['moe_router'][j])
        x = post_mixer(x, mixes, g1, sh2, sc2, g2, row(W['norm_ffn_g'][i]), W['w_out'][i].astype(BF16), router,
                       wg_, wu_, wd_, row(W['final_norm_g']), tm_post, 1408, last)
    return x, new_re, new_im, new_gla, new_conv, new_v


def kernel(x_prompt, x_sample, c_prompt, c_sample, state_s5_re, state_s5_im, state_gla, cache_conv, ada_w, ada_b, norm_mix_g, norm_ffn_g, w_in, b_in, s5_a_re, s5_a_im, s5_log_dt, s5_b_re, s5_b_im, s5_c_re, s5_c_im, s5_d, s5_w_glu, s5_b_glu, gla_w_gate2, gla_b_gate2, gla_onorm_g, conv_w_dw, conv_b_dw, conv_ln_g, conv_ln_b, conv_w_pw, conv_b_pw, gmlp_ln_g, gmlp_ln_b, gmlp_w_s, gmlp_b_s, merge_g, w_out, ffn_w_gate, ffn_w_up, ffn_w_down, moe_router, moe_w_gate, moe_w_up, moe_w_down, final_norm_g):
    W = dict(norm_mix_g=norm_mix_g, norm_ffn_g=norm_ffn_g, w_in=w_in, b_in=b_in, s5_a_re=s5_a_re, s5_a_im=s5_a_im,
             s5_log_dt=s5_log_dt, s5_b_re=s5_b_re, s5_b_im=s5_b_im, s5_c_re=s5_c_re, s5_c_im=s5_c_im, s5_d=s5_d,
             s5_w_glu=s5_w_glu, s5_b_glu=s5_b_glu, gla_w_gate2=gla_w_gate2, gla_b_gate2=gla_b_gate2,
             gla_onorm_g=gla_onorm_g, conv_w_dw=conv_w_dw, conv_b_dw=conv_b_dw, conv_ln_g=conv_ln_g,
             conv_ln_b=conv_ln_b, conv_w_pw=conv_w_pw, conv_b_pw=conv_b_pw, gmlp_ln_g=gmlp_ln_g,
             gmlp_ln_b=gmlp_ln_b, gmlp_w_s=gmlp_w_s, gmlp_b_s=gmlp_b_s, merge_g=merge_g, w_out=w_out,
             ffn_w_gate=ffn_w_gate, ffn_w_up=ffn_w_up, ffn_w_down=ffn_w_down, moe_router=moe_router,
             moe_w_gate=moe_w_gate, moe_w_up=moe_w_up, moe_w_down=moe_w_down, final_norm_g=final_norm_g)
    depth = w_in.shape[0]
    bp, tp, d = x_prompt.shape
    bs, ts, _ = x_sample.shape

    m = ada_modulation(jnp.concatenate([c_prompt, c_sample], axis=0), ada_w, ada_b)
    mods_p = [[m[i, :bp, k * d:(k + 1) * d].reshape(bp, 1, d) for k in range(6)] for i in range(depth)]
    mods_s = [[m[i, bp:, k * d:(k + 1) * d].reshape(1, bs, d) for k in range(6)] for i in range(depth)]

    z_re = jnp.zeros((depth, bp, S5_GROUPS, S5_STATE), F32)
    z_gla = jnp.zeros((depth, bp, GLA_HEADS, GLA_DK, GLA_DV), F32)
    z_conv = jnp.zeros((depth, bp, CONV_HIST, CONV_DIM), x_prompt.dtype)
    y_p, p_re, p_im, p_gla, p_conv, _ = _run_branch(
        x_prompt, mods_p, (z_re, z_re, z_gla, z_conv), W, nseq=bp, nb=bp, t_len=tp, seq_form=True,
        tm_pre=min(512, tp), tm_post=min(512, tp), tc_s5=min(128, tp), tc_conv=min(128, tp), tt_seq=min(512, tp))

    xs = jnp.transpose(x_sample, (1, 0, 2)).reshape(1, ts * bs, d)
    y_s, s_re, s_im, s_gla, s_conv, s_v = _run_branch(
        xs, mods_s, (state_s5_re, state_s5_im, state_gla, cache_conv), W, nseq=1, nb=bs, t_len=ts, seq_form=False,
        tm_pre=ts * bs, tm_post=min(512, ts * bs), tc_s5=ts, tc_conv=ts, tt_seq=None)
    y_s = jnp.transpose(y_s.reshape(ts, bs, d), (1, 0, 2))
    s_v = [jnp.transpose(v.reshape(ts, bs, GMLP_WIDTH), (1, 0, 2)) for v in s_v]

    st = jnp.stack
    return (y_p, y_s, st(p_re), st(p_im), st(p_gla), st(p_conv),
            st(s_re), st(s_im), st(s_gla), st(s_conv), st(s_v))
```

```python
import functools
import math

import jax
import jax.numpy as jnp
from jax import lax
from jax.experimental import pallas as pl
from jax.experimental.pallas import tpu as pltpu

D_MODEL = 1024
S5_WIDTH = 256
S5_GROUP = 16
S5_GROUPS = 16
S5_STATE = 64
S5_LANES = S5_GROUPS * S5_STATE
GLA_HEADS = 4
GLA_DV = 64
GLA_DK = 32
GLA_WIDTH = 256
GLA_KEY_WIDTH = 128
GLA_GATE_RANK = 16
GLA_TAU = 16.0
GLA_CHUNK = 64
GLA_STATE_LANES = GLA_HEADS * GLA_DK * GLA_DV
CONV_DIM = 256
CONV_WIDTH = 31
CONV_HIST = CONV_WIDTH - 1
GMLP_WIDTH = 256
GMLP_HEADS = 4
GMLP_HEAD_DIM = 64
GMLP_CHUNK = 128
D_FF = 2816
N_EXPERTS = 8
EPS = 1e-6

LANE = 128
PW_S5 = 256
PW_GLA = 128 + 128 + 256 + 256 + LANE
PW_CONV = 512
PW_MLP = 512
PW_TOTAL = PW_S5 + PW_GLA + PW_CONV + PW_MLP
VMEM_LIMIT = 56 * 1024 * 1024

F32 = jnp.float32
BF16 = jnp.bfloat16
HI = lax.Precision.HIGHEST


def _cparams(sem):
    return pltpu.CompilerParams(dimension_semantics=sem, vmem_limit_bytes=VMEM_LIMIT)


def _rms(x):
    return x * lax.rsqrt(jnp.mean(x * x, axis=-1, keepdims=True) + EPS)


def _layernorm(x, g, b):
    mu = jnp.mean(x, axis=-1, keepdims=True)
    xc = x - mu
    var = jnp.mean(xc * xc, axis=-1, keepdims=True)
    return xc * lax.rsqrt(var + EPS) * g + b


def _silu(x):
    return x * jax.nn.sigmoid(x)


def _gelu_tanh(x):
    return 0.5 * x * (1.0 + jnp.tanh(math.sqrt(2.0 / math.pi) * (x + 0.044715 * (x * x * x))))


def _log_sigmoid(x):
    return jnp.minimum(x, 0.0) - jnp.log(1.0 + jnp.exp(-jnp.abs(x)))


def _same_block(shape, row_block, col_block):
    r = lax.broadcasted_iota(jnp.int32, shape, 0) >> (row_block.bit_length() - 1)
    c = lax.broadcasted_iota(jnp.int32, shape, 1) >> (col_block.bit_length() - 1)
    return r == c


def _modulate(y, sc, sh):
    rm = sc.shape[0]
    if rm == 1:
        return y * (1.0 + sc) + sh
    rows, d = y.shape
    y3 = y.reshape(rows // rm, rm, d)
    return (y3 * (1.0 + sc)[None] + sh[None]).reshape(rows, d)


def _gate(y, g):
    rm = g.shape[0]
    if rm == 1:
        return y * g
    rows, d = y.shape
    return (y.reshape(rows // rm, rm, d) * g[None]).reshape(rows, d)


def _ada_kernel(c_ref, w_ref, b_ref, o_ref):
    c = c_ref[...]
    s = _silu(c).astype(BF16)
    o_ref[0] = jnp.dot(s, w_ref[0].astype(BF16), preferred_element_type=F32) + b_ref[0]


def ada_modulation(c_all, ada_w, ada_b):
    depth, d, n6 = ada_w.shape
    rows = c_all.shape[0]
    tn = 1536
    return pl.pallas_call(
        _ada_kernel,
        out_shape=jax.ShapeDtypeStruct((depth, rows, n6), F32),
        grid=(depth, n6 // tn),
        in_specs=[pl.BlockSpec((rows, d), lambda l, j: (0, 0)),
                  pl.BlockSpec((1, d, tn), lambda l, j: (l, 0, j)),
                  pl.BlockSpec((1, 1, tn), lambda l, j: (l, 0, j))],
        out_specs=pl.BlockSpec((1, rows, tn), lambda l, j: (l, 0, j)),
        compiler_params=_cparams(("arbitrary", "arbitrary")),
        name="ada_modulation",
    )(c_all, ada_w, ada_b.reshape(depth, 1, n6))


def _pre_kernel(x_ref, sh_ref, sc_ref, g_ref, w_ref, b_ref, o_s5, o_gla, o_conv, o_mlp):
    x = x_ref[0]
    y = _modulate(_rms(x) * g_ref[...], sc_ref[0], sh_ref[0])
    p = jnp.dot(y.astype(BF16), w_ref[...], preferred_element_type=F32) + b_ref[...]
    o_s5[...] = p[:, 0:PW_S5]
    o_gla[...] = p[:, PW_S5:PW_S5 + PW_GLA]
    o_conv[...] = p[:, PW_S5 + PW_GLA:PW_S5 + PW_GLA + PW_CONV]
    o_mlp[...] = p[:, PW_S5 + PW_GLA + PW_CONV:PW_TOTAL]


def pre_mixer(x, sh, sc, g, w, b, tm):
    s, r, d = x.shape
    rm = sh.shape[1]
    widths = (PW_S5, PW_GLA, PW_CONV, PW_MLP)
    return pl.pallas_call(
        _pre_kernel,
        out_shape=[jax.ShapeDtypeStruct((r, s * w_), F32) for w_ in widths],
        grid=(s, r // tm),
        in_specs=[pl.BlockSpec((1, tm, d), lambda b_, i: (b_, i, 0)),
                  pl.BlockSpec((1, rm, d), lambda b_, i: (b_, 0, 0)),
                  pl.BlockSpec((1, rm, d), lambda b_, i: (b_, 0, 0)),
                  pl.BlockSpec((1, d), lambda b_, i: (0, 0)),
                  pl.BlockSpec((d, PW_TOTAL), lambda b_, i: (0, 0)),
                  pl.BlockSpec((1, PW_TOTAL), lambda b_, i: (0, 0))],
        out_specs=[pl.BlockSpec((tm, w_), lambda b_, i: (i, b_)) for w_ in widths],
        compiler_params=_cparams(("arbitrary", "arbitrary")),
        name="pre_mixer",
    )(x, sh, sc, g, w, b)


def _s5_disc_kernel(lr_ref, li_ref, ldt_ref, br_ref, bi_ref, abr_ref, abi_ref, bbr_ref, bbi_ref):
    lr = lr_ref[...]
    li = li_ref[...]
    dt = jnp.exp(ldt_ref[...])
    mag = jnp.exp(lr * dt)
    ang = li * dt
    ab_re = mag * jnp.cos(ang)
    ab_im = mag * jnp.sin(ang)
    den = lr * lr + li * li
    nr = ab_re - 1.0
    f_re = (nr * lr + ab_im * li) / den
    f_im = (ab_im * lr - nr * li) / den
    br = br_ref[...]
    bi = bi_ref[...]
    abr_ref[...] = ab_re
    abi_ref[...] = ab_im
    bbr_ref[...] = f_re * br - f_im * bi
    bbi_ref[...] = f_re * bi + f_im * br


def s5_discretise(a_re, a_im, log_dt, b_re, b_im):
    n = b_re.shape[-1]
    gp = a_re.size
    bc = lambda a: jnp.broadcast_to(a.reshape(gp, 1), (gp, n))
    ldt = jnp.broadcast_to(log_dt[:, None], a_re.shape)
    outs = pl.pallas_call(
        _s5_disc_kernel,
        out_shape=[jax.ShapeDtypeStruct((gp, n), F32)] * 4,
        name="s5_discretise",
    )(bc(a_re), bc(a_im), bc(ldt), b_re.reshape(gp, n), b_im.reshape(gp, n))
    ab_re, ab_im, bb_re, bb_im = outs
    return ab_re[:, 0], ab_im[:, 0], bb_re, bb_im


def _block_diag_in(bb):
    g, p, n = S5_GROUPS, S5_STATE, S5_GROUP
    b3 = bb.reshape(g, p, n)
    eye = jnp.eye(g, dtype=bb.dtype)
    return jnp.einsum('gpn,gh->gnhp', b3, eye).reshape(g * n, g * p)


def _block_diag_out(c):
    g, p, n = S5_GROUPS, S5_STATE, S5_GROUP
    eye = jnp.eye(g, dtype=c.dtype)
    return jnp.einsum('gnp,gh->gphn', c, eye).reshape(g * p, g * n)


def _s5_kernel(u_ref, h0_ref, bblk_ref, cre_ref, cim_ref, ar_ref, ai_ref, d_ref, wglu_ref, bglu_ref, mg_ref,
               o_ref, hT_ref, xs_ref, hs_ref, *, nb, tc):
    i = pl.program_id(0)

    @pl.when(i == 0)
    def _():
        hs_ref[...] = h0_ref[...]

    u = u_ref[...]
    xs_ref[...] = jnp.dot(u.astype(BF16), bblk_ref[...], preferred_element_type=F32)
    ar = jnp.broadcast_to(ar_ref[...], (nb, S5_LANES))
    ai = jnp.broadcast_to(ai_ref[...], (nb, S5_LANES))

    def step(t, carry):
        hr, hi = carry
        row = pl.multiple_of(t * nb, nb)
        xr = xs_ref[pl.ds(row, nb), 0:S5_LANES]
        xi = xs_ref[pl.ds(row, nb), S5_LANES:2 * S5_LANES]
        nr = ar * hr - ai * hi + xr
        ni = ar * hi + ai * hr + xi
        xs_ref[pl.ds(row, nb), 0:S5_LANES] = nr
        xs_ref[pl.ds(row, nb), S5_LANES:2 * S5_LANES] = ni
        return nr, ni

    hr, hi = lax.fori_loop(0, tc, step, (hs_ref[:, 0:S5_LANES], hs_ref[:, S5_LANES:2 * S5_LANES]),
                           unroll=True if tc <= 8 else 4)
    hs_ref[:, 0:S5_LANES] = hr
    hs_ref[:, S5_LANES:2 * S5_LANES] = hi

    y = (jnp.dot(xs_ref[:, 0:S5_LANES].astype(BF16), cre_ref[...], preferred_element_type=F32)
         - jnp.dot(xs_ref[:, S5_LANES:2 * S5_LANES].astype(BF16), cim_ref[...], preferred_element_type=F32))
    y = y + d_ref[...] * u
    y = _gelu_tanh(y)
    y = y * jax.nn.sigmoid(jnp.dot(y.astype(BF16), wglu_ref[...], preferred_element_type=F32) + bglu_ref[...])
    o_ref[...] = (_rms(y) * mg_ref[...]).astype(o_ref.dtype)

    @pl.when(i == pl.num_programs(0) - 1)
    def _():
        hT_ref[...] = hs_ref[...]


def s5_mixer(u, h0, bblk, cre, cim, ar, ai, d, wglu, bglu, mg, nb, tc):
    rows = u.shape[0]
    rc = nb * tc
    full = lambda shape: pl.BlockSpec(shape, lambda i: (0,) * len(shape))
    return pl.pallas_call(
        functools.partial(_s5_kernel, nb=nb, tc=tc),
        out_shape=[jax.ShapeDtypeStruct((rows, S5_WIDTH), BF16),
                   jax.ShapeDtypeStruct((nb, 2 * S5_LANES), F32)],
        grid=(rows // rc,),
        in_specs=[pl.BlockSpec((rc, S5_WIDTH), lambda i: (i, 0)),
                  full((nb, 2 * S5_LANES)),
                  full((S5_WIDTH, 2 * S5_LANES)),
                  full((S5_LANES, S5_WIDTH)), full((S5_LANES, S5_WIDTH)),
                  full((1, S5_LANES)), full((1, S5_LANES)),
                  full((1, S5_WIDTH)), full((S5_WIDTH, S5_WIDTH)), full((1, S5_WIDTH)), full((1, S5_WIDTH))],
        out_specs=[pl.BlockSpec((rc, S5_WIDTH), lambda i: (i, 0)),
                   full((nb, 2 * S5_LANES))],
        scratch_shapes=[pltpu.VMEM((rc, 2 * S5_LANES), F32), pltpu.VMEM((nb, 2 * S5_LANES), F32)],
        compiler_params=_cparams(("arbitrary",)),
        name="s5_mixer",
    )(u, h0, bblk, cre, cim, ar, ai, d, wglu, bglu, mg)


CONV_ROWS = 64


def _conv_kernel(ag_ref, c0_ref, wdw_ref, bdw_ref, lng_ref, lnb_ref, wpw_ref, bpw_ref, mg_ref,
                 o_ref, buf_ref, zc_ref, y_ref, *, nb, tc):
    i = pl.program_id(0)
    hist = CONV_HIST * nb
    rc = nb * tc

    @pl.when(i == 0)
    def _():
        zc_ref[0:hist, :] = c0_ref[...]

    @pl.when(i > 0)
    def _():
        zc_ref[0:hist, :] = zc_ref[rc:rc + hist, :]

    a = ag_ref[:, 0:CONV_DIM]
    g = ag_ref[:, CONV_DIM:2 * CONV_DIM]
    zc_ref[hist:hist + rc, :] = a * jax.nn.sigmoid(g)

    w = wdw_ref[...]

    def tile(j, carry):
        r0 = pl.multiple_of(j * CONV_ROWS, CONV_ROWS)
        acc = jnp.zeros((CONV_ROWS, CONV_DIM), F32)
        for k in range(CONV_WIDTH):
            acc = acc + w[k:k + 1, :] * zc_ref[pl.ds(r0 + k * nb, CONV_ROWS), :]
        y_ref[pl.ds(r0, CONV_ROWS), :] = acc
        return carry

    lax.fori_loop(0, rc // CONV_ROWS, tile, 0)
    y = y_ref[...] + bdw_ref[...]
    y = _silu(_layernorm(y, lng_ref[...], lnb_ref[...]))
    y = jnp.dot(y.astype(BF16), wpw_ref[...], preferred_element_type=F32) + bpw_ref[...]
    o_ref[...] = (_rms(y) * mg_ref[...]).astype(o_ref.dtype)

    @pl.when(i == pl.num_programs(0) - 1)
    def _():
        buf_ref[...] = zc_ref[rc:rc + hist, :]


def conv_mixer(ag, c0, wdw, bdw, lng, lnb, wpw, bpw, mg, nb, tc):
    rows = ag.shape[0]
    rc = nb * tc
    hist = CONV_HIST * nb
    assert rows == rc or tc >= CONV_HIST
    full = lambda shape: pl.BlockSpec(shape, lambda i: (0,) * len(shape))
    return pl.pallas_call(
        functools.partial(_conv_kernel, nb=nb, tc=tc),
        out_shape=[jax.ShapeDtypeStruct((rows, CONV_DIM), BF16),
                   jax.ShapeDtypeStruct((hist, CONV_DIM), F32)],
        grid=(rows // rc,),
        in_specs=[pl.BlockSpec((rc, 2 * CONV_DIM), lambda i: (i, 0)),
                  full((hist, CONV_DIM)), full((CONV_WIDTH, CONV_DIM)),
                  full((1, CONV_DIM)), full((1, CONV_DIM)), full((1, CONV_DIM)),
                  full((CONV_DIM, CONV_DIM)), full((1, CONV_DIM)), full((1, CONV_DIM))],
        out_specs=[pl.BlockSpec((rc, CONV_DIM), lambda i: (i, 0)), full((hist, CONV_DIM))],
        scratch_shapes=[pltpu.VMEM((hist + rc, CONV_DIM), F32), pltpu.VMEM((rc, CONV_DIM), F32)],
        compiler_params=_cparams(("arbitrary",)),
        name="conv_mixer",
    )(ag, c0, wdw, bdw, lng, lnb, wpw, bpw, mg)


def _gmlp_seq_kernel(uv_ref, lng_ref, lnb_ref, wcat_ref, bias_ref, mg_ref, o_ref, *, tt):
    n_chunks = tt // GMLP_CHUNK
    kc = GMLP_HEADS * GMLP_CHUNK
    rowi = lax.broadcasted_iota(jnp.int32, (GMLP_CHUNK, kc), 0)
    coli = lax.broadcasted_iota(jnp.int32, (GMLP_CHUNK, kc), 1)
    wcat = jnp.where((coli & (GMLP_CHUNK - 1)) <= rowi, wcat_ref[...], 0.0).astype(BF16)
    sel = _same_block((kc, GMLP_WIDTH), GMLP_CHUNK, GMLP_HEAD_DIM)
    for c in range(n_chunks):
        rows = slice(c * GMLP_CHUNK, (c + 1) * GMLP_CHUNK)
        u = uv_ref[rows, 0:GMLP_WIDTH]
        v = uv_ref[rows, GMLP_WIDTH:2 * GMLP_WIDTH]
        vn = _layernorm(v, lng_ref[...], lnb_ref[...])
        vbd = jnp.where(sel, jnp.concatenate([vn] * GMLP_HEADS, axis=0), 0.0).astype(BF16)
        mixed = jnp.dot(wcat, vbd, preferred_element_type=F32) + bias_ref[...]
        o_ref[rows, :] = (_rms(u * mixed) * mg_ref[...]).astype(o_ref.dtype)


def gmlp_seq(uv, nseq, lng, lnb, wcat, bias, mg, tt):
    t = uv.shape[0]
    full = lambda shape: pl.BlockSpec(shape, lambda b_, i: (0,) * len(shape))
    return pl.pallas_call(
        functools.partial(_gmlp_seq_kernel, tt=tt),
        out_shape=jax.ShapeDtypeStruct((t, nseq * GMLP_WIDTH), BF16),
        grid=(nseq, t // tt),
        in_specs=[pl.BlockSpec((tt, 2 * GMLP_WIDTH), lambda b_, i: (i, b_)),
                  full((1, GMLP_WIDTH)), full((1, GMLP_WIDTH)),
                  full((GMLP_CHUNK, GMLP_HEADS * GMLP_CHUNK)), full((GMLP_CHUNK, GMLP_WIDTH)),
                  full((1, GMLP_WIDTH))],
        out_specs=pl.BlockSpec((tt, GMLP_WIDTH), lambda b_, i: (i, b_)),
        compiler_params=_cparams(("arbitrary", "arbitrary")),
        name="gmlp_seq",
    )(uv, lng, lnb, wcat, bias, mg)


def _gmlp_short_kernel(uv_ref, lng_ref, lnb_ref, wrow_ref, brow_ref, mg_ref, o_ref, vn_ref, *, nb, t_len):
    u = uv_ref[:, 0:GMLP_WIDTH]
    v = uv_ref[:, GMLP_WIDTH:2 * GMLP_WIDTH]
    vn = _layernorm(v, lng_ref[...], lnb_ref[...])
    vn_ref[...] = vn
    wrow = wrow_ref[...]
    brow = brow_ref[...]
    for t in range(t_len):
        mixed = jnp.zeros((nb, GMLP_WIDTH), F32) + brow[t:t + 1, :]
        for j in range(t + 1):
            mixed = mixed + wrow[t * t_len + j:t * t_len + j + 1, :] * vn[j * nb:(j + 1) * nb, :]
        o = u[t * nb:(t + 1) * nb, :] * mixed
        o_ref[t * nb:(t + 1) * nb, :] = (_rms(o) * mg_ref[...]).astype(o_ref.dtype)


def gmlp_short(uv, lng, lnb, wrow, brow, mg, nb, t_len):
    rows = uv.shape[0]
    return pl.pallas_call(
        functools.partial(_gmlp_short_kernel, nb=nb, t_len=t_len),
        out_shape=[jax.ShapeDtypeStruct((rows, GMLP_WIDTH), BF16),
                   jax.ShapeDtypeStruct((rows, GMLP_WIDTH), F32)],
        compiler_params=pltpu.CompilerParams(vmem_limit_bytes=VMEM_LIMIT),
        name="gmlp_short",
    )(uv, lng, lnb, wrow, brow, mg)


def _gla_tail(o, r, gmean, onorm, mg):
    ms = jnp.dot(o * o, gmean, preferred_element_type=F32, precision=HI)
    o = o * lax.rsqrt(ms + EPS) * onorm
    o = o * _silu(r)
    return _rms(o) * mg


def _head_mean_matrix():
    return jnp.where(_same_block((GLA_WIDTH, GLA_WIDTH), GLA_DV, GLA_DV), 1.0 / GLA_DV, 0.0).astype(F32)


def _gla_seq_kernel(x_ref, s0_ref, wg_ref, bg_ref, onorm_ref, mg_ref, o_ref, sT_ref, s_ref, *, tt):
    i = pl.program_id(1)
    L = GLA_CHUNK
    kw, vw = GLA_KEY_WIDTH, GLA_WIDTH

    @pl.when(i == 0)
    def _():
        s_ref[...] = s0_ref[0]

    tri = (lax.broadcasted_iota(jnp.int32, (L, L), 1) <= lax.broadcasted_iota(jnp.int32, (L, L), 0)).astype(F32)
    kbd_sel = _same_block((GLA_HEADS * L, kw), L, GLA_DK)
    vbd_sel = _same_block((GLA_HEADS * L, vw), L, GLA_DV)
    causal = ((lax.broadcasted_iota(jnp.int32, (L, GLA_HEADS * L), 1) & (L - 1))
              <= lax.broadcasted_iota(jnp.int32, (L, GLA_HEADS * L), 0))
    s_sel = _same_block((kw, vw), GLA_DK, GLA_DV)
    eye = (lax.broadcasted_iota(jnp.int32, (kw, kw), 0) == lax.broadcasted_iota(jnp.int32, (kw, kw), 1))
    gmean = _head_mean_matrix()
    scale = GLA_DK ** -0.5

    def chunk(c, carry):
        r0 = pl.multiple_of(c * L, L)
        q = x_ref[pl.ds(r0, L), 0:kw] * scale
        k = x_ref[pl.ds(r0, L), kw:2 * kw]
        v = x_ref[pl.ds(r0, L), 2 * kw:2 * kw + vw]
        r = x_ref[pl.ds(r0, L), 2 * kw + vw:2 * kw + 2 * vw]
        gl = x_ref[pl.ds(r0, L), 2 * kw + 2 * vw:2 * kw + 2 * vw + LANE]
        la = _log_sigmoid(jnp.dot(gl.astype(BF16), wg_ref[...], preferred_element_type=F32) + bg_ref[...])
        la = la / GLA_TAU
        bc = jnp.dot(tri, la, preferred_element_type=F32, precision=HI)
        b_last = bc[L - 1:L, :]
        qt = (q * jnp.exp(bc)).astype(BF16)
        kt = k * jnp.exp(-bc)
        kbd = jnp.where(kbd_sel, jnp.concatenate([kt] * GLA_HEADS, axis=0), 0.0).astype(BF16)
        att = lax.dot_general(qt, kbd, (((1,), (1,)), ((), ())), preferred_element_type=F32)
        att = jnp.where(causal, att, 0.0).astype(BF16)
        vb = v.astype(BF16)
        vbd = jnp.where(vbd_sel, jnp.concatenate([vb] * GLA_HEADS, axis=0), jnp.zeros((), BF16))
        s = s_ref[...]
        o = (jnp.dot(att, vbd, preferred_element_type=F32)
             + jnp.dot(qt, s.astype(BF16), preferred_element_type=F32))
        kdec = (k * jnp.exp(b_last - bc)).astype(BF16)
        upd = lax.dot_general(kdec, vb, (((0,), (0,)), ((), ())), preferred_element_type=F32)
        dmat = jnp.where(eye, jnp.broadcast_to(jnp.exp(b_last), (kw, kw)), 0.0)
        s_ref[...] = (jnp.dot(dmat, s, preferred_element_type=F32, precision=HI)
                      + jnp.where(s_sel, upd, 0.0))
        o_ref[pl.ds(r0, L), :] = _gla_tail(o, r, gmean, onorm_ref[...], mg_ref[...]).astype(o_ref.dtype)
        return carry

    lax.fori_loop(0, tt // L, chunk, 0)

    @pl.when(i == pl.num_programs(1) - 1)
    def _():
        sT_ref[0] = s_ref[...]


def gla_seq(x, nseq, s0, wg, bg, onorm, mg, tt):
    t = x.shape[0]
    full = lambda shape: pl.BlockSpec(shape, lambda b_, i: (0,) * len(shape))
    return pl.pallas_call(
        functools.partial(_gla_seq_kernel, tt=tt),
        out_shape=[jax.ShapeDtypeStruct((t, nseq * GLA_WIDTH), BF16),
                   jax.ShapeDtypeStruct((nseq, GLA_KEY_WIDTH, GLA_WIDTH), F32)],
        grid=(nseq, t // tt),
        in_specs=[pl.BlockSpec((tt, PW_GLA), lambda b_, i: (i, b_)),
                  pl.BlockSpec((1, GLA_KEY_WIDTH, GLA_WIDTH), lambda b_, i: (b_, 0, 0)),
                  full((LANE, GLA_KEY_WIDTH)), full((1, GLA_KEY_WIDTH)),
                  full((1, GLA_WIDTH)), full((1, GLA_WIDTH))],
        out_specs=[pl.BlockSpec((tt, GLA_WIDTH), lambda b_, i: (i, b_)),
                   pl.BlockSpec((1, GLA_KEY_WIDTH, GLA_WIDTH), lambda b_, i: (b_, 0, 0))],
        scratch_shapes=[pltpu.VMEM((GLA_KEY_WIDTH, GLA_WIDTH), F32)],
        compiler_params=_cparams(("arbitrary", "arbitrary")),
        name="gla_seq",
    )(x, s0, wg, bg, onorm, mg)


def _split3(x):
    a = x.astype(BF16)
    r1 = x - a.astype(F32)
    b = r1.astype(BF16)
    c = (r1 - b.astype(F32)).astype(BF16)
    return a, b, c


def _gla_rec_kernel(x_ref, s0_ref, ek_ref, ev_ref, wg_ref, bg_ref, onorm_ref, mg_ref, o_ref, sT_ref,
                    *, nb, t_len):
    kw, vw = GLA_KEY_WIDTH, GLA_WIDTH
    hl = GLA_DK * GLA_DV
    sT_ref[...] = s0_ref[...]
    gmean = _head_mean_matrix()
    scale = GLA_DK ** -0.5

    def step(t, carry):
        r0 = pl.multiple_of(t * nb, nb)
        q = x_ref[pl.ds(r0, nb), 0:kw] * scale
        k = x_ref[pl.ds(r0, nb), kw:2 * kw]
        v = x_ref[pl.ds(r0, nb), 2 * kw:2 * kw + vw]
        r = x_ref[pl.ds(r0, nb), 2 * kw + vw:2 * kw + 2 * vw]
        gl = x_ref[pl.ds(r0, nb), 2 * kw + 2 * vw:2 * kw + 2 * vw + LANE]
        la = _log_sigmoid(jnp.dot(gl.astype(BF16), wg_ref[...], preferred_element_type=F32) + bg_ref[...])
        a = jnp.exp(la / GLA_TAU)
        a3 = _split3(a)
        qb = q.astype(BF16)
        kb = k.astype(BF16)
        vb = v.astype(BF16)
        outs = []
        for h in range(GLA_HEADS):
            lanes = slice(h * hl, (h + 1) * hl)
            ek = ek_ref[:, lanes]
            a_e = (jnp.dot(a3[0], ek, preferred_element_type=F32)
                   + jnp.dot(a3[1], ek, preferred_element_type=F32)
                   + jnp.dot(a3[2], ek, preferred_element_type=F32))
            k_e = jnp.dot(kb, ek, preferred_element_type=F32)
            q_e = jnp.dot(qb, ek, preferred_element_type=F32)
            v_e = jnp.dot(vb, ev_ref[:, lanes], preferred_element_type=F32)
            s_new = a_e * sT_ref[:, lanes] + k_e * v_e
            sT_ref[:, lanes] = s_new
            prod = q_e * s_new
            acc = prod[:, 0:LANE]
            for j in range(1, hl // LANE):
                acc = acc + prod[:, j * LANE:(j + 1) * LANE]
            outs.append(acc[:, 0:GLA_DV] + acc[:, GLA_DV:2 * GLA_DV])
        o = jnp.concatenate(outs, axis=1)
        o_ref[pl.ds(r0, nb), :] = _gla_tail(o, r, gmean, onorm_ref[...], mg_ref[...]).astype(o_ref.dtype)
        return carry

    lax.fori_loop(0, t_len, step, 0)


def gla_recurrent(x, s0, ek, ev, wg, bg, onorm, mg, nb, t_len):
    rows = x.shape[0]
    return pl.pallas_call(
        functools.partial(_gla_rec_kernel, nb=nb, t_len=t_len),
        out_shape=[jax.ShapeDtypeStruct((rows, GLA_WIDTH), BF16),
                   jax.ShapeDtypeStruct((nb, GLA_STATE_LANES), F32)],
        compiler_params=pltpu.CompilerParams(vmem_limit_bytes=VMEM_LIMIT),
        name="gla_recurrent",
    )(x, s0, ek, ev, wg, bg, onorm, mg)


def _gla_expanders():
    lane = jnp.arange(GLA_STATE_LANES)
    h = lane // (GLA_DK * GLA_DV)
    dk = (lane // GLA_DV) % GLA_DK
    dv = lane % GLA_DV
    ek = (jnp.arange(GLA_KEY_WIDTH)[:, None] == (h * GLA_DK + dk)[None, :]).astype(BF16)
    ev = (jnp.arange(GLA_WIDTH)[:, None] == (h * GLA_DV + dv)[None, :]).astype(BF16)
    return ek, ev


def _mix_residual(x_ref, m_refs, g1_ref, wout_ref):
    mix = jnp.concatenate([m[...] for m in m_refs], axis=1)
    proj = jnp.dot(mix, wout_ref[...], preferred_element_type=F32)
    return x_ref[0] + _gate(proj, g1_ref[0])


def _post_dense_kernel(x_ref, m0_ref, m1_ref, m2_ref, m3_ref, g1_ref, sh2_ref, sc2_ref, g2_ref, ng_ref, wout_ref,
                       wg_ref, wu_ref, wd_ref, fg_ref, o_ref, h2_ref, acc_ref, *, final_norm):
    c = pl.program_id(2)

    @pl.when(c == 0)
    def _():
        x1 = _mix_residual(x_ref, (m0_ref, m1_ref, m2_ref, m3_ref), g1_ref, wout_ref)
        o_ref[0] = x1
        h = _modulate(_rms(x1) * ng_ref[...], sc2_ref[0], sh2_ref[0])
        h2_ref[...] = h.astype(BF16)
        acc_ref[...] = jnp.zeros_like(acc_ref)

    h2 = h2_ref[...]
    t = (_silu(jnp.dot(h2, wg_ref[...], preferred_element_type=F32))
         * jnp.dot(h2, wu_ref[...], preferred_element_type=F32))
    acc_ref[...] += jnp.dot(t.astype(BF16), wd_ref[...], preferred_element_type=F32)

    @pl.when(c == pl.num_programs(2) - 1)
    def _():
        x2 = o_ref[0] + _gate(acc_ref[...], g2_ref[0])
        if final_norm:
            x2 = _rms(x2) * fg_ref[...]
        o_ref[0] = x2


def post_dense(x, mixes, g1, sh2, sc2, g2, ng, wout, wg, wu, wd, fg, tm, tf, final_norm):
    s, r, d = x.shape
    rm = g1.shape[1]
    ff = wg.shape[1]
    mod = pl.BlockSpec((1, rm, d), lambda b_, i, c: (b_, 0, 0))
    const = lambda shape: pl.BlockSpec(shape, lambda b_, i, c: (0,) * len(shape))
    mixspec = pl.BlockSpec((tm, 256), lambda b_, i, c: (i, b_))
    return pl.pallas_call(
        functools.partial(_post_dense_kernel, final_norm=final_norm),
        out_shape=jax.ShapeDtypeStruct((s, r, d), F32),
        grid=(s, r // tm, ff // tf),
        in_specs=[pl.BlockSpec((1, tm, d), lambda b_, i, c: (b_, i, 0)),
                  mixspec, mixspec, mixspec, mixspec,
                  mod, mod, mod, mod,
                  const((1, d)), const((d, d)),
                  pl.BlockSpec((d, tf), lambda b_, i, c: (0, c)),
                  pl.BlockSpec((d, tf), lambda b_, i, c: (0, c)),
                  pl.BlockSpec((tf, d), lambda b_, i, c: (c, 0)),
                  const((1, d))],
        out_specs=pl.BlockSpec((1, tm, d), lambda b_, i, c: (b_, i, 0)),
        scratch_shapes=[pltpu.VMEM((tm, d), BF16), pltpu.VMEM((tm, d), F32)],
        compiler_params=_cparams(("arbitrary", "arbitrary", "arbitrary")),
        name="post_dense",
    )(x, *mixes, g1, sh2, sc2, g2, ng, wout, wg, wu, wd, fg)


def _route_kernel(x_ref, m0_ref, m1_ref, m2_ref, m3_ref, g1_ref, sh2_ref, sc2_ref, ng_ref, wout_ref, router_ref,
                  *rest):
    x1_ref, h2_ref, route_ref = rest[-3:]
    x1 = _mix_residual(x_ref, (m0_ref, m1_ref, m2_ref, m3_ref), g1_ref, wout_ref)
    x1_ref[0] = x1
    h = _modulate(_rms(x1) * ng_ref[...], sc2_ref[0], sh2_ref[0])
    h2_ref[...] = h
    logits = jnp.dot(h, router_ref[...], preferred_element_type=F32, precision=HI)
    lane = lax.broadcasted_iota(jnp.int32, logits.shape, 1).astype(F32)
    neg = jnp.float32(-jnp.inf)
    logits = jnp.where(lane < N_EXPERTS, logits, neg)
    m1 = jnp.max(logits, axis=1, keepdims=True)
    i1 = jnp.min(jnp.where(logits == m1, lane, float(LANE)), axis=1, keepdims=True)
    others = jnp.where(lane == i1, neg, logits)
    m2 = jnp.max(others, axis=1, keepdims=True)
    i2 = jnp.min(jnp.where(others == m2, lane, float(LANE)), axis=1, keepdims=True)
    e2 = jnp.exp(m2 - m1)
    den = 1.0 + e2
    route_ref[...] = (jnp.where(lane == 0.0, i1, 0.0) + jnp.where(lane == 1.0, i2, 0.0)
                      + jnp.where(lane == 2.0, 1.0 / den, 0.0) + jnp.where(lane == 3.0, e2 / den, 0.0))


def moe_route(x, mixes, g1, sh2, sc2, ng, wout, router, tm, row0, shared):
    s, r, d = x.shape
    rm = g1.shape[1]
    nt = r // tm
    blk0 = row0 // tm
    n_total = shared[0].shape[0]
    mod = pl.BlockSpec((1, rm, d), lambda b_, i: (b_, 0, 0))
    const = lambda shape: pl.BlockSpec(shape, lambda b_, i: (0,) * len(shape))
    mixspec = pl.BlockSpec((tm, 256), lambda b_, i: (i, b_))
    in_specs = [pl.BlockSpec((1, tm, d), lambda b_, i: (b_, i, 0)),
                mixspec, mixspec, mixspec, mixspec, mod, mod, mod,
                const((1, d)), const((d, d)), const((d, LANE))]
    args = [x, *mixes, g1, sh2, sc2, ng, wout, router]
    in_specs += [pl.BlockSpec(memory_space=pl.ANY), pl.BlockSpec(memory_space=pl.ANY)]
    aliases = {len(args): 1, len(args) + 1: 2}
    args += list(shared)
    return pl.pallas_call(
        _route_kernel,
        out_shape=[jax.ShapeDtypeStruct((s, r, d), F32),
                   jax.ShapeDtypeStruct((n_total, d), F32),
                   jax.ShapeDtypeStruct((n_total, LANE), F32)],
        grid=(s, nt),
        in_specs=in_specs,
        out_specs=[pl.BlockSpec((1, tm, d), lambda b_, i: (b_, i, 0)),
                   pl.BlockSpec((tm, d), lambda b_, i: (blk0 + b_ * nt + i, 0)),
                   pl.BlockSpec((tm, LANE), lambda b_, i: (blk0 + b_ * nt + i, 0))],
        input_output_aliases=aliases,
        compiler_params=_cparams(("arbitrary", "arbitrary")),
        name="moe_route",
    )(*args)


def _route_tables(route, tg, n_tiles):
    n_total = route.shape[0]
    flat_e = route[:, 0:2].astype(jnp.int32).reshape(-1)
    order = jnp.argsort(flat_e, stable=True).astype(jnp.int32)
    counts = jnp.sum(flat_e[:, None] == jnp.arange(N_EXPERTS, dtype=jnp.int32)[None, :], axis=0).astype(jnp.int32)
    tiles_per = (counts + tg - 1) // tg
    tile_end = jnp.cumsum(tiles_per)
    n_used = tile_end[-1]
    tile_id = jnp.arange(n_tiles, dtype=jnp.int32)
    tile_ok = tile_id < n_used
    tile_e = jnp.searchsorted(tile_end, jnp.minimum(tile_id, n_used - 1), side='right').astype(jnp.int32)
    pad_start = (tile_end - tiles_per) * tg
    sort_start = jnp.cumsum(counts) - counts
    slot = jnp.arange(n_tiles * tg, dtype=jnp.int32)
    e_s = tile_e[slot // tg]
    k = slot - pad_start[e_s]
    valid = tile_ok[slot // tg] & (k < counts[e_s])
    a = order[jnp.clip(sort_start[e_s] + k, 0, 2 * n_total - 1)]
    src = jnp.where(valid, a >> 1, 0)
    dst = jnp.where(valid, (a & 1) * n_total + (a >> 1), 0)
    first_tile = tile_end - tiles_per
    n_valid = jnp.clip(counts[tile_e] - (tile_id - first_tile[tile_e]) * tg, 0, tg)
    n_valid = jnp.where(tile_ok, n_valid, 0).astype(jnp.int32)
    return tile_e, n_valid, src, dst


def _experts_kernel(te_ref, nv_ref, src_ref, dst_ref, h2_hbm, wg_ref, wu_ref, wd_ref, out_hbm,
                    xbuf, obuf, gsem, ssem, *, tg, n_tiles, tf):
    j = pl.program_id(0)
    slot = lax.rem(j, 2)
    other = 1 - slot
    ok = nv_ref[j] > 0

    def gather(tile, s_):
        base = tile * tg

        def body(r, c):
            tok = src_ref[base + r]
            pltpu.make_async_copy(h2_hbm.at[pl.ds(tok, 1), :], xbuf.at[s_, pl.ds(r, 1), :], gsem.at[s_]).start()
            return c

        lax.fori_loop(0, tg, body, 0, unroll=8)

    def scatter_row(tile, s_):
        base = tile * tg

        def body(r, c):
            row = dst_ref[base + r]
            pltpu.make_async_copy(obuf.at[s_, pl.ds(r, 1), :], out_hbm.at[pl.ds(row, 1), :], ssem.at[s_]).start()
            return c

        return body

    def scatter(tile, s_):
        n = nv_ref[tile]

        @pl.when(n == tg)
        def _():
            lax.fori_loop(0, tg, scatter_row(tile, s_), 0, unroll=8)

        @pl.when(n < tg)
        def _():
            lax.fori_loop(0, n, scatter_row(tile, s_), 0)

    def wait_all(buf, sem, s_):
        pltpu.make_async_copy(buf.at[s_], buf.at[s_], sem.at[s_]).wait()

    def wait_scatter(tile, s_):
        n = nv_ref[tile]

        @pl.when(n == tg)
        def _():
            wait_all(obuf, ssem, s_)

        @pl.when(n < tg)
        def _():
            def body(r, c):
                pltpu.make_async_copy(obuf.at[s_, pl.ds(0, 1), :], out_hbm.at[pl.ds(0, 1), :], ssem.at[s_]).wait()
                return c

            lax.fori_loop(0, n, body, 0)

    @pl.when(jnp.logical_and(j == 0, ok))
    def _():
        gather(0, 0)

    @pl.when(ok)
    def _():
        wait_all(xbuf, gsem, slot)

    nxt = jnp.minimum(j + 1, n_tiles - 1)

    @pl.when(jnp.logical_and(j + 1 < n_tiles, nv_ref[nxt] > 0))
    def _():
        gather(j + 1, other)

    @pl.when(j >= 2)
    def _():
        wait_scatter(j - 2, slot)

    @pl.when(ok)
    def _():
        x = xbuf[slot].astype(BF16)
        acc = jnp.zeros((tg, x.shape[1]), F32)
        for c in range(wg_ref.shape[2] // tf):
            cols = slice(c * tf, (c + 1) * tf)
            t = (_silu(jnp.dot(x, wg_ref[0, :, cols], preferred_element_type=F32))
                 * jnp.dot(x, wu_ref[0, :, cols], preferred_element_type=F32))
            acc = acc + jnp.dot(t.astype(BF16), wd_ref[0, cols, :], preferred_element_type=F32)
        obuf[slot] = acc
        scatter(j, slot)

    @pl.when(j == n_tiles - 1)
    def _():
        wait_scatter(j - 1, other)
        wait_scatter(j, slot)


def moe_experts(h2, tables, wg, wu, wd, tg, n_tiles, tf):
    n_exp, d, ff = wg.shape
    n_total = h2.shape[0]
    assert n_tiles >= 2
    tile_e, n_valid, src, dst = tables
    grid_spec = pltpu.PrefetchScalarGridSpec(
        num_scalar_prefetch=4,
        grid=(n_tiles,),
        in_specs=[pl.BlockSpec(memory_space=pl.ANY),
                  pl.BlockSpec((1, d, ff), lambda j, te, ok, s_, d_: (te[j], 0, 0)),
                  pl.BlockSpec((1, d, ff), lambda j, te, ok, s_, d_: (te[j], 0, 0)),
                  pl.BlockSpec((1, ff, d), lambda j, te, ok, s_, d_: (te[j], 0, 0))],
        out_specs=pl.BlockSpec(memory_space=pl.ANY),
        scratch_shapes=[pltpu.VMEM((2, tg, d), F32), pltpu.VMEM((2, tg, d), F32),
                        pltpu.SemaphoreType.DMA((2,)), pltpu.SemaphoreType.DMA((2,))])
    return pl.pallas_call(
        functools.partial(_experts_kernel, tg=tg, n_tiles=n_tiles, tf=tf),
        out_shape=jax.ShapeDtypeStruct((2 * n_total, d), F32),
        grid_spec=grid_spec,
        compiler_params=_cparams(("arbitrary",)),
        name="moe_experts",
    )(tile_e, n_valid, src, dst, h2, wg, wu, wd)


def _combine_kernel(x1_ref, y0_ref, y1_ref, route_ref, g2_ref, fg_ref, o_ref, *, final_norm):
    r = route_ref[...]
    f = r[:, 2:3] * y0_ref[0] + r[:, 3:4] * y1_ref[0]
    x2 = x1_ref[0] + _gate(f, g2_ref[0])
    if final_norm:
        x2 = _rms(x2) * fg_ref[...]
    o_ref[0] = x2


def moe_combine(x1, y, route, g2, fg, tm, row0, final_norm):
    s, r, d = x1.shape
    rm = g2.shape[1]
    nt = r // tm
    blk0 = row0 // tm
    return pl.pallas_call(
        functools.partial(_combine_kernel, final_norm=final_norm),
        out_shape=jax.ShapeDtypeStruct((s, r, d), F32),
        grid=(s, nt),
        in_specs=[pl.BlockSpec((1, tm, d), lambda b_, i: (b_, i, 0)),
                  pl.BlockSpec((1, tm, d), lambda b_, i: (0, blk0 + b_ * nt + i, 0)),
                  pl.BlockSpec((1, tm, d), lambda b_, i: (1, blk0 + b_ * nt + i, 0)),
                  pl.BlockSpec((tm, LANE), lambda b_, i: (blk0 + b_ * nt + i, 0)),
                  pl.BlockSpec((1, rm, d), lambda b_, i: (b_, 0, 0)),
                  pl.BlockSpec((1, d), lambda b_, i: (0, 0))],
        out_specs=pl.BlockSpec((1, tm, d), lambda b_, i: (b_, i, 0)),
        compiler_params=_cparams(("arbitrary", "arbitrary")),
        name="moe_combine",
    )(x1, y, y, route, g2, fg)


def _reorder_w_in(w_in, b_in):
    cut = PW_S5 + 128 + 128 + 256 + 256 + GLA_GATE_RANK
    pad = LANE - GLA_GATE_RANK
    w = jnp.concatenate([w_in[:, :cut], jnp.zeros((w_in.shape[0], pad), w_in.dtype), w_in[:, cut:]], axis=1)
    b = jnp.concatenate([b_in[:cut], jnp.zeros((pad,), b_in.dtype), b_in[cut:]])
    return w.astype(BF16), b.reshape(1, PW_TOTAL)


def _row(a):
    return a.reshape(1, -1)


class _Branch:
    def __init__(self, nseq, nb, t_len, seq_form, tm_pre, tm_post, tc, tt_seq, row0):
        self.nseq, self.nb, self.t_len, self.seq_form = nseq, nb, t_len, seq_form
        self.tm_pre, self.tm_post, self.tc, self.tt_seq, self.row0 = tm_pre, tm_post, tc, tt_seq, row0


def _mixers(x, mods, states, W, i, br, out):
    nseq, nb, t_len, seq_form = br.nseq, br.nb, br.t_len, br.seq_form
    s5_re0, s5_im0, gla0, conv0 = states
    new_re, new_im, new_gla, new_conv, new_v = out
    row = _row
    if True:
        sh1, sc1 = mods[0], mods[1]
        tm_pre, tc_s5, tc_conv, tt_seq = br.tm_pre, br.tc, br.tc, br.tt_seq
        w_in, b_in = _reorder_w_in(W['w_in'][i], W['b_in'][i])
        p_s5, p_gla, p_conv, p_mlp = pre_mixer(x, sh1, sc1, row(W['norm_mix_g'][i]), w_in, b_in, tm_pre)
        tmaj = lambda a, w_: a.reshape(t_len * nb, w_)
        mg = W['merge_g'][i]

        ab_re, ab_im, bb_re, bb_im = s5_discretise(W['s5_a_re'][i], W['s5_a_im'][i], W['s5_log_dt'][i],
                                                   W['s5_b_re'][i], W['s5_b_im'][i])
        bblk = jnp.concatenate([_block_diag_in(bb_re), _block_diag_in(bb_im)], axis=1).astype(BF16)
        cre = _block_diag_out(W['s5_c_re'][i]).astype(BF16)
        cim = _block_diag_out(W['s5_c_im'][i]).astype(BF16)
        h0 = jnp.concatenate([s5_re0[i].reshape(nb, S5_LANES), s5_im0[i].reshape(nb, S5_LANES)], axis=1)
        o_s5, h_t = s5_mixer(tmaj(p_s5, PW_S5), h0, bblk, cre, cim, row(ab_re), row(ab_im), row(W['s5_d'][i]),
                             W['s5_w_glu'][i].astype(BF16), row(W['s5_b_glu'][i]), row(mg[0:256]), nb, tc_s5)
        new_re.append(h_t[:, :S5_LANES].reshape(nb, S5_GROUPS, S5_STATE))
        new_im.append(h_t[:, S5_LANES:].reshape(nb, S5_GROUPS, S5_STATE))

        wg2 = jnp.zeros((LANE, GLA_KEY_WIDTH), F32).at[:GLA_GATE_RANK].set(W['gla_w_gate2'][i]).astype(BF16)
        bg2 = row(W['gla_b_gate2'][i])
        onorm = row(W['gla_onorm_g'][i])
        if seq_form:
            eye = jnp.eye(GLA_HEADS, dtype=F32)
            s0 = jnp.einsum('bhkv,hg->bhkgv', gla0[i], eye).reshape(nseq, GLA_KEY_WIDTH, GLA_WIDTH)
            o_gla, s_t = gla_seq(p_gla, nseq, s0, wg2, bg2, onorm, row(mg[256:512]), tt_seq)
            s4 = s_t.reshape(nseq, GLA_HEADS, GLA_DK, GLA_HEADS, GLA_DV)
            new_gla.append(jnp.stack([s4[:, h, :, h, :] for h in range(GLA_HEADS)], axis=1))
        else:
            ek, ev = _gla_expanders()
            o_gla, s_t = gla_recurrent(p_gla, gla0[i].reshape(nb, GLA_STATE_LANES), ek, ev, wg2, bg2, onorm,
                                       row(mg[256:512]), nb, t_len)
            new_gla.append(s_t.reshape(nb, GLA_HEADS, GLA_DK, GLA_DV))

        c0 = jnp.transpose(conv0[i], (1, 0, 2)).reshape(CONV_HIST * nb, CONV_DIM)
        o_conv, buf = conv_mixer(tmaj(p_conv, PW_CONV), c0, W['conv_w_dw'][i], row(W['conv_b_dw'][i]),
                                 row(W['conv_ln_g'][i]), row(W['conv_ln_b'][i]), W['conv_w_pw'][i].astype(BF16),
                                 row(W['conv_b_pw'][i]), row(mg[512:768]), nb, tc_conv)
        new_conv.append(jnp.transpose(buf.reshape(CONV_HIST, nb, CONV_DIM), (1, 0, 2)))

        ws = W['gmlp_w_s'][i]
        bs = W['gmlp_b_s'][i]
        lng, lnb = row(W['gmlp_ln_g'][i]), row(W['gmlp_ln_b'][i])
        if seq_form:
            wcat = jnp.transpose(ws, (1, 0, 2)).reshape(GMLP_CHUNK, GMLP_HEADS * GMLP_CHUNK)
            bias = jnp.repeat(bs.T, GMLP_HEAD_DIM, axis=1)
            o_mlp = gmlp_seq(p_mlp, nseq, lng, lnb, wcat, bias, row(mg[768:1024]), tt_seq)
            new_v.append(None)
        else:
            tri = jnp.tril(jnp.ones((t_len, t_len), F32))
            wrow = jnp.repeat(jnp.transpose(ws[:, :t_len, :t_len] * tri[None], (1, 2, 0)).reshape(t_len * t_len, GMLP_HEADS),
                              GMLP_HEAD_DIM, axis=1)
            brow = jnp.repeat(bs[:, :t_len].T, GMLP_HEAD_DIM, axis=1)
            o_mlp, vn = gmlp_short(p_mlp, lng, lnb, wrow, brow, row(mg[768:1024]), nb, t_len)
            new_v.append(vn)

        per_seq = lambda a: a.reshape(t_len, nb * 256) if seq_form else a
        return [per_seq(o_s5), per_seq(o_gla), per_seq(o_conv), per_seq(o_mlp)]


FF_TILE = 1408
EXPERT_ROWS = 512


def _channel_mixer(xs, mixes, mods, W, i, branches, last):
    ng, wout, fg = _row(W['norm_ffn_g'][i]), W['w_out'][i].astype(BF16), _row(W['final_norm_g'])
    j = i // 2
    if i % 2 == 0:
        wg, wu, wd = (W['ffn_w_gate'][j].astype(BF16), W['ffn_w_up'][j].astype(BF16),
                      W['ffn_w_down'][j].astype(BF16))
        return [post_dense(x, mx, m[2], m[3], m[4], m[5], ng, wout, wg, wu, wd, fg, br.tm_post, FF_TILE, last)
                for x, mx, m, br in zip(xs, mixes, mods, branches)]
    n_total = sum(x.shape[0] * x.shape[1] for x in xs)
    tg = EXPERT_ROWS
    n_tiles = 2 * n_total // tg + N_EXPERTS
    router = jnp.zeros((D_MODEL, LANE), F32).at[:, :N_EXPERTS].set(W['moe_router'][j])
    h2, route = jnp.zeros((n_total, D_MODEL), F32), jnp.zeros((n_total, LANE), F32)
    x1s = []
    for x, mx, m, br in zip(xs, mixes, mods, branches):
        x1, h2, route = moe_route(x, mx, m[2], m[3], m[4], ng, wout, router, br.tm_post, br.row0, (h2, route))
        x1s.append(x1)
    tables = _route_tables(route, tg, n_tiles)
    y = moe_experts(h2, tables, W['moe_w_gate'][j].astype(BF16), W['moe_w_up'][j].astype(BF16),
                    W['moe_w_down'][j].astype(BF16), tg, n_tiles, FF_TILE)
    y = y.reshape(2, n_total, D_MODEL)
    return [moe_combine(x1, y, route, m[5], fg, br.tm_post, br.row0, last)
            for x1, m, br in zip(x1s, mods, branches)]


def kernel(x_prompt, x_sample, c_prompt, c_sample, state_s5_re, state_s5_im, state_gla, cache_conv, ada_w, ada_b, norm_mix_g, norm_ffn_g, w_in, b_in, s5_a_re, s5_a_im, s5_log_dt, s5_b_re, s5_b_im, s5_c_re, s5_c_im, s5_d, s5_w_glu, s5_b_glu, gla_w_gate2, gla_b_gate2, gla_onorm_g, conv_w_dw, conv_b_dw, conv_ln_g, conv_ln_b, conv_w_pw, conv_b_pw, gmlp_ln_g, gmlp_ln_b, gmlp_w_s, gmlp_b_s, merge_g, w_out, ffn_w_gate, ffn_w_up, ffn_w_down, moe_router, moe_w_gate, moe_w_up, moe_w_down, final_norm_g):
    W = dict(norm_mix_g=norm_mix_g, norm_ffn_g=norm_ffn_g, w_in=w_in, b_in=b_in, s5_a_re=s5_a_re, s5_a_im=s5_a_im,
             s5_log_dt=s5_log_dt, s5_b_re=s5_b_re, s5_b_im=s5_b_im, s5_c_re=s5_c_re, s5_c_im=s5_c_im, s5_d=s5_d,
             s5_w_glu=s5_w_glu, s5_b_glu=s5_b_glu, gla_w_gate2=gla_w_gate2, gla_b_gate2=gla_b_gate2,
             gla_onorm_g=gla_onorm_g, conv_w_dw=conv_w_dw, conv_b_dw=conv_b_dw, conv_ln_g=conv_ln_g,
             conv_ln_b=conv_ln_b, conv_w_pw=conv_w_pw, conv_b_pw=conv_b_pw, gmlp_ln_g=gmlp_ln_g,
             gmlp_ln_b=gmlp_ln_b, gmlp_w_s=gmlp_w_s, gmlp_b_s=gmlp_b_s, merge_g=merge_g, w_out=w_out,
             ffn_w_gate=ffn_w_gate, ffn_w_up=ffn_w_up, ffn_w_down=ffn_w_down, moe_router=moe_router,
             moe_w_gate=moe_w_gate, moe_w_up=moe_w_up, moe_w_down=moe_w_down, final_norm_g=final_norm_g)
    depth = w_in.shape[0]
    bp, tp, d = x_prompt.shape
    bs, ts, _ = x_sample.shape

    m = ada_modulation(jnp.concatenate([c_prompt, c_sample], axis=0), ada_w, ada_b)
    mods_p = [[m[i, :bp, k * d:(k + 1) * d].reshape(bp, 1, d) for k in range(6)] for i in range(depth)]
    mods_s = [[m[i, bp:, k * d:(k + 1) * d].reshape(1, bs, d) for k in range(6)] for i in range(depth)]

    z_re = jnp.zeros((depth, bp, S5_GROUPS, S5_STATE), F32)
    z_gla = jnp.zeros((depth, bp, GLA_HEADS, GLA_DK, GLA_DV), F32)
    z_conv = jnp.zeros((depth, bp, CONV_HIST, CONV_DIM), x_prompt.dtype)
    states = [(z_re, z_re, z_gla, z_conv), (state_s5_re, state_s5_im, state_gla, cache_conv)]
    branches = [_Branch(nseq=bp, nb=bp, t_len=tp, seq_form=True, tm_pre=min(512, tp), tm_post=min(512, tp),
                        tc=min(128, tp), tt_seq=min(512, tp), row0=0),
                _Branch(nseq=1, nb=bs, t_len=ts, seq_form=False, tm_pre=ts * bs, tm_post=min(512, ts * bs),
                        tc=ts, tt_seq=None, row0=bp * tp)]
    xs = [x_prompt, jnp.transpose(x_sample, (1, 0, 2)).reshape(1, ts * bs, d)]
    outs = [([], [], [], [], []), ([], [], [], [], [])]
    for i in range(depth):
        mods = [mods_p[i], mods_s[i]]
        mixes = [_mixers(x, m, st_, W, i, br, o)
                 for x, m, st_, br, o in zip(xs, mods, states, branches, outs)]
        xs = _channel_mixer(xs, mixes, mods, W, i, branches, i == depth - 1)

    y_p = xs[0]
    y_s = jnp.transpose(xs[1].reshape(ts, bs, d), (1, 0, 2))
    p_re, p_im, p_gla, p_conv, _ = outs[0]
    s_re, s_im, s_gla, s_conv, s_v = outs[1]
    s_v = [jnp.transpose(v.reshape(ts, bs, GMLP_WIDTH), (1, 0, 2)) for v in s_v]
    st = jnp.stack
    return (y_p, y_s, st(p_re), st(p_im), st(p_gla), st(p_conv),
            st(s_re), st(s_im), st(s_gla), st(s_conv), st(s_v))
```

```python
import functools
import math

import jax
import jax.numpy as jnp
from jax import lax
from jax.experimental import pallas as pl
from jax.experimental.pallas import tpu as pltpu

D_MODEL = 1024
S5_WIDTH = 256
S5_GROUP = 16
S5_GROUPS = 16
S5_STATE = 64
S5_LANES = S5_GROUPS * S5_STATE
GLA_HEADS = 4
GLA_DV = 64
GLA_DK = 32
GLA_WIDTH = 256
GLA_KEY_WIDTH = 128
GLA_GATE_RANK = 16
GLA_TAU = 16.0
GLA_CHUNK = 64
GLA_STATE_LANES = GLA_HEADS * GLA_DK * GLA_DV
CONV_DIM = 256
CONV_WIDTH = 31
CONV_HIST = CONV_WIDTH - 1
GMLP_WIDTH = 256
GMLP_HEADS = 4
GMLP_HEAD_DIM = 64
GMLP_CHUNK = 128
D_FF = 2816
N_EXPERTS = 8
EPS = 1e-6

LANE = 128
PW_S5 = 256
PW_GLA = 128 + 128 + 256 + 256 + LANE
PW_CONV = 512
PW_MLP = 512
PW_TOTAL = PW_S5 + PW_GLA + PW_CONV + PW_MLP
VMEM_LIMIT = 56 * 1024 * 1024

F32 = jnp.float32
BF16 = jnp.bfloat16
HI = lax.Precision.HIGHEST


def _cparams(sem):
    return pltpu.CompilerParams(dimension_semantics=sem, vmem_limit_bytes=VMEM_LIMIT)


def _rms(x):
    return x * lax.rsqrt(jnp.mean(x * x, axis=-1, keepdims=True) + EPS)


def _layernorm(x, g, b):
    mu = jnp.mean(x, axis=-1, keepdims=True)
    xc = x - mu
    var = jnp.mean(xc * xc, axis=-1, keepdims=True)
    return xc * lax.rsqrt(var + EPS) * g + b


def _silu(x):
    return x * jax.nn.sigmoid(x)


def _gelu_tanh(x):
    return 0.5 * x * (1.0 + jnp.tanh(math.sqrt(2.0 / math.pi) * (x + 0.044715 * (x * x * x))))


def _log_sigmoid(x):
    return jnp.minimum(x, 0.0) - jnp.log(1.0 + jnp.exp(-jnp.abs(x)))


def _same_block(shape, row_block, col_block):
    r = lax.broadcasted_iota(jnp.int32, shape, 0) >> (row_block.bit_length() - 1)
    c = lax.broadcasted_iota(jnp.int32, shape, 1) >> (col_block.bit_length() - 1)
    return r == c


def _modulate(y, sc, sh):
    rm = sc.shape[0]
    if rm == 1:
        return y * (1.0 + sc) + sh
    rows, d = y.shape
    y3 = y.reshape(rows // rm, rm, d)
    return (y3 * (1.0 + sc)[None] + sh[None]).reshape(rows, d)


def _gate(y, g):
    rm = g.shape[0]
    if rm == 1:
        return y * g
    rows, d = y.shape
    return (y.reshape(rows // rm, rm, d) * g[None]).reshape(rows, d)


def _ada_kernel(c_ref, w_ref, b_ref, o_ref):
    c = c_ref[...]
    s = _silu(c).astype(BF16)
    o_ref[0] = jnp.dot(s, w_ref[0].astype(BF16), preferred_element_type=F32) + b_ref[0]


def ada_modulation(c_all, ada_w, ada_b):
    depth, d, n6 = ada_w.shape
    rows = c_all.shape[0]
    tn = 1536
    return pl.pallas_call(
        _ada_kernel,
        out_shape=jax.ShapeDtypeStruct((depth, rows, n6), F32),
        grid=(depth, n6 // tn),
        in_specs=[pl.BlockSpec((rows, d), lambda l, j: (0, 0)),
                  pl.BlockSpec((1, d, tn), lambda l, j: (l, 0, j)),
                  pl.BlockSpec((1, 1, tn), lambda l, j: (l, 0, j))],
        out_specs=pl.BlockSpec((1, rows, tn), lambda l, j: (l, 0, j)),
        compiler_params=_cparams(("arbitrary", "arbitrary")),
        name="ada_modulation",
    )(c_all, ada_w, ada_b.reshape(depth, 1, n6))


def _pre_kernel(x_ref, sh_ref, sc_ref, g_ref, w_ref, b_ref, o_s5, o_gla, o_conv, o_mlp):
    x = x_ref[0]
    y = _modulate(_rms(x) * g_ref[...], sc_ref[0], sh_ref[0])
    p = jnp.dot(y.astype(BF16), w_ref[...], preferred_element_type=F32) + b_ref[...]
    o_s5[...] = p[:, 0:PW_S5]
    o_gla[...] = p[:, PW_S5:PW_S5 + PW_GLA]
    o_conv[...] = p[:, PW_S5 + PW_GLA:PW_S5 + PW_GLA + PW_CONV]
    o_mlp[...] = p[:, PW_S5 + PW_GLA + PW_CONV:PW_TOTAL]


def pre_mixer(x, sh, sc, g, w, b, tm):
    s, r, d = x.shape
    rm = sh.shape[1]
    widths = (PW_S5, PW_GLA, PW_CONV, PW_MLP)
    return pl.pallas_call(
        _pre_kernel,
        out_shape=[jax.ShapeDtypeStruct((r, s * w_), F32) for w_ in widths],
        grid=(s, r // tm),
        in_specs=[pl.BlockSpec((1, tm, d), lambda b_, i: (b_, i, 0)),
                  pl.BlockSpec((1, rm, d), lambda b_, i: (b_, 0, 0)),
                  pl.BlockSpec((1, rm, d), lambda b_, i: (b_, 0, 0)),
                  pl.BlockSpec((1, d), lambda b_, i: (0, 0)),
                  pl.BlockSpec((d, PW_TOTAL), lambda b_, i: (0, 0)),
                  pl.BlockSpec((1, PW_TOTAL), lambda b_, i: (0, 0))],
        out_specs=[pl.BlockSpec((tm, w_), lambda b_, i: (i, b_)) for w_ in widths],
        compiler_params=_cparams(("arbitrary", "arbitrary")),
        name="pre_mixer",
    )(x, sh, sc, g, w, b)


def _s5_disc_kernel(lr_ref, li_ref, ldt_ref, br_ref, bi_ref, abr_ref, abi_ref, bbr_ref, bbi_ref):
    lr = lr_ref[...]
    li = li_ref[...]
    dt = jnp.exp(ldt_ref[...])
    mag = jnp.exp(lr * dt)
    ang = li * dt
    ab_re = mag * jnp.cos(ang)
    ab_im = mag * jnp.sin(ang)
    den = lr * lr + li * li
    nr = ab_re - 1.0
    f_re = (nr * lr + ab_im * li) / den
    f_im = (ab_im * lr - nr * li) / den
    br = br_ref[...]
    bi = bi_ref[...]
    abr_ref[...] = ab_re
    abi_ref[...] = ab_im
    bbr_ref[...] = f_re * br - f_im * bi
    bbi_ref[...] = f_re * bi + f_im * br


def s5_discretise(a_re, a_im, log_dt, b_re, b_im):
    n = b_re.shape[-1]
    gp = a_re.size
    bc = lambda a: jnp.broadcast_to(a.reshape(gp, 1), (gp, n))
    ldt = jnp.broadcast_to(log_dt[:, None], a_re.shape)
    outs = pl.pallas_call(
        _s5_disc_kernel,
        out_shape=[jax.ShapeDtypeStruct((gp, n), F32)] * 4,
        name="s5_discretise",
    )(bc(a_re), bc(a_im), bc(ldt), b_re.reshape(gp, n), b_im.reshape(gp, n))
    ab_re, ab_im, bb_re, bb_im = outs
    return ab_re[:, 0], ab_im[:, 0], bb_re, bb_im


def _block_diag_in(bb):
    g, p, n = S5_GROUPS, S5_STATE, S5_GROUP
    b3 = bb.reshape(g, p, n)
    eye = jnp.eye(g, dtype=bb.dtype)
    return jnp.einsum('gpn,gh->gnhp', b3, eye).reshape(g * n, g * p)


def _block_diag_out(c):
    g, p, n = S5_GROUPS, S5_STATE, S5_GROUP
    eye = jnp.eye(g, dtype=c.dtype)
    return jnp.einsum('gnp,gh->gphn', c, eye).reshape(g * p, g * n)


def _s5_kernel(u_ref, h0_ref, bblk_ref, cre_ref, cim_ref, ar_ref, ai_ref, d_ref, wglu_ref, bglu_ref, mg_ref,
               o_ref, hT_ref, xs_ref, hs_ref, *, nb, tc):
    i = pl.program_id(0)

    @pl.when(i == 0)
    def _():
        hs_ref[...] = h0_ref[...]

    u = u_ref[...]
    xs_ref[...] = jnp.dot(u.astype(BF16), bblk_ref[...], preferred_element_type=F32)
    ar = jnp.broadcast_to(ar_ref[...], (nb, S5_LANES))
    ai = jnp.broadcast_to(ai_ref[...], (nb, S5_LANES))

    def step(t, carry):
        hr, hi = carry
        row = pl.multiple_of(t * nb, nb)
        xr = xs_ref[pl.ds(row, nb), 0:S5_LANES]
        xi = xs_ref[pl.ds(row, nb), S5_LANES:2 * S5_LANES]
        nr = ar * hr - ai * hi + xr
        ni = ar * hi + ai * hr + xi
        xs_ref[pl.ds(row, nb), 0:S5_LANES] = nr
        xs_ref[pl.ds(row, nb), S5_LANES:2 * S5_LANES] = ni
        return nr, ni

    hr, hi = lax.fori_loop(0, tc, step, (hs_ref[:, 0:S5_LANES], hs_ref[:, S5_LANES:2 * S5_LANES]),
                           unroll=True if tc <= 8 else 4)
    hs_ref[:, 0:S5_LANES] = hr
    hs_ref[:, S5_LANES:2 * S5_LANES] = hi

    y = (jnp.dot(xs_ref[:, 0:S5_LANES].astype(BF16), cre_ref[...], preferred_element_type=F32)
         - jnp.dot(xs_ref[:, S5_LANES:2 * S5_LANES].astype(BF16), cim_ref[...], preferred_element_type=F32))
    y = y + d_ref[...] * u
    y = _gelu_tanh(y)
    y = y * jax.nn.sigmoid(jnp.dot(y.astype(BF16), wglu_ref[...], preferred_element_type=F32) + bglu_ref[...])
    o_ref[...] = (_rms(y) * mg_ref[...]).astype(o_ref.dtype)

    @pl.when(i == pl.num_programs(0) - 1)
    def _():
        hT_ref[...] = hs_ref[...]


def s5_mixer(u, h0, bblk, cre, cim, ar, ai, d, wglu, bglu, mg, nb, tc):
    rows = u.shape[0]
    rc = nb * tc
    full = lambda shape: pl.BlockSpec(shape, lambda i: (0,) * len(shape))
    return pl.pallas_call(
        functools.partial(_s5_kernel, nb=nb, tc=tc),
        out_shape=[jax.ShapeDtypeStruct((rows, S5_WIDTH), BF16),
                   jax.ShapeDtypeStruct((nb, 2 * S5_LANES), F32)],
        grid=(rows // rc,),
        in_specs=[pl.BlockSpec((rc, S5_WIDTH), lambda i: (i, 0)),
                  full((nb, 2 * S5_LANES)),
                  full((S5_WIDTH, 2 * S5_LANES)),
                  full((S5_LANES, S5_WIDTH)), full((S5_LANES, S5_WIDTH)),
                  full((1, S5_LANES)), full((1, S5_LANES)),
                  full((1, S5_WIDTH)), full((S5_WIDTH, S5_WIDTH)), full((1, S5_WIDTH)), full((1, S5_WIDTH))],
        out_specs=[pl.BlockSpec((rc, S5_WIDTH), lambda i: (i, 0)),
                   full((nb, 2 * S5_LANES))],
        scratch_shapes=[pltpu.VMEM((rc, 2 * S5_LANES), F32), pltpu.VMEM((nb, 2 * S5_LANES), F32)],
        compiler_params=_cparams(("arbitrary",)),
        name="s5_mixer",
    )(u, h0, bblk, cre, cim, ar, ai, d, wglu, bglu, mg)


CONV_ROWS = 64


def _conv_kernel(ag_ref, c0_ref, wdw_ref, bdw_ref, lng_ref, lnb_ref, wpw_ref, bpw_ref, mg_ref,
                 o_ref, buf_ref, zc_ref, y_ref, *, nb, tc):
    i = pl.program_id(0)
    hist = CONV_HIST * nb
    rc = nb * tc

    @pl.when(i == 0)
    def _():
        zc_ref[0:hist, :] = c0_ref[...]

    @pl.when(i > 0)
    def _():
        zc_ref[0:hist, :] = zc_ref[rc:rc + hist, :]

    a = ag_ref[:, 0:CONV_DIM]
    g = ag_ref[:, CONV_DIM:2 * CONV_DIM]
    zc_ref[hist:hist + rc, :] = a * jax.nn.sigmoid(g)

    w = wdw_ref[...]

    def tile(j, carry):
        r0 = pl.multiple_of(j * CONV_ROWS, CONV_ROWS)
        acc = jnp.zeros((CONV_ROWS, CONV_DIM), F32)
        for k in range(CONV_WIDTH):
            acc = acc + w[k:k + 1, :] * zc_ref[pl.ds(r0 + k * nb, CONV_ROWS), :]
        y_ref[pl.ds(r0, CONV_ROWS), :] = acc
        return carry

    lax.fori_loop(0, rc // CONV_ROWS, tile, 0)
    y = y_ref[...] + bdw_ref[...]
    y = _silu(_layernorm(y, lng_ref[...], lnb_ref[...]))
    y = jnp.dot(y.astype(BF16), wpw_ref[...], preferred_element_type=F32) + bpw_ref[...]
    o_ref[...] = (_rms(y) * mg_ref[...]).astype(o_ref.dtype)

    @pl.when(i == pl.num_programs(0) - 1)
    def _():
        buf_ref[...] = zc_ref[rc:rc + hist, :]


def conv_mixer(ag, c0, wdw, bdw, lng, lnb, wpw, bpw, mg, nb, tc):
    rows = ag.shape[0]
    rc = nb * tc
    hist = CONV_HIST * nb
    assert rows == rc or tc >= CONV_HIST
    full = lambda shape: pl.BlockSpec(shape, lambda i: (0,) * len(shape))
    return pl.pallas_call(
        functools.partial(_conv_kernel, nb=nb, tc=tc),
        out_shape=[jax.ShapeDtypeStruct((rows, CONV_DIM), BF16),
                   jax.ShapeDtypeStruct((hist, CONV_DIM), F32)],
        grid=(rows // rc,),
        in_specs=[pl.BlockSpec((rc, 2 * CONV_DIM), lambda i: (i, 0)),
                  full((hist, CONV_DIM)), full((CONV_WIDTH, CONV_DIM)),
                  full((1, CONV_DIM)), full((1, CONV_DIM)), full((1, CONV_DIM)),
                  full((CONV_DIM, CONV_DIM)), full((1, CONV_DIM)), full((1, CONV_DIM))],
        out_specs=[pl.BlockSpec((rc, CONV_DIM), lambda i: (i, 0)), full((hist, CONV_DIM))],
        scratch_shapes=[pltpu.VMEM((hist + rc, CONV_DIM), F32), pltpu.VMEM((rc, CONV_DIM), F32)],
        compiler_params=_cparams(("arbitrary",)),
        name="conv_mixer",
    )(ag, c0, wdw, bdw, lng, lnb, wpw, bpw, mg)


def _gmlp_seq_kernel(uv_ref, lng_ref, lnb_ref, wcat_ref, bias_ref, mg_ref, o_ref, *, tt):
    n_chunks = tt // GMLP_CHUNK
    kc = GMLP_HEADS * GMLP_CHUNK
    rowi = lax.broadcasted_iota(jnp.int32, (GMLP_CHUNK, kc), 0)
    coli = lax.broadcasted_iota(jnp.int32, (GMLP_CHUNK, kc), 1)
    wcat = jnp.where((coli & (GMLP_CHUNK - 1)) <= rowi, wcat_ref[...], 0.0).astype(BF16)
    sel = _same_block((kc, GMLP_WIDTH), GMLP_CHUNK, GMLP_HEAD_DIM)
    for c in range(n_chunks):
        rows = slice(c * GMLP_CHUNK, (c + 1) * GMLP_CHUNK)
        u = uv_ref[rows, 0:GMLP_WIDTH]
        v = uv_ref[rows, GMLP_WIDTH:2 * GMLP_WIDTH]
        vn = _layernorm(v, lng_ref[...], lnb_ref[...])
        vbd = jnp.where(sel, jnp.concatenate([vn] * GMLP_HEADS, axis=0), 0.0).astype(BF16)
        mixed = jnp.dot(wcat, vbd, preferred_element_type=F32) + bias_ref[...]
        o_ref[rows, :] = (_rms(u * mixed) * mg_ref[...]).astype(o_ref.dtype)


def gmlp_seq(uv, nseq, lng, lnb, wcat, bias, mg, tt):
    t = uv.shape[0]
    full = lambda shape: pl.BlockSpec(shape, lambda b_, i: (0,) * len(shape))
    return pl.pallas_call(
        functools.partial(_gmlp_seq_kernel, tt=tt),
        out_shape=jax.ShapeDtypeStruct((t, nseq * GMLP_WIDTH), BF16),
        grid=(nseq, t // tt),
        in_specs=[pl.BlockSpec((tt, 2 * GMLP_WIDTH), lambda b_, i: (i, b_)),
                  full((1, GMLP_WIDTH)), full((1, GMLP_WIDTH)),
                  full((GMLP_CHUNK, GMLP_HEADS * GMLP_CHUNK)), full((GMLP_CHUNK, GMLP_WIDTH)),
                  full((1, GMLP_WIDTH))],
        out_specs=pl.BlockSpec((tt, GMLP_WIDTH), lambda b_, i: (i, b_)),
        compiler_params=_cparams(("arbitrary", "arbitrary")),
        name="gmlp_seq",
    )(uv, lng, lnb, wcat, bias, mg)


def _gmlp_short_kernel(uv_ref, lng_ref, lnb_ref, wrow_ref, brow_ref, mg_ref, o_ref, vn_ref, *, nb, t_len):
    u = uv_ref[:, 0:GMLP_WIDTH]
    v = uv_ref[:, GMLP_WIDTH:2 * GMLP_WIDTH]
    vn = _layernorm(v, lng_ref[...], lnb_ref[...])
    vn_ref[...] = vn
    wrow = wrow_ref[...]
    brow = brow_ref[...]
    for t in range(t_len):
        mixed = jnp.zeros((nb, GMLP_WIDTH), F32) + brow[t:t + 1, :]
        for j in range(t + 1):
            mixed = mixed + wrow[t * t_len + j:t * t_len + j + 1, :] * vn[j * nb:(j + 1) * nb, :]
        o = u[t * nb:(t + 1) * nb, :] * mixed
        o_ref[t * nb:(t + 1) * nb, :] = (_rms(o) * mg_ref[...]).astype(o_ref.dtype)


def gmlp_short(uv, lng, lnb, wrow, brow, mg, nb, t_len):
    rows = uv.shape[0]
    return pl.pallas_call(
        functools.partial(_gmlp_short_kernel, nb=nb, t_len=t_len),
        out_shape=[jax.ShapeDtypeStruct((rows, GMLP_WIDTH), BF16),
                   jax.ShapeDtypeStruct((rows, GMLP_WIDTH), F32)],
        compiler_params=pltpu.CompilerParams(vmem_limit_bytes=VMEM_LIMIT),
        name="gmlp_short",
    )(uv, lng, lnb, wrow, brow, mg)


def _split3(x):
    a = x.astype(BF16)
    r1 = x - a.astype(F32)
    b = r1.astype(BF16)
    c = (r1 - b.astype(F32)).astype(BF16)
    return a, b, c


def _dot_exact_rhs(x, m):
    return sum(jnp.dot(t, m, preferred_element_type=F32) for t in _split3(x))


def _dot_exact_lhs(m, x):
    return sum(jnp.dot(m, t, preferred_element_type=F32) for t in _split3(x))


def _gla_tail(o, r, gmean, onorm, mg):
    ms = _dot_exact_rhs(o * o, gmean)
    o = o * lax.rsqrt(ms + EPS) * onorm
    o = o * _silu(r)
    return _rms(o) * mg


def _head_mean_matrix():
    return jnp.where(_same_block((GLA_WIDTH, GLA_WIDTH), GLA_DV, GLA_DV), 1.0 / GLA_DV, 0.0).astype(BF16)


GLA_SEQS = 4


def _gla_seq_kernel(x_ref, s0_ref, wg_ref, bg_ref, onorm_ref, mg_ref, o_ref, sT_ref, s_ref, *, tt):
    i = pl.program_id(1)
    L = GLA_CHUNK
    kw, vw = GLA_KEY_WIDTH, GLA_WIDTH

    @pl.when(i == 0)
    def _():
        s_ref[...] = s0_ref[...]

    tri = (lax.broadcasted_iota(jnp.int32, (L, L), 1) <= lax.broadcasted_iota(jnp.int32, (L, L), 0)).astype(BF16)
    kbd_sel = _same_block((GLA_HEADS * L, kw), L, GLA_DK)
    vbd_sel = _same_block((GLA_HEADS * L, vw), L, GLA_DV)
    causal = ((lax.broadcasted_iota(jnp.int32, (L, GLA_HEADS * L), 1) & (L - 1))
              <= lax.broadcasted_iota(jnp.int32, (L, GLA_HEADS * L), 0))
    s_sel = _same_block((vw, kw), GLA_DV, GLA_DK)
    gmean = _head_mean_matrix()
    scale = GLA_DK ** -0.5
    nt_dims = (((1,), (1,)), ((), ()))
    tn_dims = (((0,), (0,)), ((), ()))

    def chunk(c, carry):
        r0 = pl.multiple_of(c * L, L)
        for g in range(GLA_SEQS):
            x0 = g * PW_GLA
            q = x_ref[pl.ds(r0, L), x0:x0 + kw] * scale
            k = x_ref[pl.ds(r0, L), x0 + kw:x0 + 2 * kw]
            v = x_ref[pl.ds(r0, L), x0 + 2 * kw:x0 + 2 * kw + vw]
            r = x_ref[pl.ds(r0, L), x0 + 2 * kw + vw:x0 + 2 * kw + 2 * vw]
            gl = x_ref[pl.ds(r0, L), x0 + 2 * kw + 2 * vw:x0 + PW_GLA]
            la = _log_sigmoid(jnp.dot(gl.astype(BF16), wg_ref[...], preferred_element_type=F32) + bg_ref[...])
            la = la / GLA_TAU
            bc = _dot_exact_lhs(tri, la)
            b_last = bc[L - 1:L, :]
            qt = (q * jnp.exp(bc)).astype(BF16)
            kt = k * jnp.exp(-bc)
            kbd = jnp.where(kbd_sel, jnp.concatenate([kt] * GLA_HEADS, axis=0), 0.0).astype(BF16)
            att = lax.dot_general(qt, kbd, nt_dims, preferred_element_type=F32)
            att = jnp.where(causal, att, 0.0).astype(BF16)
            vb = v.astype(BF16)
            vbd = jnp.where(vbd_sel, jnp.concatenate([vb] * GLA_HEADS, axis=0), jnp.zeros((), BF16))
            st = s_ref[g]
            o = (jnp.dot(att, vbd, preferred_element_type=F32)
                 + lax.dot_general(qt, st.astype(BF16), nt_dims, preferred_element_type=F32))
            kdec = (k * jnp.exp(b_last - bc)).astype(BF16)
            upd = lax.dot_general(vb, kdec, tn_dims, preferred_element_type=F32)
            s_ref[g] = st * jnp.exp(b_last) + jnp.where(s_sel, upd, 0.0)
            o_ref[pl.ds(r0, L), g * vw:(g + 1) * vw] = _gla_tail(o, r, gmean, onorm_ref[...],
                                                                 mg_ref[...]).astype(o_ref.dtype)
        return carry

    lax.fori_loop(0, tt // L, chunk, 0)

    @pl.when(i == pl.num_programs(1) - 1)
    def _():
        sT_ref[...] = s_ref[...]


def gla_seq(x, nseq, s0, wg, bg, onorm, mg, tt):
    t = x.shape[0]
    g = GLA_SEQS
    assert nseq % g == 0
    full = lambda shape: pl.BlockSpec(shape, lambda b_, i: (0,) * len(shape))
    state = pl.BlockSpec((g, GLA_WIDTH, GLA_KEY_WIDTH), lambda b_, i: (b_, 0, 0))
    return pl.pallas_call(
        functools.partial(_gla_seq_kernel, tt=tt),
        out_shape=[jax.ShapeDtypeStruct((t, nseq * GLA_WIDTH), BF16),
                   jax.ShapeDtypeStruct((nseq, GLA_WIDTH, GLA_KEY_WIDTH), F32)],
        grid=(nseq // g, t // tt),
        in_specs=[pl.BlockSpec((tt, g * PW_GLA), lambda b_, i: (i, b_)),
                  state,
                  full((LANE, GLA_KEY_WIDTH)), full((1, GLA_KEY_WIDTH)),
                  full((1, GLA_WIDTH)), full((1, GLA_WIDTH))],
        out_specs=[pl.BlockSpec((tt, g * GLA_WIDTH), lambda b_, i: (i, b_)), state],
        scratch_shapes=[pltpu.VMEM((g, GLA_WIDTH, GLA_KEY_WIDTH), F32)],
        compiler_params=_cparams(("arbitrary", "arbitrary")),
        name="gla_seq",
    )(x, s0, wg, bg, onorm, mg)


def _gla_rec_kernel(x_ref, s0_ref, ek_ref, ev_ref, wg_ref, bg_ref, onorm_ref, mg_ref, o_ref, sT_ref,
                    *, nb, t_len):
    kw, vw = GLA_KEY_WIDTH, GLA_WIDTH
    hl = GLA_DK * GLA_DV
    sT_ref[...] = s0_ref[...]
    gmean = _head_mean_matrix()
    scale = GLA_DK ** -0.5

    def step(t, carry):
        r0 = pl.multiple_of(t * nb, nb)
        q = x_ref[pl.ds(r0, nb), 0:kw] * scale
        k = x_ref[pl.ds(r0, nb), kw:2 * kw]
        v = x_ref[pl.ds(r0, nb), 2 * kw:2 * kw + vw]
        r = x_ref[pl.ds(r0, nb), 2 * kw + vw:2 * kw + 2 * vw]
        gl = x_ref[pl.ds(r0, nb), 2 * kw + 2 * vw:2 * kw + 2 * vw + LANE]
        la = _log_sigmoid(jnp.dot(gl.astype(BF16), wg_ref[...], preferred_element_type=F32) + bg_ref[...])
        a = jnp.exp(la / GLA_TAU)
        a3 = _split3(a)
        qb = q.astype(BF16)
        kb = k.astype(BF16)
        vb = v.astype(BF16)
        outs = []
        for h in range(GLA_HEADS):
            lanes = slice(h * hl, (h + 1) * hl)
            ek = ek_ref[:, lanes]
            a_e = (jnp.dot(a3[0], ek, preferred_element_type=F32)
                   + jnp.dot(a3[1], ek, preferred_element_type=F32)
                   + jnp.dot(a3[2], ek, preferred_element_type=F32))
            k_e = jnp.dot(kb, ek, preferred_element_type=F32)
            q_e = jnp.dot(qb, ek, preferred_element_type=F32)
            v_e = jnp.dot(vb, ev_ref[:, lanes], preferred_element_type=F32)
            s_new = a_e * sT_ref[:, lanes] + k_e * v_e
            sT_ref[:, lanes] = s_new
            prod = q_e * s_new
            acc = prod[:, 0:LANE]
            for j in range(1, hl // LANE):
                acc = acc + prod[:, j * LANE:(j + 1) * LANE]
            outs.append(acc[:, 0:GLA_DV] + acc[:, GLA_DV:2 * GLA_DV])
        o = jnp.concatenate(outs, axis=1)
        o_ref[pl.ds(r0, nb), :] = _gla_tail(o, r, gmean, onorm_ref[...], mg_ref[...]).astype(o_ref.dtype)
        return carry

    lax.fori_loop(0, t_len, step, 0)


def gla_recurrent(x, s0, ek, ev, wg, bg, onorm, mg, nb, t_len):
    rows = x.shape[0]
    return pl.pallas_call(
        functools.partial(_gla_rec_kernel, nb=nb, t_len=t_len),
        out_shape=[jax.ShapeDtypeStruct((rows, GLA_WIDTH), BF16),
                   jax.ShapeDtypeStruct((nb, GLA_STATE_LANES), F32)],
        compiler_params=pltpu.CompilerParams(vmem_limit_bytes=VMEM_LIMIT),
        name="gla_recurrent",
    )(x, s0, ek, ev, wg, bg, onorm, mg)


def _gla_expanders():
    lane = jnp.arange(GLA_STATE_LANES)
    h = lane // (GLA_DK * GLA_DV)
    dk = (lane // GLA_DV) % GLA_DK
    dv = lane % GLA_DV
    ek = (jnp.arange(GLA_KEY_WIDTH)[:, None] == (h * GLA_DK + dk)[None, :]).astype(BF16)
    ev = (jnp.arange(GLA_WIDTH)[:, None] == (h * GLA_DV + dv)[None, :]).astype(BF16)
    return ek, ev


def _mix_residual(x_ref, m_refs, g1_ref, wout_ref):
    mix = jnp.concatenate([m[...] for m in m_refs], axis=1)
    proj = jnp.dot(mix, wout_ref[...], preferred_element_type=F32)
    return x_ref[0] + _gate(proj, g1_ref[0])


def _post_dense_kernel(x_ref, m0_ref, m1_ref, m2_ref, m3_ref, g1_ref, sh2_ref, sc2_ref, g2_ref, ng_ref, wout_ref,
                       wg_ref, wu_ref, wd_ref, fg_ref, o_ref, h2_ref, acc_ref, *, final_norm):
    c = pl.program_id(2)

    @pl.when(c == 0)
    def _():
        x1 = _mix_residual(x_ref, (m0_ref, m1_ref, m2_ref, m3_ref), g1_ref, wout_ref)
        o_ref[0] = x1
        h = _modulate(_rms(x1) * ng_ref[...], sc2_ref[0], sh2_ref[0])
        h2_ref[...] = h.astype(BF16)
        acc_ref[...] = jnp.zeros_like(acc_ref)

    h2 = h2_ref[...]
    t = (_silu(jnp.dot(h2, wg_ref[...], preferred_element_type=F32))
         * jnp.dot(h2, wu_ref[...], preferred_element_type=F32))
    acc_ref[...] += jnp.dot(t.astype(BF16), wd_ref[...], preferred_element_type=F32)

    @pl.when(c == pl.num_programs(2) - 1)
    def _():
        x2 = o_ref[0] + _gate(acc_ref[...], g2_ref[0])
        if final_norm:
            x2 = _rms(x2) * fg_ref[...]
        o_ref[0] = x2


def post_dense(x, mixes, g1, sh2, sc2, g2, ng, wout, wg, wu, wd, fg, tm, tf, final_norm):
    s, r, d = x.shape
    rm = g1.shape[1]
    ff = wg.shape[1]
    mod = pl.BlockSpec((1, rm, d), lambda b_, i, c: (b_, 0, 0))
    const = lambda shape: pl.BlockSpec(shape, lambda b_, i, c: (0,) * len(shape))
    mixspec = pl.BlockSpec((tm, 256), lambda b_, i, c: (i, b_))
    return pl.pallas_call(
        functools.partial(_post_dense_kernel, final_norm=final_norm),
        out_shape=jax.ShapeDtypeStruct((s, r, d), F32),
        grid=(s, r // tm, ff // tf),
        in_specs=[pl.BlockSpec((1, tm, d), lambda b_, i, c: (b_, i, 0)),
                  mixspec, mixspec, mixspec, mixspec,
                  mod, mod, mod, mod,
                  const((1, d)), const((d, d)),
                  pl.BlockSpec((d, tf), lambda b_, i, c: (0, c)),
                  pl.BlockSpec((d, tf), lambda b_, i, c: (0, c)),
                  pl.BlockSpec((tf, d), lambda b_, i, c: (c, 0)),
                  const((1, d))],
        out_specs=pl.BlockSpec((1, tm, d), lambda b_, i, c: (b_, i, 0)),
        scratch_shapes=[pltpu.VMEM((tm, d), BF16), pltpu.VMEM((tm, d), F32)],
        compiler_params=_cparams(("arbitrary", "arbitrary", "arbitrary")),
        name="post_dense",
    )(x, *mixes, g1, sh2, sc2, g2, ng, wout, wg, wu, wd, fg)


ROW_TILE = 8


def _store_row_tiles(ref, x, lead=()):
    rows = x.shape[0]
    for s in range(ROW_TILE):
        ref[(*lead, pl.ds(s, rows, stride=ROW_TILE), slice(None))] = x[:, s * LANE:(s + 1) * LANE]


def _load_row_tiles(ref, rows, lead=()):
    return jnp.concatenate([ref[(*lead, pl.ds(s, rows, stride=ROW_TILE), slice(None))] for s in range(ROW_TILE)],
                           axis=1)


def _route_kernel(x_ref, m0_ref, m1_ref, m2_ref, m3_ref, g1_ref, sh2_ref, sc2_ref, ng_ref, wout_ref, router_ref,
                  *rest):
    x1_ref, h2_ref, route_ref = rest[-3:]
    x1 = _mix_residual(x_ref, (m0_ref, m1_ref, m2_ref, m3_ref), g1_ref, wout_ref)
    x1_ref[0] = x1
    h = _modulate(_rms(x1) * ng_ref[...], sc2_ref[0], sh2_ref[0])
    _store_row_tiles(h2_ref, h)
    h_hi = h.astype(BF16)
    h_lo = (h - h_hi.astype(F32)).astype(BF16)
    w = router_ref[...]
    w_hi = w.astype(BF16)
    w_lo = (w - w_hi.astype(F32)).astype(BF16)
    logits = (jnp.dot(h_hi, w_hi, preferred_element_type=F32) + jnp.dot(h_lo, w_hi, preferred_element_type=F32)
              + jnp.dot(h_hi, w_lo, preferred_element_type=F32))
    lane = lax.broadcasted_iota(jnp.int32, logits.shape, 1).astype(F32)
    neg = jnp.float32(-jnp.inf)
    logits = jnp.where(lane < N_EXPERTS, logits, neg)
    m1 = jnp.max(logits, axis=1, keepdims=True)
    i1 = jnp.min(jnp.where(logits == m1, lane, float(LANE)), axis=1, keepdims=True)
    others = jnp.where(lane == i1, neg, logits)
    m2 = jnp.max(others, axis=1, keepdims=True)
    i2 = jnp.min(jnp.where(others == m2, lane, float(LANE)), axis=1, keepdims=True)
    e2 = jnp.exp(m2 - m1)
    den = 1.0 + e2
    route_ref[...] = (jnp.where(lane == 0.0, i1, 0.0) + jnp.where(lane == 1.0, i2, 0.0)
                      + jnp.where(lane == 2.0, 1.0 / den, 0.0) + jnp.where(lane == 3.0, e2 / den, 0.0))


def moe_route(x, mixes, g1, sh2, sc2, ng, wout, router, tm, row0, shared):
    s, r, d = x.shape
    rm = g1.shape[1]
    nt = r // tm
    blk0 = row0 // tm
    n_total = shared[1].shape[0]
    mod = pl.BlockSpec((1, rm, d), lambda b_, i: (b_, 0, 0))
    const = lambda shape: pl.BlockSpec(shape, lambda b_, i: (0,) * len(shape))
    mixspec = pl.BlockSpec((tm, 256), lambda b_, i: (i, b_))
    in_specs = [pl.BlockSpec((1, tm, d), lambda b_, i: (b_, i, 0)),
                mixspec, mixspec, mixspec, mixspec, mod, mod, mod,
                const((1, d)), const((d, d)), const((d, LANE))]
    args = [x, *mixes, g1, sh2, sc2, ng, wout, router]
    in_specs += [pl.BlockSpec(memory_space=pl.ANY), pl.BlockSpec(memory_space=pl.ANY)]
    aliases = {len(args): 1, len(args) + 1: 2}
    args += list(shared)
    return pl.pallas_call(
        _route_kernel,
        out_shape=[jax.ShapeDtypeStruct((s, r, d), F32),
                   jax.ShapeDtypeStruct((n_total * ROW_TILE, LANE), F32),
                   jax.ShapeDtypeStruct((n_total, LANE), F32)],
        grid=(s, nt),
        in_specs=in_specs,
        out_specs=[pl.BlockSpec((1, tm, d), lambda b_, i: (b_, i, 0)),
                   pl.BlockSpec((tm * ROW_TILE, LANE), lambda b_, i: (blk0 + b_ * nt + i, 0)),
                   pl.BlockSpec((tm, LANE), lambda b_, i: (blk0 + b_ * nt + i, 0))],
        input_output_aliases=aliases,
        compiler_params=_cparams(("arbitrary", "arbitrary")),
        name="moe_route",
    )(*args)


def _route_tables(route, tg, n_tiles):
    n_total = route.shape[0]
    flat_e = route[:, 0:2].astype(jnp.int32).reshape(-1)
    order = jnp.argsort(flat_e, stable=True).astype(jnp.int32)
    counts = jnp.sum(flat_e[:, None] == jnp.arange(N_EXPERTS, dtype=jnp.int32)[None, :], axis=0).astype(jnp.int32)
    tiles_per = (counts + tg - 1) // tg
    tile_end = jnp.cumsum(tiles_per)
    n_used = tile_end[-1]
    tile_id = jnp.arange(n_tiles, dtype=jnp.int32)
    tile_ok = tile_id < n_used
    tile_e = jnp.searchsorted(tile_end, jnp.minimum(tile_id, n_used - 1), side='right').astype(jnp.int32)
    sort_start = jnp.cumsum(counts) - counts
    done = (tile_id - (tile_end - tiles_per)[tile_e]) * tg
    n_valid = jnp.where(tile_ok, jnp.clip(counts[tile_e] - done, 0, tg), 0).astype(jnp.int32)
    tile_start = jnp.where(tile_ok, sort_start[tile_e] + done, 0).astype(jnp.int32)
    pad = jnp.zeros((tg,), jnp.int32)
    src = jnp.concatenate([(order >> 1) * ROW_TILE, pad])
    dst = jnp.concatenate([((order & 1) * n_total + (order >> 1)) * ROW_TILE, pad])
    return tile_e, n_valid, tile_start, src, dst


def _experts_kernel(te_ref, nv_ref, ts_ref, src_ref, dst_ref, h2_hbm, wg_ref, wu_ref, wd_ref, out_hbm,
                    xbuf, obuf, gsem, ssem, *, tg, n_tiles, tf):
    j = pl.program_id(0)
    slot = lax.rem(j, 2)
    other = 1 - slot
    ok = nv_ref[j] > 0

    def row_tile(buf, s_, r):
        return buf.at[s_, pl.ds(pl.multiple_of(r * ROW_TILE, ROW_TILE), ROW_TILE), :]

    def gather(tile, s_):
        base = ts_ref[tile]

        def body(r, c):
            row = pl.multiple_of(src_ref[base + r], ROW_TILE)
            pltpu.make_async_copy(h2_hbm.at[pl.ds(row, ROW_TILE), :], row_tile(xbuf, s_, r), gsem.at[s_]).start()
            return c

        lax.fori_loop(0, tg, body, 0, unroll=8)

    def scatter_row(tile, s_):
        base = ts_ref[tile]

        def body(r, c):
            row = pl.multiple_of(dst_ref[base + r], ROW_TILE)
            pltpu.make_async_copy(row_tile(obuf, s_, r), out_hbm.at[pl.ds(row, ROW_TILE), :], ssem.at[s_]).start()
            return c

        return body

    def scatter(tile, s_):
        n = nv_ref[tile]

        @pl.when(n == tg)
        def _():
            lax.fori_loop(0, tg, scatter_row(tile, s_), 0, unroll=8)

        @pl.when(n < tg)
        def _():
            lax.fori_loop(0, n, scatter_row(tile, s_), 0)

    def wait_all(buf, sem, s_):
        pltpu.make_async_copy(buf.at[s_], buf.at[s_], sem.at[s_]).wait()

    def wait_scatter(tile, s_):
        n = nv_ref[tile]

        @pl.when(n == tg)
        def _():
            wait_all(obuf, ssem, s_)

        @pl.when(n < tg)
        def _():
            def body(r, c):
                pltpu.make_async_copy(obuf.at[s_, pl.ds(0, ROW_TILE), :], out_hbm.at[pl.ds(0, ROW_TILE), :],
                                      ssem.at[s_]).wait()
                return c

            lax.fori_loop(0, n, body, 0)

    @pl.when(jnp.logical_and(j == 0, ok))
    def _():
        gather(0, 0)

    @pl.when(ok)
    def _():
        wait_all(xbuf, gsem, slot)

    nxt = jnp.minimum(j + 1, n_tiles - 1)

    @pl.when(jnp.logical_and(j + 1 < n_tiles, nv_ref[nxt] > 0))
    def _():
        gather(j + 1, other)

    @pl.when(j >= 2)
    def _():
        wait_scatter(j - 2, slot)

    @pl.when(ok)
    def _():
        x = _load_row_tiles(xbuf, tg, lead=(slot,)).astype(BF16)
        acc = jnp.zeros((tg, x.shape[1]), F32)
        for c in range(wg_ref.shape[2] // tf):
            cols = slice(c * tf, (c + 1) * tf)
            t = (_silu(jnp.dot(x, wg_ref[0, :, cols], preferred_element_type=F32))
                 * jnp.dot(x, wu_ref[0, :, cols], preferred_element_type=F32))
            acc = acc + jnp.dot(t.astype(BF16), wd_ref[0, cols, :], preferred_element_type=F32)
        _store_row_tiles(obuf, acc, lead=(slot,))
        scatter(j, slot)

    @pl.when(j == n_tiles - 1)
    def _():
        wait_scatter(j - 1, other)
        wait_scatter(j, slot)


def moe_experts(h2, tables, wg, wu, wd, tg, n_tiles, tf):
    n_exp, d, ff = wg.shape
    assert d == ROW_TILE * LANE and n_tiles >= 2
    tile_e, n_valid, tile_start, src, dst = tables
    wspec = lambda shape: pl.BlockSpec(shape, lambda j, te, *_: (te[j], 0, 0))
    grid_spec = pltpu.PrefetchScalarGridSpec(
        num_scalar_prefetch=5,
        grid=(n_tiles,),
        in_specs=[pl.BlockSpec(memory_space=pl.ANY), wspec((1, d, ff)), wspec((1, d, ff)), wspec((1, ff, d))],
        out_specs=pl.BlockSpec(memory_space=pl.ANY),
        scratch_shapes=[pltpu.VMEM((2, tg * ROW_TILE, LANE), F32), pltpu.VMEM((2, tg * ROW_TILE, LANE), F32),
                        pltpu.SemaphoreType.DMA((2,)), pltpu.SemaphoreType.DMA((2,))])
    return pl.pallas_call(
        functools.partial(_experts_kernel, tg=tg, n_tiles=n_tiles, tf=tf),
        out_shape=jax.ShapeDtypeStruct((2 * h2.shape[0], LANE), F32),
        grid_spec=grid_spec,
        compiler_params=_cparams(("arbitrary",)),
        name="moe_experts",
    )(tile_e, n_valid, tile_start, src, dst, h2, wg, wu, wd)


def _combine_kernel(x1_ref, y0_ref, y1_ref, route_ref, g2_ref, fg_ref, o_ref, *, final_norm):
    r = route_ref[...]
    rows = r.shape[0]
    f = r[:, 2:3] * _load_row_tiles(y0_ref, rows, lead=(0,)) + r[:, 3:4] * _load_row_tiles(y1_ref, rows, lead=(0,))
    x2 = x1_ref[0] + _gate(f, g2_ref[0])
    if final_norm:
        x2 = _rms(x2) * fg_ref[...]
    o_ref[0] = x2


def moe_combine(x1, y, route, g2, fg, tm, row0, final_norm):
    s, r, d = x1.shape
    rm = g2.shape[1]
    nt = r // tm
    blk0 = row0 // tm
    return pl.pallas_call(
        functools.partial(_combine_kernel, final_norm=final_norm),
        out_shape=jax.ShapeDtypeStruct((s, r, d), F32),
        grid=(s, nt),
        in_specs=[pl.BlockSpec((1, tm, d), lambda b_, i: (b_, i, 0)),
                  pl.BlockSpec((1, tm * ROW_TILE, LANE), lambda b_, i: (0, blk0 + b_ * nt + i, 0)),
                  pl.BlockSpec((1, tm * ROW_TILE, LANE), lambda b_, i: (1, blk0 + b_ * nt + i, 0)),
                  pl.BlockSpec((tm, LANE), lambda b_, i: (blk0 + b_ * nt + i, 0)),
                  pl.BlockSpec((1, rm, d), lambda b_, i: (b_, 0, 0)),
                  pl.BlockSpec((1, d), lambda b_, i: (0, 0))],
        out_specs=pl.BlockSpec((1, tm, d), lambda b_, i: (b_, i, 0)),
        compiler_params=_cparams(("arbitrary", "arbitrary")),
        name="moe_combine",
    )(x1, y, y, route, g2, fg)


def _reorder_w_in(w_in, b_in):
    cut = PW_S5 + 128 + 128 + 256 + 256 + GLA_GATE_RANK
    pad = LANE - GLA_GATE_RANK
    w = jnp.concatenate([w_in[:, :cut], jnp.zeros((w_in.shape[0], pad), w_in.dtype), w_in[:, cut:]], axis=1)
    b = jnp.concatenate([b_in[:cut], jnp.zeros((pad,), b_in.dtype), b_in[cut:]])
    return w.astype(BF16), b.reshape(1, PW_TOTAL)


def _row(a):
    return a.reshape(1, -1)


class _Branch:
    def __init__(self, nseq, nb, t_len, seq_form, tm_pre, tm_post, tc, tt_seq, row0):
        self.nseq, self.nb, self.t_len, self.seq_form = nseq, nb, t_len, seq_form
        self.tm_pre, self.tm_post, self.tc, self.tt_seq, self.row0 = tm_pre, tm_post, tc, tt_seq, row0


def _layer_params(W, i):
    row = _row
    p = {}
    p['w_in'], p['b_in'] = _reorder_w_in(W['w_in'][i], W['b_in'][i])
    p['norm_g'] = row(W['norm_mix_g'][i])
    mg = W['merge_g'][i]
    p['mg'] = [row(mg[k * 256:(k + 1) * 256]) for k in range(4)]
    ab_re, ab_im, bb_re, bb_im = s5_discretise(W['s5_a_re'][i], W['s5_a_im'][i], W['s5_log_dt'][i],
                                               W['s5_b_re'][i], W['s5_b_im'][i])
    p['s5'] = (jnp.concatenate([_block_diag_in(bb_re), _block_diag_in(bb_im)], axis=1).astype(BF16),
               _block_diag_out(W['s5_c_re'][i]).astype(BF16), _block_diag_out(W['s5_c_im'][i]).astype(BF16),
               row(ab_re), row(ab_im), row(W['s5_d'][i]), W['s5_w_glu'][i].astype(BF16), row(W['s5_b_glu'][i]))
    wg2 = jnp.zeros((LANE, GLA_KEY_WIDTH), F32).at[:GLA_GATE_RANK].set(W['gla_w_gate2'][i]).astype(BF16)
    p['gla'] = (wg2, row(W['gla_b_gate2'][i]), row(W['gla_onorm_g'][i]))
    p['conv'] = (W['conv_w_dw'][i], row(W['conv_b_dw'][i]), row(W['conv_ln_g'][i]), row(W['conv_ln_b'][i]),
                 W['conv_w_pw'][i].astype(BF16), row(W['conv_b_pw'][i]))
    p['gmlp_ln'] = (row(W['gmlp_ln_g'][i]), row(W['gmlp_ln_b'][i]))
    p['gmlp_ws'], p['gmlp_bs'] = W['gmlp_w_s'][i], W['gmlp_b_s'][i]
    return p


def _mixers(x, mods, states, p, i, br, out):
    nseq, nb, t_len, seq_form = br.nseq, br.nb, br.t_len, br.seq_form
    s5_re0, s5_im0, gla0, conv0 = states
    new_re, new_im, new_gla, new_conv, new_v = out
    mg = p['mg']
    p_s5, p_gla, p_conv, p_mlp = pre_mixer(x, mods[0], mods[1], p['norm_g'], p['w_in'], p['b_in'], br.tm_pre)
    tmaj = lambda a, w_: a.reshape(t_len * nb, w_)

    h0 = jnp.concatenate([s5_re0[i].reshape(nb, S5_LANES), s5_im0[i].reshape(nb, S5_LANES)], axis=1)
    o_s5, h_t = s5_mixer(tmaj(p_s5, PW_S5), h0, *p['s5'], mg[0], nb, br.tc)
    new_re.append(h_t[:, :S5_LANES].reshape(nb, S5_GROUPS, S5_STATE))
    new_im.append(h_t[:, S5_LANES:].reshape(nb, S5_GROUPS, S5_STATE))

    if seq_form:
        eye = jnp.eye(GLA_HEADS, dtype=F32)
        s0 = jnp.einsum('bhkv,hg->bhvgk', gla0[i], eye).reshape(nseq, GLA_WIDTH, GLA_KEY_WIDTH)
        o_gla, s_t = gla_seq(p_gla, nseq, s0, *p['gla'], mg[1], br.tt_seq)
        s5d = s_t.reshape(nseq, GLA_HEADS, GLA_DV, GLA_HEADS, GLA_DK)
        new_gla.append(jnp.stack([jnp.swapaxes(s5d[:, h, :, h, :], 1, 2) for h in range(GLA_HEADS)], axis=1))
    else:
        ek, ev = _gla_expanders()
        o_gla, s_t = gla_recurrent(p_gla, gla0[i].reshape(nb, GLA_STATE_LANES), ek, ev, *p['gla'], mg[1], nb, t_len)
        new_gla.append(s_t.reshape(nb, GLA_HEADS, GLA_DK, GLA_DV))

    c0 = jnp.transpose(conv0[i], (1, 0, 2)).reshape(CONV_HIST * nb, CONV_DIM)
    o_conv, buf = conv_mixer(tmaj(p_conv, PW_CONV), c0, *p['conv'], mg[2], nb, br.tc)
    new_conv.append(jnp.transpose(buf.reshape(CONV_HIST, nb, CONV_DIM), (1, 0, 2)))

    ws, bs = p['gmlp_ws'], p['gmlp_bs']
    if seq_form:
        wcat = jnp.transpose(ws, (1, 0, 2)).reshape(GMLP_CHUNK, GMLP_HEADS * GMLP_CHUNK)
        bias = jnp.repeat(bs.T, GMLP_HEAD_DIM, axis=1)
        o_mlp = gmlp_seq(p_mlp, nseq, *p['gmlp_ln'], wcat, bias, mg[3], br.tt_seq)
        new_v.append(None)
    else:
        tri = jnp.tril(jnp.ones((t_len, t_len), F32))
        wrow = jnp.repeat(jnp.transpose(ws[:, :t_len, :t_len] * tri[None], (1, 2, 0)).reshape(t_len * t_len, GMLP_HEADS),
                          GMLP_HEAD_DIM, axis=1)
        brow = jnp.repeat(bs[:, :t_len].T, GMLP_HEAD_DIM, axis=1)
        o_mlp, vn = gmlp_short(p_mlp, *p['gmlp_ln'], wrow, brow, mg[3], nb, t_len)
        new_v.append(vn)

    per_seq = lambda a: a.reshape(t_len, nb * 256) if seq_form else a
    return [per_seq(o_s5), per_seq(o_gla), per_seq(o_conv), per_seq(o_mlp)]


FF_TILE = 1408
EXPERT_ROWS = 512


def _channel_mixer(xs, mixes, mods, W, i, branches, last):
    ng, wout, fg = _row(W['norm_ffn_g'][i]), W['w_out'][i].astype(BF16), _row(W['final_norm_g'])
    j = i // 2
    if i % 2 == 0:
        wg, wu, wd = (W['ffn_w_gate'][j].astype(BF16), W['ffn_w_up'][j].astype(BF16),
                      W['ffn_w_down'][j].astype(BF16))
        return [post_dense(x, mx, m[2], m[3], m[4], m[5], ng, wout, wg, wu, wd, fg, br.tm_post, FF_TILE, last)
                for x, mx, m, br in zip(xs, mixes, mods, branches)]
    n_total = sum(x.shape[0] * x.shape[1] for x in xs)
    tg = EXPERT_ROWS
    n_tiles = 2 * n_total // tg + N_EXPERTS
    router = jnp.zeros((D_MODEL, LANE), F32).at[:, :N_EXPERTS].set(W['moe_router'][j])
    h2, route = jnp.zeros((n_total * ROW_TILE, LANE), F32), jnp.zeros((n_total, LANE), F32)
    x1s = []
    for x, mx, m, br in zip(xs, mixes, mods, branches):
        x1, h2, route = moe_route(x, mx, m[2], m[3], m[4], ng, wout, router, br.tm_post, br.row0, (h2, route))
        x1s.append(x1)
    tables = _route_tables(route, tg, n_tiles)
    y = moe_experts(h2, tables, W['moe_w_gate'][j].astype(BF16), W['moe_w_up'][j].astype(BF16),
                    W['moe_w_down'][j].astype(BF16), tg, n_tiles, FF_TILE)
    y = y.reshape(2, n_total * ROW_TILE, LANE)
    return [moe_combine(x1, y, route, m[5], fg, br.tm_post, br.row0, last)
            for x1, m, br in zip(x1s, mods, branches)]


def kernel(x_prompt, x_sample, c_prompt, c_sample, state_s5_re, state_s5_im, state_gla, cache_conv, ada_w, ada_b, norm_mix_g, norm_ffn_g, w_in, b_in, s5_a_re, s5_a_im, s5_log_dt, s5_b_re, s5_b_im, s5_c_re, s5_c_im, s5_d, s5_w_glu, s5_b_glu, gla_w_gate2, gla_b_gate2, gla_onorm_g, conv_w_dw, conv_b_dw, conv_ln_g, conv_ln_b, conv_w_pw, conv_b_pw, gmlp_ln_g, gmlp_ln_b, gmlp_w_s, gmlp_b_s, merge_g, w_out, ffn_w_gate, ffn_w_up, ffn_w_down, moe_router, moe_w_gate, moe_w_up, moe_w_down, final_norm_g):
    W = dict(norm_mix_g=norm_mix_g, norm_ffn_g=norm_ffn_g, w_in=w_in, b_in=b_in, s5_a_re=s5_a_re, s5_a_im=s5_a_im,
             s5_log_dt=s5_log_dt, s5_b_re=s5_b_re, s5_b_im=s5_b_im, s5_c_re=s5_c_re, s5_c_im=s5_c_im, s5_d=s5_d,
             s5_w_glu=s5_w_glu, s5_b_glu=s5_b_glu, gla_w_gate2=gla_w_gate2, gla_b_gate2=gla_b_gate2,
             gla_onorm_g=gla_onorm_g, conv_w_dw=conv_w_dw, conv_b_dw=conv_b_dw, conv_ln_g=conv_ln_g,
             conv_ln_b=conv_ln_b, conv_w_pw=conv_w_pw, conv_b_pw=conv_b_pw, gmlp_ln_g=gmlp_ln_g,
             gmlp_ln_b=gmlp_ln_b, gmlp_w_s=gmlp_w_s, gmlp_b_s=gmlp_b_s, merge_g=merge_g, w_out=w_out,
             ffn_w_gate=ffn_w_gate, ffn_w_up=ffn_w_up, ffn_w_down=ffn_w_down, moe_router=moe_router,
             moe_w_gate=moe_w_gate, moe_w_up=moe_w_up, moe_w_down=moe_w_down, final_norm_g=final_norm_g)
    depth = w_in.shape[0]
    bp, tp, d = x_prompt.shape
    bs, ts, _ = x_sample.shape

    m = ada_modulation(jnp.concatenate([c_prompt, c_sample], axis=0), ada_w, ada_b)
    mods_p = [[m[i, :bp, k * d:(k + 1) * d].reshape(bp, 1, d) for k in range(6)] for i in range(depth)]
    mods_s = [[m[i, bp:, k * d:(k + 1) * d].reshape(1, bs, d) for k in range(6)] for i in range(depth)]

    z_re = jnp.zeros((depth, bp, S5_GROUPS, S5_STATE), F32)
    z_gla = jnp.zeros((depth, bp, GLA_HEADS, GLA_DK, GLA_DV), F32)
    z_conv = jnp.zeros((depth, bp, CONV_HIST, CONV_DIM), x_prompt.dtype)
    states = [(z_re, z_re, z_gla, z_conv), (state_s5_re, state_s5_im, state_gla, cache_conv)]
    branches = [_Branch(nseq=bp, nb=bp, t_len=tp, seq_form=True, tm_pre=min(512, tp), tm_post=min(512, tp),
                        tc=min(128, tp), tt_seq=min(512, tp), row0=0),
                _Branch(nseq=1, nb=bs, t_len=ts, seq_form=False, tm_pre=ts * bs, tm_post=min(512, ts * bs),
                        tc=ts, tt_seq=None, row0=bp * tp)]
    xs = [x_prompt, jnp.transpose(x_sample, (1, 0, 2)).reshape(1, ts * bs, d)]
    outs = [([], [], [], [], []), ([], [], [], [], [])]
    for i in range(depth):
        mods = [mods_p[i], mods_s[i]]
        params = _layer_params(W, i)
        mixes = [_mixers(x, m, st_, params, i, br, o)
                 for x, m, st_, br, o in zip(xs, mods, states, branches, outs)]
        xs = _channel_mixer(xs, mixes, mods, W, i, branches, i == depth - 1)

    y_p = xs[0]
    y_s = jnp.transpose(xs[1].reshape(ts, bs, d), (1, 0, 2))
    p_re, p_im, p_gla, p_conv, _ = outs[0]
    s_re, s_im, s_gla, s_conv, s_v = outs[1]
    s_v = [jnp.transpose(v.reshape(ts, bs, GMLP_WIDTH), (1, 0, 2)) for v in s_v]
    st = jnp.stack
    return (y_p, y_s, st(p_re), st(p_im), st(p_gla), st(p_conv),
            st(s_re), st(s_im), st(s_gla), st(s_conv), st(s_v))
```

```python
import functools
import math

import jax
import jax.numpy as jnp
from jax import lax
from jax.experimental import pallas as pl
from jax.experimental.pallas import tpu as pltpu

D_MODEL = 1024
S5_WIDTH = 256
S5_GROUP = 16
S5_GROUPS = 16
S5_STATE = 64
S5_LANES = S5_GROUPS * S5_STATE
GLA_HEADS = 4
GLA_DV = 64
GLA_DK = 32
GLA_WIDTH = 256
GLA_KEY_WIDTH = 128
GLA_GATE_RANK = 16
GLA_TAU = 16.0
GLA_CHUNK = 64
GLA_STATE_LANES = GLA_HEADS * GLA_DK * GLA_DV
CONV_DIM = 256
CONV_WIDTH = 31
CONV_HIST = CONV_WIDTH - 1
GMLP_WIDTH = 256
GMLP_HEADS = 4
GMLP_HEAD_DIM = 64
GMLP_CHUNK = 128
D_FF = 2816
N_EXPERTS = 8
EPS = 1e-6

LANE = 128
PW_S5 = 256
PW_GLA = 128 + 128 + 256 + 256 + LANE
PW_CONV = 512
PW_MLP = 512
PW_TOTAL = PW_S5 + PW_GLA + PW_CONV + PW_MLP
VMEM_LIMIT = 56 * 1024 * 1024

F32 = jnp.float32
BF16 = jnp.bfloat16
HI = lax.Precision.HIGHEST


def _cparams(sem):
    return pltpu.CompilerParams(dimension_semantics=sem, vmem_limit_bytes=VMEM_LIMIT)


def _rms(x):
    return x * lax.rsqrt(jnp.mean(x * x, axis=-1, keepdims=True) + EPS)


def _layernorm(x, g, b):
    mu = jnp.mean(x, axis=-1, keepdims=True)
    xc = x - mu
    var = jnp.mean(xc * xc, axis=-1, keepdims=True)
    return xc * lax.rsqrt(var + EPS) * g + b


def _silu(x):
    return x * jax.nn.sigmoid(x)


def _gelu_tanh(x):
    return 0.5 * x * (1.0 + jnp.tanh(math.sqrt(2.0 / math.pi) * (x + 0.044715 * (x * x * x))))


def _log_sigmoid(x):
    return jnp.minimum(x, 0.0) - jnp.log(1.0 + jnp.exp(-jnp.abs(x)))


def _same_block(shape, row_block, col_block):
    r = lax.broadcasted_iota(jnp.int32, shape, 0) >> (row_block.bit_length() - 1)
    c = lax.broadcasted_iota(jnp.int32, shape, 1) >> (col_block.bit_length() - 1)
    return r == c


def _modulate(y, sc, sh):
    rm = sc.shape[0]
    if rm == 1:
        return y * (1.0 + sc) + sh
    rows, d = y.shape
    y3 = y.reshape(rows // rm, rm, d)
    return (y3 * (1.0 + sc)[None] + sh[None]).reshape(rows, d)


def _gate(y, g):
    rm = g.shape[0]
    if rm == 1:
        return y * g
    rows, d = y.shape
    return (y.reshape(rows // rm, rm, d) * g[None]).reshape(rows, d)


def _ada_kernel(c_ref, w_ref, b_ref, o_ref):
    c = c_ref[...]
    s = _silu(c).astype(BF16)
    o_ref[0] = jnp.dot(s, w_ref[0].astype(BF16), preferred_element_type=F32) + b_ref[0]


def ada_modulation(c_all, ada_w, ada_b):
    depth, d, n6 = ada_w.shape
    rows = c_all.shape[0]
    tn = 1536
    return pl.pallas_call(
        _ada_kernel,
        out_shape=jax.ShapeDtypeStruct((depth, rows, n6), F32),
        grid=(depth, n6 // tn),
        in_specs=[pl.BlockSpec((rows, d), lambda l, j: (0, 0)),
                  pl.BlockSpec((1, d, tn), lambda l, j: (l, 0, j)),
                  pl.BlockSpec((1, 1, tn), lambda l, j: (l, 0, j))],
        out_specs=pl.BlockSpec((1, rows, tn), lambda l, j: (l, 0, j)),
        compiler_params=_cparams(("arbitrary", "arbitrary")),
        name="ada_modulation",
    )(c_all, ada_w, ada_b.reshape(depth, 1, n6))


def _pre_kernel(x_ref, sh_ref, sc_ref, g_ref, w_ref, b_ref, o_s5, o_gla, o_conv, o_mlp):
    x = x_ref[0]
    y = _modulate(_rms(x) * g_ref[...], sc_ref[0], sh_ref[0])
    p = jnp.dot(y.astype(BF16), w_ref[...], preferred_element_type=F32) + b_ref[...]
    o_s5[...] = p[:, 0:PW_S5]
    o_gla[...] = p[:, PW_S5:PW_S5 + PW_GLA]
    o_conv[...] = p[:, PW_S5 + PW_GLA:PW_S5 + PW_GLA + PW_CONV]
    o_mlp[...] = p[:, PW_S5 + PW_GLA + PW_CONV:PW_TOTAL]


def pre_mixer(x, sh, sc, g, w, b, tm):
    s, r, d = x.shape
    rm = sh.shape[1]
    widths = (PW_S5, PW_GLA, PW_CONV, PW_MLP)
    return pl.pallas_call(
        _pre_kernel,
        out_shape=[jax.ShapeDtypeStruct((r, s * w_), F32) for w_ in widths],
        grid=(s, r // tm),
        in_specs=[pl.BlockSpec((1, tm, d), lambda b_, i: (b_, i, 0)),
                  pl.BlockSpec((1, rm, d), lambda b_, i: (b_, 0, 0)),
                  pl.BlockSpec((1, rm, d), lambda b_, i: (b_, 0, 0)),
                  pl.BlockSpec((1, d), lambda b_, i: (0, 0)),
                  pl.BlockSpec((d, PW_TOTAL), lambda b_, i: (0, 0)),
                  pl.BlockSpec((1, PW_TOTAL), lambda b_, i: (0, 0))],
        out_specs=[pl.BlockSpec((tm, w_), lambda b_, i: (i, b_)) for w_ in widths],
        compiler_params=_cparams(("arbitrary", "arbitrary")),
        name="pre_mixer",
    )(x, sh, sc, g, w, b)


def _s5_disc_kernel(lr_ref, li_ref, ldt_ref, br_ref, bi_ref, abr_ref, abi_ref, bbr_ref, bbi_ref):
    lr = lr_ref[...]
    li = li_ref[...]
    dt = jnp.exp(ldt_ref[...])
    mag = jnp.exp(lr * dt)
    ang = li * dt
    ab_re = mag * jnp.cos(ang)
    ab_im = mag * jnp.sin(ang)
    den = lr * lr + li * li
    nr = ab_re - 1.0
    f_re = (nr * lr + ab_im * li) / den
    f_im = (ab_im * lr - nr * li) / den
    br = br_ref[...]
    bi = bi_ref[...]
    abr_ref[...] = ab_re
    abi_ref[...] = ab_im
    bbr_ref[...] = f_re * br - f_im * bi
    bbi_ref[...] = f_re * bi + f_im * br


def s5_discretise(a_re, a_im, log_dt, b_re, b_im):
    n = b_re.shape[-1]
    gp = a_re.size
    bc = lambda a: jnp.broadcast_to(a.reshape(gp, 1), (gp, n))
    ldt = jnp.broadcast_to(log_dt[:, None], a_re.shape)
    outs = pl.pallas_call(
        _s5_disc_kernel,
        out_shape=[jax.ShapeDtypeStruct((gp, n), F32)] * 4,
        name="s5_discretise",
    )(bc(a_re), bc(a_im), bc(ldt), b_re.reshape(gp, n), b_im.reshape(gp, n))
    ab_re, ab_im, bb_re, bb_im = outs
    return ab_re[:, 0], ab_im[:, 0], bb_re, bb_im


def _block_diag_in(bb):
    g, p, n = S5_GROUPS, S5_STATE, S5_GROUP
    b3 = bb.reshape(g, p, n)
    eye = jnp.eye(g, dtype=bb.dtype)
    return jnp.einsum('gpn,gh->gnhp', b3, eye).reshape(g * n, g * p)


def _block_diag_out(c):
    g, p, n = S5_GROUPS, S5_STATE, S5_GROUP
    eye = jnp.eye(g, dtype=c.dtype)
    return jnp.einsum('gnp,gh->gphn', c, eye).reshape(g * p, g * n)


def _s5_kernel(u_ref, h0_ref, bblk_ref, cre_ref, cim_ref, ar_ref, ai_ref, d_ref, wglu_ref, bglu_ref, mg_ref,
               o_ref, hT_ref, xs_ref, hs_ref, *, nb, tc):
    i = pl.program_id(0)

    @pl.when(i == 0)
    def _():
        hs_ref[...] = h0_ref[...]

    u = u_ref[...]
    xs_ref[...] = jnp.dot(u.astype(BF16), bblk_ref[...], preferred_element_type=F32)
    ar = jnp.broadcast_to(ar_ref[...], (nb, S5_LANES))
    ai = jnp.broadcast_to(ai_ref[...], (nb, S5_LANES))

    def step(t, carry):
        hr, hi = carry
        row = pl.multiple_of(t * nb, nb)
        xr = xs_ref[pl.ds(row, nb), 0:S5_LANES]
        xi = xs_ref[pl.ds(row, nb), S5_LANES:2 * S5_LANES]
        nr = ar * hr - ai * hi + xr
        ni = ar * hi + ai * hr + xi
        xs_ref[pl.ds(row, nb), 0:S5_LANES] = nr
        xs_ref[pl.ds(row, nb), S5_LANES:2 * S5_LANES] = ni
        return nr, ni

    hr, hi = lax.fori_loop(0, tc, step, (hs_ref[:, 0:S5_LANES], hs_ref[:, S5_LANES:2 * S5_LANES]),
                           unroll=True if tc <= 8 else 4)
    hs_ref[:, 0:S5_LANES] = hr
    hs_ref[:, S5_LANES:2 * S5_LANES] = hi

    y = (jnp.dot(xs_ref[:, 0:S5_LANES].astype(BF16), cre_ref[...], preferred_element_type=F32)
         - jnp.dot(xs_ref[:, S5_LANES:2 * S5_LANES].astype(BF16), cim_ref[...], preferred_element_type=F32))
    y = y + d_ref[...] * u
    y = _gelu_tanh(y)
    y = y * jax.nn.sigmoid(jnp.dot(y.astype(BF16), wglu_ref[...], preferred_element_type=F32) + bglu_ref[...])
    o_ref[...] = (_rms(y) * mg_ref[...]).astype(o_ref.dtype)

    @pl.when(i == pl.num_programs(0) - 1)
    def _():
        hT_ref[...] = hs_ref[...]


def s5_mixer(u, h0, bblk, cre, cim, ar, ai, d, wglu, bglu, mg, nb, tc):
    rows = u.shape[0]
    rc = nb * tc
    full = lambda shape: pl.BlockSpec(shape, lambda i: (0,) * len(shape))
    return pl.pallas_call(
        functools.partial(_s5_kernel, nb=nb, tc=tc),
        out_shape=[jax.ShapeDtypeStruct((rows, S5_WIDTH), BF16),
                   jax.ShapeDtypeStruct((nb, 2 * S5_LANES), F32)],
        grid=(rows // rc,),
        in_specs=[pl.BlockSpec((rc, S5_WIDTH), lambda i: (i, 0)),
                  full((nb, 2 * S5_LANES)),
                  full((S5_WIDTH, 2 * S5_LANES)),
                  full((S5_LANES, S5_WIDTH)), full((S5_LANES, S5_WIDTH)),
                  full((1, S5_LANES)), full((1, S5_LANES)),
                  full((1, S5_WIDTH)), full((S5_WIDTH, S5_WIDTH)), full((1, S5_WIDTH)), full((1, S5_WIDTH))],
        out_specs=[pl.BlockSpec((rc, S5_WIDTH), lambda i: (i, 0)),
                   full((nb, 2 * S5_LANES))],
        scratch_shapes=[pltpu.VMEM((rc, 2 * S5_LANES), F32), pltpu.VMEM((nb, 2 * S5_LANES), F32)],
        compiler_params=_cparams(("arbitrary",)),
        name="s5_mixer",
    )(u, h0, bblk, cre, cim, ar, ai, d, wglu, bglu, mg)


CONV_ROWS = 64


def _conv_kernel(ag_ref, c0_ref, wdw_ref, bdw_ref, lng_ref, lnb_ref, wpw_ref, bpw_ref, mg_ref,
                 o_ref, buf_ref, zc_ref, y_ref, *, nb, tc):
    i = pl.program_id(0)
    hist = CONV_HIST * nb
    rc = nb * tc

    @pl.when(i == 0)
    def _():
        zc_ref[0:hist, :] = c0_ref[...]

    @pl.when(i > 0)
    def _():
        zc_ref[0:hist, :] = zc_ref[rc:rc + hist, :]

    a = ag_ref[:, 0:CONV_DIM]
    g = ag_ref[:, CONV_DIM:2 * CONV_DIM]
    zc_ref[hist:hist + rc, :] = a * jax.nn.sigmoid(g)

    w = wdw_ref[...]

    def tile(j, carry):
        r0 = pl.multiple_of(j * CONV_ROWS, CONV_ROWS)
        acc = jnp.zeros((CONV_ROWS, CONV_DIM), F32)
        for k in range(CONV_WIDTH):
            acc = acc + w[k:k + 1, :] * zc_ref[pl.ds(r0 + k * nb, CONV_ROWS), :]
        y_ref[pl.ds(r0, CONV_ROWS), :] = acc
        return carry

    lax.fori_loop(0, rc // CONV_ROWS, tile, 0)
    y = y_ref[...] + bdw_ref[...]
    y = _silu(_layernorm(y, lng_ref[...], lnb_ref[...]))
    y = jnp.dot(y.astype(BF16), wpw_ref[...], preferred_element_type=F32) + bpw_ref[...]
    o_ref[...] = (_rms(y) * mg_ref[...]).astype(o_ref.dtype)

    @pl.when(i == pl.num_programs(0) - 1)
    def _():
        buf_ref[...] = zc_ref[rc:rc + hist, :]


def conv_mixer(ag, c0, wdw, bdw, lng, lnb, wpw, bpw, mg, nb, tc):
    rows = ag.shape[0]
    rc = nb * tc
    hist = CONV_HIST * nb
    assert rows == rc or tc >= CONV_HIST
    full = lambda shape: pl.BlockSpec(shape, lambda i: (0,) * len(shape))
    return pl.pallas_call(
        functools.partial(_conv_kernel, nb=nb, tc=tc),
        out_shape=[jax.ShapeDtypeStruct((rows, CONV_DIM), BF16),
                   jax.ShapeDtypeStruct((hist, CONV_DIM), F32)],
        grid=(rows // rc,),
        in_specs=[pl.BlockSpec((rc, 2 * CONV_DIM), lambda i: (i, 0)),
                  full((hist, CONV_DIM)), full((CONV_WIDTH, CONV_DIM)),
                  full((1, CONV_DIM)), full((1, CONV_DIM)), full((1, CONV_DIM)),
                  full((CONV_DIM, CONV_DIM)), full((1, CONV_DIM)), full((1, CONV_DIM))],
        out_specs=[pl.BlockSpec((rc, CONV_DIM), lambda i: (i, 0)), full((hist, CONV_DIM))],
        scratch_shapes=[pltpu.VMEM((hist + rc, CONV_DIM), F32), pltpu.VMEM((rc, CONV_DIM), F32)],
        compiler_params=_cparams(("arbitrary",)),
        name="conv_mixer",
    )(ag, c0, wdw, bdw, lng, lnb, wpw, bpw, mg)


def _gmlp_seq_kernel(uv_ref, lng_ref, lnb_ref, wcat_ref, bias_ref, mg_ref, o_ref, *, tt):
    n_chunks = tt // GMLP_CHUNK
    kc = GMLP_HEADS * GMLP_CHUNK
    rowi = lax.broadcasted_iota(jnp.int32, (GMLP_CHUNK, kc), 0)
    coli = lax.broadcasted_iota(jnp.int32, (GMLP_CHUNK, kc), 1)
    wcat = jnp.where((coli & (GMLP_CHUNK - 1)) <= rowi, wcat_ref[...], 0.0).astype(BF16)
    sel = _same_block((kc, GMLP_WIDTH), GMLP_CHUNK, GMLP_HEAD_DIM)
    for c in range(n_chunks):
        rows = slice(c * GMLP_CHUNK, (c + 1) * GMLP_CHUNK)
        u = uv_ref[rows, 0:GMLP_WIDTH]
        v = uv_ref[rows, GMLP_WIDTH:2 * GMLP_WIDTH]
        vn = _layernorm(v, lng_ref[...], lnb_ref[...])
        vbd = jnp.where(sel, jnp.concatenate([vn] * GMLP_HEADS, axis=0), 0.0).astype(BF16)
        mixed = jnp.dot(wcat, vbd, preferred_element_type=F32) + bias_ref[...]
        o_ref[rows, :] = (_rms(u * mixed) * mg_ref[...]).astype(o_ref.dtype)


def gmlp_seq(uv, nseq, lng, lnb, wcat, bias, mg, tt):
    t = uv.shape[0]
    full = lambda shape: pl.BlockSpec(shape, lambda b_, i: (0,) * len(shape))
    return pl.pallas_call(
        functools.partial(_gmlp_seq_kernel, tt=tt),
        out_shape=jax.ShapeDtypeStruct((t, nseq * GMLP_WIDTH), BF16),
        grid=(nseq, t // tt),
        in_specs=[pl.BlockSpec((tt, 2 * GMLP_WIDTH), lambda b_, i: (i, b_)),
                  full((1, GMLP_WIDTH)), full((1, GMLP_WIDTH)),
                  full((GMLP_CHUNK, GMLP_HEADS * GMLP_CHUNK)), full((GMLP_CHUNK, GMLP_WIDTH)),
                  full((1, GMLP_WIDTH))],
        out_specs=pl.BlockSpec((tt, GMLP_WIDTH), lambda b_, i: (i, b_)),
        compiler_params=_cparams(("arbitrary", "arbitrary")),
        name="gmlp_seq",
    )(uv, lng, lnb, wcat, bias, mg)


def _gmlp_short_kernel(uv_ref, lng_ref, lnb_ref, wrow_ref, brow_ref, mg_ref, o_ref, vn_ref, *, nb, t_len):
    u = uv_ref[:, 0:GMLP_WIDTH]
    v = uv_ref[:, GMLP_WIDTH:2 * GMLP_WIDTH]
    vn = _layernorm(v, lng_ref[...], lnb_ref[...])
    vn_ref[...] = vn
    wrow = wrow_ref[...]
    brow = brow_ref[...]
    for t in range(t_len):
        mixed = jnp.zeros((nb, GMLP_WIDTH), F32) + brow[t:t + 1, :]
        for j in range(t + 1):
            mixed = mixed + wrow[t * t_len + j:t * t_len + j + 1, :] * vn[j * nb:(j + 1) * nb, :]
        o = u[t * nb:(t + 1) * nb, :] * mixed
        o_ref[t * nb:(t + 1) * nb, :] = (_rms(o) * mg_ref[...]).astype(o_ref.dtype)


def gmlp_short(uv, lng, lnb, wrow, brow, mg, nb, t_len):
    rows = uv.shape[0]
    return pl.pallas_call(
        functools.partial(_gmlp_short_kernel, nb=nb, t_len=t_len),
        out_shape=[jax.ShapeDtypeStruct((rows, GMLP_WIDTH), BF16),
                   jax.ShapeDtypeStruct((rows, GMLP_WIDTH), F32)],
        compiler_params=pltpu.CompilerParams(vmem_limit_bytes=VMEM_LIMIT),
        name="gmlp_short",
    )(uv, lng, lnb, wrow, brow, mg)


def _split3(x):
    a = x.astype(BF16)
    r1 = x - a.astype(F32)
    b = r1.astype(BF16)
    c = (r1 - b.astype(F32)).astype(BF16)
    return a, b, c


def _dot_exact_rhs(x, m):
    return sum(jnp.dot(t, m, preferred_element_type=F32) for t in _split3(x))


def _dot_exact_lhs(m, x):
    return sum(jnp.dot(m, t, preferred_element_type=F32) for t in _split3(x))


def _gla_tail(o, r, gmean, onorm, mg):
    ms = _dot_exact_rhs(o * o, gmean)
    o = o * lax.rsqrt(ms + EPS) * onorm
    o = o * _silu(r)
    return _rms(o) * mg


def _head_mean_matrix():
    return jnp.where(_same_block((GLA_WIDTH, GLA_WIDTH), GLA_DV, GLA_DV), 1.0 / GLA_DV, 0.0).astype(BF16)


GLA_SEQS = 4


def _gla_seq_kernel(x_ref, s0_ref, wg_ref, bg_ref, onorm_ref, mg_ref, o_ref, sT_ref, s_ref, *, tt):
    i = pl.program_id(1)
    L = GLA_CHUNK
    kw, vw = GLA_KEY_WIDTH, GLA_WIDTH

    @pl.when(i == 0)
    def _():
        s_ref[...] = s0_ref[...]

    tri = (lax.broadcasted_iota(jnp.int32, (L, L), 1) <= lax.broadcasted_iota(jnp.int32, (L, L), 0)).astype(BF16)
    kbd_sel = _same_block((GLA_HEADS * L, kw), L, GLA_DK)
    vbd_sel = _same_block((GLA_HEADS * L, vw), L, GLA_DV)
    causal = ((lax.broadcasted_iota(jnp.int32, (L, GLA_HEADS * L), 1) & (L - 1))
              <= lax.broadcasted_iota(jnp.int32, (L, GLA_HEADS * L), 0))
    s_sel = _same_block((vw, kw), GLA_DV, GLA_DK)
    gmean = _head_mean_matrix()
    scale = GLA_DK ** -0.5
    nt_dims = (((1,), (1,)), ((), ()))
    tn_dims = (((0,), (0,)), ((), ()))

    def chunk(c, carry):
        r0 = pl.multiple_of(c * L, L)
        for g in range(GLA_SEQS):
            x0 = g * PW_GLA
            q = x_ref[pl.ds(r0, L), x0:x0 + kw] * scale
            k = x_ref[pl.ds(r0, L), x0 + kw:x0 + 2 * kw]
            v = x_ref[pl.ds(r0, L), x0 + 2 * kw:x0 + 2 * kw + vw]
            r = x_ref[pl.ds(r0, L), x0 + 2 * kw + vw:x0 + 2 * kw + 2 * vw]
            gl = x_ref[pl.ds(r0, L), x0 + 2 * kw + 2 * vw:x0 + PW_GLA]
            la = _log_sigmoid(jnp.dot(gl.astype(BF16), wg_ref[...], preferred_element_type=F32) + bg_ref[...])
            la = la / GLA_TAU
            bc = _dot_exact_lhs(tri, la)
            b_last = bc[L - 1:L, :]
            qt = (q * jnp.exp(bc)).astype(BF16)
            kt = k * jnp.exp(-bc)
            kbd = jnp.where(kbd_sel, jnp.concatenate([kt] * GLA_HEADS, axis=0), 0.0).astype(BF16)
            att = lax.dot_general(qt, kbd, nt_dims, preferred_element_type=F32)
            att = jnp.where(causal, att, 0.0).astype(BF16)
            vb = v.astype(BF16)
            vbd = jnp.where(vbd_sel, jnp.concatenate([vb] * GLA_HEADS, axis=0), jnp.zeros((), BF16))
            st = s_ref[g]
            o = (jnp.dot(att, vbd, preferred_element_type=F32)
                 + lax.dot_general(qt, st.astype(BF16), nt_dims, preferred_element_type=F32))
            kdec = (k * jnp.exp(b_last - bc)).astype(BF16)
            upd = lax.dot_general(vb, kdec, tn_dims, preferred_element_type=F32)
            s_ref[g] = st * jnp.exp(b_last) + jnp.where(s_sel, upd, 0.0)
            o_ref[pl.ds(r0, L), g * vw:(g + 1) * vw] = _gla_tail(o, r, gmean, onorm_ref[...],
                                                                 mg_ref[...]).astype(o_ref.dtype)
        return carry

    lax.fori_loop(0, tt // L, chunk, 0)

    @pl.when(i == pl.num_programs(1) - 1)
    def _():
        sT_ref[...] = s_ref[...]


def gla_seq(x, nseq, s0, wg, bg, onorm, mg, tt):
    t = x.shape[0]
    g = GLA_SEQS
    assert nseq % g == 0
    full = lambda shape: pl.BlockSpec(shape, lambda b_, i: (0,) * len(shape))
    state = pl.BlockSpec((g, GLA_WIDTH, GLA_KEY_WIDTH), lambda b_, i: (b_, 0, 0))
    return pl.pallas_call(
        functools.partial(_gla_seq_kernel, tt=tt),
        out_shape=[jax.ShapeDtypeStruct((t, nseq * GLA_WIDTH), BF16),
                   jax.ShapeDtypeStruct((nseq, GLA_WIDTH, GLA_KEY_WIDTH), F32)],
        grid=(nseq // g, t // tt),
        in_specs=[pl.BlockSpec((tt, g * PW_GLA), lambda b_, i: (i, b_)),
                  state,
                  full((LANE, GLA_KEY_WIDTH)), full((1, GLA_KEY_WIDTH)),
                  full((1, GLA_WIDTH)), full((1, GLA_WIDTH))],
        out_specs=[pl.BlockSpec((tt, g * GLA_WIDTH), lambda b_, i: (i, b_)), state],
        scratch_shapes=[pltpu.VMEM((g, GLA_WIDTH, GLA_KEY_WIDTH), F32)],
        compiler_params=_cparams(("arbitrary", "arbitrary")),
        name="gla_seq",
    )(x, s0, wg, bg, onorm, mg)


def _gla_rec_kernel(x_ref, s0_ref, ek_ref, ev_ref, wg_ref, bg_ref, onorm_ref, mg_ref, o_ref, sT_ref,
                    *, nb, t_len):
    kw, vw = GLA_KEY_WIDTH, GLA_WIDTH
    hl = GLA_DK * GLA_DV
    sT_ref[...] = s0_ref[...]
    gmean = _head_mean_matrix()
    scale = GLA_DK ** -0.5

    def step(t, carry):
        r0 = pl.multiple_of(t * nb, nb)
        q = x_ref[pl.ds(r0, nb), 0:kw] * scale
        k = x_ref[pl.ds(r0, nb), kw:2 * kw]
        v = x_ref[pl.ds(r0, nb), 2 * kw:2 * kw + vw]
        r = x_ref[pl.ds(r0, nb), 2 * kw + vw:2 * kw + 2 * vw]
        gl = x_ref[pl.ds(r0, nb), 2 * kw + 2 * vw:2 * kw + 2 * vw + LANE]
        la = _log_sigmoid(jnp.dot(gl.astype(BF16), wg_ref[...], preferred_element_type=F32) + bg_ref[...])
        a = jnp.exp(la / GLA_TAU)
        a3 = _split3(a)
        qb = q.astype(BF16)
        kb = k.astype(BF16)
        vb = v.astype(BF16)
        outs = []
        for h in range(GLA_HEADS):
            lanes = slice(h * hl, (h + 1) * hl)
            ek = ek_ref[:, lanes]
            a_e = (jnp.dot(a3[0], ek, preferred_element_type=F32)
                   + jnp.dot(a3[1], ek, preferred_element_type=F32)
                   + jnp.dot(a3[2], ek, preferred_element_type=F32))
            k_e = jnp.dot(kb, ek, preferred_element_type=F32)
            q_e = jnp.dot(qb, ek, preferred_element_type=F32)
            v_e = jnp.dot(vb, ev_ref[:, lanes], preferred_element_type=F32)
            s_new = a_e * sT_ref[:, lanes] + k_e * v_e
            sT_ref[:, lanes] = s_new
            prod = q_e * s_new
            acc = prod[:, 0:LANE]
            for j in range(1, hl // LANE):
                acc = acc + prod[:, j * LANE:(j + 1) * LANE]
            outs.append(acc[:, 0:GLA_DV] + acc[:, GLA_DV:2 * GLA_DV])
        o = jnp.concatenate(outs, axis=1)
        o_ref[pl.ds(r0, nb), :] = _gla_tail(o, r, gmean, onorm_ref[...], mg_ref[...]).astype(o_ref.dtype)
        return carry

    lax.fori_loop(0, t_len, step, 0)


def gla_recurrent(x, s0, ek, ev, wg, bg, onorm, mg, nb, t_len):
    rows = x.shape[0]
    return pl.pallas_call(
        functools.partial(_gla_rec_kernel, nb=nb, t_len=t_len),
        out_shape=[jax.ShapeDtypeStruct((rows, GLA_WIDTH), BF16),
                   jax.ShapeDtypeStruct((nb, GLA_STATE_LANES), F32)],
        compiler_params=pltpu.CompilerParams(vmem_limit_bytes=VMEM_LIMIT),
        name="gla_recurrent",
    )(x, s0, ek, ev, wg, bg, onorm, mg)


def _gla_expanders():
    lane = jnp.arange(GLA_STATE_LANES)
    h = lane // (GLA_DK * GLA_DV)
    dk = (lane // GLA_DV) % GLA_DK
    dv = lane % GLA_DV
    ek = (jnp.arange(GLA_KEY_WIDTH)[:, None] == (h * GLA_DK + dk)[None, :]).astype(BF16)
    ev = (jnp.arange(GLA_WIDTH)[:, None] == (h * GLA_DV + dv)[None, :]).astype(BF16)
    return ek, ev


def _mix_residual(x_ref, m_refs, g1_ref, wout_ref):
    mix = jnp.concatenate([m[...] for m in m_refs], axis=1)
    proj = jnp.dot(mix, wout_ref[...], preferred_element_type=F32)
    return x_ref[0] + _gate(proj, g1_ref[0])


def _swiglu(h, wg_ref, wu_ref, wd_ref, tf, lead=()):
    ff = wg_ref.shape[-1]
    acc = jnp.zeros((h.shape[0], wd_ref.shape[-1]), F32)
    for c in range(ff // tf):
        cols = slice(c * tf, (c + 1) * tf)
        t = (_silu(jnp.dot(h, wg_ref[(*lead, slice(None), cols)], preferred_element_type=F32))
             * jnp.dot(h, wu_ref[(*lead, slice(None), cols)], preferred_element_type=F32))
        acc = acc + jnp.dot(t.astype(BF16), wd_ref[(*lead, cols, slice(None))], preferred_element_type=F32)
    return acc


def _post_dense_kernel(x_ref, m0_ref, m1_ref, m2_ref, m3_ref, g1_ref, sh2_ref, sc2_ref, g2_ref, ng_ref, wout_ref,
                       wg_ref, wu_ref, wd_ref, fg_ref, o_ref, *, final_norm, tf):
    x1 = _mix_residual(x_ref, (m0_ref, m1_ref, m2_ref, m3_ref), g1_ref, wout_ref)
    h = _modulate(_rms(x1) * ng_ref[...], sc2_ref[0], sh2_ref[0]).astype(BF16)
    x2 = x1 + _gate(_swiglu(h, wg_ref, wu_ref, wd_ref, tf), g2_ref[0])
    if final_norm:
        x2 = _rms(x2) * fg_ref[...]
    o_ref[0] = x2


def post_dense(x, mixes, g1, sh2, sc2, g2, ng, wout, wg, wu, wd, fg, tm, tf, final_norm):
    s, r, d = x.shape
    rm = g1.shape[1]
    ff = wg.shape[1]
    mod = pl.BlockSpec((1, rm, d), lambda b_, i: (b_, 0, 0))
    const = lambda shape: pl.BlockSpec(shape, lambda b_, i: (0,) * len(shape))
    resident = lambda shape: pl.BlockSpec(shape, lambda b_, i: (0,) * len(shape), pipeline_mode=pl.Buffered(1))
    mixspec = pl.BlockSpec((tm, 256), lambda b_, i: (i, b_))
    return pl.pallas_call(
        functools.partial(_post_dense_kernel, final_norm=final_norm, tf=tf),
        out_shape=jax.ShapeDtypeStruct((s, r, d), F32),
        grid=(s, r // tm),
        in_specs=[pl.BlockSpec((1, tm, d), lambda b_, i: (b_, i, 0)),
                  mixspec, mixspec, mixspec, mixspec,
                  mod, mod, mod, mod,
                  const((1, d)), resident((d, d)),
                  resident((d, ff)), resident((d, ff)), resident((ff, d)),
                  const((1, d))],
        out_specs=pl.BlockSpec((1, tm, d), lambda b_, i: (b_, i, 0)),
        compiler_params=_cparams(("arbitrary", "arbitrary")),
        name="post_dense",
    )(x, *mixes, g1, sh2, sc2, g2, ng, wout, wg, wu, wd, fg)


ROW_TILE = 8


def _store_row_tiles(ref, x, lead=()):
    rows = x.shape[0]
    for s in range(ROW_TILE):
        ref[(*lead, pl.ds(s, rows, stride=ROW_TILE), slice(None))] = x[:, s * LANE:(s + 1) * LANE]


def _load_row_tiles(ref, rows, lead=()):
    return jnp.concatenate([ref[(*lead, pl.ds(s, rows, stride=ROW_TILE), slice(None))] for s in range(ROW_TILE)],
                           axis=1)


def _route_kernel(x_ref, m0_ref, m1_ref, m2_ref, m3_ref, g1_ref, sh2_ref, sc2_ref, ng_ref, wout_ref, router_ref,
                  *rest):
    x1_ref, h2_ref, route_ref = rest[-3:]
    x1 = _mix_residual(x_ref, (m0_ref, m1_ref, m2_ref, m3_ref), g1_ref, wout_ref)
    x1_ref[0] = x1
    h = _modulate(_rms(x1) * ng_ref[...], sc2_ref[0], sh2_ref[0])
    _store_row_tiles(h2_ref, h)
    h_hi = h.astype(BF16)
    h_lo = (h - h_hi.astype(F32)).astype(BF16)
    w = router_ref[...]
    w_hi = w.astype(BF16)
    w_lo = (w - w_hi.astype(F32)).astype(BF16)
    logits = (jnp.dot(h_hi, w_hi, preferred_element_type=F32) + jnp.dot(h_lo, w_hi, preferred_element_type=F32)
              + jnp.dot(h_hi, w_lo, preferred_element_type=F32))
    lane = lax.broadcasted_iota(jnp.int32, logits.shape, 1).astype(F32)
    neg = jnp.float32(-jnp.inf)
    logits = jnp.where(lane < N_EXPERTS, logits, neg)
    m1 = jnp.max(logits, axis=1, keepdims=True)
    i1 = jnp.min(jnp.where(logits == m1, lane, float(LANE)), axis=1, keepdims=True)
    others = jnp.where(lane == i1, neg, logits)
    m2 = jnp.max(others, axis=1, keepdims=True)
    i2 = jnp.min(jnp.where(others == m2, lane, float(LANE)), axis=1, keepdims=True)
    e2 = jnp.exp(m2 - m1)
    den = 1.0 + e2
    route_ref[...] = (jnp.where(lane == 0.0, i1, 0.0) + jnp.where(lane == 1.0, i2, 0.0)
                      + jnp.where(lane == 2.0, 1.0 / den, 0.0) + jnp.where(lane == 3.0, e2 / den, 0.0))


def moe_route(x, mixes, g1, sh2, sc2, ng, wout, router, tm, row0, shared):
    s, r, d = x.shape
    rm = g1.shape[1]
    nt = r // tm
    blk0 = row0 // tm
    n_total = shared[1].shape[0]
    mod = pl.BlockSpec((1, rm, d), lambda b_, i: (b_, 0, 0))
    const = lambda shape: pl.BlockSpec(shape, lambda b_, i: (0,) * len(shape))
    mixspec = pl.BlockSpec((tm, 256), lambda b_, i: (i, b_))
    in_specs = [pl.BlockSpec((1, tm, d), lambda b_, i: (b_, i, 0)),
                mixspec, mixspec, mixspec, mixspec, mod, mod, mod,
                const((1, d)), const((d, d)), const((d, LANE))]
    args = [x, *mixes, g1, sh2, sc2, ng, wout, router]
    in_specs += [pl.BlockSpec(memory_space=pl.ANY), pl.BlockSpec(memory_space=pl.ANY)]
    aliases = {len(args): 1, len(args) + 1: 2}
    args += list(shared)
    return pl.pallas_call(
        _route_kernel,
        out_shape=[jax.ShapeDtypeStruct((s, r, d), F32),
                   jax.ShapeDtypeStruct((n_total * ROW_TILE, LANE), F32),
                   jax.ShapeDtypeStruct((n_total, LANE), F32)],
        grid=(s, nt),
        in_specs=in_specs,
        out_specs=[pl.BlockSpec((1, tm, d), lambda b_, i: (b_, i, 0)),
                   pl.BlockSpec((tm * ROW_TILE, LANE), lambda b_, i: (blk0 + b_ * nt + i, 0)),
                   pl.BlockSpec((tm, LANE), lambda b_, i: (blk0 + b_ * nt + i, 0))],
        input_output_aliases=aliases,
        compiler_params=_cparams(("arbitrary", "arbitrary")),
        name="moe_route",
    )(*args)


def _route_tables(route, tg, n_tiles):
    n_total = route.shape[0]
    flat_e = route[:, 0:2].astype(jnp.int32).reshape(-1)
    order = jnp.argsort(flat_e, stable=True).astype(jnp.int32)
    counts = jnp.sum(flat_e[:, None] == jnp.arange(N_EXPERTS, dtype=jnp.int32)[None, :], axis=0).astype(jnp.int32)
    tiles_per = (counts + tg - 1) // tg
    tile_end = jnp.cumsum(tiles_per)
    n_used = tile_end[-1]
    tile_id = jnp.arange(n_tiles, dtype=jnp.int32)
    tile_ok = tile_id < n_used
    tile_e = jnp.sum(jnp.minimum(tile_id, n_used - 1)[:, None] >= tile_end[None, :], axis=1).astype(jnp.int32)
    sort_start = jnp.cumsum(counts) - counts
    done = (tile_id - (tile_end - tiles_per)[tile_e]) * tg
    n_valid = jnp.where(tile_ok, jnp.clip(counts[tile_e] - done, 0, tg), 0).astype(jnp.int32)
    tile_start = jnp.where(tile_ok, sort_start[tile_e] + done, 0).astype(jnp.int32)
    pad = jnp.zeros((tg,), jnp.int32)
    src = jnp.concatenate([(order >> 1) * ROW_TILE, pad])
    dst = jnp.concatenate([((order & 1) * n_total + (order >> 1)) * ROW_TILE, pad])
    return tile_e, n_valid, tile_start, src, dst


DMA_UNROLL = 8


def _experts_kernel(te_ref, nv_ref, ts_ref, src_ref, dst_ref, h2_hbm, wg_ref, wu_ref, wd_ref, out_hbm,
                    xbuf, obuf, gsem, ssem, *, tg, n_tiles, tf):
    j = pl.program_id(0)
    slot = lax.rem(j, 2)
    other = 1 - slot
    ok = nv_ref[j] > 0

    def row_tile(buf, s_, r):
        return buf.at[s_, pl.ds(pl.multiple_of(r * ROW_TILE, ROW_TILE), ROW_TILE), :]

    def gather_row(tile, s_, r, priority=0):
        row = pl.multiple_of(src_ref[ts_ref[tile] + r], ROW_TILE)
        pltpu.make_async_copy(h2_hbm.at[pl.ds(row, ROW_TILE), :], row_tile(xbuf, s_, r),
                              gsem.at[s_]).start(priority=priority)

    def scatter_row(tile, s_, r, priority=0):
        row = pl.multiple_of(dst_ref[ts_ref[tile] + r], ROW_TILE)
        pltpu.make_async_copy(row_tile(obuf, s_, r), out_hbm.at[pl.ds(row, ROW_TILE), :],
                              ssem.at[s_]).start(priority=priority)

    def full_tile(issue_row, tile, s_):
        def body(r8, c):
            for u in range(DMA_UNROLL):
                issue_row(tile, s_, r8 * DMA_UNROLL + u, priority=u % 2)
            return c

        lax.fori_loop(0, tg // DMA_UNROLL, body, 0)

    def gather(tile, s_):
        full_tile(gather_row, tile, s_)

    def scatter(tile, s_):
        n = nv_ref[tile]

        @pl.when(n == tg)
        def _():
            full_tile(scatter_row, tile, s_)

        @pl.when(n < tg)
        def _():
            def body(r, c):
                scatter_row(tile, s_, r)
                return c

            lax.fori_loop(0, n, body, 0)

    def wait_all(buf, sem, s_):
        pltpu.make_async_copy(buf.at[s_], buf.at[s_], sem.at[s_]).wait()

    def wait_scatter(tile, s_):
        n = nv_ref[tile]

        @pl.when(n == tg)
        def _():
            wait_all(obuf, ssem, s_)

        @pl.when(n < tg)
        def _():
            def body(r, c):
                pltpu.make_async_copy(obuf.at[s_, pl.ds(0, ROW_TILE), :], out_hbm.at[pl.ds(0, ROW_TILE), :],
                                      ssem.at[s_]).wait()
                return c

            lax.fori_loop(0, n, body, 0)

    @pl.when(jnp.logical_and(j == 0, ok))
    def _():
        gather(0, 0)

    @pl.when(ok)
    def _():
        wait_all(xbuf, gsem, slot)

    nxt = jnp.minimum(j + 1, n_tiles - 1)

    @pl.when(jnp.logical_and(j + 1 < n_tiles, nv_ref[nxt] > 0))
    def _():
        gather(j + 1, other)

    @pl.when(j >= 2)
    def _():
        wait_scatter(j - 2, slot)

    @pl.when(ok)
    def _():
        x = _load_row_tiles(xbuf, tg, lead=(slot,)).astype(BF16)
        _store_row_tiles(obuf, _swiglu(x, wg_ref, wu_ref, wd_ref, tf, lead=(0,)), lead=(slot,))
        scatter(j, slot)

    @pl.when(j == n_tiles - 1)
    def _():
        wait_scatter(j - 1, other)
        wait_scatter(j, slot)


def moe_experts(h2, tables, wg, wu, wd, tg, n_tiles, tf):
    n_exp, d, ff = wg.shape
    assert d == ROW_TILE * LANE and n_tiles >= 2
    tile_e, n_valid, tile_start, src, dst = tables
    wspec = lambda shape: pl.BlockSpec(shape, lambda j, te, *_: (te[j], 0, 0))
    grid_spec = pltpu.PrefetchScalarGridSpec(
        num_scalar_prefetch=5,
        grid=(n_tiles,),
        in_specs=[pl.BlockSpec(memory_space=pl.ANY), wspec((1, d, ff)), wspec((1, d, ff)), wspec((1, ff, d))],
        out_specs=pl.BlockSpec(memory_space=pl.ANY),
        scratch_shapes=[pltpu.VMEM((2, tg * ROW_TILE, LANE), F32), pltpu.VMEM((2, tg * ROW_TILE, LANE), F32),
                        pltpu.SemaphoreType.DMA((2,)), pltpu.SemaphoreType.DMA((2,))])
    return pl.pallas_call(
        functools.partial(_experts_kernel, tg=tg, n_tiles=n_tiles, tf=tf),
        out_shape=jax.ShapeDtypeStruct((2 * h2.shape[0], LANE), F32),
        grid_spec=grid_spec,
        compiler_params=_cparams(("arbitrary",)),
        name="moe_experts",
    )(tile_e, n_valid, tile_start, src, dst, h2, wg, wu, wd)


def _combine_kernel(x1_ref, y0_ref, y1_ref, route_ref, g2_ref, fg_ref, o_ref, *, final_norm):
    r = route_ref[...]
    rows = r.shape[0]
    f = r[:, 2:3] * _load_row_tiles(y0_ref, rows, lead=(0,)) + r[:, 3:4] * _load_row_tiles(y1_ref, rows, lead=(0,))
    x2 = x1_ref[0] + _gate(f, g2_ref[0])
    if final_norm:
        x2 = _rms(x2) * fg_ref[...]
    o_ref[0] = x2


def moe_combine(x1, y, route, g2, fg, tm, row0, final_norm):
    s, r, d = x1.shape
    rm = g2.shape[1]
    nt = r // tm
    blk0 = row0 // tm
    return pl.pallas_call(
        functools.partial(_combine_kernel, final_norm=final_norm),
        out_shape=jax.ShapeDtypeStruct((s, r, d), F32),
        grid=(s, nt),
        in_specs=[pl.BlockSpec((1, tm, d), lambda b_, i: (b_, i, 0)),
                  pl.BlockSpec((1, tm * ROW_TILE, LANE), lambda b_, i: (0, blk0 + b_ * nt + i, 0)),
                  pl.BlockSpec((1, tm * ROW_TILE, LANE), lambda b_, i: (1, blk0 + b_ * nt + i, 0)),
                  pl.BlockSpec((tm, LANE), lambda b_, i: (blk0 + b_ * nt + i, 0)),
                  pl.BlockSpec((1, rm, d), lambda b_, i: (b_, 0, 0)),
                  pl.BlockSpec((1, d), lambda b_, i: (0, 0))],
        out_specs=pl.BlockSpec((1, tm, d), lambda b_, i: (b_, i, 0)),
        compiler_params=_cparams(("arbitrary", "arbitrary")),
        name="moe_combine",
    )(x1, y, y, route, g2, fg)


def _reorder_w_in(w_in, b_in):
    cut = PW_S5 + 128 + 128 + 256 + 256 + GLA_GATE_RANK
    pad = LANE - GLA_GATE_RANK
    w = jnp.concatenate([w_in[:, :cut], jnp.zeros((w_in.shape[0], pad), w_in.dtype), w_in[:, cut:]], axis=1)
    b = jnp.concatenate([b_in[:cut], jnp.zeros((pad,), b_in.dtype), b_in[cut:]])
    return w.astype(BF16), b.reshape(1, PW_TOTAL)


def _row(a):
    return a.reshape(1, -1)


class _Branch:
    def __init__(self, nseq, nb, t_len, seq_form, tm_pre, tm_post, tc, tt_seq, row0):
        self.nseq, self.nb, self.t_len, self.seq_form = nseq, nb, t_len, seq_form
        self.tm_pre, self.tm_post, self.tc, self.tt_seq, self.row0 = tm_pre, tm_post, tc, tt_seq, row0


def _layer_params(W, i):
    row = _row
    p = {}
    p['w_in'], p['b_in'] = _reorder_w_in(W['w_in'][i], W['b_in'][i])
    p['norm_g'] = row(W['norm_mix_g'][i])
    mg = W['merge_g'][i]
    p['mg'] = [row(mg[k * 256:(k + 1) * 256]) for k in range(4)]
    ab_re, ab_im, bb_re, bb_im = s5_discretise(W['s5_a_re'][i], W['s5_a_im'][i], W['s5_log_dt'][i],
                                               W['s5_b_re'][i], W['s5_b_im'][i])
    p['s5'] = (jnp.concatenate([_block_diag_in(bb_re), _block_diag_in(bb_im)], axis=1).astype(BF16),
               _block_diag_out(W['s5_c_re'][i]).astype(BF16), _block_diag_out(W['s5_c_im'][i]).astype(BF16),
               row(ab_re), row(ab_im), row(W['s5_d'][i]), W['s5_w_glu'][i].astype(BF16), row(W['s5_b_glu'][i]))
    wg2 = jnp.zeros((LANE, GLA_KEY_WIDTH), F32).at[:GLA_GATE_RANK].set(W['gla_w_gate2'][i]).astype(BF16)
    p['gla'] = (wg2, row(W['gla_b_gate2'][i]), row(W['gla_onorm_g'][i]))
    p['conv'] = (W['conv_w_dw'][i], row(W['conv_b_dw'][i]), row(W['conv_ln_g'][i]), row(W['conv_ln_b'][i]),
                 W['conv_w_pw'][i].astype(BF16), row(W['conv_b_pw'][i]))
    p['gmlp_ln'] = (row(W['gmlp_ln_g'][i]), row(W['gmlp_ln_b'][i]))
    p['gmlp_ws'], p['gmlp_bs'] = W['gmlp_w_s'][i], W['gmlp_b_s'][i]
    return p


def _mixers(x, mods, states, p, i, br, out):
    nseq, nb, t_len, seq_form = br.nseq, br.nb, br.t_len, br.seq_form
    s5_re0, s5_im0, gla0, conv0 = states
    new_re, new_im, new_gla, new_conv, new_v = out
    mg = p['mg']
    p_s5, p_gla, p_conv, p_mlp = pre_mixer(x, mods[0], mods[1], p['norm_g'], p['w_in'], p['b_in'], br.tm_pre)
    tmaj = lambda a, w_: a.reshape(t_len * nb, w_)

    h0 = jnp.concatenate([s5_re0[i].reshape(nb, S5_LANES), s5_im0[i].reshape(nb, S5_LANES)], axis=1)
    o_s5, h_t = s5_mixer(tmaj(p_s5, PW_S5), h0, *p['s5'], mg[0], nb, br.tc)
    new_re.append(h_t[:, :S5_LANES].reshape(nb, S5_GROUPS, S5_STATE))
    new_im.append(h_t[:, S5_LANES:].reshape(nb, S5_GROUPS, S5_STATE))

    if seq_form:
        eye = jnp.eye(GLA_HEADS, dtype=F32)
        s0 = jnp.einsum('bhkv,hg->bhvgk', gla0[i], eye).reshape(nseq, GLA_WIDTH, GLA_KEY_WIDTH)
        o_gla, s_t = gla_seq(p_gla, nseq, s0, *p['gla'], mg[1], br.tt_seq)
        s5d = s_t.reshape(nseq, GLA_HEADS, GLA_DV, GLA_HEADS, GLA_DK)
        new_gla.append(jnp.stack([jnp.swapaxes(s5d[:, h, :, h, :], 1, 2) for h in range(GLA_HEADS)], axis=1))
    else:
        ek, ev = _gla_expanders()
        o_gla, s_t = gla_recurrent(p_gla, gla0[i].reshape(nb, GLA_STATE_LANES), ek, ev, *p['gla'], mg[1], nb, t_len)
        new_gla.append(s_t.reshape(nb, GLA_HEADS, GLA_DK, GLA_DV))

    c0 = jnp.transpose(conv0[i], (1, 0, 2)).reshape(CONV_HIST * nb, CONV_DIM)
    o_conv, buf = conv_mixer(tmaj(p_conv, PW_CONV), c0, *p['conv'], mg[2], nb, br.tc)
    new_conv.append(jnp.transpose(buf.reshape(CONV_HIST, nb, CONV_DIM), (1, 0, 2)))

    ws, bs = p['gmlp_ws'], p['gmlp_bs']
    if seq_form:
        wcat = jnp.transpose(ws, (1, 0, 2)).reshape(GMLP_CHUNK, GMLP_HEADS * GMLP_CHUNK)
        bias = jnp.repeat(bs.T, GMLP_HEAD_DIM, axis=1)
        o_mlp = gmlp_seq(p_mlp, nseq, *p['gmlp_ln'], wcat, bias, mg[3], br.tt_seq)
        new_v.append(None)
    else:
        tri = jnp.tril(jnp.ones((t_len, t_len), F32))
        wrow = jnp.repeat(jnp.transpose(ws[:, :t_len, :t_len] * tri[None], (1, 2, 0)).reshape(t_len * t_len, GMLP_HEADS),
                          GMLP_HEAD_DIM, axis=1)
        brow = jnp.repeat(bs[:, :t_len].T, GMLP_HEAD_DIM, axis=1)
        o_mlp, vn = gmlp_short(p_mlp, *p['gmlp_ln'], wrow, brow, mg[3], nb, t_len)
        new_v.append(vn)

    per_seq = lambda a: a.reshape(t_len, nb * 256) if seq_form else a
    return [per_seq(o_s5), per_seq(o_gla), per_seq(o_conv), per_seq(o_mlp)]


FF_TILE = 1408
EXPERT_ROWS = 512


def _channel_mixer(xs, mixes, mods, W, i, branches, last):
    ng, wout, fg = _row(W['norm_ffn_g'][i]), W['w_out'][i].astype(BF16), _row(W['final_norm_g'])
    j = i // 2
    if i % 2 == 0:
        wg, wu, wd = (W['ffn_w_gate'][j].astype(BF16), W['ffn_w_up'][j].astype(BF16),
                      W['ffn_w_down'][j].astype(BF16))
        return [post_dense(x, mx, m[2], m[3], m[4], m[5], ng, wout, wg, wu, wd, fg, br.tm_post, FF_TILE, last)
                for x, mx, m, br in zip(xs, mixes, mods, branches)]
    n_total = sum(x.shape[0] * x.shape[1] for x in xs)
    tg = EXPERT_ROWS
    n_tiles = 2 * n_total // tg + N_EXPERTS
    router = jnp.zeros((D_MODEL, LANE), F32).at[:, :N_EXPERTS].set(W['moe_router'][j])
    h2, route = jnp.zeros((n_total * ROW_TILE, LANE), F32), jnp.zeros((n_total, LANE), F32)
    x1s = []
    for x, mx, m, br in zip(xs, mixes, mods, branches):
        x1, h2, route = moe_route(x, mx, m[2], m[3], m[4], ng, wout, router, br.tm_post, br.row0, (h2, route))
        x1s.append(x1)
    tables = _route_tables(route, tg, n_tiles)
    y = moe_experts(h2, tables, W['moe_w_gate'][j].astype(BF16), W['moe_w_up'][j].astype(BF16),
                    W['moe_w_down'][j].astype(BF16), tg, n_tiles, FF_TILE)
    y = y.reshape(2, n_total * ROW_TILE, LANE)
    return [moe_combine(x1, y, route, m[5], fg, br.tm_post, br.row0, last)
            for x1, m, br in zip(x1s, mods, branches)]


def kernel(x_prompt, x_sample, c_prompt, c_sample, state_s5_re, state_s5_im, state_gla, cache_conv, ada_w, ada_b, norm_mix_g, norm_ffn_g, w_in, b_in, s5_a_re, s5_a_im, s5_log_dt, s5_b_re, s5_b_im, s5_c_re, s5_c_im, s5_d, s5_w_glu, s5_b_glu, gla_w_gate2, gla_b_gate2, gla_onorm_g, conv_w_dw, conv_b_dw, conv_ln_g, conv_ln_b, conv_w_pw, conv_b_pw, gmlp_ln_g, gmlp_ln_b, gmlp_w_s, gmlp_b_s, merge_g, w_out, ffn_w_gate, ffn_w_up, ffn_w_down, moe_router, moe_w_gate, moe_w_up, moe_w_down, final_norm_g):
    W = dict(norm_mix_g=norm_mix_g, norm_ffn_g=norm_ffn_g, w_in=w_in, b_in=b_in, s5_a_re=s5_a_re, s5_a_im=s5_a_im,
             s5_log_dt=s5_log_dt, s5_b_re=s5_b_re, s5_b_im=s5_b_im, s5_c_re=s5_c_re, s5_c_im=s5_c_im, s5_d=s5_d,
             s5_w_glu=s5_w_glu, s5_b_glu=s5_b_glu, gla_w_gate2=gla_w_gate2, gla_b_gate2=gla_b_gate2,
             gla_onorm_g=gla_onorm_g, conv_w_dw=conv_w_dw, conv_b_dw=conv_b_dw, conv_ln_g=conv_ln_g,
             conv_ln_b=conv_ln_b, conv_w_pw=conv_w_pw, conv_b_pw=conv_b_pw, gmlp_ln_g=gmlp_ln_g,
             gmlp_ln_b=gmlp_ln_b, gmlp_w_s=gmlp_w_s, gmlp_b_s=gmlp_b_s, merge_g=merge_g, w_out=w_out,
             ffn_w_gate=ffn_w_gate, ffn_w_up=ffn_w_up, ffn_w_down=ffn_w_down, moe_router=moe_router,
             moe_w_gate=moe_w_gate, moe_w_up=moe_w_up, moe_w_down=moe_w_down, final_norm_g=final_norm_g)
    depth = w_in.shape[0]
    bp, tp, d = x_prompt.shape
    bs, ts, _ = x_sample.shape

    m = ada_modulation(jnp.concatenate([c_prompt, c_sample], axis=0), ada_w, ada_b)
    mods_p = [[m[i, :bp, k * d:(k + 1) * d].reshape(bp, 1, d) for k in range(6)] for i in range(depth)]
    mods_s = [[m[i, bp:, k * d:(k + 1) * d].reshape(1, bs, d) for k in range(6)] for i in range(depth)]

    z_re = jnp.zeros((depth, bp, S5_GROUPS, S5_STATE), F32)
    z_gla = jnp.zeros((depth, bp, GLA_HEADS, GLA_DK, GLA_DV), F32)
    z_conv = jnp.zeros((depth, bp, CONV_HIST, CONV_DIM), x_prompt.dtype)
    states = [(z_re, z_re, z_gla, z_conv), (state_s5_re, state_s5_im, state_gla, cache_conv)]
    branches = [_Branch(nseq=bp, nb=bp, t_len=tp, seq_form=True, tm_pre=min(512, tp), tm_post=min(512, tp),
                        tc=min(128, tp), tt_seq=min(512, tp), row0=0),
                _Branch(nseq=1, nb=bs, t_len=ts, seq_form=False, tm_pre=ts * bs, tm_post=min(512, ts * bs),
                        tc=ts, tt_seq=None, row0=bp * tp)]
    xs = [x_prompt, jnp.transpose(x_sample, (1, 0, 2)).reshape(1, ts * bs, d)]
    outs = [([], [], [], [], []), ([], [], [], [], [])]
    for i in range(depth):
        mods = [mods_p[i], mods_s[i]]
        params = _layer_params(W, i)
        mixes = [_mixers(x, m, st_, params, i, br, o)
                 for x, m, st_, br, o in zip(xs, mods, states, branches, outs)]
        xs = _channel_mixer(xs, mixes, mods, W, i, branches, i == depth - 1)

    y_p = xs[0]
    y_s = jnp.transpose(xs[1].reshape(ts, bs, d), (1, 0, 2))
    p_re, p_im, p_gla, p_conv, _ = outs[0]
    s_re, s_im, s_gla, s_conv, s_v = outs[1]
    s_v = [jnp.transpose(v.reshape(ts, bs, GMLP_WIDTH), (1, 0, 2)) for v in s_v]
    st = jnp.stack
    return (y_p, y_s, st(p_re), st(p_im), st(p_gla), st(p_conv),
            st(s_re), st(s_im), st(s_gla), st(s_conv), st(s_v))
```

```python
import functools
import math

import jax
import jax.numpy as jnp
from jax import lax
from jax.experimental import pallas as pl
from jax.experimental.pallas import tpu as pltpu

D_MODEL = 1024
S5_WIDTH = 256
S5_GROUP = 16
S5_GROUPS = 16
S5_STATE = 64
S5_LANES = S5_GROUPS * S5_STATE
GLA_HEADS = 4
GLA_DV = 64
GLA_DK = 32
GLA_WIDTH = 256
GLA_KEY_WIDTH = 128
GLA_GATE_RANK = 16
GLA_TAU = 16.0
GLA_CHUNK = 64
GLA_STATE_LANES = GLA_HEADS * GLA_DK * GLA_DV
CONV_DIM = 256
CONV_WIDTH = 31
CONV_HIST = CONV_WIDTH - 1
GMLP_WIDTH = 256
GMLP_HEADS = 4
GMLP_HEAD_DIM = 64
GMLP_CHUNK = 128
D_FF = 2816
N_EXPERTS = 8
EPS = 1e-6

LANE = 128
PW_S5 = 256
PW_GLA = 128 + 128 + 256 + 256 + LANE
PW_CONV = 512
PW_MLP = 512
PW_TOTAL = PW_S5 + PW_GLA + PW_CONV + PW_MLP
VMEM_LIMIT = 56 * 1024 * 1024

F32 = jnp.float32
BF16 = jnp.bfloat16
HI = lax.Precision.HIGHEST


def _cparams(sem):
    return pltpu.CompilerParams(dimension_semantics=sem, vmem_limit_bytes=VMEM_LIMIT)


def _rms(x):
    return x * lax.rsqrt(jnp.mean(x * x, axis=-1, keepdims=True) + EPS)


def _layernorm(x, g, b):
    mu = jnp.mean(x, axis=-1, keepdims=True)
    xc = x - mu
    var = jnp.mean(xc * xc, axis=-1, keepdims=True)
    return xc * lax.rsqrt(var + EPS) * g + b


def _silu(x):
    return x * jax.nn.sigmoid(x)


def _gelu_tanh(x):
    return 0.5 * x * (1.0 + jnp.tanh(math.sqrt(2.0 / math.pi) * (x + 0.044715 * (x * x * x))))


def _log_sigmoid(x):
    return jnp.minimum(x, 0.0) - jnp.log(1.0 + jnp.exp(-jnp.abs(x)))


def _same_block(shape, row_block, col_block):
    r = lax.broadcasted_iota(jnp.int32, shape, 0) >> (row_block.bit_length() - 1)
    c = lax.broadcasted_iota(jnp.int32, shape, 1) >> (col_block.bit_length() - 1)
    return r == c


def _modulate(y, sc, sh):
    rm = sc.shape[0]
    if rm == 1:
        return y * (1.0 + sc) + sh
    rows, d = y.shape
    y3 = y.reshape(rows // rm, rm, d)
    return (y3 * (1.0 + sc)[None] + sh[None]).reshape(rows, d)


def _gate(y, g):
    rm = g.shape[0]
    if rm == 1:
        return y * g
    rows, d = y.shape
    return (y.reshape(rows // rm, rm, d) * g[None]).reshape(rows, d)


def _ada_kernel(c_ref, w_ref, b_ref, o_ref):
    c = c_ref[...]
    s = _silu(c).astype(BF16)
    o_ref[0] = jnp.dot(s, w_ref[0].astype(BF16), preferred_element_type=F32) + b_ref[0]


def ada_modulation(c_all, ada_w, ada_b):
    depth, d, n6 = ada_w.shape
    rows = c_all.shape[0]
    tn = 1536
    return pl.pallas_call(
        _ada_kernel,
        out_shape=jax.ShapeDtypeStruct((depth, rows, n6), F32),
        grid=(depth, n6 // tn),
        in_specs=[pl.BlockSpec((rows, d), lambda l, j: (0, 0)),
                  pl.BlockSpec((1, d, tn), lambda l, j: (l, 0, j)),
                  pl.BlockSpec((1, 1, tn), lambda l, j: (l, 0, j))],
        out_specs=pl.BlockSpec((1, rows, tn), lambda l, j: (l, 0, j)),
        compiler_params=_cparams(("arbitrary", "arbitrary")),
        name="ada_modulation",
    )(c_all, ada_w, ada_b.reshape(depth, 1, n6))


def _pre_kernel(x_ref, sh_ref, sc_ref, g_ref, w_ref, b_ref, o_s5, o_gla, o_conv, o_mlp):
    x = x_ref[0]
    y = _modulate(_rms(x) * g_ref[...], sc_ref[0], sh_ref[0])
    p = jnp.dot(y.astype(BF16), w_ref[...], preferred_element_type=F32) + b_ref[...]
    o_s5[...] = p[:, 0:PW_S5]
    o_gla[...] = p[:, PW_S5:PW_S5 + PW_GLA]
    o_conv[...] = p[:, PW_S5 + PW_GLA:PW_S5 + PW_GLA + PW_CONV]
    o_mlp[...] = p[:, PW_S5 + PW_GLA + PW_CONV:PW_TOTAL]


def pre_mixer(x, sh, sc, g, w, b, tm):
    s, r, d = x.shape
    rm = sh.shape[1]
    widths = (PW_S5, PW_GLA, PW_CONV, PW_MLP)
    return pl.pallas_call(
        _pre_kernel,
        out_shape=[jax.ShapeDtypeStruct((r, s * w_), F32) for w_ in widths],
        grid=(s, r // tm),
        in_specs=[pl.BlockSpec((1, tm, d), lambda b_, i: (b_, i, 0)),
                  pl.BlockSpec((1, rm, d), lambda b_, i: (b_, 0, 0)),
                  pl.BlockSpec((1, rm, d), lambda b_, i: (b_, 0, 0)),
                  pl.BlockSpec((1, d), lambda b_, i: (0, 0)),
                  pl.BlockSpec((d, PW_TOTAL), lambda b_, i: (0, 0)),
                  pl.BlockSpec((1, PW_TOTAL), lambda b_, i: (0, 0))],
        out_specs=[pl.BlockSpec((tm, w_), lambda b_, i: (i, b_)) for w_ in widths],
        compiler_params=_cparams(("arbitrary", "arbitrary")),
        name="pre_mixer",
    )(x, sh, sc, g, w, b)


def _s5_disc_kernel(lr_ref, li_ref, ldt_ref, br_ref, bi_ref, abr_ref, abi_ref, bbr_ref, bbi_ref):
    lr = lr_ref[...]
    li = li_ref[...]
    dt = jnp.exp(ldt_ref[...])
    mag = jnp.exp(lr * dt)
    ang = li * dt
    ab_re = mag * jnp.cos(ang)
    ab_im = mag * jnp.sin(ang)
    den = lr * lr + li * li
    nr = ab_re - 1.0
    f_re = (nr * lr + ab_im * li) / den
    f_im = (ab_im * lr - nr * li) / den
    br = br_ref[...]
    bi = bi_ref[...]
    abr_ref[...] = ab_re
    abi_ref[...] = ab_im
    bbr_ref[...] = f_re * br - f_im * bi
    bbi_ref[...] = f_re * bi + f_im * br


def s5_discretise(a_re, a_im, log_dt, b_re, b_im):
    n = b_re.shape[-1]
    gp = a_re.size
    bc = lambda a: jnp.broadcast_to(a.reshape(gp, 1), (gp, n))
    ldt = jnp.broadcast_to(log_dt[:, None], a_re.shape)
    outs = pl.pallas_call(
        _s5_disc_kernel,
        out_shape=[jax.ShapeDtypeStruct((gp, n), F32)] * 4,
        name="s5_discretise",
    )(bc(a_re), bc(a_im), bc(ldt), b_re.reshape(gp, n), b_im.reshape(gp, n))
    ab_re, ab_im, bb_re, bb_im = outs
    return ab_re[:, 0], ab_im[:, 0], bb_re, bb_im


def _block_diag_in(bb):
    g, p, n = S5_GROUPS, S5_STATE, S5_GROUP
    b3 = bb.reshape(g, p, n)
    eye = jnp.eye(g, dtype=bb.dtype)
    return jnp.einsum('gpn,gh->gnhp', b3, eye).reshape(g * n, g * p)


def _block_diag_out(c):
    g, p, n = S5_GROUPS, S5_STATE, S5_GROUP
    eye = jnp.eye(g, dtype=c.dtype)
    return jnp.einsum('gnp,gh->gphn', c, eye).reshape(g * p, g * n)


def _to_time_major(x_ref, cols, tm_ref, row0, nsl):
    rt = x_ref.shape[0]
    w = x_ref.shape[1] // nsl
    start, width = cols
    for l in range(nsl):
        for h in range(width // LANE):
            c0 = l * w + start + h * LANE
            tm_ref[h, pl.ds(row0 + l, rt, stride=nsl), :] = x_ref[:, c0:c0 + LANE]


def _from_time_major(tm_ref, o_ref, nsl):
    rt = o_ref.shape[0]
    nh = tm_ref.shape[0]
    for l in range(nsl):
        piece = jnp.concatenate([tm_ref[h, pl.ds(l, rt, stride=nsl), :] for h in range(nh)], axis=1)
        o_ref[:, l * nh * LANE:(l + 1) * nh * LANE] = piece.astype(o_ref.dtype)


def _lane_tiles(tm_ref, rows=slice(None)):
    return jnp.concatenate([tm_ref[h, rows, :] for h in range(tm_ref.shape[0])], axis=1)


def _set_lane_tiles(tm_ref, x):
    for h in range(tm_ref.shape[0]):
        tm_ref[h] = x[:, h * LANE:(h + 1) * LANE]


def _s5_kernel(u_ref, h0_ref, bblk_ref, cre_ref, cim_ref, ar_ref, ai_ref, d_ref, wglu_ref, bglu_ref, mg_ref,
               o_ref, hT_ref, xs_ref, hs_ref, tm_ref, *, nb, tc, nsl):
    i = pl.program_id(0)

    @pl.when(i == 0)
    def _():
        hs_ref[...] = h0_ref[...]

    _to_time_major(u_ref, (0, S5_WIDTH), tm_ref, 0, nsl)
    u = _lane_tiles(tm_ref)
    xs_ref[...] = jnp.dot(u.astype(BF16), bblk_ref[...], preferred_element_type=F32)
    ar = jnp.broadcast_to(ar_ref[...], (nb, S5_LANES))
    ai = jnp.broadcast_to(ai_ref[...], (nb, S5_LANES))

    def step(t, carry):
        hr, hi = carry
        row = pl.multiple_of(t * nb, nb)
        xr = xs_ref[pl.ds(row, nb), 0:S5_LANES]
        xi = xs_ref[pl.ds(row, nb), S5_LANES:2 * S5_LANES]
        nr = ar * hr - ai * hi + xr
        ni = ar * hi + ai * hr + xi
        xs_ref[pl.ds(row, nb), 0:S5_LANES] = nr
        xs_ref[pl.ds(row, nb), S5_LANES:2 * S5_LANES] = ni
        return nr, ni

    hr, hi = lax.fori_loop(0, tc, step, (hs_ref[:, 0:S5_LANES], hs_ref[:, S5_LANES:2 * S5_LANES]),
                           unroll=True if tc <= 8 else 4)
    hs_ref[:, 0:S5_LANES] = hr
    hs_ref[:, S5_LANES:2 * S5_LANES] = hi

    y = (jnp.dot(xs_ref[:, 0:S5_LANES].astype(BF16), cre_ref[...], preferred_element_type=F32)
         - jnp.dot(xs_ref[:, S5_LANES:2 * S5_LANES].astype(BF16), cim_ref[...], preferred_element_type=F32))
    y = y + d_ref[...] * u
    y = _gelu_tanh(y)
    y = y * jax.nn.sigmoid(jnp.dot(y.astype(BF16), wglu_ref[...], preferred_element_type=F32) + bglu_ref[...])
    _set_lane_tiles(tm_ref, _rms(y) * mg_ref[...])
    _from_time_major(tm_ref, o_ref, nsl)

    @pl.when(i == pl.num_programs(0) - 1)
    def _():
        hT_ref[...] = hs_ref[...]


def s5_mixer(u, h0, bblk, cre, cim, ar, ai, d, wglu, bglu, mg, nb, tc, nsl):
    rows = u.shape[0]
    rc = nb * tc
    rt = rc // nsl
    full = lambda shape: pl.BlockSpec(shape, lambda i: (0,) * len(shape))
    return pl.pallas_call(
        functools.partial(_s5_kernel, nb=nb, tc=tc, nsl=nsl),
        out_shape=[jax.ShapeDtypeStruct((rows, nsl * S5_WIDTH), BF16),
                   jax.ShapeDtypeStruct((nb, 2 * S5_LANES), F32)],
        grid=(rows // rt,),
        in_specs=[pl.BlockSpec((rt, nsl * S5_WIDTH), lambda i: (i, 0)),
                  full((nb, 2 * S5_LANES)),
                  full((S5_WIDTH, 2 * S5_LANES)),
                  full((S5_LANES, S5_WIDTH)), full((S5_LANES, S5_WIDTH)),
                  full((1, S5_LANES)), full((1, S5_LANES)),
                  full((1, S5_WIDTH)), full((S5_WIDTH, S5_WIDTH)), full((1, S5_WIDTH)), full((1, S5_WIDTH))],
        out_specs=[pl.BlockSpec((rt, nsl * S5_WIDTH), lambda i: (i, 0)),
                   full((nb, 2 * S5_LANES))],
        scratch_shapes=[pltpu.VMEM((rc, 2 * S5_LANES), F32), pltpu.VMEM((nb, 2 * S5_LANES), F32),
                        pltpu.VMEM((S5_WIDTH // LANE, rc, LANE), F32)],
        compiler_params=_cparams(("arbitrary",)),
        name="s5_mixer",
    )(u, h0, bblk, cre, cim, ar, ai, d, wglu, bglu, mg)


CONV_ROWS = 64


def _conv_kernel(ag_ref, c0_ref, wdw_ref, bdw_ref, lng_ref, lnb_ref, wpw_ref, bpw_ref, mg_ref,
                 o_ref, buf_ref, zc_ref, y_ref, *, nb, tc, nsl):
    i = pl.program_id(0)
    hist = CONV_HIST * nb
    rc = nb * tc
    rt = rc // nsl
    n_lt = CONV_DIM // LANE

    @pl.when(i == 0)
    def _():
        for h in range(n_lt):
            zc_ref[h, 0:hist, :] = c0_ref[:, h * LANE:(h + 1) * LANE]

    @pl.when(i > 0)
    def _():
        for h in range(n_lt):
            zc_ref[h, 0:hist, :] = zc_ref[h, rc:rc + hist, :]

    w_seq = 2 * CONV_DIM
    for l in range(nsl):
        a = ag_ref[:, l * w_seq:l * w_seq + CONV_DIM]
        g = ag_ref[:, l * w_seq + CONV_DIM:(l + 1) * w_seq]
        z = a * jax.nn.sigmoid(g)
        for h in range(n_lt):
            zc_ref[h, pl.ds(hist + l, rt, stride=nsl), :] = z[:, h * LANE:(h + 1) * LANE]

    w = wdw_ref[...]

    def tile(j, carry):
        r0 = pl.multiple_of(j * CONV_ROWS, CONV_ROWS)
        for h in range(n_lt):
            acc = jnp.zeros((CONV_ROWS, LANE), F32)
            for k in range(CONV_WIDTH):
                acc = acc + w[k:k + 1, h * LANE:(h + 1) * LANE] * zc_ref[h, pl.ds(r0 + k * nb, CONV_ROWS), :]
            y_ref[h, pl.ds(r0, CONV_ROWS), :] = acc
        return carry

    lax.fori_loop(0, rc // CONV_ROWS, tile, 0)
    y = _lane_tiles(y_ref) + bdw_ref[...]
    y = _silu(_layernorm(y, lng_ref[...], lnb_ref[...]))
    y = jnp.dot(y.astype(BF16), wpw_ref[...], preferred_element_type=F32) + bpw_ref[...]
    _set_lane_tiles(y_ref, _rms(y) * mg_ref[...])
    _from_time_major(y_ref, o_ref, nsl)

    @pl.when(i == pl.num_programs(0) - 1)
    def _():
        buf_ref[...] = _lane_tiles(zc_ref, slice(rc, rc + hist))


def conv_mixer(ag, c0, wdw, bdw, lng, lnb, wpw, bpw, mg, nb, tc, nsl):
    rows = ag.shape[0]
    rc = nb * tc
    rt = rc // nsl
    hist = CONV_HIST * nb
    assert rows == rt or tc >= CONV_HIST
    n_lt = CONV_DIM // LANE
    full = lambda shape: pl.BlockSpec(shape, lambda i: (0,) * len(shape))
    return pl.pallas_call(
        functools.partial(_conv_kernel, nb=nb, tc=tc, nsl=nsl),
        out_shape=[jax.ShapeDtypeStruct((rows, nsl * CONV_DIM), BF16),
                   jax.ShapeDtypeStruct((hist, CONV_DIM), F32)],
        grid=(rows // rt,),
        in_specs=[pl.BlockSpec((rt, nsl * 2 * CONV_DIM), lambda i: (i, 0)),
                  full((hist, CONV_DIM)), full((CONV_WIDTH, CONV_DIM)),
                  full((1, CONV_DIM)), full((1, CONV_DIM)), full((1, CONV_DIM)),
                  full((CONV_DIM, CONV_DIM)), full((1, CONV_DIM)), full((1, CONV_DIM))],
        out_specs=[pl.BlockSpec((rt, nsl * CONV_DIM), lambda i: (i, 0)), full((hist, CONV_DIM))],
        scratch_shapes=[pltpu.VMEM((n_lt, hist + rc, LANE), F32), pltpu.VMEM((n_lt, rc, LANE), F32)],
        compiler_params=_cparams(("arbitrary",)),
        name="conv_mixer",
    )(ag, c0, wdw, bdw, lng, lnb, wpw, bpw, mg)


def _gmlp_seq_kernel(uv_ref, lng_ref, lnb_ref, wcat_ref, bias_ref, mg_ref, o_ref, *, tt):
    n_chunks = tt // GMLP_CHUNK
    kc = GMLP_HEADS * GMLP_CHUNK
    rowi = lax.broadcasted_iota(jnp.int32, (GMLP_CHUNK, kc), 0)
    coli = lax.broadcasted_iota(jnp.int32, (GMLP_CHUNK, kc), 1)
    wcat = jnp.where((coli & (GMLP_CHUNK - 1)) <= rowi, wcat_ref[...], 0.0).astype(BF16)
    sel = _same_block((kc, GMLP_WIDTH), GMLP_CHUNK, GMLP_HEAD_DIM)
    for c in range(n_chunks):
        rows = slice(c * GMLP_CHUNK, (c + 1) * GMLP_CHUNK)
        u = uv_ref[rows, 0:GMLP_WIDTH]
        v = uv_ref[rows, GMLP_WIDTH:2 * GMLP_WIDTH]
        vn = _layernorm(v, lng_ref[...], lnb_ref[...])
        vbd = jnp.where(sel, jnp.concatenate([vn] * GMLP_HEADS, axis=0), 0.0).astype(BF16)
        mixed = jnp.dot(wcat, vbd, preferred_element_type=F32) + bias_ref[...]
        o_ref[rows, :] = (_rms(u * mixed) * mg_ref[...]).astype(o_ref.dtype)


def gmlp_seq(uv, nseq, lng, lnb, wcat, bias, mg, tt):
    t = uv.shape[0]
    full = lambda shape: pl.BlockSpec(shape, lambda b_, i: (0,) * len(shape))
    return pl.pallas_call(
        functools.partial(_gmlp_seq_kernel, tt=tt),
        out_shape=jax.ShapeDtypeStruct((t, nseq * GMLP_WIDTH), BF16),
        grid=(nseq, t // tt),
        in_specs=[pl.BlockSpec((tt, 2 * GMLP_WIDTH), lambda b_, i: (i, b_)),
                  full((1, GMLP_WIDTH)), full((1, GMLP_WIDTH)),
                  full((GMLP_CHUNK, GMLP_HEADS * GMLP_CHUNK)), full((GMLP_CHUNK, GMLP_WIDTH)),
                  full((1, GMLP_WIDTH))],
        out_specs=pl.BlockSpec((tt, GMLP_WIDTH), lambda b_, i: (i, b_)),
        compiler_params=_cparams(("arbitrary", "arbitrary")),
        name="gmlp_seq",
    )(uv, lng, lnb, wcat, bias, mg)


def _gmlp_short_kernel(uv_ref, lng_ref, lnb_ref, wrow_ref, brow_ref, mg_ref, o_ref, vn_ref, *, nb, t_len):
    u = uv_ref[:, 0:GMLP_WIDTH]
    v = uv_ref[:, GMLP_WIDTH:2 * GMLP_WIDTH]
    vn = _layernorm(v, lng_ref[...], lnb_ref[...])
    vn_ref[...] = vn
    wrow = wrow_ref[...]
    brow = brow_ref[...]
    for t in range(t_len):
        mixed = jnp.zeros((nb, GMLP_WIDTH), F32) + brow[t:t + 1, :]
        for j in range(t + 1):
            mixed = mixed + wrow[t * t_len + j:t * t_len + j + 1, :] * vn[j * nb:(j + 1) * nb, :]
        o = u[t * nb:(t + 1) * nb, :] * mixed
        o_ref[t * nb:(t + 1) * nb, :] = (_rms(o) * mg_ref[...]).astype(o_ref.dtype)


def gmlp_short(uv, lng, lnb, wrow, brow, mg, nb, t_len):
    rows = uv.shape[0]
    return pl.pallas_call(
        functools.partial(_gmlp_short_kernel, nb=nb, t_len=t_len),
        out_shape=[jax.ShapeDtypeStruct((rows, GMLP_WIDTH), BF16),
                   jax.ShapeDtypeStruct((rows, GMLP_WIDTH), F32)],
        compiler_params=pltpu.CompilerParams(vmem_limit_bytes=VMEM_LIMIT),
        name="gmlp_short",
    )(uv, lng, lnb, wrow, brow, mg)


def _split3(x):
    a = x.astype(BF16)
    r1 = x - a.astype(F32)
    b = r1.astype(BF16)
    c = (r1 - b.astype(F32)).astype(BF16)
    return a, b, c


def _dot_exact_rhs(x, m):
    return sum(jnp.dot(t, m, preferred_element_type=F32) for t in _split3(x))


def _dot_exact_lhs(m, x):
    return sum(jnp.dot(m, t, preferred_element_type=F32) for t in _split3(x))


def _gla_tail(o, r, gmean, onorm, mg):
    ms = _dot_exact_rhs(o * o, gmean)
    o = o * lax.rsqrt(ms + EPS) * onorm
    o = o * _silu(r)
    return _rms(o) * mg


def _head_mean_matrix():
    return jnp.where(_same_block((GLA_WIDTH, GLA_WIDTH), GLA_DV, GLA_DV), 1.0 / GLA_DV, 0.0).astype(BF16)


GLA_SEQS = 4


def _gla_seq_kernel(x_ref, s0_ref, wg_ref, bg_ref, onorm_ref, mg_ref, o_ref, sT_ref,
                    s_ref, qt_ref, kt_ref, kd_ref, dl_ref, oacc_ref, *, tt):
    i = pl.program_id(1)
    L = GLA_CHUNK
    n_ch = tt // L
    kw, vw = GLA_KEY_WIDTH, GLA_WIDTH

    @pl.when(i == 0)
    def _():
        s_ref[...] = s0_ref[...]

    rows_i = lax.broadcasted_iota(jnp.int32, (tt, tt), 0)
    cols_i = lax.broadcasted_iota(jnp.int32, (tt, tt), 1)
    shift = L.bit_length() - 1
    same_chunk = (rows_i >> shift) == (cols_i >> shift)
    tri = jnp.logical_and(same_chunk, cols_i <= rows_i).astype(BF16)
    chunk_sum = same_chunk.astype(BF16)
    chunk_rows = (lax.broadcasted_iota(jnp.int32, (n_ch, tt), 0)
                  == (lax.broadcasted_iota(jnp.int32, (n_ch, tt), 1) >> shift)).astype(BF16)
    kbd_sel = _same_block((GLA_HEADS * L, kw), L, GLA_DK)
    vbd_sel = _same_block((GLA_HEADS * L, vw), L, GLA_DV)
    causal = ((lax.broadcasted_iota(jnp.int32, (L, GLA_HEADS * L), 1) & (L - 1))
              <= lax.broadcasted_iota(jnp.int32, (L, GLA_HEADS * L), 0))
    s_sel = _same_block((vw, kw), GLA_DV, GLA_DK)
    gmean = _head_mean_matrix()
    scale = GLA_DK ** -0.5
    nt_dims = (((1,), (1,)), ((), ()))
    tn_dims = (((0,), (0,)), ((), ()))
    zero = jnp.zeros((), BF16)

    for g in range(GLA_SEQS):
        x0 = g * PW_GLA
        q = x_ref[:, x0:x0 + kw] * scale
        k = x_ref[:, x0 + kw:x0 + 2 * kw]
        gl = x_ref[:, x0 + 2 * kw + 2 * vw:x0 + PW_GLA]
        la = _log_sigmoid(jnp.dot(gl.astype(BF16), wg_ref[...], preferred_element_type=F32) + bg_ref[...])
        la = la / GLA_TAU
        bc = _dot_exact_lhs(tri, la)
        b_end = _dot_exact_lhs(chunk_sum, la)
        qt_ref[g] = (q * jnp.exp(bc)).astype(BF16)
        kt_ref[g] = (k * jnp.exp(-bc)).astype(BF16)
        kd_ref[g] = (k * jnp.exp(b_end - bc)).astype(BF16)
        dl_ref[g] = jnp.exp(_dot_exact_lhs(chunk_rows, la))

    def chunk(c, carry):
        r0 = pl.multiple_of(c * L, L)
        for g in range(GLA_SEQS):
            x0 = g * PW_GLA
            qt = qt_ref[g, pl.ds(r0, L), :]
            kt = kt_ref[g, pl.ds(r0, L), :]
            kdec = kd_ref[g, pl.ds(r0, L), :]
            vb = x_ref[pl.ds(r0, L), x0 + 2 * kw:x0 + 2 * kw + vw].astype(BF16)
            kbd = jnp.where(kbd_sel, jnp.concatenate([kt] * GLA_HEADS, axis=0), zero)
            att = lax.dot_general(qt, kbd, nt_dims, preferred_element_type=F32)
            att = jnp.where(causal, att, 0.0).astype(BF16)
            vbd = jnp.where(vbd_sel, jnp.concatenate([vb] * GLA_HEADS, axis=0), zero)
            st = s_ref[g]
            oacc_ref[g, pl.ds(r0, L), :] = (jnp.dot(att, vbd, preferred_element_type=F32)
                                            + lax.dot_general(qt, st.astype(BF16), nt_dims,
                                                              preferred_element_type=F32))
            upd = lax.dot_general(vb, kdec, tn_dims, preferred_element_type=F32)
            s_ref[g] = st * dl_ref[g, pl.ds(c, 1), :] + jnp.where(s_sel, upd, 0.0)
        return carry

    lax.fori_loop(0, n_ch, chunk, 0)

    for g in range(GLA_SEQS):
        x0 = g * PW_GLA
        r = x_ref[:, x0 + 2 * kw + vw:x0 + 2 * kw + 2 * vw]
        o_ref[:, g * vw:(g + 1) * vw] = _gla_tail(oacc_ref[g], r, gmean, onorm_ref[...],
                                                  mg_ref[...]).astype(o_ref.dtype)

    @pl.when(i == pl.num_programs(1) - 1)
    def _():
        sT_ref[...] = s_ref[...]


def gla_seq(x, nseq, s0, wg, bg, onorm, mg, tt):
    t = x.shape[0]
    g = GLA_SEQS
    assert nseq % g == 0
    full = lambda shape: pl.BlockSpec(shape, lambda b_, i: (0,) * len(shape))
    state = pl.BlockSpec((g, GLA_WIDTH, GLA_KEY_WIDTH), lambda b_, i: (b_, 0, 0))
    return pl.pallas_call(
        functools.partial(_gla_seq_kernel, tt=tt),
        out_shape=[jax.ShapeDtypeStruct((t, nseq * GLA_WIDTH), BF16),
                   jax.ShapeDtypeStruct((nseq, GLA_WIDTH, GLA_KEY_WIDTH), F32)],
        grid=(nseq // g, t // tt),
        in_specs=[pl.BlockSpec((tt, g * PW_GLA), lambda b_, i: (i, b_)),
                  state,
                  full((LANE, GLA_KEY_WIDTH)), full((1, GLA_KEY_WIDTH)),
                  full((1, GLA_WIDTH)), full((1, GLA_WIDTH))],
        out_specs=[pl.BlockSpec((tt, g * GLA_WIDTH), lambda b_, i: (i, b_)), state],
        scratch_shapes=[pltpu.VMEM((g, GLA_WIDTH, GLA_KEY_WIDTH), F32),
                        pltpu.VMEM((g, tt, GLA_KEY_WIDTH), BF16), pltpu.VMEM((g, tt, GLA_KEY_WIDTH), BF16),
                        pltpu.VMEM((g, tt, GLA_KEY_WIDTH), BF16),
                        pltpu.VMEM((g, tt // GLA_CHUNK, GLA_KEY_WIDTH), F32),
                        pltpu.VMEM((g, tt, GLA_WIDTH), F32)],
        compiler_params=_cparams(("arbitrary", "arbitrary")),
        name="gla_seq",
    )(x, s0, wg, bg, onorm, mg)


def _gla_rec_kernel(x_ref, s0_ref, ek_ref, ev_ref, wg_ref, bg_ref, onorm_ref, mg_ref, o_ref, sT_ref,
                    *, nb, t_len):
    kw, vw = GLA_KEY_WIDTH, GLA_WIDTH
    hl = GLA_DK * GLA_DV
    sT_ref[...] = s0_ref[...]
    gmean = _head_mean_matrix()
    scale = GLA_DK ** -0.5

    def step(t, carry):
        r0 = pl.multiple_of(t * nb, nb)
        q = x_ref[pl.ds(r0, nb), 0:kw] * scale
        k = x_ref[pl.ds(r0, nb), kw:2 * kw]
        v = x_ref[pl.ds(r0, nb), 2 * kw:2 * kw + vw]
        r = x_ref[pl.ds(r0, nb), 2 * kw + vw:2 * kw + 2 * vw]
        gl = x_ref[pl.ds(r0, nb), 2 * kw + 2 * vw:2 * kw + 2 * vw + LANE]
        la = _log_sigmoid(jnp.dot(gl.astype(BF16), wg_ref[...], preferred_element_type=F32) + bg_ref[...])
        a = jnp.exp(la / GLA_TAU)
        a3 = _split3(a)
        qb = q.astype(BF16)
        kb = k.astype(BF16)
        vb = v.astype(BF16)
        outs = []
        for h in range(GLA_HEADS):
            lanes = slice(h * hl, (h + 1) * hl)
            ek = ek_ref[:, lanes]
            a_e = (jnp.dot(a3[0], ek, preferred_element_type=F32)
                   + jnp.dot(a3[1], ek, preferred_element_type=F32)
                   + jnp.dot(a3[2], ek, preferred_element_type=F32))
            k_e = jnp.dot(kb, ek, preferred_element_type=F32)
            q_e = jnp.dot(qb, ek, preferred_element_type=F32)
            v_e = jnp.dot(vb, ev_ref[:, lanes], preferred_element_type=F32)
            s_new = a_e * sT_ref[:, lanes] + k_e * v_e
            sT_ref[:, lanes] = s_new
            prod = q_e * s_new
            acc = prod[:, 0:LANE]
            for j in range(1, hl // LANE):
                acc = acc + prod[:, j * LANE:(j + 1) * LANE]
            outs.append(acc[:, 0:GLA_DV] + acc[:, GLA_DV:2 * GLA_DV])
        o = jnp.concatenate(outs, axis=1)
        o_ref[pl.ds(r0, nb), :] = _gla_tail(o, r, gmean, onorm_ref[...], mg_ref[...]).astype(o_ref.dtype)
        return carry

    lax.fori_loop(0, t_len, step, 0)


def gla_recurrent(x, s0, ek, ev, wg, bg, onorm, mg, nb, t_len):
    rows = x.shape[0]
    return pl.pallas_call(
        functools.partial(_gla_rec_kernel, nb=nb, t_len=t_len),
        out_shape=[jax.ShapeDtypeStruct((rows, GLA_WIDTH), BF16),
                   jax.ShapeDtypeStruct((nb, GLA_STATE_LANES), F32)],
        compiler_params=pltpu.CompilerParams(vmem_limit_bytes=VMEM_LIMIT),
        name="gla_recurrent",
    )(x, s0, ek, ev, wg, bg, onorm, mg)


def _gla_expanders():
    lane = jnp.arange(GLA_STATE_LANES)
    h = lane // (GLA_DK * GLA_DV)
    dk = (lane // GLA_DV) % GLA_DK
    dv = lane % GLA_DV
    ek = (jnp.arange(GLA_KEY_WIDTH)[:, None] == (h * GLA_DK + dk)[None, :]).astype(BF16)
    ev = (jnp.arange(GLA_WIDTH)[:, None] == (h * GLA_DV + dv)[None, :]).astype(BF16)
    return ek, ev


def _mix_residual(x_ref, m_refs, g1_ref, wout_ref):
    mix = jnp.concatenate([m[...] for m in m_refs], axis=1)
    proj = jnp.dot(mix, wout_ref[...], preferred_element_type=F32)
    return x_ref[0] + _gate(proj, g1_ref[0])


def _swiglu(h, wg_ref, wu_ref, wd_ref, tf, lead=()):
    ff = wg_ref.shape[-1]
    acc = jnp.zeros((h.shape[0], wd_ref.shape[-1]), F32)
    for c in range(ff // tf):
        cols = slice(c * tf, (c + 1) * tf)
        t = (_silu(jnp.dot(h, wg_ref[(*lead, slice(None), cols)], preferred_element_type=F32))
             * jnp.dot(h, wu_ref[(*lead, slice(None), cols)], preferred_element_type=F32))
        acc = acc + jnp.dot(t.astype(BF16), wd_ref[(*lead, cols, slice(None))], preferred_element_type=F32)
    return acc


def _post_dense_kernel(x_ref, m0_ref, m1_ref, m2_ref, m3_ref, g1_ref, sh2_ref, sc2_ref, g2_ref, ng_ref, wout_ref,
                       wg_ref, wu_ref, wd_ref, fg_ref, o_ref, *, final_norm, tf):
    x1 = _mix_residual(x_ref, (m0_ref, m1_ref, m2_ref, m3_ref), g1_ref, wout_ref)
    h = _modulate(_rms(x1) * ng_ref[...], sc2_ref[0], sh2_ref[0]).astype(BF16)
    x2 = x1 + _gate(_swiglu(h, wg_ref, wu_ref, wd_ref, tf), g2_ref[0])
    if final_norm:
        x2 = _rms(x2) * fg_ref[...]
    o_ref[0] = x2


def post_dense(x, mixes, g1, sh2, sc2, g2, ng, wout, wg, wu, wd, fg, tm, tf, final_norm):
    s, r, d = x.shape
    rm = g1.shape[1]
    ff = wg.shape[1]
    mod = pl.BlockSpec((1, rm, d), lambda b_, i: (b_, 0, 0))
    const = lambda shape: pl.BlockSpec(shape, lambda b_, i: (0,) * len(shape))
    resident = lambda shape: pl.BlockSpec(shape, lambda b_, i: (0,) * len(shape), pipeline_mode=pl.Buffered(1))
    mixspec = pl.BlockSpec((tm, 256), lambda b_, i: (i, b_))
    return pl.pallas_call(
        functools.partial(_post_dense_kernel, final_norm=final_norm, tf=tf),
        out_shape=jax.ShapeDtypeStruct((s, r, d), F32),
        grid=(s, r // tm),
        in_specs=[pl.BlockSpec((1, tm, d), lambda b_, i: (b_, i, 0)),
                  mixspec, mixspec, mixspec, mixspec,
                  mod, mod, mod, mod,
                  const((1, d)), resident((d, d)),
                  resident((d, ff)), resident((d, ff)), resident((ff, d)),
                  const((1, d))],
        out_specs=pl.BlockSpec((1, tm, d), lambda b_, i: (b_, i, 0)),
        compiler_params=_cparams(("arbitrary", "arbitrary")),
        name="post_dense",
    )(x, *mixes, g1, sh2, sc2, g2, ng, wout, wg, wu, wd, fg)


ROW_TILE = 8


def _store_row_tiles(ref, x, lead=()):
    rows = x.shape[0]
    for s in range(ROW_TILE):
        ref[(*lead, pl.ds(s, rows, stride=ROW_TILE), slice(None))] = x[:, s * LANE:(s + 1) * LANE]


def _load_row_tiles(ref, rows, lead=()):
    return jnp.concatenate([ref[(*lead, pl.ds(s, rows, stride=ROW_TILE), slice(None))] for s in range(ROW_TILE)],
                           axis=1)


def _route_kernel(x_ref, m0_ref, m1_ref, m2_ref, m3_ref, g1_ref, sh2_ref, sc2_ref, ng_ref, wout_ref, router_ref,
                  *rest):
    x1_ref, h2_ref, route_ref = rest[-3:]
    x1 = _mix_residual(x_ref, (m0_ref, m1_ref, m2_ref, m3_ref), g1_ref, wout_ref)
    x1_ref[0] = x1
    h = _modulate(_rms(x1) * ng_ref[...], sc2_ref[0], sh2_ref[0])
    _store_row_tiles(h2_ref, h)
    h_hi = h.astype(BF16)
    h_lo = (h - h_hi.astype(F32)).astype(BF16)
    w = router_ref[...]
    w_hi = w.astype(BF16)
    w_lo = (w - w_hi.astype(F32)).astype(BF16)
    logits = (jnp.dot(h_hi, w_hi, preferred_element_type=F32) + jnp.dot(h_lo, w_hi, preferred_element_type=F32)
              + jnp.dot(h_hi, w_lo, preferred_element_type=F32))
    lane = lax.broadcasted_iota(jnp.int32, logits.shape, 1).astype(F32)
    neg = jnp.float32(-jnp.inf)
    logits = jnp.where(lane < N_EXPERTS, logits, neg)
    m1 = jnp.max(logits, axis=1, keepdims=True)
    i1 = jnp.min(jnp.where(logits == m1, lane, float(LANE)), axis=1, keepdims=True)
    others = jnp.where(lane == i1, neg, logits)
    m2 = jnp.max(others, axis=1, keepdims=True)
    i2 = jnp.min(jnp.where(others == m2, lane, float(LANE)), axis=1, keepdims=True)
    e2 = jnp.exp(m2 - m1)
    den = 1.0 + e2
    route_ref[...] = (jnp.where(lane == 0.0, i1, 0.0) + jnp.where(lane == 1.0, i2, 0.0)
                      + jnp.where(lane == 2.0, 1.0 / den, 0.0) + jnp.where(lane == 3.0, e2 / den, 0.0))


def moe_route(x, mixes, g1, sh2, sc2, ng, wout, router, tm, row0, shared):
    s, r, d = x.shape
    rm = g1.shape[1]
    nt = r // tm
    blk0 = row0 // tm
    n_total = shared[1].shape[0]
    mod = pl.BlockSpec((1, rm, d), lambda b_, i: (b_, 0, 0))
    const = lambda shape: pl.BlockSpec(shape, lambda b_, i: (0,) * len(shape))
    mixspec = pl.BlockSpec((tm, 256), lambda b_, i: (i, b_))
    in_specs = [pl.BlockSpec((1, tm, d), lambda b_, i: (b_, i, 0)),
                mixspec, mixspec, mixspec, mixspec, mod, mod, mod,
                const((1, d)), const((d, d)), const((d, LANE))]
    args = [x, *mixes, g1, sh2, sc2, ng, wout, router]
    in_specs += [pl.BlockSpec(memory_space=pl.ANY), pl.BlockSpec(memory_space=pl.ANY)]
    aliases = {len(args): 1, len(args) + 1: 2}
    args += list(shared)
    return pl.pallas_call(
        _route_kernel,
        out_shape=[jax.ShapeDtypeStruct((s, r, d), F32),
                   jax.ShapeDtypeStruct((n_total * ROW_TILE, LANE), F32),
                   jax.ShapeDtypeStruct((n_total, LANE), F32)],
        grid=(s, nt),
        in_specs=in_specs,
        out_specs=[pl.BlockSpec((1, tm, d), lambda b_, i: (b_, i, 0)),
                   pl.BlockSpec((tm * ROW_TILE, LANE), lambda b_, i: (blk0 + b_ * nt + i, 0)),
                   pl.BlockSpec((tm, LANE), lambda b_, i: (blk0 + b_ * nt + i, 0))],
        input_output_aliases=aliases,
        compiler_params=_cparams(("arbitrary", "arbitrary")),
        name="moe_route",
    )(*args)


def _route_tables(route, tg, n_tiles):
    n_total = route.shape[0]
    flat_e = route[:, 0:2].astype(jnp.int32).reshape(-1)
    order = jnp.argsort(flat_e, stable=True).astype(jnp.int32)
    counts = jnp.sum(flat_e[:, None] == jnp.arange(N_EXPERTS, dtype=jnp.int32)[None, :], axis=0).astype(jnp.int32)
    tiles_per = (counts + tg - 1) // tg
    tile_end = jnp.cumsum(tiles_per)
    n_used = tile_end[-1]
    tile_id = jnp.arange(n_tiles, dtype=jnp.int32)
    tile_ok = tile_id < n_used
    tile_e = jnp.sum(jnp.minimum(tile_id, n_used - 1)[:, None] >= tile_end[None, :], axis=1).astype(jnp.int32)
    sort_start = jnp.cumsum(counts) - counts
    done = (tile_id - (tile_end - tiles_per)[tile_e]) * tg
    n_valid = jnp.where(tile_ok, jnp.clip(counts[tile_e] - done, 0, tg), 0).astype(jnp.int32)
    tile_start = jnp.where(tile_ok, sort_start[tile_e] + done, 0).astype(jnp.int32)
    pad = jnp.zeros((tg,), jnp.int32)
    src = jnp.concatenate([(order >> 1) * ROW_TILE, pad])
    dst = jnp.concatenate([((order & 1) * n_total + (order >> 1)) * ROW_TILE, pad])
    return tile_e, n_valid, tile_start, src, dst


DMA_UNROLL = 8


def _experts_kernel(te_ref, nv_ref, ts_ref, src_ref, dst_ref, h2_hbm, wg_ref, wu_ref, wd_ref, out_hbm,
                    xbuf, obuf, gsem, ssem, *, tg, n_tiles, tf):
    j = pl.program_id(0)
    slot = lax.rem(j, 2)
    other = 1 - slot
    ok = nv_ref[j] > 0

    def row_tile(buf, s_, r):
        return buf.at[s_, pl.ds(pl.multiple_of(r * ROW_TILE, ROW_TILE), ROW_TILE), :]

    def gather_row(tile, s_, r, priority=0):
        row = pl.multiple_of(src_ref[ts_ref[tile] + r], ROW_TILE)
        pltpu.make_async_copy(h2_hbm.at[pl.ds(row, ROW_TILE), :], row_tile(xbuf, s_, r),
                              gsem.at[s_]).start(priority=priority)

    def scatter_row(tile, s_, r, priority=0):
        row = pl.multiple_of(dst_ref[ts_ref[tile] + r], ROW_TILE)
        pltpu.make_async_copy(row_tile(obuf, s_, r), out_hbm.at[pl.ds(row, ROW_TILE), :],
                              ssem.at[s_]).start(priority=priority)

    def full_tile(issue_row, tile, s_):
        def body(r8, c):
            for u in range(DMA_UNROLL):
                issue_row(tile, s_, r8 * DMA_UNROLL + u, priority=u % 2)
            return c

        lax.fori_loop(0, tg // DMA_UNROLL, body, 0)

    def gather(tile, s_):
        full_tile(gather_row, tile, s_)

    def scatter(tile, s_):
        n = nv_ref[tile]

        @pl.when(n == tg)
        def _():
            full_tile(scatter_row, tile, s_)

        @pl.when(n < tg)
        def _():
            def body(r, c):
                scatter_row(tile, s_, r)
                return c

            lax.fori_loop(0, n, body, 0)

    def wait_all(buf, sem, s_):
        pltpu.make_async_copy(buf.at[s_], buf.at[s_], sem.at[s_]).wait()

    def wait_scatter(tile, s_):
        n = nv_ref[tile]

        @pl.when(n == tg)
        def _():
            wait_all(obuf, ssem, s_)

        @pl.when(n < tg)
        def _():
            def body(r, c):
                pltpu.make_async_copy(obuf.at[s_, pl.ds(0, ROW_TILE), :], out_hbm.at[pl.ds(0, ROW_TILE), :],
                                      ssem.at[s_]).wait()
                return c

            lax.fori_loop(0, n, body, 0)

    @pl.when(jnp.logical_and(j == 0, ok))
    def _():
        gather(0, 0)

    @pl.when(ok)
    def _():
        wait_all(xbuf, gsem, slot)

    nxt = jnp.minimum(j + 1, n_tiles - 1)

    @pl.when(jnp.logical_and(j + 1 < n_tiles, nv_ref[nxt] > 0))
    def _():
        gather(j + 1, other)

    @pl.when(j >= 2)
    def _():
        wait_scatter(j - 2, slot)

    @pl.when(ok)
    def _():
        x = _load_row_tiles(xbuf, tg, lead=(slot,)).astype(BF16)
        _store_row_tiles(obuf, _swiglu(x, wg_ref, wu_ref, wd_ref, tf, lead=(0,)), lead=(slot,))
        scatter(j, slot)

    @pl.when(j == n_tiles - 1)
    def _():
        wait_scatter(j - 1, other)
        wait_scatter(j, slot)


def moe_experts(h2, tables, wg, wu, wd, tg, n_tiles, tf):
    n_exp, d, ff = wg.shape
    assert d == ROW_TILE * LANE and n_tiles >= 2
    tile_e, n_valid, tile_start, src, dst = tables
    wspec = lambda shape: pl.BlockSpec(shape, lambda j, te, *_: (te[j], 0, 0))
    grid_spec = pltpu.PrefetchScalarGridSpec(
        num_scalar_prefetch=5,
        grid=(n_tiles,),
        in_specs=[pl.BlockSpec(memory_space=pl.ANY), wspec((1, d, ff)), wspec((1, d, ff)), wspec((1, ff, d))],
        out_specs=pl.BlockSpec(memory_space=pl.ANY),
        scratch_shapes=[pltpu.VMEM((2, tg * ROW_TILE, LANE), F32), pltpu.VMEM((2, tg * ROW_TILE, LANE), F32),
                        pltpu.SemaphoreType.DMA((2,)), pltpu.SemaphoreType.DMA((2,))])
    return pl.pallas_call(
        functools.partial(_experts_kernel, tg=tg, n_tiles=n_tiles, tf=tf),
        out_shape=jax.ShapeDtypeStruct((2 * h2.shape[0], LANE), F32),
        grid_spec=grid_spec,
        compiler_params=_cparams(("arbitrary",)),
        name="moe_experts",
    )(tile_e, n_valid, tile_start, src, dst, h2, wg, wu, wd)


def _combine_kernel(x1_ref, y0_ref, y1_ref, route_ref, g2_ref, fg_ref, o_ref, *, final_norm):
    r = route_ref[...]
    rows = r.shape[0]
    f = r[:, 2:3] * _load_row_tiles(y0_ref, rows, lead=(0,)) + r[:, 3:4] * _load_row_tiles(y1_ref, rows, lead=(0,))
    x2 = x1_ref[0] + _gate(f, g2_ref[0])
    if final_norm:
        x2 = _rms(x2) * fg_ref[...]
    o_ref[0] = x2


def moe_combine(x1, y, route, g2, fg, tm, row0, final_norm):
    s, r, d = x1.shape
    rm = g2.shape[1]
    nt = r // tm
    blk0 = row0 // tm
    return pl.pallas_call(
        functools.partial(_combine_kernel, final_norm=final_norm),
        out_shape=jax.ShapeDtypeStruct((s, r, d), F32),
        grid=(s, nt),
        in_specs=[pl.BlockSpec((1, tm, d), lambda b_, i: (b_, i, 0)),
                  pl.BlockSpec((1, tm * ROW_TILE, LANE), lambda b_, i: (0, blk0 + b_ * nt + i, 0)),
                  pl.BlockSpec((1, tm * ROW_TILE, LANE), lambda b_, i: (1, blk0 + b_ * nt + i, 0)),
                  pl.BlockSpec((tm, LANE), lambda b_, i: (blk0 + b_ * nt + i, 0)),
                  pl.BlockSpec((1, rm, d), lambda b_, i: (b_, 0, 0)),
                  pl.BlockSpec((1, d), lambda b_, i: (0, 0))],
        out_specs=pl.BlockSpec((1, tm, d), lambda b_, i: (b_, i, 0)),
        compiler_params=_cparams(("arbitrary", "arbitrary")),
        name="moe_combine",
    )(x1, y, y, route, g2, fg)


def _reorder_w_in(w_in, b_in):
    cut = PW_S5 + 128 + 128 + 256 + 256 + GLA_GATE_RANK
    pad = LANE - GLA_GATE_RANK
    w = jnp.concatenate([w_in[:, :cut], jnp.zeros((w_in.shape[0], pad), w_in.dtype), w_in[:, cut:]], axis=1)
    b = jnp.concatenate([b_in[:cut], jnp.zeros((pad,), b_in.dtype), b_in[cut:]])
    return w.astype(BF16), b.reshape(1, PW_TOTAL)


def _row(a):
    return a.reshape(1, -1)


class _Branch:
    def __init__(self, nseq, nb, t_len, seq_form, tm_pre, tm_post, tc, tt_seq, row0):
        self.nseq, self.nb, self.t_len, self.seq_form = nseq, nb, t_len, seq_form
        self.tm_pre, self.tm_post, self.tc, self.tt_seq, self.row0 = tm_pre, tm_post, tc, tt_seq, row0


def _layer_params(W, i):
    row = _row
    p = {}
    p['w_in'], p['b_in'] = _reorder_w_in(W['w_in'][i], W['b_in'][i])
    p['norm_g'] = row(W['norm_mix_g'][i])
    mg = W['merge_g'][i]
    p['mg'] = [row(mg[k * 256:(k + 1) * 256]) for k in range(4)]
    ab_re, ab_im, bb_re, bb_im = s5_discretise(W['s5_a_re'][i], W['s5_a_im'][i], W['s5_log_dt'][i],
                                               W['s5_b_re'][i], W['s5_b_im'][i])
    p['s5'] = (jnp.concatenate([_block_diag_in(bb_re), _block_diag_in(bb_im)], axis=1).astype(BF16),
               _block_diag_out(W['s5_c_re'][i]).astype(BF16), _block_diag_out(W['s5_c_im'][i]).astype(BF16),
               row(ab_re), row(ab_im), row(W['s5_d'][i]), W['s5_w_glu'][i].astype(BF16), row(W['s5_b_glu'][i]))
    wg2 = jnp.zeros((LANE, GLA_KEY_WIDTH), F32).at[:GLA_GATE_RANK].set(W['gla_w_gate2'][i]).astype(BF16)
    p['gla'] = (wg2, row(W['gla_b_gate2'][i]), row(W['gla_onorm_g'][i]))
    p['conv'] = (W['conv_w_dw'][i], row(W['conv_b_dw'][i]), row(W['conv_ln_g'][i]), row(W['conv_ln_b'][i]),
                 W['conv_w_pw'][i].astype(BF16), row(W['conv_b_pw'][i]))
    p['gmlp_ln'] = (row(W['gmlp_ln_g'][i]), row(W['gmlp_ln_b'][i]))
    p['gmlp_ws'], p['gmlp_bs'] = W['gmlp_w_s'][i], W['gmlp_b_s'][i]
    return p


def _mixers(x, mods, states, p, i, br, out):
    nseq, nb, t_len, seq_form = br.nseq, br.nb, br.t_len, br.seq_form
    s5_re0, s5_im0, gla0, conv0 = states
    new_re, new_im, new_gla, new_conv, new_v = out
    mg = p['mg']
    p_s5, p_gla, p_conv, p_mlp = pre_mixer(x, mods[0], mods[1], p['norm_g'], p['w_in'], p['b_in'], br.tm_pre)

    h0 = jnp.concatenate([s5_re0[i].reshape(nb, S5_LANES), s5_im0[i].reshape(nb, S5_LANES)], axis=1)
    o_s5, h_t = s5_mixer(p_s5, h0, *p['s5'], mg[0], nb, br.tc, nseq)
    new_re.append(h_t[:, :S5_LANES].reshape(nb, S5_GROUPS, S5_STATE))
    new_im.append(h_t[:, S5_LANES:].reshape(nb, S5_GROUPS, S5_STATE))

    if seq_form:
        eye = jnp.eye(GLA_HEADS, dtype=F32)
        s0 = jnp.einsum('bhkv,hg->bhvgk', gla0[i], eye).reshape(nseq, GLA_WIDTH, GLA_KEY_WIDTH)
        o_gla, s_t = gla_seq(p_gla, nseq, s0, *p['gla'], mg[1], br.tt_seq)
        s5d = s_t.reshape(nseq, GLA_HEADS, GLA_DV, GLA_HEADS, GLA_DK)
        new_gla.append(jnp.stack([jnp.swapaxes(s5d[:, h, :, h, :], 1, 2) for h in range(GLA_HEADS)], axis=1))
    else:
        ek, ev = _gla_expanders()
        o_gla, s_t = gla_recurrent(p_gla, gla0[i].reshape(nb, GLA_STATE_LANES), ek, ev, *p['gla'], mg[1], nb, t_len)
        new_gla.append(s_t.reshape(nb, GLA_HEADS, GLA_DK, GLA_DV))

    c0 = jnp.transpose(conv0[i], (1, 0, 2)).reshape(CONV_HIST * nb, CONV_DIM)
    o_conv, buf = conv_mixer(p_conv, c0, *p['conv'], mg[2], nb, br.tc, nseq)
    new_conv.append(jnp.transpose(buf.reshape(CONV_HIST, nb, CONV_DIM), (1, 0, 2)))

    ws, bs = p['gmlp_ws'], p['gmlp_bs']
    if seq_form:
        wcat = jnp.transpose(ws, (1, 0, 2)).reshape(GMLP_CHUNK, GMLP_HEADS * GMLP_CHUNK)
        bias = jnp.repeat(bs.T, GMLP_HEAD_DIM, axis=1)
        o_mlp = gmlp_seq(p_mlp, nseq, *p['gmlp_ln'], wcat, bias, mg[3], br.tt_seq)
        new_v.append(None)
    else:
        tri = jnp.tril(jnp.ones((t_len, t_len), F32))
        wrow = jnp.repeat(jnp.transpose(ws[:, :t_len, :t_len] * tri[None], (1, 2, 0)).reshape(t_len * t_len, GMLP_HEADS),
                          GMLP_HEAD_DIM, axis=1)
        brow = jnp.repeat(bs[:, :t_len].T, GMLP_HEAD_DIM, axis=1)
        o_mlp, vn = gmlp_short(p_mlp, *p['gmlp_ln'], wrow, brow, mg[3], nb, t_len)
        new_v.append(vn)

    return [o_s5, o_gla, o_conv, o_mlp]


FF_TILE = 1408
EXPERT_ROWS = 512


def _channel_mixer(xs, mixes, mods, W, i, branches, last):
    ng, wout, fg = _row(W['norm_ffn_g'][i]), W['w_out'][i].astype(BF16), _row(W['final_norm_g'])
    j = i // 2
    if i % 2 == 0:
        wg, wu, wd = (W['ffn_w_gate'][j].astype(BF16), W['ffn_w_up'][j].astype(BF16),
                      W['ffn_w_down'][j].astype(BF16))
        return [post_dense(x, mx, m[2], m[3], m[4], m[5], ng, wout, wg, wu, wd, fg, br.tm_post, FF_TILE, last)
                for x, mx, m, br in zip(xs, mixes, mods, branches)]
    n_total = sum(x.shape[0] * x.shape[1] for x in xs)
    tg = EXPERT_ROWS
    n_tiles = 2 * n_total // tg + N_EXPERTS
    router = jnp.zeros((D_MODEL, LANE), F32).at[:, :N_EXPERTS].set(W['moe_router'][j])
    h2, route = jnp.zeros((n_total * ROW_TILE, LANE), F32), jnp.zeros((n_total, LANE), F32)
    x1s = []
    for x, mx, m, br in zip(xs, mixes, mods, branches):
        x1, h2, route = moe_route(x, mx, m[2], m[3], m[4], ng, wout, router, br.tm_post, br.row0, (h2, route))
        x1s.append(x1)
    tables = _route_tables(route, tg, n_tiles)
    y = moe_experts(h2, tables, W['moe_w_gate'][j].astype(BF16), W['moe_w_up'][j].astype(BF16),
                    W['moe_w_down'][j].astype(BF16), tg, n_tiles, FF_TILE)
    y = y.reshape(2, n_total * ROW_TILE, LANE)
    return [moe_combine(x1, y, route, m[5], fg, br.tm_post, br.row0, last)
            for x1, m, br in zip(x1s, mods, branches)]


def kernel(x_prompt, x_sample, c_prompt, c_sample, state_s5_re, state_s5_im, state_gla, cache_conv, ada_w, ada_b, norm_mix_g, norm_ffn_g, w_in, b_in, s5_a_re, s5_a_im, s5_log_dt, s5_b_re, s5_b_im, s5_c_re, s5_c_im, s5_d, s5_w_glu, s5_b_glu, gla_w_gate2, gla_b_gate2, gla_onorm_g, conv_w_dw, conv_b_dw, conv_ln_g, conv_ln_b, conv_w_pw, conv_b_pw, gmlp_ln_g, gmlp_ln_b, gmlp_w_s, gmlp_b_s, merge_g, w_out, ffn_w_gate, ffn_w_up, ffn_w_down, moe_router, moe_w_gate, moe_w_up, moe_w_down, final_norm_g):
    W = dict(norm_mix_g=norm_mix_g, norm_ffn_g=norm_ffn_g, w_in=w_in, b_in=b_in, s5_a_re=s5_a_re, s5_a_im=s5_a_im,
             s5_log_dt=s5_log_dt, s5_b_re=s5_b_re, s5_b_im=s5_b_im, s5_c_re=s5_c_re, s5_c_im=s5_c_im, s5_d=s5_d,
             s5_w_glu=s5_w_glu, s5_b_glu=s5_b_glu, gla_w_gate2=gla_w_gate2, gla_b_gate2=gla_b_gate2,
             gla_onorm_g=gla_onorm_g, conv_w_dw=conv_w_dw, conv_b_dw=conv_b_dw, conv_ln_g=conv_ln_g,
             conv_ln_b=conv_ln_b, conv_w_pw=conv_w_pw, conv_b_pw=conv_b_pw, gmlp_ln_g=gmlp_ln_g,
             gmlp_ln_b=gmlp_ln_b, gmlp_w_s=gmlp_w_s, gmlp_b_s=gmlp_b_s, merge_g=merge_g, w_out=w_out,
             ffn_w_gate=ffn_w_gate, ffn_w_up=ffn_w_up, ffn_w_down=ffn_w_down, moe_router=moe_router,
             moe_w_gate=moe_w_gate, moe_w_up=moe_w_up, moe_w_down=moe_w_down, final_norm_g=final_norm_g)
    depth = w_in.shape[0]
    bp, tp, d = x_prompt.shape
    bs, ts, _ = x_sample.shape

    m = ada_modulation(jnp.concatenate([c_prompt, c_sample], axis=0), ada_w, ada_b)
    mods_p = [[m[i, :bp, k * d:(k + 1) * d].reshape(bp, 1, d) for k in range(6)] for i in range(depth)]
    mods_s = [[m[i, bp:, k * d:(k + 1) * d].reshape(1, bs, d) for k in range(6)] for i in range(depth)]

    z_re = jnp.zeros((depth, bp, S5_GROUPS, S5_STATE), F32)
    z_gla = jnp.zeros((depth, bp, GLA_HEADS, GLA_DK, GLA_DV), F32)
    z_conv = jnp.zeros((depth, bp, CONV_HIST, CONV_DIM), x_prompt.dtype)
    states = [(z_re, z_re, z_gla, z_conv), (state_s5_re, state_s5_im, state_gla, cache_conv)]
    branches = [_Branch(nseq=bp, nb=bp, t_len=tp, seq_form=True, tm_pre=min(512, tp), tm_post=min(512, tp),
                        tc=min(128, tp), tt_seq=min(512, tp), row0=0),
                _Branch(nseq=1, nb=bs, t_len=ts, seq_form=False, tm_pre=ts * bs, tm_post=min(512, ts * bs),
                        tc=ts, tt_seq=None, row0=bp * tp)]
    xs = [x_prompt, jnp.transpose(x_sample, (1, 0, 2)).reshape(1, ts * bs, d)]
    outs = [([], [], [], [], []), ([], [], [], [], [])]
    for i in range(depth):
        mods = [mods_p[i], mods_s[i]]
        params = _layer_params(W, i)
        mixes = [_mixers(x, m, st_, params, i, br, o)
                 for x, m, st_, br, o in zip(xs, mods, states, branches, outs)]
        xs = _channel_mixer(xs, mixes, mods, W, i, branches, i == depth - 1)

    y_p = xs[0]
    y_s = jnp.transpose(xs[1].reshape(ts, bs, d), (1, 0, 2))
    p_re, p_im, p_gla, p_conv, _ = outs[0]
    s_re, s_im, s_gla, s_conv, s_v = outs[1]
    s_v = [jnp.transpose(v.reshape(ts, bs, GMLP_WIDTH), (1, 0, 2)) for v in s_v]
    st = jnp.stack
    return (y_p, y_s, st(p_re), st(p_im), st(p_gla), st(p_conv),
            st(s_re), st(s_im), st(s_gla), st(s_conv), st(s_v))
```

```python
import functools
import math

import jax
import jax.numpy as jnp
from jax import lax
from jax.experimental import pallas as pl
from jax.experimental.pallas import tpu as pltpu

D_MODEL = 1024
S5_WIDTH = 256
S5_GROUP = 16
S5_GROUPS = 16
S5_STATE = 64
S5_LANES = S5_GROUPS * S5_STATE
GLA_HEADS = 4
GLA_DV = 64
GLA_DK = 32
GLA_WIDTH = 256
GLA_KEY_WIDTH = 128
GLA_GATE_RANK = 16
GLA_TAU = 16.0
GLA_CHUNK = 64
GLA_STATE_LANES = GLA_HEADS * GLA_DK * GLA_DV
CONV_DIM = 256
CONV_WIDTH = 31
CONV_HIST = CONV_WIDTH - 1
GMLP_WIDTH = 256
GMLP_HEADS = 4
GMLP_HEAD_DIM = 64
GMLP_CHUNK = 128
D_FF = 2816
N_EXPERTS = 8
EPS = 1e-6

LANE = 128
PW_S5 = 256
PW_GLA = 128 + 128 + 256 + 256 + LANE
PW_CONV = 512
PW_MLP = 512
PW_TOTAL = PW_S5 + PW_GLA + PW_CONV + PW_MLP
VMEM_LIMIT = 56 * 1024 * 1024

F32 = jnp.float32
BF16 = jnp.bfloat16
HI = lax.Precision.HIGHEST


def _cparams(sem):
    return pltpu.CompilerParams(dimension_semantics=sem, vmem_limit_bytes=VMEM_LIMIT)


def _rms(x):
    return x * lax.rsqrt(jnp.mean(x * x, axis=-1, keepdims=True) + EPS)


def _layernorm(x, g, b):
    mu = jnp.mean(x, axis=-1, keepdims=True)
    xc = x - mu
    var = jnp.mean(xc * xc, axis=-1, keepdims=True)
    return xc * lax.rsqrt(var + EPS) * g + b


def _silu(x):
    return x * jax.nn.sigmoid(x)


def _gelu_tanh(x):
    return 0.5 * x * (1.0 + jnp.tanh(math.sqrt(2.0 / math.pi) * (x + 0.044715 * (x * x * x))))


def _log_sigmoid(x):
    return jnp.minimum(x, 0.0) - jnp.log(1.0 + jnp.exp(-jnp.abs(x)))


def _same_block(shape, row_block, col_block):
    r = lax.broadcasted_iota(jnp.int32, shape, 0) >> (row_block.bit_length() - 1)
    c = lax.broadcasted_iota(jnp.int32, shape, 1) >> (col_block.bit_length() - 1)
    return r == c


def _modulate(y, sc, sh):
    rm = sc.shape[0]
    if rm == 1:
        return y * (1.0 + sc) + sh
    rows, d = y.shape
    y3 = y.reshape(rows // rm, rm, d)
    return (y3 * (1.0 + sc)[None] + sh[None]).reshape(rows, d)


def _gate(y, g):
    rm = g.shape[0]
    if rm == 1:
        return y * g
    rows, d = y.shape
    return (y.reshape(rows // rm, rm, d) * g[None]).reshape(rows, d)


def _ada_kernel(c_ref, w_ref, b_ref, o_ref):
    c = c_ref[...]
    s = _silu(c).astype(BF16)
    o_ref[0] = jnp.dot(s, w_ref[0].astype(BF16), preferred_element_type=F32) + b_ref[0]


def ada_modulation(c_all, ada_w, ada_b):
    depth, d, n6 = ada_w.shape
    rows = c_all.shape[0]
    tn = 1536
    return pl.pallas_call(
        _ada_kernel,
        out_shape=jax.ShapeDtypeStruct((depth, rows, n6), F32),
        grid=(depth, n6 // tn),
        in_specs=[pl.BlockSpec((rows, d), lambda l, j: (0, 0)),
                  pl.BlockSpec((1, d, tn), lambda l, j: (l, 0, j)),
                  pl.BlockSpec((1, 1, tn), lambda l, j: (l, 0, j))],
        out_specs=pl.BlockSpec((1, rows, tn), lambda l, j: (l, 0, j)),
        compiler_params=_cparams(("arbitrary", "arbitrary")),
        name="ada_modulation",
    )(c_all, ada_w, ada_b.reshape(depth, 1, n6))


def _pre_kernel(x_ref, sh_ref, sc_ref, g_ref, w_ref, b_ref, o_s5, o_gla, o_conv, o_mlp):
    x = x_ref[0]
    y = _modulate(_rms(x) * g_ref[...], sc_ref[0], sh_ref[0])
    p = jnp.dot(y.astype(BF16), w_ref[...], preferred_element_type=F32) + b_ref[...]
    o_s5[...] = p[:, 0:PW_S5]
    o_gla[...] = p[:, PW_S5:PW_S5 + PW_GLA]
    o_conv[...] = p[:, PW_S5 + PW_GLA:PW_S5 + PW_GLA + PW_CONV]
    o_mlp[...] = p[:, PW_S5 + PW_GLA + PW_CONV:PW_TOTAL]


def pre_mixer(x, sh, sc, g, w, b, tm):
    s, r, d = x.shape
    rm = sh.shape[1]
    widths = (PW_S5, PW_GLA, PW_CONV, PW_MLP)
    return pl.pallas_call(
        _pre_kernel,
        out_shape=[jax.ShapeDtypeStruct((r, s * w_), F32) for w_ in widths],
        grid=(s, r // tm),
        in_specs=[pl.BlockSpec((1, tm, d), lambda b_, i: (b_, i, 0)),
                  pl.BlockSpec((1, rm, d), lambda b_, i: (b_, 0, 0)),
                  pl.BlockSpec((1, rm, d), lambda b_, i: (b_, 0, 0)),
                  pl.BlockSpec((1, d), lambda b_, i: (0, 0)),
                  pl.BlockSpec((d, PW_TOTAL), lambda b_, i: (0, 0)),
                  pl.BlockSpec((1, PW_TOTAL), lambda b_, i: (0, 0))],
        out_specs=[pl.BlockSpec((tm, w_), lambda b_, i: (i, b_)) for w_ in widths],
        compiler_params=_cparams(("arbitrary", "arbitrary")),
        name="pre_mixer",
    )(x, sh, sc, g, w, b)


def _s5_disc_kernel(lr_ref, li_ref, ldt_ref, br_ref, bi_ref, abr_ref, abi_ref, bbr_ref, bbi_ref):
    lr = lr_ref[...]
    li = li_ref[...]
    dt = jnp.exp(ldt_ref[...])
    mag = jnp.exp(lr * dt)
    ang = li * dt
    ab_re = mag * jnp.cos(ang)
    ab_im = mag * jnp.sin(ang)
    den = lr * lr + li * li
    nr = ab_re - 1.0
    f_re = (nr * lr + ab_im * li) / den
    f_im = (ab_im * lr - nr * li) / den
    br = br_ref[...]
    bi = bi_ref[...]
    abr_ref[...] = ab_re
    abi_ref[...] = ab_im
    bbr_ref[...] = f_re * br - f_im * bi
    bbi_ref[...] = f_re * bi + f_im * br


def s5_discretise(a_re, a_im, log_dt, b_re, b_im):
    n = b_re.shape[-1]
    gp = a_re.size
    bc = lambda a: jnp.broadcast_to(a.reshape(gp, 1), (gp, n))
    ldt = jnp.broadcast_to(log_dt[:, None], a_re.shape)
    outs = pl.pallas_call(
        _s5_disc_kernel,
        out_shape=[jax.ShapeDtypeStruct((gp, n), F32)] * 4,
        name="s5_discretise",
    )(bc(a_re), bc(a_im), bc(ldt), b_re.reshape(gp, n), b_im.reshape(gp, n))
    ab_re, ab_im, bb_re, bb_im = outs
    return ab_re[:, 0], ab_im[:, 0], bb_re, bb_im


def _block_diag_in(bb):
    g, p, n = S5_GROUPS, S5_STATE, S5_GROUP
    b3 = bb.reshape(g, p, n)
    eye = jnp.eye(g, dtype=bb.dtype)
    return jnp.einsum('gpn,gh->gnhp', b3, eye).reshape(g * n, g * p)


def _block_diag_out(c):
    g, p, n = S5_GROUPS, S5_STATE, S5_GROUP
    eye = jnp.eye(g, dtype=c.dtype)
    return jnp.einsum('gnp,gh->gphn', c, eye).reshape(g * p, g * n)


def _to_time_major(x_ref, cols, tm_ref, row0, nsl):
    rt = x_ref.shape[0]
    w = x_ref.shape[1] // nsl
    start, width = cols
    for l in range(nsl):
        for h in range(width // LANE):
            c0 = l * w + start + h * LANE
            tm_ref[h, pl.ds(row0 + l, rt, stride=nsl), :] = x_ref[:, c0:c0 + LANE]


def _from_time_major(tm_ref, o_ref, nsl):
    rt = o_ref.shape[0]
    nh = tm_ref.shape[0]
    for l in range(nsl):
        piece = jnp.concatenate([tm_ref[h, pl.ds(l, rt, stride=nsl), :] for h in range(nh)], axis=1)
        o_ref[:, l * nh * LANE:(l + 1) * nh * LANE] = piece.astype(o_ref.dtype)


def _lane_tiles(tm_ref, rows=slice(None)):
    return jnp.concatenate([tm_ref[h, rows, :] for h in range(tm_ref.shape[0])], axis=1)


def _set_lane_tiles(tm_ref, x):
    for h in range(tm_ref.shape[0]):
        tm_ref[h] = x[:, h * LANE:(h + 1) * LANE]


def _s5_kernel(u_ref, h0_ref, bblk_ref, cre_ref, cim_ref, ar_ref, ai_ref, d_ref, wglu_ref, bglu_ref, mg_ref,
               o_ref, hT_ref, xs_ref, hs_ref, tm_ref, *, nb, tc, nsl):
    i = pl.program_id(0)

    @pl.when(i == 0)
    def _():
        hs_ref[...] = h0_ref[...]

    _to_time_major(u_ref, (0, S5_WIDTH), tm_ref, 0, nsl)
    u = _lane_tiles(tm_ref)
    xs_ref[...] = jnp.dot(u.astype(BF16), bblk_ref[...], preferred_element_type=F32)
    ar = jnp.broadcast_to(ar_ref[...], (nb, S5_LANES))
    ai = jnp.broadcast_to(ai_ref[...], (nb, S5_LANES))

    def step(t, carry):
        hr, hi = carry
        row = pl.multiple_of(t * nb, nb)
        xr = xs_ref[pl.ds(row, nb), 0:S5_LANES]
        xi = xs_ref[pl.ds(row, nb), S5_LANES:2 * S5_LANES]
        nr = ar * hr - ai * hi + xr
        ni = ar * hi + ai * hr + xi
        xs_ref[pl.ds(row, nb), 0:S5_LANES] = nr
        xs_ref[pl.ds(row, nb), S5_LANES:2 * S5_LANES] = ni
        return nr, ni

    hr, hi = lax.fori_loop(0, tc, step, (hs_ref[:, 0:S5_LANES], hs_ref[:, S5_LANES:2 * S5_LANES]),
                           unroll=True if tc <= 8 else 4)
    hs_ref[:, 0:S5_LANES] = hr
    hs_ref[:, S5_LANES:2 * S5_LANES] = hi

    y = (jnp.dot(xs_ref[:, 0:S5_LANES].astype(BF16), cre_ref[...], preferred_element_type=F32)
         - jnp.dot(xs_ref[:, S5_LANES:2 * S5_LANES].astype(BF16), cim_ref[...], preferred_element_type=F32))
    y = y + d_ref[...] * u
    y = _gelu_tanh(y)
    y = y * jax.nn.sigmoid(jnp.dot(y.astype(BF16), wglu_ref[...], preferred_element_type=F32) + bglu_ref[...])
    _set_lane_tiles(tm_ref, _rms(y) * mg_ref[...])
    _from_time_major(tm_ref, o_ref, nsl)

    @pl.when(i == pl.num_programs(0) - 1)
    def _():
        hT_ref[...] = hs_ref[...]


def s5_mixer(u, h0, bblk, cre, cim, ar, ai, d, wglu, bglu, mg, nb, tc, nsl):
    rows = u.shape[0]
    rc = nb * tc
    rt = rc // nsl
    full = lambda shape: pl.BlockSpec(shape, lambda i: (0,) * len(shape))
    return pl.pallas_call(
        functools.partial(_s5_kernel, nb=nb, tc=tc, nsl=nsl),
        out_shape=[jax.ShapeDtypeStruct((rows, nsl * S5_WIDTH), BF16),
                   jax.ShapeDtypeStruct((nb, 2 * S5_LANES), F32)],
        grid=(rows // rt,),
        in_specs=[pl.BlockSpec((rt, nsl * S5_WIDTH), lambda i: (i, 0)),
                  full((nb, 2 * S5_LANES)),
                  full((S5_WIDTH, 2 * S5_LANES)),
                  full((S5_LANES, S5_WIDTH)), full((S5_LANES, S5_WIDTH)),
                  full((1, S5_LANES)), full((1, S5_LANES)),
                  full((1, S5_WIDTH)), full((S5_WIDTH, S5_WIDTH)), full((1, S5_WIDTH)), full((1, S5_WIDTH))],
        out_specs=[pl.BlockSpec((rt, nsl * S5_WIDTH), lambda i: (i, 0)),
                   full((nb, 2 * S5_LANES))],
        scratch_shapes=[pltpu.VMEM((rc, 2 * S5_LANES), F32), pltpu.VMEM((nb, 2 * S5_LANES), F32),
                        pltpu.VMEM((S5_WIDTH // LANE, rc, LANE), F32)],
        compiler_params=_cparams(("arbitrary",)),
        name="s5_mixer",
    )(u, h0, bblk, cre, cim, ar, ai, d, wglu, bglu, mg)


CONV_ROWS = 64


def _conv_kernel(ag_ref, c0_ref, wdw_ref, bdw_ref, lng_ref, lnb_ref, wpw_ref, bpw_ref, mg_ref,
                 o_ref, buf_ref, zc_ref, y_ref, *, nb, tc, nsl):
    i = pl.program_id(0)
    hist = CONV_HIST * nb
    rc = nb * tc
    rt = rc // nsl
    n_lt = CONV_DIM // LANE

    @pl.when(i == 0)
    def _():
        for h in range(n_lt):
            zc_ref[h, 0:hist, :] = c0_ref[:, h * LANE:(h + 1) * LANE]

    @pl.when(i > 0)
    def _():
        for h in range(n_lt):
            zc_ref[h, 0:hist, :] = zc_ref[h, rc:rc + hist, :]

    w_seq = 2 * CONV_DIM
    for l in range(nsl):
        a = ag_ref[:, l * w_seq:l * w_seq + CONV_DIM]
        g = ag_ref[:, l * w_seq + CONV_DIM:(l + 1) * w_seq]
        z = a * jax.nn.sigmoid(g)
        for h in range(n_lt):
            zc_ref[h, pl.ds(hist + l, rt, stride=nsl), :] = z[:, h * LANE:(h + 1) * LANE]

    w = wdw_ref[...]

    def tile(j, carry):
        r0 = pl.multiple_of(j * CONV_ROWS, CONV_ROWS)
        for h in range(n_lt):
            acc = jnp.zeros((CONV_ROWS, LANE), F32)
            for k in range(CONV_WIDTH):
                acc = acc + w[k:k + 1, h * LANE:(h + 1) * LANE] * zc_ref[h, pl.ds(r0 + k * nb, CONV_ROWS), :]
            y_ref[h, pl.ds(r0, CONV_ROWS), :] = acc
        return carry

    lax.fori_loop(0, rc // CONV_ROWS, tile, 0)
    y = _lane_tiles(y_ref) + bdw_ref[...]
    y = _silu(_layernorm(y, lng_ref[...], lnb_ref[...]))
    y = jnp.dot(y.astype(BF16), wpw_ref[...], preferred_element_type=F32) + bpw_ref[...]
    _set_lane_tiles(y_ref, _rms(y) * mg_ref[...])
    _from_time_major(y_ref, o_ref, nsl)

    @pl.when(i == pl.num_programs(0) - 1)
    def _():
        buf_ref[...] = _lane_tiles(zc_ref, slice(rc, rc + hist))


def conv_mixer(ag, c0, wdw, bdw, lng, lnb, wpw, bpw, mg, nb, tc, nsl):
    rows = ag.shape[0]
    rc = nb * tc
    rt = rc // nsl
    hist = CONV_HIST * nb
    assert rows == rt or tc >= CONV_HIST
    n_lt = CONV_DIM // LANE
    full = lambda shape: pl.BlockSpec(shape, lambda i: (0,) * len(shape))
    return pl.pallas_call(
        functools.partial(_conv_kernel, nb=nb, tc=tc, nsl=nsl),
        out_shape=[jax.ShapeDtypeStruct((rows, nsl * CONV_DIM), BF16),
                   jax.ShapeDtypeStruct((hist, CONV_DIM), F32)],
        grid=(rows // rt,),
        in_specs=[pl.BlockSpec((rt, nsl * 2 * CONV_DIM), lambda i: (i, 0)),
                  full((hist, CONV_DIM)), full((CONV_WIDTH, CONV_DIM)),
                  full((1, CONV_DIM)), full((1, CONV_DIM)), full((1, CONV_DIM)),
                  full((CONV_DIM, CONV_DIM)), full((1, CONV_DIM)), full((1, CONV_DIM))],
        out_specs=[pl.BlockSpec((rt, nsl * CONV_DIM), lambda i: (i, 0)), full((hist, CONV_DIM))],
        scratch_shapes=[pltpu.VMEM((n_lt, hist + rc, LANE), F32), pltpu.VMEM((n_lt, rc, LANE), F32)],
        compiler_params=_cparams(("arbitrary",)),
        name="conv_mixer",
    )(ag, c0, wdw, bdw, lng, lnb, wpw, bpw, mg)


def _gmlp_seq_kernel(uv_ref, lng_ref, lnb_ref, wcat_ref, bias_ref, mg_ref, o_ref, *, tt):
    n_chunks = tt // GMLP_CHUNK
    kc = GMLP_HEADS * GMLP_CHUNK
    rowi = lax.broadcasted_iota(jnp.int32, (GMLP_CHUNK, kc), 0)
    coli = lax.broadcasted_iota(jnp.int32, (GMLP_CHUNK, kc), 1)
    wcat = jnp.where((coli & (GMLP_CHUNK - 1)) <= rowi, wcat_ref[...], 0.0).astype(BF16)
    sel = _same_block((kc, GMLP_WIDTH), GMLP_CHUNK, GMLP_HEAD_DIM)
    for c in range(n_chunks):
        rows = slice(c * GMLP_CHUNK, (c + 1) * GMLP_CHUNK)
        u = uv_ref[rows, 0:GMLP_WIDTH]
        v = uv_ref[rows, GMLP_WIDTH:2 * GMLP_WIDTH]
        vn = _layernorm(v, lng_ref[...], lnb_ref[...])
        vbd = jnp.where(sel, jnp.concatenate([vn] * GMLP_HEADS, axis=0), 0.0).astype(BF16)
        mixed = jnp.dot(wcat, vbd, preferred_element_type=F32) + bias_ref[...]
        o_ref[rows, :] = (_rms(u * mixed) * mg_ref[...]).astype(o_ref.dtype)


def gmlp_seq(uv, nseq, lng, lnb, wcat, bias, mg, tt):
    t = uv.shape[0]
    full = lambda shape: pl.BlockSpec(shape, lambda b_, i: (0,) * len(shape))
    return pl.pallas_call(
        functools.partial(_gmlp_seq_kernel, tt=tt),
        out_shape=jax.ShapeDtypeStruct((t, nseq * GMLP_WIDTH), BF16),
        grid=(nseq, t // tt),
        in_specs=[pl.BlockSpec((tt, 2 * GMLP_WIDTH), lambda b_, i: (i, b_)),
                  full((1, GMLP_WIDTH)), full((1, GMLP_WIDTH)),
                  full((GMLP_CHUNK, GMLP_HEADS * GMLP_CHUNK)), full((GMLP_CHUNK, GMLP_WIDTH)),
                  full((1, GMLP_WIDTH))],
        out_specs=pl.BlockSpec((tt, GMLP_WIDTH), lambda b_, i: (i, b_)),
        compiler_params=_cparams(("arbitrary", "arbitrary")),
        name="gmlp_seq",
    )(uv, lng, lnb, wcat, bias, mg)


def _gmlp_short_kernel(uv_ref, lng_ref, lnb_ref, wrow_ref, brow_ref, mg_ref, o_ref, vn_ref, *, nb, t_len):
    u = uv_ref[:, 0:GMLP_WIDTH]
    v = uv_ref[:, GMLP_WIDTH:2 * GMLP_WIDTH]
    vn = _layernorm(v, lng_ref[...], lnb_ref[...])
    vn_ref[...] = vn
    wrow = wrow_ref[...]
    brow = brow_ref[...]
    for t in range(t_len):
        mixed = jnp.zeros((nb, GMLP_WIDTH), F32) + brow[t:t + 1, :]
        for j in range(t + 1):
            mixed = mixed + wrow[t * t_len + j:t * t_len + j + 1, :] * vn[j * nb:(j + 1) * nb, :]
        o = u[t * nb:(t + 1) * nb, :] * mixed
        o_ref[t * nb:(t + 1) * nb, :] = (_rms(o) * mg_ref[...]).astype(o_ref.dtype)


def gmlp_short(uv, lng, lnb, wrow, brow, mg, nb, t_len):
    rows = uv.shape[0]
    return pl.pallas_call(
        functools.partial(_gmlp_short_kernel, nb=nb, t_len=t_len),
        out_shape=[jax.ShapeDtypeStruct((rows, GMLP_WIDTH), BF16),
                   jax.ShapeDtypeStruct((rows, GMLP_WIDTH), F32)],
        compiler_params=pltpu.CompilerParams(vmem_limit_bytes=VMEM_LIMIT),
        name="gmlp_short",
    )(uv, lng, lnb, wrow, brow, mg)


def _split3(x):
    a = x.astype(BF16)
    r1 = x - a.astype(F32)
    b = r1.astype(BF16)
    c = (r1 - b.astype(F32)).astype(BF16)
    return a, b, c


def _dot_exact_rhs(x, m):
    return sum(jnp.dot(t, m, preferred_element_type=F32) for t in _split3(x))


def _dot_exact_lhs(m, x):
    return sum(jnp.dot(m, t, preferred_element_type=F32) for t in _split3(x))


def _gla_tail(o, r, gmean, onorm, mg):
    ms = _dot_exact_rhs(o * o, gmean)
    o = o * lax.rsqrt(ms + EPS) * onorm
    o = o * _silu(r)
    return _rms(o) * mg


def _head_mean_matrix():
    return jnp.where(_same_block((GLA_WIDTH, GLA_WIDTH), GLA_DV, GLA_DV), 1.0 / GLA_DV, 0.0).astype(BF16)


GLA_SEQS = 4


def _gla_seq_kernel(x_ref, s0_ref, wg_ref, bg_ref, onorm_ref, mg_ref, o_ref, sT_ref,
                    s_ref, qt_ref, kt_ref, kd_ref, dl_ref, oacc_ref, *, tt):
    i = pl.program_id(1)
    L = GLA_CHUNK
    n_ch = tt // L
    kw, vw = GLA_KEY_WIDTH, GLA_WIDTH

    @pl.when(i == 0)
    def _():
        s_ref[...] = s0_ref[...]

    rows_i = lax.broadcasted_iota(jnp.int32, (tt, tt), 0)
    cols_i = lax.broadcasted_iota(jnp.int32, (tt, tt), 1)
    shift = L.bit_length() - 1
    same_chunk = (rows_i >> shift) == (cols_i >> shift)
    tri = jnp.logical_and(same_chunk, cols_i <= rows_i).astype(BF16)
    chunk_sum = same_chunk.astype(BF16)
    chunk_rows = (lax.broadcasted_iota(jnp.int32, (n_ch, tt), 0)
                  == (lax.broadcasted_iota(jnp.int32, (n_ch, tt), 1) >> shift)).astype(BF16)
    kbd_sel = _same_block((GLA_HEADS * L, kw), L, GLA_DK)
    vbd_sel = _same_block((GLA_HEADS * L, vw), L, GLA_DV)
    causal = ((lax.broadcasted_iota(jnp.int32, (L, GLA_HEADS * L), 1) & (L - 1))
              <= lax.broadcasted_iota(jnp.int32, (L, GLA_HEADS * L), 0))
    s_sel = _same_block((vw, kw), GLA_DV, GLA_DK)
    gmean = _head_mean_matrix()
    scale = GLA_DK ** -0.5
    nt_dims = (((1,), (1,)), ((), ()))
    tn_dims = (((0,), (0,)), ((), ()))
    zero = jnp.zeros((), BF16)

    for g in range(GLA_SEQS):
        x0 = g * PW_GLA
        q = x_ref[:, x0:x0 + kw] * scale
        k = x_ref[:, x0 + kw:x0 + 2 * kw]
        gl = x_ref[:, x0 + 2 * kw + 2 * vw:x0 + PW_GLA]
        la = _log_sigmoid(jnp.dot(gl.astype(BF16), wg_ref[...], preferred_element_type=F32) + bg_ref[...])
        la = la / GLA_TAU
        bc = _dot_exact_lhs(tri, la)
        b_end = _dot_exact_lhs(chunk_sum, la)
        qt_ref[g] = (q * jnp.exp(bc)).astype(BF16)
        kt_ref[g] = (k * jnp.exp(-bc)).astype(BF16)
        kd_ref[g] = (k * jnp.exp(b_end - bc)).astype(BF16)
        dl_ref[g] = jnp.exp(_dot_exact_lhs(chunk_rows, la))

    def chunk(c, carry):
        r0 = pl.multiple_of(c * L, L)
        for g in range(GLA_SEQS):
            x0 = g * PW_GLA
            qt = qt_ref[g, pl.ds(r0, L), :]
            kt = kt_ref[g, pl.ds(r0, L), :]
            kdec = kd_ref[g, pl.ds(r0, L), :]
            vb = x_ref[pl.ds(r0, L), x0 + 2 * kw:x0 + 2 * kw + vw].astype(BF16)
            kbd = jnp.where(kbd_sel, jnp.concatenate([kt] * GLA_HEADS, axis=0), zero)
            att = lax.dot_general(qt, kbd, nt_dims, preferred_element_type=F32)
            att = jnp.where(causal, att, 0.0).astype(BF16)
            vbd = jnp.where(vbd_sel, jnp.concatenate([vb] * GLA_HEADS, axis=0), zero)
            st = s_ref[g]
            oacc_ref[g, pl.ds(r0, L), :] = (jnp.dot(att, vbd, preferred_element_type=F32)
                                            + lax.dot_general(qt, st.astype(BF16), nt_dims,
                                                              preferred_element_type=F32))
            upd = lax.dot_general(vb, kdec, tn_dims, preferred_element_type=F32)
            s_ref[g] = st * dl_ref[g, pl.ds(c, 1), :] + jnp.where(s_sel, upd, 0.0)
        return carry

    lax.fori_loop(0, n_ch, chunk, 0)

    for g in range(GLA_SEQS):
        x0 = g * PW_GLA
        r = x_ref[:, x0 + 2 * kw + vw:x0 + 2 * kw + 2 * vw]
        o_ref[:, g * vw:(g + 1) * vw] = _gla_tail(oacc_ref[g], r, gmean, onorm_ref[...],
                                                  mg_ref[...]).astype(o_ref.dtype)

    @pl.when(i == pl.num_programs(1) - 1)
    def _():
        sT_ref[...] = s_ref[...]


def gla_seq(x, nseq, s0, wg, bg, onorm, mg, tt):
    t = x.shape[0]
    g = GLA_SEQS
    assert nseq % g == 0
    full = lambda shape: pl.BlockSpec(shape, lambda b_, i: (0,) * len(shape))
    state = pl.BlockSpec((g, GLA_WIDTH, GLA_KEY_WIDTH), lambda b_, i: (b_, 0, 0))
    return pl.pallas_call(
        functools.partial(_gla_seq_kernel, tt=tt),
        out_shape=[jax.ShapeDtypeStruct((t, nseq * GLA_WIDTH), BF16),
                   jax.ShapeDtypeStruct((nseq, GLA_WIDTH, GLA_KEY_WIDTH), F32)],
        grid=(nseq // g, t // tt),
        in_specs=[pl.BlockSpec((tt, g * PW_GLA), lambda b_, i: (i, b_)),
                  state,
                  full((LANE, GLA_KEY_WIDTH)), full((1, GLA_KEY_WIDTH)),
                  full((1, GLA_WIDTH)), full((1, GLA_WIDTH))],
        out_specs=[pl.BlockSpec((tt, g * GLA_WIDTH), lambda b_, i: (i, b_)), state],
        scratch_shapes=[pltpu.VMEM((g, GLA_WIDTH, GLA_KEY_WIDTH), F32),
                        pltpu.VMEM((g, tt, GLA_KEY_WIDTH), BF16), pltpu.VMEM((g, tt, GLA_KEY_WIDTH), BF16),
                        pltpu.VMEM((g, tt, GLA_KEY_WIDTH), BF16),
                        pltpu.VMEM((g, tt // GLA_CHUNK, GLA_KEY_WIDTH), F32),
                        pltpu.VMEM((g, tt, GLA_WIDTH), F32)],
        compiler_params=_cparams(("arbitrary", "arbitrary")),
        name="gla_seq",
    )(x, s0, wg, bg, onorm, mg)


def _gla_rec_kernel(x_ref, s0_ref, ek_ref, ev_ref, wg_ref, bg_ref, onorm_ref, mg_ref, o_ref, sT_ref,
                    *, nb, t_len):
    kw, vw = GLA_KEY_WIDTH, GLA_WIDTH
    hl = GLA_DK * GLA_DV
    sT_ref[...] = s0_ref[...]
    gmean = _head_mean_matrix()
    scale = GLA_DK ** -0.5

    def step(t, carry):
        r0 = pl.multiple_of(t * nb, nb)
        q = x_ref[pl.ds(r0, nb), 0:kw] * scale
        k = x_ref[pl.ds(r0, nb), kw:2 * kw]
        v = x_ref[pl.ds(r0, nb), 2 * kw:2 * kw + vw]
        r = x_ref[pl.ds(r0, nb), 2 * kw + vw:2 * kw + 2 * vw]
        gl = x_ref[pl.ds(r0, nb), 2 * kw + 2 * vw:2 * kw + 2 * vw + LANE]
        la = _log_sigmoid(jnp.dot(gl.astype(BF16), wg_ref[...], preferred_element_type=F32) + bg_ref[...])
        a = jnp.exp(la / GLA_TAU)
        a3 = _split3(a)
        qb = q.astype(BF16)
        kb = k.astype(BF16)
        vb = v.astype(BF16)
        outs = []
        for h in range(GLA_HEADS):
            lanes = slice(h * hl, (h + 1) * hl)
            ek = ek_ref[:, lanes]
            a_e = (jnp.dot(a3[0], ek, preferred_element_type=F32)
                   + jnp.dot(a3[1], ek, preferred_element_type=F32)
                   + jnp.dot(a3[2], ek, preferred_element_type=F32))
            k_e = jnp.dot(kb, ek, preferred_element_type=F32)
            q_e = jnp.dot(qb, ek, preferred_element_type=F32)
            v_e = jnp.dot(vb, ev_ref[:, lanes], preferred_element_type=F32)
            s_new = a_e * sT_ref[:, lanes] + k_e * v_e
            sT_ref[:, lanes] = s_new
            prod = q_e * s_new
            acc = prod[:, 0:LANE]
            for j in range(1, hl // LANE):
                acc = acc + prod[:, j * LANE:(j + 1) * LANE]
            outs.append(acc[:, 0:GLA_DV] + acc[:, GLA_DV:2 * GLA_DV])
        o = jnp.concatenate(outs, axis=1)
        o_ref[pl.ds(r0, nb), :] = _gla_tail(o, r, gmean, onorm_ref[...], mg_ref[...]).astype(o_ref.dtype)
        return carry

    lax.fori_loop(0, t_len, step, 0)


def gla_recurrent(x, s0, ek, ev, wg, bg, onorm, mg, nb, t_len):
    rows = x.shape[0]
    return pl.pallas_call(
        functools.partial(_gla_rec_kernel, nb=nb, t_len=t_len),
        out_shape=[jax.ShapeDtypeStruct((rows, GLA_WIDTH), BF16),
                   jax.ShapeDtypeStruct((nb, GLA_STATE_LANES), F32)],
        compiler_params=pltpu.CompilerParams(vmem_limit_bytes=VMEM_LIMIT),
        name="gla_recurrent",
    )(x, s0, ek, ev, wg, bg, onorm, mg)


def _gla_expanders():
    lane = jnp.arange(GLA_STATE_LANES)
    h = lane // (GLA_DK * GLA_DV)
    dk = (lane // GLA_DV) % GLA_DK
    dv = lane % GLA_DV
    ek = (jnp.arange(GLA_KEY_WIDTH)[:, None] == (h * GLA_DK + dk)[None, :]).astype(BF16)
    ev = (jnp.arange(GLA_WIDTH)[:, None] == (h * GLA_DV + dv)[None, :]).astype(BF16)
    return ek, ev


def _mix_residual(x_ref, m_refs, g1_ref, wout_ref):
    mix = jnp.concatenate([m[...] for m in m_refs], axis=1)
    proj = jnp.dot(mix, wout_ref[...], preferred_element_type=F32)
    return x_ref[0] + _gate(proj, g1_ref[0])


def _swiglu(h, wg_ref, wu_ref, wd_ref, tf, lead=(), between=None):
    ff = wg_ref.shape[-1]
    n_dots = 3 * (ff // tf)
    tick = (lambda i: between(i, n_dots)) if between is not None else (lambda i: None)
    acc = jnp.zeros((h.shape[0], wd_ref.shape[-1]), F32)
    for c in range(ff // tf):
        cols = slice(c * tf, (c + 1) * tf)
        gate = jnp.dot(h, wg_ref[(*lead, slice(None), cols)], preferred_element_type=F32)
        tick(3 * c)
        up = jnp.dot(h, wu_ref[(*lead, slice(None), cols)], preferred_element_type=F32)
        tick(3 * c + 1)
        acc = acc + jnp.dot((_silu(gate) * up).astype(BF16), wd_ref[(*lead, cols, slice(None))],
                            preferred_element_type=F32)
        tick(3 * c + 2)
    return acc


def _post_dense_kernel(x_ref, m0_ref, m1_ref, m2_ref, m3_ref, g1_ref, sh2_ref, sc2_ref, g2_ref, ng_ref, wout_ref,
                       wg_ref, wu_ref, wd_ref, fg_ref, o_ref, *, final_norm, tf):
    x1 = _mix_residual(x_ref, (m0_ref, m1_ref, m2_ref, m3_ref), g1_ref, wout_ref)
    h = _modulate(_rms(x1) * ng_ref[...], sc2_ref[0], sh2_ref[0]).astype(BF16)
    x2 = x1 + _gate(_swiglu(h, wg_ref, wu_ref, wd_ref, tf), g2_ref[0])
    if final_norm:
        x2 = _rms(x2) * fg_ref[...]
    o_ref[0] = x2


def post_dense(x, mixes, g1, sh2, sc2, g2, ng, wout, wg, wu, wd, fg, tm, tf, final_norm):
    s, r, d = x.shape
    rm = g1.shape[1]
    ff = wg.shape[1]
    mod = pl.BlockSpec((1, rm, d), lambda b_, i: (b_, 0, 0))
    const = lambda shape: pl.BlockSpec(shape, lambda b_, i: (0,) * len(shape))
    resident = lambda shape: pl.BlockSpec(shape, lambda b_, i: (0,) * len(shape), pipeline_mode=pl.Buffered(1))
    mixspec = pl.BlockSpec((tm, 256), lambda b_, i: (i, b_))
    return pl.pallas_call(
        functools.partial(_post_dense_kernel, final_norm=final_norm, tf=tf),
        out_shape=jax.ShapeDtypeStruct((s, r, d), F32),
        grid=(s, r // tm),
        in_specs=[pl.BlockSpec((1, tm, d), lambda b_, i: (b_, i, 0)),
                  mixspec, mixspec, mixspec, mixspec,
                  mod, mod, mod, mod,
                  const((1, d)), resident((d, d)),
                  resident((d, ff)), resident((d, ff)), resident((ff, d)),
                  const((1, d))],
        out_specs=pl.BlockSpec((1, tm, d), lambda b_, i: (b_, i, 0)),
        compiler_params=_cparams(("arbitrary", "arbitrary")),
        name="post_dense",
    )(x, *mixes, g1, sh2, sc2, g2, ng, wout, wg, wu, wd, fg)


ROW_TILE = 8


def _store_row_tiles(ref, x, lead=()):
    rows = x.shape[0]
    for s in range(ROW_TILE):
        ref[(*lead, pl.ds(s, rows, stride=ROW_TILE), slice(None))] = x[:, s * LANE:(s + 1) * LANE]


def _load_row_tiles(ref, rows, lead=()):
    return jnp.concatenate([ref[(*lead, pl.ds(s, rows, stride=ROW_TILE), slice(None))] for s in range(ROW_TILE)],
                           axis=1)


def _route_kernel(x_ref, m0_ref, m1_ref, m2_ref, m3_ref, g1_ref, sh2_ref, sc2_ref, ng_ref, wout_ref, router_ref,
                  *rest):
    x1_ref, h2_ref, route_ref = rest[-3:]
    x1 = _mix_residual(x_ref, (m0_ref, m1_ref, m2_ref, m3_ref), g1_ref, wout_ref)
    x1_ref[0] = x1
    h = _modulate(_rms(x1) * ng_ref[...], sc2_ref[0], sh2_ref[0])
    _store_row_tiles(h2_ref, h)
    h_hi = h.astype(BF16)
    h_lo = (h - h_hi.astype(F32)).astype(BF16)
    w = router_ref[...]
    w_hi = w.astype(BF16)
    w_lo = (w - w_hi.astype(F32)).astype(BF16)
    logits = (jnp.dot(h_hi, w_hi, preferred_element_type=F32) + jnp.dot(h_lo, w_hi, preferred_element_type=F32)
              + jnp.dot(h_hi, w_lo, preferred_element_type=F32))
    lane = lax.broadcasted_iota(jnp.int32, logits.shape, 1).astype(F32)
    neg = jnp.float32(-jnp.inf)
    logits = jnp.where(lane < N_EXPERTS, logits, neg)
    m1 = jnp.max(logits, axis=1, keepdims=True)
    i1 = jnp.min(jnp.where(logits == m1, lane, float(LANE)), axis=1, keepdims=True)
    others = jnp.where(lane == i1, neg, logits)
    m2 = jnp.max(others, axis=1, keepdims=True)
    i2 = jnp.min(jnp.where(others == m2, lane, float(LANE)), axis=1, keepdims=True)
    e2 = jnp.exp(m2 - m1)
    den = 1.0 + e2
    route_ref[...] = (jnp.where(lane == 0.0, i1, 0.0) + jnp.where(lane == 1.0, i2, 0.0)
                      + jnp.where(lane == 2.0, 1.0 / den, 0.0) + jnp.where(lane == 3.0, e2 / den, 0.0))


def moe_route(x, mixes, g1, sh2, sc2, ng, wout, router, tm, row0, shared):
    s, r, d = x.shape
    rm = g1.shape[1]
    nt = r // tm
    blk0 = row0 // tm
    n_total = shared[1].shape[0]
    mod = pl.BlockSpec((1, rm, d), lambda b_, i: (b_, 0, 0))
    const = lambda shape: pl.BlockSpec(shape, lambda b_, i: (0,) * len(shape))
    mixspec = pl.BlockSpec((tm, 256), lambda b_, i: (i, b_))
    in_specs = [pl.BlockSpec((1, tm, d), lambda b_, i: (b_, i, 0)),
                mixspec, mixspec, mixspec, mixspec, mod, mod, mod,
                const((1, d)), const((d, d)), const((d, LANE))]
    args = [x, *mixes, g1, sh2, sc2, ng, wout, router]
    in_specs += [pl.BlockSpec(memory_space=pl.ANY), pl.BlockSpec(memory_space=pl.ANY)]
    aliases = {len(args): 1, len(args) + 1: 2}
    args += list(shared)
    return pl.pallas_call(
        _route_kernel,
        out_shape=[jax.ShapeDtypeStruct((s, r, d), F32),
                   jax.ShapeDtypeStruct((n_total * ROW_TILE, LANE), F32),
                   jax.ShapeDtypeStruct((n_total, LANE), F32)],
        grid=(s, nt),
        in_specs=in_specs,
        out_specs=[pl.BlockSpec((1, tm, d), lambda b_, i: (b_, i, 0)),
                   pl.BlockSpec((tm * ROW_TILE, LANE), lambda b_, i: (blk0 + b_ * nt + i, 0)),
                   pl.BlockSpec((tm, LANE), lambda b_, i: (blk0 + b_ * nt + i, 0))],
        input_output_aliases=aliases,
        compiler_params=_cparams(("arbitrary", "arbitrary")),
        name="moe_route",
    )(*args)


def _route_tables(route, tg, n_tiles):
    n_total = route.shape[0]
    flat_e = route[:, 0:2].astype(jnp.int32).reshape(-1)
    order = jnp.argsort(flat_e, stable=True).astype(jnp.int32)
    counts = jnp.sum(flat_e[:, None] == jnp.arange(N_EXPERTS, dtype=jnp.int32)[None, :], axis=0).astype(jnp.int32)
    tiles_per = (counts + tg - 1) // tg
    tile_end = jnp.cumsum(tiles_per)
    n_used = tile_end[-1]
    tile_id = jnp.arange(n_tiles, dtype=jnp.int32)
    tile_ok = tile_id < n_used
    tile_e = jnp.sum(jnp.minimum(tile_id, n_used - 1)[:, None] >= tile_end[None, :], axis=1).astype(jnp.int32)
    sort_start = jnp.cumsum(counts) - counts
    done = (tile_id - (tile_end - tiles_per)[tile_e]) * tg
    n_valid = jnp.where(tile_ok, jnp.clip(counts[tile_e] - done, 0, tg), 0).astype(jnp.int32)
    tile_start = jnp.where(tile_ok, sort_start[tile_e] + done, 0).astype(jnp.int32)
    pad = jnp.zeros((tg,), jnp.int32)
    src = jnp.concatenate([(order >> 1) * ROW_TILE, pad])
    dst = jnp.concatenate([((order & 1) * n_total + (order >> 1)) * ROW_TILE, pad])
    return tile_e, n_valid, tile_start, src, dst


DMA_UNROLL = 8


def _experts_kernel(te_ref, nv_ref, ts_ref, src_ref, dst_ref, h2_hbm, wg_ref, wu_ref, wd_ref, out_hbm,
                    xbuf, obuf, gsem, ssem, *, tg, n_tiles, tf):
    j = pl.program_id(0)
    slot = lax.rem(j, 2)
    other = 1 - slot
    ok = nv_ref[j] > 0

    def row_tile(buf, s_, r):
        start = r * ROW_TILE if isinstance(r, int) else pl.multiple_of(r * ROW_TILE, ROW_TILE)
        return buf.at[s_, pl.ds(start, ROW_TILE), :]

    def gather_row(tile, s_, r, priority=0):
        row = pl.multiple_of(src_ref[ts_ref[tile] + r], ROW_TILE)
        pltpu.make_async_copy(h2_hbm.at[pl.ds(row, ROW_TILE), :], row_tile(xbuf, s_, r),
                              gsem.at[s_]).start(priority=priority)

    def scatter_row(tile, s_, r, priority=0):
        row = pl.multiple_of(dst_ref[ts_ref[tile] + r], ROW_TILE)
        pltpu.make_async_copy(row_tile(obuf, s_, r), out_hbm.at[pl.ds(row, ROW_TILE), :],
                              ssem.at[s_]).start(priority=priority)

    def full_tile(issue_row, tile, s_):
        def body(r8, c):
            for u in range(DMA_UNROLL):
                issue_row(tile, s_, r8 * DMA_UNROLL + u, priority=u % 2)
            return c

        lax.fori_loop(0, tg // DMA_UNROLL, body, 0)

    def gather(tile, s_):
        full_tile(gather_row, tile, s_)

    def scatter(tile, s_):
        n = nv_ref[tile]

        @pl.when(n == tg)
        def _():
            full_tile(scatter_row, tile, s_)

        @pl.when(n < tg)
        def _():
            def body(r, c):
                scatter_row(tile, s_, r)
                return c

            lax.fori_loop(0, n, body, 0)

    def wait_all(buf, sem, s_):
        pltpu.make_async_copy(buf.at[s_], buf.at[s_], sem.at[s_]).wait()

    def wait_scatter(tile, s_):
        n = nv_ref[tile]

        @pl.when(n == tg)
        def _():
            wait_all(obuf, ssem, s_)

        @pl.when(n < tg)
        def _():
            def body(r, c):
                pltpu.make_async_copy(obuf.at[s_, pl.ds(0, ROW_TILE), :], out_hbm.at[pl.ds(0, ROW_TILE), :],
                                      ssem.at[s_]).wait()
                return c

            lax.fori_loop(0, n, body, 0)

    @pl.when(jnp.logical_and(j == 0, ok))
    def _():
        gather(0, 0)

    @pl.when(jnp.logical_or(jnp.logical_and(j == 0, ok), nv_ref[jnp.maximum(j - 1, 0)] * jnp.minimum(j, 1) > 0))
    def _():
        wait_all(xbuf, gsem, slot)

    @pl.when(j >= 2)
    def _():
        wait_scatter(j - 2, slot)

    nxt = jnp.minimum(j + 1, n_tiles - 1)

    @pl.when(ok)
    def _():
        x = _load_row_tiles(xbuf, tg, lead=(slot,)).astype(BF16)

        def gather_some(i, n):
            for r in range(i * tg // n, (i + 1) * tg // n):
                gather_row(nxt, other, r, priority=r % 2)

        y = _swiglu(x, wg_ref, wu_ref, wd_ref, tf, lead=(0,), between=gather_some)
        _store_row_tiles(obuf, y, lead=(slot,))
        scatter(j, slot)

    @pl.when(j == n_tiles - 1)
    def _():
        @pl.when(ok)
        def _():
            wait_all(xbuf, gsem, other)

        wait_scatter(j - 1, other)
        wait_scatter(j, slot)


def moe_experts(h2, tables, wg, wu, wd, tg, n_tiles, tf):
    n_exp, d, ff = wg.shape
    assert d == ROW_TILE * LANE and n_tiles >= 2
    tile_e, n_valid, tile_start, src, dst = tables
    wspec = lambda shape: pl.BlockSpec(shape, lambda j, te, *_: (te[j], 0, 0))
    grid_spec = pltpu.PrefetchScalarGridSpec(
        num_scalar_prefetch=5,
        grid=(n_tiles,),
        in_specs=[pl.BlockSpec(memory_space=pl.ANY), wspec((1, d, ff)), wspec((1, d, ff)), wspec((1, ff, d))],
        out_specs=pl.BlockSpec(memory_space=pl.ANY),
        scratch_shapes=[pltpu.VMEM((2, tg * ROW_TILE, LANE), F32), pltpu.VMEM((2, tg * ROW_TILE, LANE), F32),
                        pltpu.SemaphoreType.DMA((2,)), pltpu.SemaphoreType.DMA((2,))])
    return pl.pallas_call(
        functools.partial(_experts_kernel, tg=tg, n_tiles=n_tiles, tf=tf),
        out_shape=jax.ShapeDtypeStruct((2 * h2.shape[0], LANE), F32),
        grid_spec=grid_spec,
        compiler_params=_cparams(("arbitrary",)),
        name="moe_experts",
    )(tile_e, n_valid, tile_start, src, dst, h2, wg, wu, wd)


def _combine_kernel(x1_ref, y0_ref, y1_ref, route_ref, g2_ref, fg_ref, o_ref, *, final_norm):
    r = route_ref[...]
    rows = r.shape[0]
    f = r[:, 2:3] * _load_row_tiles(y0_ref, rows, lead=(0,)) + r[:, 3:4] * _load_row_tiles(y1_ref, rows, lead=(0,))
    x2 = x1_ref[0] + _gate(f, g2_ref[0])
    if final_norm:
        x2 = _rms(x2) * fg_ref[...]
    o_ref[0] = x2


def moe_combine(x1, y, route, g2, fg, tm, row0, final_norm):
    s, r, d = x1.shape
    rm = g2.shape[1]
    nt = r // tm
    blk0 = row0 // tm
    return pl.pallas_call(
        functools.partial(_combine_kernel, final_norm=final_norm),
        out_shape=jax.ShapeDtypeStruct((s, r, d), F32),
        grid=(s, nt),
        in_specs=[pl.BlockSpec((1, tm, d), lambda b_, i: (b_, i, 0)),
                  pl.BlockSpec((1, tm * ROW_TILE, LANE), lambda b_, i: (0, blk0 + b_ * nt + i, 0)),
                  pl.BlockSpec((1, tm * ROW_TILE, LANE), lambda b_, i: (1, blk0 + b_ * nt + i, 0)),
                  pl.BlockSpec((tm, LANE), lambda b_, i: (blk0 + b_ * nt + i, 0)),
                  pl.BlockSpec((1, rm, d), lambda b_, i: (b_, 0, 0)),
                  pl.BlockSpec((1, d), lambda b_, i: (0, 0))],
        out_specs=pl.BlockSpec((1, tm, d), lambda b_, i: (b_, i, 0)),
        compiler_params=_cparams(("arbitrary", "arbitrary")),
        name="moe_combine",
    )(x1, y, y, route, g2, fg)


def _reorder_w_in(w_in, b_in):
    cut = PW_S5 + 128 + 128 + 256 + 256 + GLA_GATE_RANK
    pad = LANE - GLA_GATE_RANK
    w = jnp.concatenate([w_in[:, :cut], jnp.zeros((w_in.shape[0], pad), w_in.dtype), w_in[:, cut:]], axis=1)
    b = jnp.concatenate([b_in[:cut], jnp.zeros((pad,), b_in.dtype), b_in[cut:]])
    return w.astype(BF16), b.reshape(1, PW_TOTAL)


def _row(a):
    return a.reshape(1, -1)


class _Branch:
    def __init__(self, nseq, nb, t_len, seq_form, tm_pre, tm_post, tc, tt_seq, row0):
        self.nseq, self.nb, self.t_len, self.seq_form = nseq, nb, t_len, seq_form
        self.tm_pre, self.tm_post, self.tc, self.tt_seq, self.row0 = tm_pre, tm_post, tc, tt_seq, row0


def _layer_params(W, i):
    row = _row
    p = {}
    p['w_in'], p['b_in'] = _reorder_w_in(W['w_in'][i], W['b_in'][i])
    p['norm_g'] = row(W['norm_mix_g'][i])
    mg = W['merge_g'][i]
    p['mg'] = [row(mg[k * 256:(k + 1) * 256]) for k in range(4)]
    ab_re, ab_im, bb_re, bb_im = s5_discretise(W['s5_a_re'][i], W['s5_a_im'][i], W['s5_log_dt'][i],
                                               W['s5_b_re'][i], W['s5_b_im'][i])
    p['s5'] = (jnp.concatenate([_block_diag_in(bb_re), _block_diag_in(bb_im)], axis=1).astype(BF16),
               _block_diag_out(W['s5_c_re'][i]).astype(BF16), _block_diag_out(W['s5_c_im'][i]).astype(BF16),
               row(ab_re), row(ab_im), row(W['s5_d'][i]), W['s5_w_glu'][i].astype(BF16), row(W['s5_b_glu'][i]))
    wg2 = jnp.zeros((LANE, GLA_KEY_WIDTH), F32).at[:GLA_GATE_RANK].set(W['gla_w_gate2'][i]).astype(BF16)
    p['gla'] = (wg2, row(W['gla_b_gate2'][i]), row(W['gla_onorm_g'][i]))
    p['conv'] = (W['conv_w_dw'][i], row(W['conv_b_dw'][i]), row(W['conv_ln_g'][i]), row(W['conv_ln_b'][i]),
                 W['conv_w_pw'][i].astype(BF16), row(W['conv_b_pw'][i]))
    p['gmlp_ln'] = (row(W['gmlp_ln_g'][i]), row(W['gmlp_ln_b'][i]))
    p['gmlp_ws'], p['gmlp_bs'] = W['gmlp_w_s'][i], W['gmlp_b_s'][i]
    return p


def _mixers(x, mods, states, p, i, br, out):
    nseq, nb, t_len, seq_form = br.nseq, br.nb, br.t_len, br.seq_form
    s5_re0, s5_im0, gla0, conv0 = states
    new_re, new_im, new_gla, new_conv, new_v = out
    mg = p['mg']
    p_s5, p_gla, p_conv, p_mlp = pre_mixer(x, mods[0], mods[1], p['norm_g'], p['w_in'], p['b_in'], br.tm_pre)

    h0 = jnp.concatenate([s5_re0[i].reshape(nb, S5_LANES), s5_im0[i].reshape(nb, S5_LANES)], axis=1)
    o_s5, h_t = s5_mixer(p_s5, h0, *p['s5'], mg[0], nb, br.tc, nseq)
    new_re.append(h_t[:, :S5_LANES].reshape(nb, S5_GROUPS, S5_STATE))
    new_im.append(h_t[:, S5_LANES:].reshape(nb, S5_GROUPS, S5_STATE))

    if seq_form:
        eye = jnp.eye(GLA_HEADS, dtype=F32)
        s0 = jnp.einsum('bhkv,hg->bhvgk', gla0[i], eye).reshape(nseq, GLA_WIDTH, GLA_KEY_WIDTH)
        o_gla, s_t = gla_seq(p_gla, nseq, s0, *p['gla'], mg[1], br.tt_seq)
        s5d = s_t.reshape(nseq, GLA_HEADS, GLA_DV, GLA_HEADS, GLA_DK)
        new_gla.append(jnp.stack([jnp.swapaxes(s5d[:, h, :, h, :], 1, 2) for h in range(GLA_HEADS)], axis=1))
    else:
        ek, ev = _gla_expanders()
        o_gla, s_t = gla_recurrent(p_gla, gla0[i].reshape(nb, GLA_STATE_LANES), ek, ev, *p['gla'], mg[1], nb, t_len)
        new_gla.append(s_t.reshape(nb, GLA_HEADS, GLA_DK, GLA_DV))

    c0 = jnp.transpose(conv0[i], (1, 0, 2)).reshape(CONV_HIST * nb, CONV_DIM)
    o_conv, buf = conv_mixer(p_conv, c0, *p['conv'], mg[2], nb, br.tc, nseq)
    new_conv.append(jnp.transpose(buf.reshape(CONV_HIST, nb, CONV_DIM), (1, 0, 2)))

    ws, bs = p['gmlp_ws'], p['gmlp_bs']
    if seq_form:
        wcat = jnp.transpose(ws, (1, 0, 2)).reshape(GMLP_CHUNK, GMLP_HEADS * GMLP_CHUNK)
        bias = jnp.repeat(bs.T, GMLP_HEAD_DIM, axis=1)
        o_mlp = gmlp_seq(p_mlp, nseq, *p['gmlp_ln'], wcat, bias, mg[3], min(4 * br.tt_seq, t_len))
        new_v.append(None)
    else:
        tri = jnp.tril(jnp.ones((t_len, t_len), F32))
        wrow = jnp.repeat(jnp.transpose(ws[:, :t_len, :t_len] * tri[None], (1, 2, 0)).reshape(t_len * t_len, GMLP_HEADS),
                          GMLP_HEAD_DIM, axis=1)
        brow = jnp.repeat(bs[:, :t_len].T, GMLP_HEAD_DIM, axis=1)
        o_mlp, vn = gmlp_short(p_mlp, *p['gmlp_ln'], wrow, brow, mg[3], nb, t_len)
        new_v.append(vn)

    return [o_s5, o_gla, o_conv, o_mlp]


FF_TILE = 1408
EXPERT_FF_TILE = 256
EXPERT_ROWS = 512


def _channel_mixer(xs, mixes, mods, W, i, branches, last):
    ng, wout, fg = _row(W['norm_ffn_g'][i]), W['w_out'][i].astype(BF16), _row(W['final_norm_g'])
    j = i // 2
    if i % 2 == 0:
        wg, wu, wd = (W['ffn_w_gate'][j].astype(BF16), W['ffn_w_up'][j].astype(BF16),
                      W['ffn_w_down'][j].astype(BF16))
        return [post_dense(x, mx, m[2], m[3], m[4], m[5], ng, wout, wg, wu, wd, fg, br.tm_post, FF_TILE, last)
                for x, mx, m, br in zip(xs, mixes, mods, branches)]
    n_total = sum(x.shape[0] * x.shape[1] for x in xs)
    tg = EXPERT_ROWS
    n_tiles = 2 * n_total // tg + N_EXPERTS
    router = jnp.zeros((D_MODEL, LANE), F32).at[:, :N_EXPERTS].set(W['moe_router'][j])
    h2, route = jnp.zeros((n_total * ROW_TILE, LANE), F32), jnp.zeros((n_total, LANE), F32)
    x1s = []
    for x, mx, m, br in zip(xs, mixes, mods, branches):
        x1, h2, route = moe_route(x, mx, m[2], m[3], m[4], ng, wout, router, br.tm_post, br.row0, (h2, route))
        x1s.append(x1)
    tables = _route_tables(route, tg, n_tiles)
    y = moe_experts(h2, tables, W['moe_w_gate'][j].astype(BF16), W['moe_w_up'][j].astype(BF16),
                    W['moe_w_down'][j].astype(BF16), tg, n_tiles, EXPERT_FF_TILE)
    y = y.reshape(2, n_total * ROW_TILE, LANE)
    return [moe_combine(x1, y, route, m[5], fg, br.tm_post, br.row0, last)
            for x1, m, br in zip(x1s, mods, branches)]


def kernel(x_prompt, x_sample, c_prompt, c_sample, state_s5_re, state_s5_im, state_gla, cache_conv, ada_w, ada_b, norm_mix_g, norm_ffn_g, w_in, b_in, s5_a_re, s5_a_im, s5_log_dt, s5_b_re, s5_b_im, s5_c_re, s5_c_im, s5_d, s5_w_glu, s5_b_glu, gla_w_gate2, gla_b_gate2, gla_onorm_g, conv_w_dw, conv_b_dw, conv_ln_g, conv_ln_b, conv_w_pw, conv_b_pw, gmlp_ln_g, gmlp_ln_b, gmlp_w_s, gmlp_b_s, merge_g, w_out, ffn_w_gate, ffn_w_up, ffn_w_down, moe_router, moe_w_gate, moe_w_up, moe_w_down, final_norm_g):
    W = dict(norm_mix_g=norm_mix_g, norm_ffn_g=norm_ffn_g, w_in=w_in, b_in=b_in, s5_a_re=s5_a_re, s5_a_im=s5_a_im,
             s5_log_dt=s5_log_dt, s5_b_re=s5_b_re, s5_b_im=s5_b_im, s5_c_re=s5_c_re, s5_c_im=s5_c_im, s5_d=s5_d,
             s5_w_glu=s5_w_glu, s5_b_glu=s5_b_glu, gla_w_gate2=gla_w_gate2, gla_b_gate2=gla_b_gate2,
             gla_onorm_g=gla_onorm_g, conv_w_dw=conv_w_dw, conv_b_dw=conv_b_dw, conv_ln_g=conv_ln_g,
             conv_ln_b=conv_ln_b, conv_w_pw=conv_w_pw, conv_b_pw=conv_b_pw, gmlp_ln_g=gmlp_ln_g,
             gmlp_ln_b=gmlp_ln_b, gmlp_w_s=gmlp_w_s, gmlp_b_s=gmlp_b_s, merge_g=merge_g, w_out=w_out,
             ffn_w_gate=ffn_w_gate, ffn_w_up=ffn_w_up, ffn_w_down=ffn_w_down, moe_router=moe_router,
             moe_w_gate=moe_w_gate, moe_w_up=moe_w_up, moe_w_down=moe_w_down, final_norm_g=final_norm_g)
    depth = w_in.shape[0]
    bp, tp, d = x_prompt.shape
    bs, ts, _ = x_sample.shape

    m = ada_modulation(jnp.concatenate([c_prompt, c_sample], axis=0), ada_w, ada_b)
    mods_p = [[m[i, :bp, k * d:(k + 1) * d].reshape(bp, 1, d) for k in range(6)] for i in range(depth)]
    mods_s = [[m[i, bp:, k * d:(k + 1) * d].reshape(1, bs, d) for k in range(6)] for i in range(depth)]

    z_re = jnp.zeros((depth, bp, S5_GROUPS, S5_STATE), F32)
    z_gla = jnp.zeros((depth, bp, GLA_HEADS, GLA_DK, GLA_DV), F32)
    z_conv = jnp.zeros((depth, bp, CONV_HIST, CONV_DIM), x_prompt.dtype)
    states = [(z_re, z_re, z_gla, z_conv), (state_s5_re, state_s5_im, state_gla, cache_conv)]
    branches = [_Branch(nseq=bp, nb=bp, t_len=tp, seq_form=True, tm_pre=min(512, tp), tm_post=min(512, tp),
                        tc=min(128, tp), tt_seq=min(512, tp), row0=0),
                _Branch(nseq=1, nb=bs, t_len=ts, seq_form=False, tm_pre=ts * bs, tm_post=min(512, ts * bs),
                        tc=ts, tt_seq=None, row0=bp * tp)]
    xs = [x_prompt, jnp.transpose(x_sample, (1, 0, 2)).reshape(1, ts * bs, d)]
    outs = [([], [], [], [], []), ([], [], [], [], [])]
    for i in range(depth):
        mods = [mods_p[i], mods_s[i]]
        params = _layer_params(W, i)
        mixes = [_mixers(x, m, st_, params, i, br, o)
                 for x, m, st_, br, o in zip(xs, mods, states, branches, outs)]
        xs = _channel_mixer(xs, mixes, mods, W, i, branches, i == depth - 1)

    y_p = xs[0]
    y_s = jnp.transpose(xs[1].reshape(ts, bs, d), (1, 0, 2))
    p_re, p_im, p_gla, p_conv, _ = outs[0]
    s_re, s_im, s_gla, s_conv, s_v = outs[1]
    s_v = [jnp.transpose(v.reshape(ts, bs, GMLP_WIDTH), (1, 0, 2)) for v in s_v]
    st = jnp.stack
    return (y_p, y_s, st(p_re), st(p_im), st(p_gla), st(p_conv),
            st(s_re), st(s_im), st(s_gla), st(s_conv), st(s_v))
```

```python
import functools
import math

import jax
import jax.numpy as jnp
from jax import lax
from jax.experimental import pallas as pl
from jax.experimental.pallas import tpu as pltpu

D_MODEL = 1024
S5_WIDTH = 256
S5_GROUP = 16
S5_GROUPS = 16
S5_STATE = 64
S5_LANES = S5_GROUPS * S5_STATE
GLA_HEADS = 4
GLA_DV = 64
GLA_DK = 32
GLA_WIDTH = 256
GLA_KEY_WIDTH = 128
GLA_GATE_RANK = 16
GLA_TAU = 16.0
GLA_CHUNK = 64
GLA_STATE_LANES = GLA_HEADS * GLA_DK * GLA_DV
CONV_DIM = 256
CONV_WIDTH = 31
CONV_HIST = CONV_WIDTH - 1
GMLP_WIDTH = 256
GMLP_HEADS = 4
GMLP_HEAD_DIM = 64
GMLP_CHUNK = 128
D_FF = 2816
N_EXPERTS = 8
EPS = 1e-6

LANE = 128
PW_S5 = 256
PW_GLA = 128 + 128 + 256 + 256 + LANE
PW_CONV = 512
PW_MLP = 512
PW_TOTAL = PW_S5 + PW_GLA + PW_CONV + PW_MLP
VMEM_LIMIT = 56 * 1024 * 1024

F32 = jnp.float32
BF16 = jnp.bfloat16
HI = lax.Precision.HIGHEST


def _cparams(sem):
    return pltpu.CompilerParams(dimension_semantics=sem, vmem_limit_bytes=VMEM_LIMIT)


def _rms(x):
    return x * lax.rsqrt(jnp.mean(x * x, axis=-1, keepdims=True) + EPS)


def _layernorm(x, g, b):
    mu = jnp.mean(x, axis=-1, keepdims=True)
    xc = x - mu
    var = jnp.mean(xc * xc, axis=-1, keepdims=True)
    return xc * lax.rsqrt(var + EPS) * g + b


def _silu(x):
    return x * jax.nn.sigmoid(x)


def _gelu_tanh(x):
    return 0.5 * x * (1.0 + jnp.tanh(math.sqrt(2.0 / math.pi) * (x + 0.044715 * (x * x * x))))


def _log_sigmoid(x):
    return jnp.minimum(x, 0.0) - jnp.log(1.0 + jnp.exp(-jnp.abs(x)))


def _same_block(shape, row_block, col_block):
    r = lax.broadcasted_iota(jnp.int32, shape, 0) >> (row_block.bit_length() - 1)
    c = lax.broadcasted_iota(jnp.int32, shape, 1) >> (col_block.bit_length() - 1)
    return r == c


def _modulate(y, sc, sh):
    rm = sc.shape[0]
    if rm == 1:
        return y * (1.0 + sc) + sh
    rows, d = y.shape
    y3 = y.reshape(rows // rm, rm, d)
    return (y3 * (1.0 + sc)[None] + sh[None]).reshape(rows, d)


def _gate(y, g):
    rm = g.shape[0]
    if rm == 1:
        return y * g
    rows, d = y.shape
    return (y.reshape(rows // rm, rm, d) * g[None]).reshape(rows, d)


def _ada_kernel(c_ref, w_ref, b_ref, o_ref):
    c = c_ref[...]
    s = _silu(c).astype(BF16)
    o_ref[0] = jnp.dot(s, w_ref[0].astype(BF16), preferred_element_type=F32) + b_ref[0]


def ada_modulation(c_all, ada_w, ada_b):
    depth, d, n6 = ada_w.shape
    rows = c_all.shape[0]
    tn = 1536
    return pl.pallas_call(
        _ada_kernel,
        out_shape=jax.ShapeDtypeStruct((depth, rows, n6), F32),
        grid=(depth, n6 // tn),
        in_specs=[pl.BlockSpec((rows, d), lambda l, j: (0, 0)),
                  pl.BlockSpec((1, d, tn), lambda l, j: (l, 0, j)),
                  pl.BlockSpec((1, 1, tn), lambda l, j: (l, 0, j))],
        out_specs=pl.BlockSpec((1, rows, tn), lambda l, j: (l, 0, j)),
        compiler_params=_cparams(("arbitrary", "arbitrary")),
        name="ada_modulation",
    )(c_all, ada_w, ada_b.reshape(depth, 1, n6))


def _pre_kernel(x_ref, sh_ref, sc_ref, g_ref, w_ref, b_ref, o_s5, o_gla, o_conv, o_mlp):
    x = x_ref[0]
    y = _modulate(_rms(x) * g_ref[...], sc_ref[0], sh_ref[0])
    p = jnp.dot(y.astype(BF16), w_ref[...], preferred_element_type=F32) + b_ref[...]
    o_s5[...] = p[:, 0:PW_S5]
    o_gla[...] = p[:, PW_S5:PW_S5 + PW_GLA]
    o_conv[...] = p[:, PW_S5 + PW_GLA:PW_S5 + PW_GLA + PW_CONV]
    o_mlp[...] = p[:, PW_S5 + PW_GLA + PW_CONV:PW_TOTAL]


def pre_mixer(x, sh, sc, g, w, b, tm):
    s, r, d = x.shape
    rm = sh.shape[1]
    widths = (PW_S5, PW_GLA, PW_CONV, PW_MLP)
    return pl.pallas_call(
        _pre_kernel,
        out_shape=[jax.ShapeDtypeStruct((r, s * w_), F32) for w_ in widths],
        grid=(s, r // tm),
        in_specs=[pl.BlockSpec((1, tm, d), lambda b_, i: (b_, i, 0)),
                  pl.BlockSpec((1, rm, d), lambda b_, i: (b_, 0, 0)),
                  pl.BlockSpec((1, rm, d), lambda b_, i: (b_, 0, 0)),
                  pl.BlockSpec((1, d), lambda b_, i: (0, 0)),
                  pl.BlockSpec((d, PW_TOTAL), lambda b_, i: (0, 0)),
                  pl.BlockSpec((1, PW_TOTAL), lambda b_, i: (0, 0))],
        out_specs=[pl.BlockSpec((tm, w_), lambda b_, i: (i, b_)) for w_ in widths],
        compiler_params=_cparams(("arbitrary", "arbitrary")),
        name="pre_mixer",
    )(x, sh, sc, g, w, b)


def _s5_disc_kernel(lr_ref, li_ref, ldt_ref, br_ref, bi_ref, abr_ref, abi_ref, bbr_ref, bbi_ref):
    lr = lr_ref[...]
    li = li_ref[...]
    dt = jnp.exp(ldt_ref[...])
    mag = jnp.exp(lr * dt)
    ang = li * dt
    ab_re = mag * jnp.cos(ang)
    ab_im = mag * jnp.sin(ang)
    den = lr * lr + li * li
    nr = ab_re - 1.0
    f_re = (nr * lr + ab_im * li) / den
    f_im = (ab_im * lr - nr * li) / den
    br = br_ref[...]
    bi = bi_ref[...]
    abr_ref[...] = ab_re
    abi_ref[...] = ab_im
    bbr_ref[...] = f_re * br - f_im * bi
    bbi_ref[...] = f_re * bi + f_im * br


def s5_discretise(a_re, a_im, log_dt, b_re, b_im):
    n = b_re.shape[-1]
    gp = a_re.size
    bc = lambda a: jnp.broadcast_to(a.reshape(gp, 1), (gp, n))
    ldt = jnp.broadcast_to(log_dt[:, None], a_re.shape)
    outs = pl.pallas_call(
        _s5_disc_kernel,
        out_shape=[jax.ShapeDtypeStruct((gp, n), F32)] * 4,
        name="s5_discretise",
    )(bc(a_re), bc(a_im), bc(ldt), b_re.reshape(gp, n), b_im.reshape(gp, n))
    ab_re, ab_im, bb_re, bb_im = outs
    return ab_re[:, 0], ab_im[:, 0], bb_re, bb_im


def _block_diag_in(bb):
    g, p, n = S5_GROUPS, S5_STATE, S5_GROUP
    b3 = bb.reshape(g, p, n)
    eye = jnp.eye(g, dtype=bb.dtype)
    return jnp.einsum('gpn,gh->gnhp', b3, eye).reshape(g * n, g * p)


def _block_diag_out(c):
    g, p, n = S5_GROUPS, S5_STATE, S5_GROUP
    eye = jnp.eye(g, dtype=c.dtype)
    return jnp.einsum('gnp,gh->gphn', c, eye).reshape(g * p, g * n)


def _to_time_major(x_ref, cols, tm_ref, row0, nsl):
    rt = x_ref.shape[0]
    w = x_ref.shape[1] // nsl
    start, width = cols
    for l in range(nsl):
        for h in range(width // LANE):
            c0 = l * w + start + h * LANE
            tm_ref[h, pl.ds(row0 + l, rt, stride=nsl), :] = x_ref[:, c0:c0 + LANE]


def _from_time_major(tm_ref, o_ref, nsl):
    rt = o_ref.shape[0]
    nh = tm_ref.shape[0]
    for l in range(nsl):
        piece = jnp.concatenate([tm_ref[h, pl.ds(l, rt, stride=nsl), :] for h in range(nh)], axis=1)
        o_ref[:, l * nh * LANE:(l + 1) * nh * LANE] = piece.astype(o_ref.dtype)


def _lane_tiles(tm_ref, rows=slice(None)):
    return jnp.concatenate([tm_ref[h, rows, :] for h in range(tm_ref.shape[0])], axis=1)


def _set_lane_tiles(tm_ref, x):
    for h in range(tm_ref.shape[0]):
        tm_ref[h] = x[:, h * LANE:(h + 1) * LANE]


def _s5_kernel(u_ref, h0_ref, bblk_ref, cre_ref, cim_ref, ar_ref, ai_ref, d_ref, wglu_ref, bglu_ref, mg_ref,
               o_ref, hT_ref, xs_ref, hs_ref, tm_ref, *, nb, tc, nsl):
    i = pl.program_id(0)

    @pl.when(i == 0)
    def _():
        hs_ref[...] = h0_ref[...]

    _to_time_major(u_ref, (0, S5_WIDTH), tm_ref, 0, nsl)
    u = _lane_tiles(tm_ref)
    xs_ref[...] = jnp.dot(u.astype(BF16), bblk_ref[...], preferred_element_type=F32)
    ar = jnp.broadcast_to(ar_ref[...], (nb, S5_LANES))
    ai = jnp.broadcast_to(ai_ref[...], (nb, S5_LANES))

    def step(t, carry):
        hr, hi = carry
        row = pl.multiple_of(t * nb, nb)
        xr = xs_ref[pl.ds(row, nb), 0:S5_LANES]
        xi = xs_ref[pl.ds(row, nb), S5_LANES:2 * S5_LANES]
        nr = ar * hr - ai * hi + xr
        ni = ar * hi + ai * hr + xi
        xs_ref[pl.ds(row, nb), 0:S5_LANES] = nr
        xs_ref[pl.ds(row, nb), S5_LANES:2 * S5_LANES] = ni
        return nr, ni

    hr, hi = lax.fori_loop(0, tc, step, (hs_ref[:, 0:S5_LANES], hs_ref[:, S5_LANES:2 * S5_LANES]),
                           unroll=True if tc <= 8 else 4)
    hs_ref[:, 0:S5_LANES] = hr
    hs_ref[:, S5_LANES:2 * S5_LANES] = hi

    y = (jnp.dot(xs_ref[:, 0:S5_LANES].astype(BF16), cre_ref[...], preferred_element_type=F32)
         - jnp.dot(xs_ref[:, S5_LANES:2 * S5_LANES].astype(BF16), cim_ref[...], preferred_element_type=F32))
    y = y + d_ref[...] * u
    y = _gelu_tanh(y)
    y = y * jax.nn.sigmoid(jnp.dot(y.astype(BF16), wglu_ref[...], preferred_element_type=F32) + bglu_ref[...])
    _set_lane_tiles(tm_ref, _rms(y) * mg_ref[...])
    _from_time_major(tm_ref, o_ref, nsl)

    @pl.when(i == pl.num_programs(0) - 1)
    def _():
        hT_ref[...] = hs_ref[...]


def s5_mixer(u, h0, bblk, cre, cim, ar, ai, d, wglu, bglu, mg, nb, tc, nsl):
    rows = u.shape[0]
    rc = nb * tc
    rt = rc // nsl
    full = lambda shape: pl.BlockSpec(shape, lambda i: (0,) * len(shape))
    return pl.pallas_call(
        functools.partial(_s5_kernel, nb=nb, tc=tc, nsl=nsl),
        out_shape=[jax.ShapeDtypeStruct((rows, nsl * S5_WIDTH), BF16),
                   jax.ShapeDtypeStruct((nb, 2 * S5_LANES), F32)],
        grid=(rows // rt,),
        in_specs=[pl.BlockSpec((rt, nsl * S5_WIDTH), lambda i: (i, 0)),
                  full((nb, 2 * S5_LANES)),
                  full((S5_WIDTH, 2 * S5_LANES)),
                  full((S5_LANES, S5_WIDTH)), full((S5_LANES, S5_WIDTH)),
                  full((1, S5_LANES)), full((1, S5_LANES)),
                  full((1, S5_WIDTH)), full((S5_WIDTH, S5_WIDTH)), full((1, S5_WIDTH)), full((1, S5_WIDTH))],
        out_specs=[pl.BlockSpec((rt, nsl * S5_WIDTH), lambda i: (i, 0)),
                   full((nb, 2 * S5_LANES))],
        scratch_shapes=[pltpu.VMEM((rc, 2 * S5_LANES), F32), pltpu.VMEM((nb, 2 * S5_LANES), F32),
                        pltpu.VMEM((S5_WIDTH // LANE, rc, LANE), F32)],
        compiler_params=_cparams(("arbitrary",)),
        name="s5_mixer",
    )(u, h0, bblk, cre, cim, ar, ai, d, wglu, bglu, mg)


CONV_ROWS = 64


def _conv_kernel(ag_ref, c0_ref, wdw_ref, bdw_ref, lng_ref, lnb_ref, wpw_ref, bpw_ref, mg_ref,
                 o_ref, buf_ref, zc_ref, y_ref, *, nb, tc, nsl):
    i = pl.program_id(0)
    hist = CONV_HIST * nb
    rc = nb * tc
    rt = rc // nsl
    n_lt = CONV_DIM // LANE

    @pl.when(i == 0)
    def _():
        for h in range(n_lt):
            zc_ref[h, 0:hist, :] = c0_ref[:, h * LANE:(h + 1) * LANE]

    @pl.when(i > 0)
    def _():
        for h in range(n_lt):
            zc_ref[h, 0:hist, :] = zc_ref[h, rc:rc + hist, :]

    w_seq = 2 * CONV_DIM
    for l in range(nsl):
        a = ag_ref[:, l * w_seq:l * w_seq + CONV_DIM]
        g = ag_ref[:, l * w_seq + CONV_DIM:(l + 1) * w_seq]
        z = a * jax.nn.sigmoid(g)
        for h in range(n_lt):
            zc_ref[h, pl.ds(hist + l, rt, stride=nsl), :] = z[:, h * LANE:(h + 1) * LANE]

    w = wdw_ref[...]

    def tile(j, carry):
        r0 = pl.multiple_of(j * CONV_ROWS, CONV_ROWS)
        for h in range(n_lt):
            acc = jnp.zeros((CONV_ROWS, LANE), F32)
            for k in range(CONV_WIDTH):
                acc = acc + w[k:k + 1, h * LANE:(h + 1) * LANE] * zc_ref[h, pl.ds(r0 + k * nb, CONV_ROWS), :]
            y_ref[h, pl.ds(r0, CONV_ROWS), :] = acc
        return carry

    lax.fori_loop(0, rc // CONV_ROWS, tile, 0)
    y = _lane_tiles(y_ref) + bdw_ref[...]
    y = _silu(_layernorm(y, lng_ref[...], lnb_ref[...]))
    y = jnp.dot(y.astype(BF16), wpw_ref[...], preferred_element_type=F32) + bpw_ref[...]
    _set_lane_tiles(y_ref, _rms(y) * mg_ref[...])
    _from_time_major(y_ref, o_ref, nsl)

    @pl.when(i == pl.num_programs(0) - 1)
    def _():
        buf_ref[...] = _lane_tiles(zc_ref, slice(rc, rc + hist))


def conv_mixer(ag, c0, wdw, bdw, lng, lnb, wpw, bpw, mg, nb, tc, nsl):
    rows = ag.shape[0]
    rc = nb * tc
    rt = rc // nsl
    hist = CONV_HIST * nb
    assert rows == rt or tc >= CONV_HIST
    n_lt = CONV_DIM // LANE
    full = lambda shape: pl.BlockSpec(shape, lambda i: (0,) * len(shape))
    return pl.pallas_call(
        functools.partial(_conv_kernel, nb=nb, tc=tc, nsl=nsl),
        out_shape=[jax.ShapeDtypeStruct((rows, nsl * CONV_DIM), BF16),
                   jax.ShapeDtypeStruct((hist, CONV_DIM), F32)],
        grid=(rows // rt,),
        in_specs=[pl.BlockSpec((rt, nsl * 2 * CONV_DIM), lambda i: (i, 0)),
                  full((hist, CONV_DIM)), full((CONV_WIDTH, CONV_DIM)),
                  full((1, CONV_DIM)), full((1, CONV_DIM)), full((1, CONV_DIM)),
                  full((CONV_DIM, CONV_DIM)), full((1, CONV_DIM)), full((1, CONV_DIM))],
        out_specs=[pl.BlockSpec((rt, nsl * CONV_DIM), lambda i: (i, 0)), full((hist, CONV_DIM))],
        scratch_shapes=[pltpu.VMEM((n_lt, hist + rc, LANE), F32), pltpu.VMEM((n_lt, rc, LANE), F32)],
        compiler_params=_cparams(("arbitrary",)),
        name="conv_mixer",
    )(ag, c0, wdw, bdw, lng, lnb, wpw, bpw, mg)


def _gmlp_seq_kernel(uv_ref, lng_ref, lnb_ref, wcat_ref, bias_ref, mg_ref, o_ref, *, tt):
    n_chunks = tt // GMLP_CHUNK
    kc = GMLP_HEADS * GMLP_CHUNK
    rowi = lax.broadcasted_iota(jnp.int32, (GMLP_CHUNK, kc), 0)
    coli = lax.broadcasted_iota(jnp.int32, (GMLP_CHUNK, kc), 1)
    wcat = jnp.where((coli & (GMLP_CHUNK - 1)) <= rowi, wcat_ref[...], 0.0).astype(BF16)
    sel = _same_block((kc, GMLP_WIDTH), GMLP_CHUNK, GMLP_HEAD_DIM)
    for c in range(n_chunks):
        rows = slice(c * GMLP_CHUNK, (c + 1) * GMLP_CHUNK)
        u = uv_ref[rows, 0:GMLP_WIDTH]
        v = uv_ref[rows, GMLP_WIDTH:2 * GMLP_WIDTH]
        vn = _layernorm(v, lng_ref[...], lnb_ref[...])
        vbd = jnp.where(sel, jnp.concatenate([vn] * GMLP_HEADS, axis=0), 0.0).astype(BF16)
        mixed = jnp.dot(wcat, vbd, preferred_element_type=F32) + bias_ref[...]
        o_ref[rows, :] = (_rms(u * mixed) * mg_ref[...]).astype(o_ref.dtype)


def gmlp_seq(uv, nseq, lng, lnb, wcat, bias, mg, tt):
    t = uv.shape[0]
    full = lambda shape: pl.BlockSpec(shape, lambda b_, i: (0,) * len(shape))
    return pl.pallas_call(
        functools.partial(_gmlp_seq_kernel, tt=tt),
        out_shape=jax.ShapeDtypeStruct((t, nseq * GMLP_WIDTH), BF16),
        grid=(nseq, t // tt),
        in_specs=[pl.BlockSpec((tt, 2 * GMLP_WIDTH), lambda b_, i: (i, b_)),
                  full((1, GMLP_WIDTH)), full((1, GMLP_WIDTH)),
                  full((GMLP_CHUNK, GMLP_HEADS * GMLP_CHUNK)), full((GMLP_CHUNK, GMLP_WIDTH)),
                  full((1, GMLP_WIDTH))],
        out_specs=pl.BlockSpec((tt, GMLP_WIDTH), lambda b_, i: (i, b_)),
        compiler_params=_cparams(("arbitrary", "arbitrary")),
        name="gmlp_seq",
    )(uv, lng, lnb, wcat, bias, mg)


def _gmlp_short_kernel(uv_ref, lng_ref, lnb_ref, wrow_ref, brow_ref, mg_ref, o_ref, vn_ref, *, nb, t_len):
    u = uv_ref[:, 0:GMLP_WIDTH]
    v = uv_ref[:, GMLP_WIDTH:2 * GMLP_WIDTH]
    vn = _layernorm(v, lng_ref[...], lnb_ref[...])
    vn_ref[...] = vn
    wrow = wrow_ref[...]
    brow = brow_ref[...]
    for t in range(t_len):
        mixed = jnp.zeros((nb, GMLP_WIDTH), F32) + brow[t:t + 1, :]
        for j in range(t + 1):
            mixed = mixed + wrow[t * t_len + j:t * t_len + j + 1, :] * vn[j * nb:(j + 1) * nb, :]
        o = u[t * nb:(t + 1) * nb, :] * mixed
        o_ref[t * nb:(t + 1) * nb, :] = (_rms(o) * mg_ref[...]).astype(o_ref.dtype)


def gmlp_short(uv, lng, lnb, wrow, brow, mg, nb, t_len):
    rows = uv.shape[0]
    return pl.pallas_call(
        functools.partial(_gmlp_short_kernel, nb=nb, t_len=t_len),
        out_shape=[jax.ShapeDtypeStruct((rows, GMLP_WIDTH), BF16),
                   jax.ShapeDtypeStruct((rows, GMLP_WIDTH), F32)],
        compiler_params=pltpu.CompilerParams(vmem_limit_bytes=VMEM_LIMIT),
        name="gmlp_short",
    )(uv, lng, lnb, wrow, brow, mg)


def _split3(x):
    a = x.astype(BF16)
    r1 = x - a.astype(F32)
    b = r1.astype(BF16)
    c = (r1 - b.astype(F32)).astype(BF16)
    return a, b, c


def _dot_exact_rhs(x, m):
    return sum(jnp.dot(t, m, preferred_element_type=F32) for t in _split3(x))


def _dot_exact_lhs(m, x):
    return sum(jnp.dot(m, t, preferred_element_type=F32) for t in _split3(x))


def _gla_tail(o, r, gmean, onorm, mg):
    ms = _dot_exact_rhs(o * o, gmean)
    o = o * lax.rsqrt(ms + EPS) * onorm
    o = o * _silu(r)
    return _rms(o) * mg


def _head_mean_matrix():
    return jnp.where(_same_block((GLA_WIDTH, GLA_WIDTH), GLA_DV, GLA_DV), 1.0 / GLA_DV, 0.0).astype(BF16)


GLA_SEQS = 4


def _gla_seq_kernel(x_ref, s0_ref, wg_ref, bg_ref, onorm_ref, mg_ref, o_ref, sT_ref,
                    s_ref, qt_ref, kt_ref, kd_ref, dl_ref, oacc_ref, *, tt):
    i = pl.program_id(1)
    L = GLA_CHUNK
    n_ch = tt // L
    kw, vw = GLA_KEY_WIDTH, GLA_WIDTH

    @pl.when(i == 0)
    def _():
        s_ref[...] = s0_ref[...]

    rows_i = lax.broadcasted_iota(jnp.int32, (tt, tt), 0)
    cols_i = lax.broadcasted_iota(jnp.int32, (tt, tt), 1)
    shift = L.bit_length() - 1
    same_chunk = (rows_i >> shift) == (cols_i >> shift)
    tri = jnp.logical_and(same_chunk, cols_i <= rows_i).astype(BF16)
    chunk_sum = same_chunk.astype(BF16)
    chunk_rows = (lax.broadcasted_iota(jnp.int32, (n_ch, tt), 0)
                  == (lax.broadcasted_iota(jnp.int32, (n_ch, tt), 1) >> shift)).astype(BF16)
    kbd_sel = _same_block((GLA_HEADS * L, kw), L, GLA_DK)
    vbd_sel = _same_block((GLA_HEADS * L, vw), L, GLA_DV)
    causal = ((lax.broadcasted_iota(jnp.int32, (L, GLA_HEADS * L), 1) & (L - 1))
              <= lax.broadcasted_iota(jnp.int32, (L, GLA_HEADS * L), 0))
    s_sel = _same_block((vw, kw), GLA_DV, GLA_DK)
    gmean = _head_mean_matrix()
    scale = GLA_DK ** -0.5
    nt_dims = (((1,), (1,)), ((), ()))
    tn_dims = (((0,), (0,)), ((), ()))
    zero = jnp.zeros((), BF16)

    for g in range(GLA_SEQS):
        x0 = g * PW_GLA
        q = x_ref[:, x0:x0 + kw] * scale
        k = x_ref[:, x0 + kw:x0 + 2 * kw]
        gl = x_ref[:, x0 + 2 * kw + 2 * vw:x0 + PW_GLA]
        la = _log_sigmoid(jnp.dot(gl.astype(BF16), wg_ref[...], preferred_element_type=F32) + bg_ref[...])
        la = la / GLA_TAU
        bc = _dot_exact_lhs(tri, la)
        b_end = _dot_exact_lhs(chunk_sum, la)
        qt_ref[g] = (q * jnp.exp(bc)).astype(BF16)
        kt_ref[g] = (k * jnp.exp(-bc)).astype(BF16)
        kd_ref[g] = (k * jnp.exp(b_end - bc)).astype(BF16)
        dl_ref[g] = jnp.exp(_dot_exact_lhs(chunk_rows, la))

    def chunk(c, carry):
        r0 = pl.multiple_of(c * L, L)
        for g in range(GLA_SEQS):
            x0 = g * PW_GLA
            qt = qt_ref[g, pl.ds(r0, L), :]
            kt = kt_ref[g, pl.ds(r0, L), :]
            kdec = kd_ref[g, pl.ds(r0, L), :]
            vb = x_ref[pl.ds(r0, L), x0 + 2 * kw:x0 + 2 * kw + vw].astype(BF16)
            kbd = jnp.where(kbd_sel, jnp.concatenate([kt] * GLA_HEADS, axis=0), zero)
            att = lax.dot_general(qt, kbd, nt_dims, preferred_element_type=F32)
            att = jnp.where(causal, att, 0.0).astype(BF16)
            vbd = jnp.where(vbd_sel, jnp.concatenate([vb] * GLA_HEADS, axis=0), zero)
            st = s_ref[g]
            oacc_ref[g, pl.ds(r0, L), :] = (jnp.dot(att, vbd, preferred_element_type=F32)
                                            + lax.dot_general(qt, st.astype(BF16), nt_dims,
                                                              preferred_element_type=F32))
            upd = lax.dot_general(vb, kdec, tn_dims, preferred_element_type=F32)
            s_ref[g] = st * dl_ref[g, pl.ds(c, 1), :] + jnp.where(s_sel, upd, 0.0)
        return carry

    lax.fori_loop(0, n_ch, chunk, 0)

    for g in range(GLA_SEQS):
        x0 = g * PW_GLA
        r = x_ref[:, x0 + 2 * kw + vw:x0 + 2 * kw + 2 * vw]
        o_ref[:, g * vw:(g + 1) * vw] = _gla_tail(oacc_ref[g], r, gmean, onorm_ref[...],
                                                  mg_ref[...]).astype(o_ref.dtype)

    @pl.when(i == pl.num_programs(1) - 1)
    def _():
        sT_ref[...] = s_ref[...]


def gla_seq(x, nseq, s0, wg, bg, onorm, mg, tt):
    t = x.shape[0]
    g = GLA_SEQS
    assert nseq % g == 0
    full = lambda shape: pl.BlockSpec(shape, lambda b_, i: (0,) * len(shape))
    state = pl.BlockSpec((g, GLA_WIDTH, GLA_KEY_WIDTH), lambda b_, i: (b_, 0, 0))
    return pl.pallas_call(
        functools.partial(_gla_seq_kernel, tt=tt),
        out_shape=[jax.ShapeDtypeStruct((t, nseq * GLA_WIDTH), BF16),
                   jax.ShapeDtypeStruct((nseq, GLA_WIDTH, GLA_KEY_WIDTH), F32)],
        grid=(nseq // g, t // tt),
        in_specs=[pl.BlockSpec((tt, g * PW_GLA), lambda b_, i: (i, b_)),
                  state,
                  full((LANE, GLA_KEY_WIDTH)), full((1, GLA_KEY_WIDTH)),
                  full((1, GLA_WIDTH)), full((1, GLA_WIDTH))],
        out_specs=[pl.BlockSpec((tt, g * GLA_WIDTH), lambda b_, i: (i, b_)), state],
        scratch_shapes=[pltpu.VMEM((g, GLA_WIDTH, GLA_KEY_WIDTH), F32),
                        pltpu.VMEM((g, tt, GLA_KEY_WIDTH), BF16), pltpu.VMEM((g, tt, GLA_KEY_WIDTH), BF16),
                        pltpu.VMEM((g, tt, GLA_KEY_WIDTH), BF16),
                        pltpu.VMEM((g, tt // GLA_CHUNK, GLA_KEY_WIDTH), F32),
                        pltpu.VMEM((g, tt, GLA_WIDTH), F32)],
        compiler_params=_cparams(("arbitrary", "arbitrary")),
        name="gla_seq",
    )(x, s0, wg, bg, onorm, mg)


def _gla_rec_kernel(x_ref, s0_ref, ek_ref, ev_ref, wg_ref, bg_ref, onorm_ref, mg_ref, o_ref, sT_ref,
                    *, nb, t_len):
    kw, vw = GLA_KEY_WIDTH, GLA_WIDTH
    hl = GLA_DK * GLA_DV
    sT_ref[...] = s0_ref[...]
    gmean = _head_mean_matrix()
    scale = GLA_DK ** -0.5

    def step(t, carry):
        r0 = pl.multiple_of(t * nb, nb)
        q = x_ref[pl.ds(r0, nb), 0:kw] * scale
        k = x_ref[pl.ds(r0, nb), kw:2 * kw]
        v = x_ref[pl.ds(r0, nb), 2 * kw:2 * kw + vw]
        r = x_ref[pl.ds(r0, nb), 2 * kw + vw:2 * kw + 2 * vw]
        gl = x_ref[pl.ds(r0, nb), 2 * kw + 2 * vw:2 * kw + 2 * vw + LANE]
        la = _log_sigmoid(jnp.dot(gl.astype(BF16), wg_ref[...], preferred_element_type=F32) + bg_ref[...])
        a = jnp.exp(la / GLA_TAU)
        a3 = _split3(a)
        qb = q.astype(BF16)
        kb = k.astype(BF16)
        vb = v.astype(BF16)
        outs = []
        for h in range(GLA_HEADS):
            lanes = slice(h * hl, (h + 1) * hl)
            ek = ek_ref[:, lanes]
            a_e = (jnp.dot(a3[0], ek, preferred_element_type=F32)
                   + jnp.dot(a3[1], ek, preferred_element_type=F32)
                   + jnp.dot(a3[2], ek, preferred_element_type=F32))
            k_e = jnp.dot(kb, ek, preferred_element_type=F32)
            q_e = jnp.dot(qb, ek, preferred_element_type=F32)
            v_e = jnp.dot(vb, ev_ref[:, lanes], preferred_element_type=F32)
            s_new = a_e * sT_ref[:, lanes] + k_e * v_e
            sT_ref[:, lanes] = s_new
            prod = q_e * s_new
            acc = prod[:, 0:LANE]
            for j in range(1, hl // LANE):
                acc = acc + prod[:, j * LANE:(j + 1) * LANE]
            outs.append(acc[:, 0:GLA_DV] + acc[:, GLA_DV:2 * GLA_DV])
        o = jnp.concatenate(outs, axis=1)
        o_ref[pl.ds(r0, nb), :] = _gla_tail(o, r, gmean, onorm_ref[...], mg_ref[...]).astype(o_ref.dtype)
        return carry

    lax.fori_loop(0, t_len, step, 0)


def gla_recurrent(x, s0, ek, ev, wg, bg, onorm, mg, nb, t_len):
    rows = x.shape[0]
    return pl.pallas_call(
        functools.partial(_gla_rec_kernel, nb=nb, t_len=t_len),
        out_shape=[jax.ShapeDtypeStruct((rows, GLA_WIDTH), BF16),
                   jax.ShapeDtypeStruct((nb, GLA_STATE_LANES), F32)],
        compiler_params=pltpu.CompilerParams(vmem_limit_bytes=VMEM_LIMIT),
        name="gla_recurrent",
    )(x, s0, ek, ev, wg, bg, onorm, mg)


def _gla_expanders():
    lane = jnp.arange(GLA_STATE_LANES)
    h = lane // (GLA_DK * GLA_DV)
    dk = (lane // GLA_DV) % GLA_DK
    dv = lane % GLA_DV
    ek = (jnp.arange(GLA_KEY_WIDTH)[:, None] == (h * GLA_DK + dk)[None, :]).astype(BF16)
    ev = (jnp.arange(GLA_WIDTH)[:, None] == (h * GLA_DV + dv)[None, :]).astype(BF16)
    return ek, ev


def _mix_residual(x_ref, m_refs, g1_ref, wout_ref):
    mix = jnp.concatenate([m[...] for m in m_refs], axis=1)
    proj = jnp.dot(mix, wout_ref[...], preferred_element_type=F32)
    return x_ref[0] + _gate(proj, g1_ref[0])


def _swiglu(h, wg_ref, wu_ref, wd_ref, tf, lead=(), between=None):
    ff = wg_ref.shape[-1]
    n_dots = 3 * (ff // tf)
    tick = (lambda i: between(i, n_dots)) if between is not None else (lambda i: None)
    acc = jnp.zeros((h.shape[0], wd_ref.shape[-1]), F32)
    for c in range(ff // tf):
        cols = slice(c * tf, (c + 1) * tf)
        gate = jnp.dot(h, wg_ref[(*lead, slice(None), cols)], preferred_element_type=F32)
        tick(3 * c)
        up = jnp.dot(h, wu_ref[(*lead, slice(None), cols)], preferred_element_type=F32)
        tick(3 * c + 1)
        acc = acc + jnp.dot((_silu(gate) * up).astype(BF16), wd_ref[(*lead, cols, slice(None))],
                            preferred_element_type=F32)
        tick(3 * c + 2)
    return acc


def _post_dense_kernel(x_ref, m0_ref, m1_ref, m2_ref, m3_ref, g1_ref, sh2_ref, sc2_ref, g2_ref, ng_ref, wout_ref,
                       wg_ref, wu_ref, wd_ref, fg_ref, o_ref, *, final_norm, tf):
    x1 = _mix_residual(x_ref, (m0_ref, m1_ref, m2_ref, m3_ref), g1_ref, wout_ref)
    h = _modulate(_rms(x1) * ng_ref[...], sc2_ref[0], sh2_ref[0]).astype(BF16)
    x2 = x1 + _gate(_swiglu(h, wg_ref, wu_ref, wd_ref, tf), g2_ref[0])
    if final_norm:
        x2 = _rms(x2) * fg_ref[...]
    o_ref[0] = x2


def post_dense(x, mixes, g1, sh2, sc2, g2, ng, wout, wg, wu, wd, fg, tm, tf, final_norm):
    s, r, d = x.shape
    rm = g1.shape[1]
    ff = wg.shape[1]
    mod = pl.BlockSpec((1, rm, d), lambda b_, i: (b_, 0, 0))
    const = lambda shape: pl.BlockSpec(shape, lambda b_, i: (0,) * len(shape))
    resident = lambda shape: pl.BlockSpec(shape, lambda b_, i: (0,) * len(shape), pipeline_mode=pl.Buffered(1))
    mixspec = pl.BlockSpec((tm, 256), lambda b_, i: (i, b_))
    return pl.pallas_call(
        functools.partial(_post_dense_kernel, final_norm=final_norm, tf=tf),
        out_shape=jax.ShapeDtypeStruct((s, r, d), F32),
        grid=(s, r // tm),
        in_specs=[pl.BlockSpec((1, tm, d), lambda b_, i: (b_, i, 0)),
                  mixspec, mixspec, mixspec, mixspec,
                  mod, mod, mod, mod,
                  const((1, d)), resident((d, d)),
                  resident((d, ff)), resident((d, ff)), resident((ff, d)),
                  const((1, d))],
        out_specs=pl.BlockSpec((1, tm, d), lambda b_, i: (b_, i, 0)),
        compiler_params=_cparams(("arbitrary", "arbitrary")),
        name="post_dense",
    )(x, *mixes, g1, sh2, sc2, g2, ng, wout, wg, wu, wd, fg)


ROW_TILE = 8


def _store_row_tiles(ref, x, lead=()):
    rows = x.shape[0]
    for s in range(ROW_TILE):
        ref[(*lead, pl.ds(s, rows, stride=ROW_TILE), slice(None))] = x[:, s * LANE:(s + 1) * LANE]


def _load_row_tiles(ref, rows, lead=()):
    return jnp.concatenate([ref[(*lead, pl.ds(s, rows, stride=ROW_TILE), slice(None))] for s in range(ROW_TILE)],
                           axis=1)


def _route_kernel(x_ref, m0_ref, m1_ref, m2_ref, m3_ref, g1_ref, sh2_ref, sc2_ref, ng_ref, wout_ref, router_ref,
                  *rest):
    x1_ref, h2_ref, route_ref = rest[-3:]
    x1 = _mix_residual(x_ref, (m0_ref, m1_ref, m2_ref, m3_ref), g1_ref, wout_ref)
    x1_ref[0] = x1
    h = _modulate(_rms(x1) * ng_ref[...], sc2_ref[0], sh2_ref[0])
    _store_row_tiles(h2_ref, h)
    h_hi = h.astype(BF16)
    h_lo = (h - h_hi.astype(F32)).astype(BF16)
    w = router_ref[...]
    w_hi = w.astype(BF16)
    w_lo = (w - w_hi.astype(F32)).astype(BF16)
    logits = (jnp.dot(h_hi, w_hi, preferred_element_type=F32) + jnp.dot(h_lo, w_hi, preferred_element_type=F32)
              + jnp.dot(h_hi, w_lo, preferred_element_type=F32))
    lane = lax.broadcasted_iota(jnp.int32, logits.shape, 1).astype(F32)
    neg = jnp.float32(-jnp.inf)
    logits = jnp.where(lane < N_EXPERTS, logits, neg)
    m1 = jnp.max(logits, axis=1, keepdims=True)
    i1 = jnp.min(jnp.where(logits == m1, lane, float(LANE)), axis=1, keepdims=True)
    others = jnp.where(lane == i1, neg, logits)
    m2 = jnp.max(others, axis=1, keepdims=True)
    i2 = jnp.min(jnp.where(others == m2, lane, float(LANE)), axis=1, keepdims=True)
    e2 = jnp.exp(m2 - m1)
    den = 1.0 + e2
    route_ref[...] = (jnp.where(lane == 0.0, i1, 0.0) + jnp.where(lane == 1.0, i2, 0.0)
                      + jnp.where(lane == 2.0, 1.0 / den, 0.0) + jnp.where(lane == 3.0, e2 / den, 0.0))


def moe_route(x, mixes, g1, sh2, sc2, ng, wout, router, tm, row0, shared):
    s, r, d = x.shape
    rm = g1.shape[1]
    nt = r // tm
    blk0 = row0 // tm
    n_total = shared[1].shape[0]
    mod = pl.BlockSpec((1, rm, d), lambda b_, i: (b_, 0, 0))
    const = lambda shape: pl.BlockSpec(shape, lambda b_, i: (0,) * len(shape))
    mixspec = pl.BlockSpec((tm, 256), lambda b_, i: (i, b_))
    in_specs = [pl.BlockSpec((1, tm, d), lambda b_, i: (b_, i, 0)),
                mixspec, mixspec, mixspec, mixspec, mod, mod, mod,
                const((1, d)), const((d, d)), const((d, LANE))]
    args = [x, *mixes, g1, sh2, sc2, ng, wout, router]
    in_specs += [pl.BlockSpec(memory_space=pl.ANY), pl.BlockSpec(memory_space=pl.ANY)]
    aliases = {len(args): 1, len(args) + 1: 2}
    args += list(shared)
    return pl.pallas_call(
        _route_kernel,
        out_shape=[jax.ShapeDtypeStruct((s, r, d), F32),
                   jax.ShapeDtypeStruct((n_total * ROW_TILE, LANE), F32),
                   jax.ShapeDtypeStruct((n_total, LANE), F32)],
        grid=(s, nt),
        in_specs=in_specs,
        out_specs=[pl.BlockSpec((1, tm, d), lambda b_, i: (b_, i, 0)),
                   pl.BlockSpec((tm * ROW_TILE, LANE), lambda b_, i: (blk0 + b_ * nt + i, 0)),
                   pl.BlockSpec((tm, LANE), lambda b_, i: (blk0 + b_ * nt + i, 0))],
        input_output_aliases=aliases,
        compiler_params=_cparams(("arbitrary", "arbitrary")),
        name="moe_route",
    )(*args)


def _route_tables(route, tg, n_tiles):
    n_total = route.shape[0]
    flat_e = route[:, 0:2].astype(jnp.int32).reshape(-1)
    order = jnp.argsort(flat_e, stable=True).astype(jnp.int32)
    counts = jnp.sum(flat_e[:, None] == jnp.arange(N_EXPERTS, dtype=jnp.int32)[None, :], axis=0).astype(jnp.int32)
    tiles_per = (counts + tg - 1) // tg
    tile_end = jnp.cumsum(tiles_per)
    n_used = tile_end[-1]
    tile_id = jnp.arange(n_tiles, dtype=jnp.int32)
    tile_ok = tile_id < n_used
    tile_e = jnp.sum(jnp.minimum(tile_id, n_used - 1)[:, None] >= tile_end[None, :], axis=1).astype(jnp.int32)
    sort_start = jnp.cumsum(counts) - counts
    done = (tile_id - (tile_end - tiles_per)[tile_e]) * tg
    n_valid = jnp.where(tile_ok, jnp.clip(counts[tile_e] - done, 0, tg), 0).astype(jnp.int32)
    tile_start = jnp.where(tile_ok, sort_start[tile_e] + done, 0).astype(jnp.int32)
    pad = jnp.zeros((tg,), jnp.int32)
    src = jnp.concatenate([(order >> 1) * ROW_TILE, pad])
    dst = jnp.concatenate([((order & 1) * n_total + (order >> 1)) * ROW_TILE, pad])
    return tile_e, n_valid, tile_start, src, dst


DMA_UNROLL = 8


def _experts_kernel(te_ref, nv_ref, ts_ref, src_ref, dst_ref, h2_hbm, wg_ref, wu_ref, wd_ref, out_hbm,
                    xbuf, obuf, gsem, ssem, zsem, *, tg, n_tiles, tf, n_real):
    j = pl.program_id(0)
    slot = lax.rem(j, 2)
    other = 1 - slot
    ok = nv_ref[j] > 0
    tile_rows = tg * ROW_TILE

    @pl.when(j == 0)
    def _():
        obuf[...] = jnp.zeros_like(obuf)
        for p in range(N_EXPERTS + 1):
            pltpu.make_async_copy(obuf.at[0], out_hbm.at[pl.ds(n_real + p * tile_rows, tile_rows), :], zsem).start()
        for p in range(N_EXPERTS + 1):
            pltpu.make_async_copy(obuf.at[0], out_hbm.at[pl.ds(n_real + p * tile_rows, tile_rows), :], zsem).wait()

    prev = jnp.maximum(j - 1, 0)
    has_prev = jnp.minimum(j, 1)

    def row_tile(buf, s_, r):
        start = r * ROW_TILE if isinstance(r, int) else pl.multiple_of(r * ROW_TILE, ROW_TILE)
        return buf.at[s_, pl.ds(start, ROW_TILE), :]

    def gather_row(tile, s_, r, priority=0):
        row = pl.multiple_of(src_ref[ts_ref[tile] + r], ROW_TILE)
        pltpu.make_async_copy(h2_hbm.at[pl.ds(row, ROW_TILE), :], row_tile(xbuf, s_, r),
                              gsem.at[s_]).start(priority=priority)

    def scatter_row(n_valid, start, spare, s_, r, priority=0):
        row = jnp.where(r < n_valid, dst_ref[start + r], spare + r * ROW_TILE)
        pltpu.make_async_copy(row_tile(obuf, s_, r), out_hbm.at[pl.ds(pl.multiple_of(row, ROW_TILE), ROW_TILE), :],
                              ssem.at[s_]).start(priority=priority)

    def full_tile(issue_row):
        def body(r8, c):
            for u in range(DMA_UNROLL):
                issue_row(r8 * DMA_UNROLL + u, u % 2)
            return c

        lax.fori_loop(0, tg // DMA_UNROLL, body, 0)

    def gather(tile, s_):
        full_tile(lambda r, prio: gather_row(tile, s_, r, priority=prio))

    def wait_all(buf, sem, s_):
        pltpu.make_async_copy(buf.at[s_], buf.at[s_], sem.at[s_]).wait()

    @pl.when(jnp.logical_and(j == 0, ok))
    def _():
        gather(0, 0)

    @pl.when(jnp.logical_or(jnp.logical_and(j == 0, ok), nv_ref[jnp.maximum(j - 1, 0)] * jnp.minimum(j, 1) > 0))
    def _():
        wait_all(xbuf, gsem, slot)

    nxt = jnp.minimum(j + 1, n_tiles - 1)
    last_computing = jnp.logical_and(ok, jnp.logical_or(j == n_tiles - 1, nv_ref[nxt] == 0))

    @pl.when(ok)
    def _():
        x = _load_row_tiles(xbuf, tg, lead=(slot,)).astype(BF16)
        p_valid = nv_ref[prev] * has_prev
        p_start = ts_ref[prev]
        p_spare = n_real + jnp.where(j == 0, N_EXPERTS, te_ref[prev]) * tile_rows

        def copy_some(i, n):
            for r in range(i * tg // n, (i + 1) * tg // n):
                gather_row(nxt, other, r, priority=r % 2)
                scatter_row(p_valid, p_start, p_spare, other, r, priority=(r + 1) % 2)

        y = _swiglu(x, wg_ref, wu_ref, wd_ref, tf, lead=(0,), between=copy_some)

        @pl.when(j >= 1)
        def _():
            wait_all(obuf, ssem, slot)

        _store_row_tiles(obuf, y, lead=(slot,))

    @pl.when(last_computing)
    def _():
        wait_all(obuf, ssem, other)
        spare = n_real + te_ref[j] * tile_rows
        full_tile(lambda r, prio: scatter_row(nv_ref[j], ts_ref[j], spare, slot, r, priority=prio))
        wait_all(obuf, ssem, slot)

    @pl.when(jnp.logical_and(j == n_tiles - 1, ok))
    def _():
        wait_all(xbuf, gsem, other)


def moe_experts(h2, tables, wg, wu, wd, tg, n_tiles, tf):
    n_exp, d, ff = wg.shape
    assert d == ROW_TILE * LANE and n_tiles >= 2
    tile_e, n_valid, tile_start, src, dst = tables
    wspec = lambda shape: pl.BlockSpec(shape, lambda j, te, *_: (te[j], 0, 0))
    grid_spec = pltpu.PrefetchScalarGridSpec(
        num_scalar_prefetch=5,
        grid=(n_tiles,),
        in_specs=[pl.BlockSpec(memory_space=pl.ANY), wspec((1, d, ff)), wspec((1, d, ff)), wspec((1, ff, d))],
        out_specs=pl.BlockSpec(memory_space=pl.ANY),
        scratch_shapes=[pltpu.VMEM((2, tg * ROW_TILE, LANE), F32), pltpu.VMEM((2, tg * ROW_TILE, LANE), F32),
                        pltpu.SemaphoreType.DMA((2,)), pltpu.SemaphoreType.DMA((2,)), pltpu.SemaphoreType.DMA(())])
    n_real = 2 * h2.shape[0]
    return pl.pallas_call(
        functools.partial(_experts_kernel, tg=tg, n_tiles=n_tiles, tf=tf, n_real=n_real),
        out_shape=jax.ShapeDtypeStruct((n_real + (N_EXPERTS + 1) * tg * ROW_TILE, LANE), F32),
        grid_spec=grid_spec,
        compiler_params=_cparams(("arbitrary",)),
        name="moe_experts",
    )(tile_e, n_valid, tile_start, src, dst, h2, wg, wu, wd)


def _combine_kernel(x1_ref, y0_ref, y1_ref, route_ref, g2_ref, fg_ref, o_ref, *, final_norm):
    r = route_ref[...]
    rows = r.shape[0]
    f = r[:, 2:3] * _load_row_tiles(y0_ref, rows) + r[:, 3:4] * _load_row_tiles(y1_ref, rows)
    x2 = x1_ref[0] + _gate(f, g2_ref[0])
    if final_norm:
        x2 = _rms(x2) * fg_ref[...]
    o_ref[0] = x2


def moe_combine(x1, y, route, g2, fg, tm, row0, final_norm):
    s, r, d = x1.shape
    rm = g2.shape[1]
    nt = r // tm
    blk0 = row0 // tm
    n_total = route.shape[0]
    assert n_total % tm == 0
    blk1 = n_total // tm
    return pl.pallas_call(
        functools.partial(_combine_kernel, final_norm=final_norm),
        out_shape=jax.ShapeDtypeStruct((s, r, d), F32),
        grid=(s, nt),
        in_specs=[pl.BlockSpec((1, tm, d), lambda b_, i: (b_, i, 0)),
                  pl.BlockSpec((tm * ROW_TILE, LANE), lambda b_, i: (blk0 + b_ * nt + i, 0)),
                  pl.BlockSpec((tm * ROW_TILE, LANE), lambda b_, i: (blk1 + blk0 + b_ * nt + i, 0)),
                  pl.BlockSpec((tm, LANE), lambda b_, i: (blk0 + b_ * nt + i, 0)),
                  pl.BlockSpec((1, rm, d), lambda b_, i: (b_, 0, 0)),
                  pl.BlockSpec((1, d), lambda b_, i: (0, 0))],
        out_specs=pl.BlockSpec((1, tm, d), lambda b_, i: (b_, i, 0)),
        compiler_params=_cparams(("arbitrary", "arbitrary")),
        name="moe_combine",
    )(x1, y, y, route, g2, fg)


def _reorder_w_in(w_in, b_in):
    cut = PW_S5 + 128 + 128 + 256 + 256 + GLA_GATE_RANK
    pad = LANE - GLA_GATE_RANK
    w = jnp.concatenate([w_in[:, :cut], jnp.zeros((w_in.shape[0], pad), w_in.dtype), w_in[:, cut:]], axis=1)
    b = jnp.concatenate([b_in[:cut], jnp.zeros((pad,), b_in.dtype), b_in[cut:]])
    return w.astype(BF16), b.reshape(1, PW_TOTAL)


def _row(a):
    return a.reshape(1, -1)


class _Branch:
    def __init__(self, nseq, nb, t_len, seq_form, tm_pre, tm_post, tc, tt_seq, row0):
        self.nseq, self.nb, self.t_len, self.seq_form = nseq, nb, t_len, seq_form
        self.tm_pre, self.tm_post, self.tc, self.tt_seq, self.row0 = tm_pre, tm_post, tc, tt_seq, row0


def _layer_params(W, i):
    row = _row
    p = {}
    p['w_in'], p['b_in'] = _reorder_w_in(W['w_in'][i], W['b_in'][i])
    p['norm_g'] = row(W['norm_mix_g'][i])
    mg = W['merge_g'][i]
    p['mg'] = [row(mg[k * 256:(k + 1) * 256]) for k in range(4)]
    ab_re, ab_im, bb_re, bb_im = s5_discretise(W['s5_a_re'][i], W['s5_a_im'][i], W['s5_log_dt'][i],
                                               W['s5_b_re'][i], W['s5_b_im'][i])
    p['s5'] = (jnp.concatenate([_block_diag_in(bb_re), _block_diag_in(bb_im)], axis=1).astype(BF16),
               _block_diag_out(W['s5_c_re'][i]).astype(BF16), _block_diag_out(W['s5_c_im'][i]).astype(BF16),
               row(ab_re), row(ab_im), row(W['s5_d'][i]), W['s5_w_glu'][i].astype(BF16), row(W['s5_b_glu'][i]))
    wg2 = jnp.zeros((LANE, GLA_KEY_WIDTH), F32).at[:GLA_GATE_RANK].set(W['gla_w_gate2'][i]).astype(BF16)
    p['gla'] = (wg2, row(W['gla_b_gate2'][i]), row(W['gla_onorm_g'][i]))
    p['conv'] = (W['conv_w_dw'][i], row(W['conv_b_dw'][i]), row(W['conv_ln_g'][i]), row(W['conv_ln_b'][i]),
                 W['conv_w_pw'][i].astype(BF16), row(W['conv_b_pw'][i]))
    p['gmlp_ln'] = (row(W['gmlp_ln_g'][i]), row(W['gmlp_ln_b'][i]))
    p['gmlp_ws'], p['gmlp_bs'] = W['gmlp_w_s'][i], W['gmlp_b_s'][i]
    return p


def _mixers(x, mods, states, p, i, br, out):
    nseq, nb, t_len, seq_form = br.nseq, br.nb, br.t_len, br.seq_form
    s5_re0, s5_im0, gla0, conv0 = states
    new_re, new_im, new_gla, new_conv, new_v = out
    mg = p['mg']
    p_s5, p_gla, p_conv, p_mlp = pre_mixer(x, mods[0], mods[1], p['norm_g'], p['w_in'], p['b_in'], br.tm_pre)

    h0 = jnp.concatenate([s5_re0[i].reshape(nb, S5_LANES), s5_im0[i].reshape(nb, S5_LANES)], axis=1)
    o_s5, h_t = s5_mixer(p_s5, h0, *p['s5'], mg[0], nb, br.tc, nseq)
    new_re.append(h_t[:, :S5_LANES].reshape(nb, S5_GROUPS, S5_STATE))
    new_im.append(h_t[:, S5_LANES:].reshape(nb, S5_GROUPS, S5_STATE))

    if seq_form:
        eye = jnp.eye(GLA_HEADS, dtype=F32)
        s0 = jnp.einsum('bhkv,hg->bhvgk', gla0[i], eye).reshape(nseq, GLA_WIDTH, GLA_KEY_WIDTH)
        o_gla, s_t = gla_seq(p_gla, nseq, s0, *p['gla'], mg[1], br.tt_seq)
        s5d = s_t.reshape(nseq, GLA_HEADS, GLA_DV, GLA_HEADS, GLA_DK)
        new_gla.append(jnp.stack([jnp.swapaxes(s5d[:, h, :, h, :], 1, 2) for h in range(GLA_HEADS)], axis=1))
    else:
        ek, ev = _gla_expanders()
        o_gla, s_t = gla_recurrent(p_gla, gla0[i].reshape(nb, GLA_STATE_LANES), ek, ev, *p['gla'], mg[1], nb, t_len)
        new_gla.append(s_t.reshape(nb, GLA_HEADS, GLA_DK, GLA_DV))

    c0 = jnp.transpose(conv0[i], (1, 0, 2)).reshape(CONV_HIST * nb, CONV_DIM)
    o_conv, buf = conv_mixer(p_conv, c0, *p['conv'], mg[2], nb, br.tc, nseq)
    new_conv.append(jnp.transpose(buf.reshape(CONV_HIST, nb, CONV_DIM), (1, 0, 2)))

    ws, bs = p['gmlp_ws'], p['gmlp_bs']
    if seq_form:
        wcat = jnp.transpose(ws, (1, 0, 2)).reshape(GMLP_CHUNK, GMLP_HEADS * GMLP_CHUNK)
        bias = jnp.repeat(bs.T, GMLP_HEAD_DIM, axis=1)
        o_mlp = gmlp_seq(p_mlp, nseq, *p['gmlp_ln'], wcat, bias, mg[3], min(4 * br.tt_seq, t_len))
        new_v.append(None)
    else:
        tri = jnp.tril(jnp.ones((t_len, t_len), F32))
        wrow = jnp.repeat(jnp.transpose(ws[:, :t_len, :t_len] * tri[None], (1, 2, 0)).reshape(t_len * t_len, GMLP_HEADS),
                          GMLP_HEAD_DIM, axis=1)
        brow = jnp.repeat(bs[:, :t_len].T, GMLP_HEAD_DIM, axis=1)
        o_mlp, vn = gmlp_short(p_mlp, *p['gmlp_ln'], wrow, brow, mg[3], nb, t_len)
        new_v.append(vn)

    return [o_s5, o_gla, o_conv, o_mlp]


FF_TILE = 1408
EXPERT_FF_TILE = 256
EXPERT_ROWS = 512


def _channel_mixer(xs, mixes, mods, W, i, branches, last):
    ng, wout, fg = _row(W['norm_ffn_g'][i]), W['w_out'][i].astype(BF16), _row(W['final_norm_g'])
    j = i // 2
    if i % 2 == 0:
        wg, wu, wd = (W['ffn_w_gate'][j].astype(BF16), W['ffn_w_up'][j].astype(BF16),
                      W['ffn_w_down'][j].astype(BF16))
        return [post_dense(x, mx, m[2], m[3], m[4], m[5], ng, wout, wg, wu, wd, fg, br.tm_post, FF_TILE, last)
                for x, mx, m, br in zip(xs, mixes, mods, branches)]
    n_total = sum(x.shape[0] * x.shape[1] for x in xs)
    tg = EXPERT_ROWS
    n_tiles = 2 * n_total // tg + N_EXPERTS
    router = jnp.zeros((D_MODEL, LANE), F32).at[:, :N_EXPERTS].set(W['moe_router'][j])
    h2, route = jnp.zeros((n_total * ROW_TILE, LANE), F32), jnp.zeros((n_total, LANE), F32)
    x1s = []
    for x, mx, m, br in zip(xs, mixes, mods, branches):
        x1, h2, route = moe_route(x, mx, m[2], m[3], m[4], ng, wout, router, br.tm_post, br.row0, (h2, route))
        x1s.append(x1)
    tables = _route_tables(route, tg, n_tiles)
    y = moe_experts(h2, tables, W['moe_w_gate'][j].astype(BF16), W['moe_w_up'][j].astype(BF16),
                    W['moe_w_down'][j].astype(BF16), tg, n_tiles, EXPERT_FF_TILE)
    return [moe_combine(x1, y, route, m[5], fg, br.tm_post, br.row0, last)
            for x1, m, br in zip(x1s, mods, branches)]


def kernel(x_prompt, x_sample, c_prompt, c_sample, state_s5_re, state_s5_im, state_gla, cache_conv, ada_w, ada_b, norm_mix_g, norm_ffn_g, w_in, b_in, s5_a_re, s5_a_im, s5_log_dt, s5_b_re, s5_b_im, s5_c_re, s5_c_im, s5_d, s5_w_glu, s5_b_glu, gla_w_gate2, gla_b_gate2, gla_onorm_g, conv_w_dw, conv_b_dw, conv_ln_g, conv_ln_b, conv_w_pw, conv_b_pw, gmlp_ln_g, gmlp_ln_b, gmlp_w_s, gmlp_b_s, merge_g, w_out, ffn_w_gate, ffn_w_up, ffn_w_down, moe_router, moe_w_gate, moe_w_up, moe_w_down, final_norm_g):
    W = dict(norm_mix_g=norm_mix_g, norm_ffn_g=norm_ffn_g, w_in=w_in, b_in=b_in, s5_a_re=s5_a_re, s5_a_im=s5_a_im,
             s5_log_dt=s5_log_dt, s5_b_re=s5_b_re, s5_b_im=s5_b_im, s5_c_re=s5_c_re, s5_c_im=s5_c_im, s5_d=s5_d,
             s5_w_glu=s5_w_glu, s5_b_glu=s5_b_glu, gla_w_gate2=gla_w_gate2, gla_b_gate2=gla_b_gate2,
             gla_onorm_g=gla_onorm_g, conv_w_dw=conv_w_dw, conv_b_dw=conv_b_dw, conv_ln_g=conv_ln_g,
             conv_ln_b=conv_ln_b, conv_w_pw=conv_w_pw, conv_b_pw=conv_b_pw, gmlp_ln_g=gmlp_ln_g,
             gmlp_ln_b=gmlp_ln_b, gmlp_w_s=gmlp_w_s, gmlp_b_s=gmlp_b_s, merge_g=merge_g, w_out=w_out,
             ffn_w_gate=ffn_w_gate, ffn_w_up=ffn_w_up, ffn_w_down=ffn_w_down, moe_router=moe_router,
             moe_w_gate=moe_w_gate, moe_w_up=moe_w_up, moe_w_down=moe_w_down, final_norm_g=final_norm_g)
    depth = w_in.shape[0]
    bp, tp, d = x_prompt.shape
    bs, ts, _ = x_sample.shape

    m = ada_modulation(jnp.concatenate([c_prompt, c_sample], axis=0), ada_w, ada_b)
    mods_p = [[m[i, :bp, k * d:(k + 1) * d].reshape(bp, 1, d) for k in range(6)] for i in range(depth)]
    mods_s = [[m[i, bp:, k * d:(k + 1) * d].reshape(1, bs, d) for k in range(6)] for i in range(depth)]

    z_re = jnp.zeros((depth, bp, S5_GROUPS, S5_STATE), F32)
    z_gla = jnp.zeros((depth, bp, GLA_HEADS, GLA_DK, GLA_DV), F32)
    z_conv = jnp.zeros((depth, bp, CONV_HIST, CONV_DIM), x_prompt.dtype)
    states = [(z_re, z_re, z_gla, z_conv), (state_s5_re, state_s5_im, state_gla, cache_conv)]
    branches = [_Branch(nseq=bp, nb=bp, t_len=tp, seq_form=True, tm_pre=min(512, tp), tm_post=min(512, tp),
                        tc=min(128, tp), tt_seq=min(512, tp), row0=0),
                _Branch(nseq=1, nb=bs, t_len=ts, seq_form=False, tm_pre=ts * bs, tm_post=min(512, ts * bs),
                        tc=ts, tt_seq=None, row0=bp * tp)]
    xs = [x_prompt, jnp.transpose(x_sample, (1, 0, 2)).reshape(1, ts * bs, d)]
    outs = [([], [], [], [], []), ([], [], [], [], [])]
    for i in range(depth):
        mods = [mods_p[i], mods_s[i]]
        params = _layer_params(W, i)
        mixes = [_mixers(x, m, st_, params, i, br, o)
                 for x, m, st_, br, o in zip(xs, mods, states, branches, outs)]
        xs = _channel_mixer(xs, mixes, mods, W, i, branches, i == depth - 1)

    y_p = xs[0]
    y_s = jnp.transpose(xs[1].reshape(ts, bs, d), (1, 0, 2))
    p_re, p_im, p_gla, p_conv, _ = outs[0]
    s_re, s_im, s_gla, s_conv, s_v = outs[1]
    s_v = [jnp.transpose(v.reshape(ts, bs, GMLP_WIDTH), (1, 0, 2)) for v in s_v]
    st = jnp.stack
    return (y_p, y_s, st(p_re), st(p_im), st(p_gla), st(p_conv),
            st(s_re), st(s_im), st(s_gla), st(s_conv), st(s_v))
```

```python
import functools
import math

import jax
import jax.numpy as jnp
from jax import lax
from jax.experimental import pallas as pl
from jax.experimental.pallas import tpu as pltpu

D_MODEL = 1024
S5_WIDTH = 256
S5_GROUP = 16
S5_GROUPS = 16
S5_STATE = 64
S5_LANES = S5_GROUPS * S5_STATE
GLA_HEADS = 4
GLA_DV = 64
GLA_DK = 32
GLA_WIDTH = 256
GLA_KEY_WIDTH = 128
GLA_GATE_RANK = 16
GLA_TAU = 16.0
GLA_CHUNK = 64
GLA_STATE_LANES = GLA_HEADS * GLA_DK * GLA_DV
CONV_DIM = 256
CONV_WIDTH = 31
CONV_HIST = CONV_WIDTH - 1
GMLP_WIDTH = 256
GMLP_HEADS = 4
GMLP_HEAD_DIM = 64
GMLP_CHUNK = 128
D_FF = 2816
N_EXPERTS = 8
EPS = 1e-6

LANE = 128
PW_S5 = 256
PW_GLA = 128 + 128 + 256 + 256 + LANE
PW_CONV = 512
PW_MLP = 512
PW_TOTAL = PW_S5 + PW_GLA + PW_CONV + PW_MLP
VMEM_LIMIT = 56 * 1024 * 1024

F32 = jnp.float32
BF16 = jnp.bfloat16
HI = lax.Precision.HIGHEST


def _cparams(sem):
    return pltpu.CompilerParams(dimension_semantics=sem, vmem_limit_bytes=VMEM_LIMIT)


def _rms(x):
    return x * lax.rsqrt(jnp.mean(x * x, axis=-1, keepdims=True) + EPS)


def _layernorm(x, g, b):
    mu = jnp.mean(x, axis=-1, keepdims=True)
    xc = x - mu
    var = jnp.mean(xc * xc, axis=-1, keepdims=True)
    return xc * lax.rsqrt(var + EPS) * g + b


def _silu(x):
    return x * jax.nn.sigmoid(x)


def _gelu_tanh(x):
    return 0.5 * x * (1.0 + jnp.tanh(math.sqrt(2.0 / math.pi) * (x + 0.044715 * (x * x * x))))


def _log_sigmoid(x):
    return jnp.minimum(x, 0.0) - jnp.log(1.0 + jnp.exp(-jnp.abs(x)))


def _same_block(shape, row_block, col_block):
    r = lax.broadcasted_iota(jnp.int32, shape, 0) >> (row_block.bit_length() - 1)
    c = lax.broadcasted_iota(jnp.int32, shape, 1) >> (col_block.bit_length() - 1)
    return r == c


def _modulate(y, sc, sh):
    rm = sc.shape[0]
    if rm == 1:
        return y * (1.0 + sc) + sh
    rows, d = y.shape
    y3 = y.reshape(rows // rm, rm, d)
    return (y3 * (1.0 + sc)[None] + sh[None]).reshape(rows, d)


def _gate(y, g):
    rm = g.shape[0]
    if rm == 1:
        return y * g
    rows, d = y.shape
    return (y.reshape(rows // rm, rm, d) * g[None]).reshape(rows, d)


def _ada_kernel(c_ref, w_ref, b_ref, o_ref):
    c = c_ref[...]
    s = _silu(c).astype(BF16)
    o_ref[0] = jnp.dot(s, w_ref[0].astype(BF16), preferred_element_type=F32) + b_ref[0]


def ada_modulation(c_all, ada_w, ada_b):
    depth, d, n6 = ada_w.shape
    rows = c_all.shape[0]
    tn = 1536
    return pl.pallas_call(
        _ada_kernel,
        out_shape=jax.ShapeDtypeStruct((depth, rows, n6), F32),
        grid=(depth, n6 // tn),
        in_specs=[pl.BlockSpec((rows, d), lambda l, j: (0, 0)),
                  pl.BlockSpec((1, d, tn), lambda l, j: (l, 0, j)),
                  pl.BlockSpec((1, 1, tn), lambda l, j: (l, 0, j))],
        out_specs=pl.BlockSpec((1, rows, tn), lambda l, j: (l, 0, j)),
        compiler_params=_cparams(("arbitrary", "arbitrary")),
        name="ada_modulation",
    )(c_all, ada_w, ada_b.reshape(depth, 1, n6))


def _pre_kernel(x_ref, sh_ref, sc_ref, g_ref, w_ref, b_ref, o_s5, o_gla, o_conv, o_mlp):
    x = x_ref[0]
    y = _modulate(_rms(x) * g_ref[...], sc_ref[0], sh_ref[0])
    p = jnp.dot(y.astype(BF16), w_ref[...], preferred_element_type=F32) + b_ref[...]
    o_s5[...] = p[:, 0:PW_S5]
    o_gla[...] = p[:, PW_S5:PW_S5 + PW_GLA]
    o_conv[...] = p[:, PW_S5 + PW_GLA:PW_S5 + PW_GLA + PW_CONV]
    o_mlp[...] = p[:, PW_S5 + PW_GLA + PW_CONV:PW_TOTAL]


def pre_mixer(x, sh, sc, g, w, b, tm):
    s, r, d = x.shape
    rm = sh.shape[1]
    widths = (PW_S5, PW_GLA, PW_CONV, PW_MLP)
    return pl.pallas_call(
        _pre_kernel,
        out_shape=[jax.ShapeDtypeStruct((r, s * w_), F32) for w_ in widths],
        grid=(s, r // tm),
        in_specs=[pl.BlockSpec((1, tm, d), lambda b_, i: (b_, i, 0)),
                  pl.BlockSpec((1, rm, d), lambda b_, i: (b_, 0, 0)),
                  pl.BlockSpec((1, rm, d), lambda b_, i: (b_, 0, 0)),
                  pl.BlockSpec((1, d), lambda b_, i: (0, 0)),
                  pl.BlockSpec((d, PW_TOTAL), lambda b_, i: (0, 0)),
                  pl.BlockSpec((1, PW_TOTAL), lambda b_, i: (0, 0))],
        out_specs=[pl.BlockSpec((tm, w_), lambda b_, i: (i, b_)) for w_ in widths],
        compiler_params=_cparams(("arbitrary", "arbitrary")),
        name="pre_mixer",
    )(x, sh, sc, g, w, b)


def _s5_disc_kernel(lr_ref, li_ref, ldt_ref, br_ref, bi_ref, abr_ref, abi_ref, bbr_ref, bbi_ref):
    lr = lr_ref[...]
    li = li_ref[...]
    dt = jnp.exp(ldt_ref[...])
    mag = jnp.exp(lr * dt)
    ang = li * dt
    ab_re = mag * jnp.cos(ang)
    ab_im = mag * jnp.sin(ang)
    den = lr * lr + li * li
    nr = ab_re - 1.0
    f_re = (nr * lr + ab_im * li) / den
    f_im = (ab_im * lr - nr * li) / den
    br = br_ref[...]
    bi = bi_ref[...]
    abr_ref[...] = ab_re
    abi_ref[...] = ab_im
    bbr_ref[...] = f_re * br - f_im * bi
    bbi_ref[...] = f_re * bi + f_im * br


def s5_discretise(a_re, a_im, log_dt, b_re, b_im):
    n = b_re.shape[-1]
    gp = a_re.size
    bc = lambda a: jnp.broadcast_to(a.reshape(gp, 1), (gp, n))
    ldt = jnp.broadcast_to(log_dt[:, None], a_re.shape)
    outs = pl.pallas_call(
        _s5_disc_kernel,
        out_shape=[jax.ShapeDtypeStruct((gp, n), F32)] * 4,
        name="s5_discretise",
    )(bc(a_re), bc(a_im), bc(ldt), b_re.reshape(gp, n), b_im.reshape(gp, n))
    ab_re, ab_im, bb_re, bb_im = outs
    return ab_re[:, 0], ab_im[:, 0], bb_re, bb_im


def _block_diag_in(bb):
    g, p, n = S5_GROUPS, S5_STATE, S5_GROUP
    b3 = bb.reshape(g, p, n)
    eye = jnp.eye(g, dtype=bb.dtype)
    return jnp.einsum('gpn,gh->gnhp', b3, eye).reshape(g * n, g * p)


def _block_diag_out(c):
    g, p, n = S5_GROUPS, S5_STATE, S5_GROUP
    eye = jnp.eye(g, dtype=c.dtype)
    return jnp.einsum('gnp,gh->gphn', c, eye).reshape(g * p, g * n)


def _to_time_major(x_ref, cols, tm_ref, row0, nsl):
    rt = x_ref.shape[0]
    w = x_ref.shape[1] // nsl
    start, width = cols
    for l in range(nsl):
        for h in range(width // LANE):
            c0 = l * w + start + h * LANE
            tm_ref[h, pl.ds(row0 + l, rt, stride=nsl), :] = x_ref[:, c0:c0 + LANE]


def _from_time_major(tm_ref, o_ref, nsl):
    rt = o_ref.shape[0]
    nh = tm_ref.shape[0]
    for l in range(nsl):
        piece = jnp.concatenate([tm_ref[h, pl.ds(l, rt, stride=nsl), :] for h in range(nh)], axis=1)
        o_ref[:, l * nh * LANE:(l + 1) * nh * LANE] = piece.astype(o_ref.dtype)


def _lane_tiles(tm_ref, rows=slice(None)):
    return jnp.concatenate([tm_ref[h, rows, :] for h in range(tm_ref.shape[0])], axis=1)


def _set_lane_tiles(tm_ref, x):
    for h in range(tm_ref.shape[0]):
        tm_ref[h] = x[:, h * LANE:(h + 1) * LANE]


def _s5_kernel(u_ref, h0_ref, bblk_ref, cre_ref, cim_ref, ar_ref, ai_ref, d_ref, wglu_ref, bglu_ref, mg_ref,
               o_ref, hT_ref, xs_ref, hs_ref, tm_ref, *, nb, tc, nsl):
    i = pl.program_id(0)

    @pl.when(i == 0)
    def _():
        hs_ref[...] = h0_ref[...]

    _to_time_major(u_ref, (0, S5_WIDTH), tm_ref, 0, nsl)
    u = _lane_tiles(tm_ref)
    xs_ref[...] = jnp.dot(u.astype(BF16), bblk_ref[...], preferred_element_type=F32)
    ar = jnp.broadcast_to(ar_ref[...], (nb, S5_LANES))
    ai = jnp.broadcast_to(ai_ref[...], (nb, S5_LANES))

    def step(t, carry):
        hr, hi = carry
        row = pl.multiple_of(t * nb, nb)
        xr = xs_ref[pl.ds(row, nb), 0:S5_LANES]
        xi = xs_ref[pl.ds(row, nb), S5_LANES:2 * S5_LANES]
        nr = ar * hr - ai * hi + xr
        ni = ar * hi + ai * hr + xi
        xs_ref[pl.ds(row, nb), 0:S5_LANES] = nr
        xs_ref[pl.ds(row, nb), S5_LANES:2 * S5_LANES] = ni
        return nr, ni

    hr, hi = lax.fori_loop(0, tc, step, (hs_ref[:, 0:S5_LANES], hs_ref[:, S5_LANES:2 * S5_LANES]),
                           unroll=True if tc <= 8 else 4)
    hs_ref[:, 0:S5_LANES] = hr
    hs_ref[:, S5_LANES:2 * S5_LANES] = hi

    y = (jnp.dot(xs_ref[:, 0:S5_LANES].astype(BF16), cre_ref[...], preferred_element_type=F32)
         - jnp.dot(xs_ref[:, S5_LANES:2 * S5_LANES].astype(BF16), cim_ref[...], preferred_element_type=F32))
    y = y + d_ref[...] * u
    y = _gelu_tanh(y)
    y = y * jax.nn.sigmoid(jnp.dot(y.astype(BF16), wglu_ref[...], preferred_element_type=F32) + bglu_ref[...])
    _set_lane_tiles(tm_ref, _rms(y) * mg_ref[...])
    _from_time_major(tm_ref, o_ref, nsl)

    @pl.when(i == pl.num_programs(0) - 1)
    def _():
        hT_ref[...] = hs_ref[...]


def s5_mixer(u, h0, bblk, cre, cim, ar, ai, d, wglu, bglu, mg, nb, tc, nsl):
    rows = u.shape[0]
    rc = nb * tc
    rt = rc // nsl
    full = lambda shape: pl.BlockSpec(shape, lambda i: (0,) * len(shape))
    return pl.pallas_call(
        functools.partial(_s5_kernel, nb=nb, tc=tc, nsl=nsl),
        out_shape=[jax.ShapeDtypeStruct((rows, nsl * S5_WIDTH), BF16),
                   jax.ShapeDtypeStruct((nb, 2 * S5_LANES), F32)],
        grid=(rows // rt,),
        in_specs=[pl.BlockSpec((rt, nsl * S5_WIDTH), lambda i: (i, 0)),
                  full((nb, 2 * S5_LANES)),
                  full((S5_WIDTH, 2 * S5_LANES)),
                  full((S5_LANES, S5_WIDTH)), full((S5_LANES, S5_WIDTH)),
                  full((1, S5_LANES)), full((1, S5_LANES)),
                  full((1, S5_WIDTH)), full((S5_WIDTH, S5_WIDTH)), full((1, S5_WIDTH)), full((1, S5_WIDTH))],
        out_specs=[pl.BlockSpec((rt, nsl * S5_WIDTH), lambda i: (i, 0)),
                   full((nb, 2 * S5_LANES))],
        scratch_shapes=[pltpu.VMEM((rc, 2 * S5_LANES), F32), pltpu.VMEM((nb, 2 * S5_LANES), F32),
                        pltpu.VMEM((S5_WIDTH // LANE, rc, LANE), F32)],
        compiler_params=_cparams(("arbitrary",)),
        name="s5_mixer",
    )(u, h0, bblk, cre, cim, ar, ai, d, wglu, bglu, mg)


CONV_ROWS = 64


def _conv_kernel(ag_ref, c0_ref, wdw_ref, bdw_ref, lng_ref, lnb_ref, wpw_ref, bpw_ref, mg_ref,
                 o_ref, buf_ref, zc_ref, y_ref, *, nb, tc, nsl):
    i = pl.program_id(0)
    hist = CONV_HIST * nb
    rc = nb * tc
    rt = rc // nsl
    n_lt = CONV_DIM // LANE

    @pl.when(i == 0)
    def _():
        for h in range(n_lt):
            zc_ref[h, 0:hist, :] = c0_ref[:, h * LANE:(h + 1) * LANE]

    @pl.when(i > 0)
    def _():
        for h in range(n_lt):
            zc_ref[h, 0:hist, :] = zc_ref[h, rc:rc + hist, :]

    w_seq = 2 * CONV_DIM
    for l in range(nsl):
        a = ag_ref[:, l * w_seq:l * w_seq + CONV_DIM]
        g = ag_ref[:, l * w_seq + CONV_DIM:(l + 1) * w_seq]
        z = a * jax.nn.sigmoid(g)
        for h in range(n_lt):
            zc_ref[h, pl.ds(hist + l, rt, stride=nsl), :] = z[:, h * LANE:(h + 1) * LANE]

    w = wdw_ref[...]

    def tile(j, carry):
        r0 = pl.multiple_of(j * CONV_ROWS, CONV_ROWS)
        for h in range(n_lt):
            acc = jnp.zeros((CONV_ROWS, LANE), F32)
            for k in range(CONV_WIDTH):
                acc = acc + w[k:k + 1, h * LANE:(h + 1) * LANE] * zc_ref[h, pl.ds(r0 + k * nb, CONV_ROWS), :]
            y_ref[h, pl.ds(r0, CONV_ROWS), :] = acc
        return carry

    lax.fori_loop(0, rc // CONV_ROWS, tile, 0)
    y = _lane_tiles(y_ref) + bdw_ref[...]
    y = _silu(_layernorm(y, lng_ref[...], lnb_ref[...]))
    y = jnp.dot(y.astype(BF16), wpw_ref[...], preferred_element_type=F32) + bpw_ref[...]
    _set_lane_tiles(y_ref, _rms(y) * mg_ref[...])
    _from_time_major(y_ref, o_ref, nsl)

    @pl.when(i == pl.num_programs(0) - 1)
    def _():
        buf_ref[...] = _lane_tiles(zc_ref, slice(rc, rc + hist))


def conv_mixer(ag, c0, wdw, bdw, lng, lnb, wpw, bpw, mg, nb, tc, nsl):
    rows = ag.shape[0]
    rc = nb * tc
    rt = rc // nsl
    hist = CONV_HIST * nb
    assert rows == rt or tc >= CONV_HIST
    n_lt = CONV_DIM // LANE
    full = lambda shape: pl.BlockSpec(shape, lambda i: (0,) * len(shape))
    return pl.pallas_call(
        functools.partial(_conv_kernel, nb=nb, tc=tc, nsl=nsl),
        out_shape=[jax.ShapeDtypeStruct((rows, nsl * CONV_DIM), BF16),
                   jax.ShapeDtypeStruct((hist, CONV_DIM), F32)],
        grid=(rows // rt,),
        in_specs=[pl.BlockSpec((rt, nsl * 2 * CONV_DIM), lambda i: (i, 0)),
                  full((hist, CONV_DIM)), full((CONV_WIDTH, CONV_DIM)),
                  full((1, CONV_DIM)), full((1, CONV_DIM)), full((1, CONV_DIM)),
                  full((CONV_DIM, CONV_DIM)), full((1, CONV_DIM)), full((1, CONV_DIM))],
        out_specs=[pl.BlockSpec((rt, nsl * CONV_DIM), lambda i: (i, 0)), full((hist, CONV_DIM))],
        scratch_shapes=[pltpu.VMEM((n_lt, hist + rc, LANE), F32), pltpu.VMEM((n_lt, rc, LANE), F32)],
        compiler_params=_cparams(("arbitrary",)),
        name="conv_mixer",
    )(ag, c0, wdw, bdw, lng, lnb, wpw, bpw, mg)


def _gmlp_seq_kernel(uv_ref, lng_ref, lnb_ref, wcat_ref, bias_ref, mg_ref, o_ref, *, tt):
    n_chunks = tt // GMLP_CHUNK
    kc = GMLP_HEADS * GMLP_CHUNK
    rowi = lax.broadcasted_iota(jnp.int32, (GMLP_CHUNK, kc), 0)
    coli = lax.broadcasted_iota(jnp.int32, (GMLP_CHUNK, kc), 1)
    wcat = jnp.where((coli & (GMLP_CHUNK - 1)) <= rowi, wcat_ref[...], 0.0).astype(BF16)
    sel = _same_block((kc, GMLP_WIDTH), GMLP_CHUNK, GMLP_HEAD_DIM)
    for c in range(n_chunks):
        rows = slice(c * GMLP_CHUNK, (c + 1) * GMLP_CHUNK)
        u = uv_ref[rows, 0:GMLP_WIDTH]
        v = uv_ref[rows, GMLP_WIDTH:2 * GMLP_WIDTH]
        vn = _layernorm(v, lng_ref[...], lnb_ref[...])
        vbd = jnp.where(sel, jnp.concatenate([vn] * GMLP_HEADS, axis=0), 0.0).astype(BF16)
        mixed = jnp.dot(wcat, vbd, preferred_element_type=F32) + bias_ref[...]
        o_ref[rows, :] = (_rms(u * mixed) * mg_ref[...]).astype(o_ref.dtype)


def gmlp_seq(uv, nseq, lng, lnb, wcat, bias, mg, tt):
    t = uv.shape[0]
    full = lambda shape: pl.BlockSpec(shape, lambda b_, i: (0,) * len(shape))
    return pl.pallas_call(
        functools.partial(_gmlp_seq_kernel, tt=tt),
        out_shape=jax.ShapeDtypeStruct((t, nseq * GMLP_WIDTH), BF16),
        grid=(nseq, t // tt),
        in_specs=[pl.BlockSpec((tt, 2 * GMLP_WIDTH), lambda b_, i: (i, b_)),
                  full((1, GMLP_WIDTH)), full((1, GMLP_WIDTH)),
                  full((GMLP_CHUNK, GMLP_HEADS * GMLP_CHUNK)), full((GMLP_CHUNK, GMLP_WIDTH)),
                  full((1, GMLP_WIDTH))],
        out_specs=pl.BlockSpec((tt, GMLP_WIDTH), lambda b_, i: (i, b_)),
        compiler_params=_cparams(("arbitrary", "arbitrary")),
        name="gmlp_seq",
    )(uv, lng, lnb, wcat, bias, mg)


def _gmlp_short_kernel(uv_ref, lng_ref, lnb_ref, wrow_ref, brow_ref, mg_ref, o_ref, vn_ref, *, nb, t_len):
    u = uv_ref[:, 0:GMLP_WIDTH]
    v = uv_ref[:, GMLP_WIDTH:2 * GMLP_WIDTH]
    vn = _layernorm(v, lng_ref[...], lnb_ref[...])
    vn_ref[...] = vn
    wrow = wrow_ref[...]
    brow = brow_ref[...]
    for t in range(t_len):
        mixed = jnp.zeros((nb, GMLP_WIDTH), F32) + brow[t:t + 1, :]
        for j in range(t + 1):
            mixed = mixed + wrow[t * t_len + j:t * t_len + j + 1, :] * vn[j * nb:(j + 1) * nb, :]
        o = u[t * nb:(t + 1) * nb, :] * mixed
        o_ref[t * nb:(t + 1) * nb, :] = (_rms(o) * mg_ref[...]).astype(o_ref.dtype)


def gmlp_short(uv, lng, lnb, wrow, brow, mg, nb, t_len):
    rows = uv.shape[0]
    return pl.pallas_call(
        functools.partial(_gmlp_short_kernel, nb=nb, t_len=t_len),
        out_shape=[jax.ShapeDtypeStruct((rows, GMLP_WIDTH), BF16),
                   jax.ShapeDtypeStruct((rows, GMLP_WIDTH), F32)],
        compiler_params=pltpu.CompilerParams(vmem_limit_bytes=VMEM_LIMIT),
        name="gmlp_short",
    )(uv, lng, lnb, wrow, brow, mg)


def _split3(x):
    a = x.astype(BF16)
    r1 = x - a.astype(F32)
    b = r1.astype(BF16)
    c = (r1 - b.astype(F32)).astype(BF16)
    return a, b, c


def _dot_exact_rhs(x, m):
    return sum(jnp.dot(t, m, preferred_element_type=F32) for t in _split3(x))


def _dot_exact_lhs(m, x):
    return sum(jnp.dot(m, t, preferred_element_type=F32) for t in _split3(x))


def _gla_tail(o, r, gmean, onorm, mg):
    ms = _dot_exact_rhs(o * o, gmean)
    o = o * lax.rsqrt(ms + EPS) * onorm
    o = o * _silu(r)
    return _rms(o) * mg


def _head_mean_matrix():
    return jnp.where(_same_block((GLA_WIDTH, GLA_WIDTH), GLA_DV, GLA_DV), 1.0 / GLA_DV, 0.0).astype(BF16)


GLA_SEQS = 4


def _gla_seq_kernel(x_ref, s0_ref, wg_ref, bg_ref, onorm_ref, mg_ref, o_ref, sT_ref,
                    s_ref, qt_ref, kt_ref, kd_ref, dl_ref, oacc_ref, *, tt):
    i = pl.program_id(1)
    L = GLA_CHUNK
    n_ch = tt // L
    kw, vw = GLA_KEY_WIDTH, GLA_WIDTH

    @pl.when(i == 0)
    def _():
        s_ref[...] = s0_ref[...]

    rows_i = lax.broadcasted_iota(jnp.int32, (tt, tt), 0)
    cols_i = lax.broadcasted_iota(jnp.int32, (tt, tt), 1)
    shift = L.bit_length() - 1
    same_chunk = (rows_i >> shift) == (cols_i >> shift)
    tri = jnp.logical_and(same_chunk, cols_i <= rows_i).astype(BF16)
    chunk_sum = same_chunk.astype(BF16)
    chunk_rows = (lax.broadcasted_iota(jnp.int32, (n_ch, tt), 0)
                  == (lax.broadcasted_iota(jnp.int32, (n_ch, tt), 1) >> shift)).astype(BF16)
    kbd_sel = _same_block((GLA_HEADS * L, kw), L, GLA_DK)
    vbd_sel = _same_block((GLA_HEADS * L, vw), L, GLA_DV)
    causal = ((lax.broadcasted_iota(jnp.int32, (L, GLA_HEADS * L), 1) & (L - 1))
              <= lax.broadcasted_iota(jnp.int32, (L, GLA_HEADS * L), 0))
    s_sel = _same_block((vw, kw), GLA_DV, GLA_DK)
    gmean = _head_mean_matrix()
    scale = GLA_DK ** -0.5
    nt_dims = (((1,), (1,)), ((), ()))
    tn_dims = (((0,), (0,)), ((), ()))
    zero = jnp.zeros((), BF16)

    for g in range(GLA_SEQS):
        x0 = g * PW_GLA
        q = x_ref[:, x0:x0 + kw] * scale
        k = x_ref[:, x0 + kw:x0 + 2 * kw]
        gl = x_ref[:, x0 + 2 * kw + 2 * vw:x0 + PW_GLA]
        la = _log_sigmoid(jnp.dot(gl.astype(BF16), wg_ref[...], preferred_element_type=F32) + bg_ref[...])
        la = la / GLA_TAU
        bc = _dot_exact_lhs(tri, la)
        b_end = _dot_exact_lhs(chunk_sum, la)
        qt_ref[g] = (q * jnp.exp(bc)).astype(BF16)
        kt_ref[g] = (k * jnp.exp(-bc)).astype(BF16)
        kd_ref[g] = (k * jnp.exp(b_end - bc)).astype(BF16)
        dl_ref[g] = jnp.exp(_dot_exact_lhs(chunk_rows, la))

    def chunk(c, carry):
        r0 = pl.multiple_of(c * L, L)
        for g in range(GLA_SEQS):
            x0 = g * PW_GLA
            qt = qt_ref[g, pl.ds(r0, L), :]
            kt = kt_ref[g, pl.ds(r0, L), :]
            kdec = kd_ref[g, pl.ds(r0, L), :]
            vb = x_ref[pl.ds(r0, L), x0 + 2 * kw:x0 + 2 * kw + vw].astype(BF16)
            kbd = jnp.where(kbd_sel, jnp.concatenate([kt] * GLA_HEADS, axis=0), zero)
            att = lax.dot_general(qt, kbd, nt_dims, preferred_element_type=F32)
            att = jnp.where(causal, att, 0.0).astype(BF16)
            vbd = jnp.where(vbd_sel, jnp.concatenate([vb] * GLA_HEADS, axis=0), zero)
            st = s_ref[g]
            oacc_ref[g, pl.ds(r0, L), :] = (jnp.dot(att, vbd, preferred_element_type=F32)
                                            + lax.dot_general(qt, st.astype(BF16), nt_dims,
                                                              preferred_element_type=F32))
            upd = lax.dot_general(vb, kdec, tn_dims, preferred_element_type=F32)
            s_ref[g] = st * dl_ref[g, pl.ds(c, 1), :] + jnp.where(s_sel, upd, 0.0)
        return carry

    lax.fori_loop(0, n_ch, chunk, 0)

    for g in range(GLA_SEQS):
        x0 = g * PW_GLA
        r = x_ref[:, x0 + 2 * kw + vw:x0 + 2 * kw + 2 * vw]
        o_ref[:, g * vw:(g + 1) * vw] = _gla_tail(oacc_ref[g], r, gmean, onorm_ref[...],
                                                  mg_ref[...]).astype(o_ref.dtype)

    @pl.when(i == pl.num_programs(1) - 1)
    def _():
        sT_ref[...] = s_ref[...]


def gla_seq(x, nseq, s0, wg, bg, onorm, mg, tt):
    t = x.shape[0]
    g = GLA_SEQS
    assert nseq % g == 0
    full = lambda shape: pl.BlockSpec(shape, lambda b_, i: (0,) * len(shape))
    state = pl.BlockSpec((g, GLA_WIDTH, GLA_KEY_WIDTH), lambda b_, i: (b_, 0, 0))
    return pl.pallas_call(
        functools.partial(_gla_seq_kernel, tt=tt),
        out_shape=[jax.ShapeDtypeStruct((t, nseq * GLA_WIDTH), BF16),
                   jax.ShapeDtypeStruct((nseq, GLA_WIDTH, GLA_KEY_WIDTH), F32)],
        grid=(nseq // g, t // tt),
        in_specs=[pl.BlockSpec((tt, g * PW_GLA), lambda b_, i: (i, b_)),
                  state,
                  full((LANE, GLA_KEY_WIDTH)), full((1, GLA_KEY_WIDTH)),
                  full((1, GLA_WIDTH)), full((1, GLA_WIDTH))],
        out_specs=[pl.BlockSpec((tt, g * GLA_WIDTH), lambda b_, i: (i, b_)), state],
        scratch_shapes=[pltpu.VMEM((g, GLA_WIDTH, GLA_KEY_WIDTH), F32),
                        pltpu.VMEM((g, tt, GLA_KEY_WIDTH), BF16), pltpu.VMEM((g, tt, GLA_KEY_WIDTH), BF16),
                        pltpu.VMEM((g, tt, GLA_KEY_WIDTH), BF16),
                        pltpu.VMEM((g, tt // GLA_CHUNK, GLA_KEY_WIDTH), F32),
                        pltpu.VMEM((g, tt, GLA_WIDTH), F32)],
        compiler_params=_cparams(("arbitrary", "arbitrary")),
        name="gla_seq",
    )(x, s0, wg, bg, onorm, mg)


def _gla_rec_kernel(x_ref, s0_ref, ek_ref, ev_ref, wg_ref, bg_ref, onorm_ref, mg_ref, o_ref, sT_ref,
                    *, nb, t_len):
    kw, vw = GLA_KEY_WIDTH, GLA_WIDTH
    hl = GLA_DK * GLA_DV
    sT_ref[...] = s0_ref[...]
    gmean = _head_mean_matrix()
    scale = GLA_DK ** -0.5

    def step(t, carry):
        r0 = pl.multiple_of(t * nb, nb)
        q = x_ref[pl.ds(r0, nb), 0:kw] * scale
        k = x_ref[pl.ds(r0, nb), kw:2 * kw]
        v = x_ref[pl.ds(r0, nb), 2 * kw:2 * kw + vw]
        r = x_ref[pl.ds(r0, nb), 2 * kw + vw:2 * kw + 2 * vw]
        gl = x_ref[pl.ds(r0, nb), 2 * kw + 2 * vw:2 * kw + 2 * vw + LANE]
        la = _log_sigmoid(jnp.dot(gl.astype(BF16), wg_ref[...], preferred_element_type=F32) + bg_ref[...])
        a = jnp.exp(la / GLA_TAU)
        a3 = _split3(a)
        qb = q.astype(BF16)
        kb = k.astype(BF16)
        vb = v.astype(BF16)
        outs = []
        for h in range(GLA_HEADS):
            lanes = slice(h * hl, (h + 1) * hl)
            ek = ek_ref[:, lanes]
            a_e = (jnp.dot(a3[0], ek, preferred_element_type=F32)
                   + jnp.dot(a3[1], ek, preferred_element_type=F32)
                   + jnp.dot(a3[2], ek, preferred_element_type=F32))
            k_e = jnp.dot(kb, ek, preferred_element_type=F32)
            q_e = jnp.dot(qb, ek, preferred_element_type=F32)
            v_e = jnp.dot(vb, ev_ref[:, lanes], preferred_element_type=F32)
            s_new = a_e * sT_ref[:, lanes] + k_e * v_e
            sT_ref[:, lanes] = s_new
            prod = q_e * s_new
            acc = prod[:, 0:LANE]
            for j in range(1, hl // LANE):
                acc = acc + prod[:, j * LANE:(j + 1) * LANE]
            outs.append(acc[:, 0:GLA_DV] + acc[:, GLA_DV:2 * GLA_DV])
        o = jnp.concatenate(outs, axis=1)
        o_ref[pl.ds(r0, nb), :] = _gla_tail(o, r, gmean, onorm_ref[...], mg_ref[...]).astype(o_ref.dtype)
        return carry

    lax.fori_loop(0, t_len, step, 0)


def gla_recurrent(x, s0, ek, ev, wg, bg, onorm, mg, nb, t_len):
    rows = x.shape[0]
    return pl.pallas_call(
        functools.partial(_gla_rec_kernel, nb=nb, t_len=t_len),
        out_shape=[jax.ShapeDtypeStruct((rows, GLA_WIDTH), BF16),
                   jax.ShapeDtypeStruct((nb, GLA_STATE_LANES), F32)],
        compiler_params=pltpu.CompilerParams(vmem_limit_bytes=VMEM_LIMIT),
        name="gla_recurrent",
    )(x, s0, ek, ev, wg, bg, onorm, mg)


def _gla_expanders():
    lane = jnp.arange(GLA_STATE_LANES)
    h = lane // (GLA_DK * GLA_DV)
    dk = (lane // GLA_DV) % GLA_DK
    dv = lane % GLA_DV
    ek = (jnp.arange(GLA_KEY_WIDTH)[:, None] == (h * GLA_DK + dk)[None, :]).astype(BF16)
    ev = (jnp.arange(GLA_WIDTH)[:, None] == (h * GLA_DV + dv)[None, :]).astype(BF16)
    return ek, ev


def _mix_residual(x_ref, m_refs, g1_ref, wout_ref):
    mix = jnp.concatenate([m[...] for m in m_refs], axis=1)
    proj = jnp.dot(mix, wout_ref[...], preferred_element_type=F32)
    return x_ref[0] + _gate(proj, g1_ref[0])


def _swiglu(h, wg_ref, wu_ref, wd_ref, tf, lead=(), between=None):
    ff = wg_ref.shape[-1]
    n_dots = 3 * (ff // tf)
    tick = (lambda i: between(i, n_dots)) if between is not None else (lambda i: None)
    acc = jnp.zeros((h.shape[0], wd_ref.shape[-1]), F32)
    for c in range(ff // tf):
        cols = slice(c * tf, (c + 1) * tf)
        gate = jnp.dot(h, wg_ref[(*lead, slice(None), cols)], preferred_element_type=F32)
        tick(3 * c)
        up = jnp.dot(h, wu_ref[(*lead, slice(None), cols)], preferred_element_type=F32)
        tick(3 * c + 1)
        acc = acc + jnp.dot((_silu(gate) * up).astype(BF16), wd_ref[(*lead, cols, slice(None))],
                            preferred_element_type=F32)
        tick(3 * c + 2)
    return acc


def _post_dense_kernel(x_ref, m0_ref, m1_ref, m2_ref, m3_ref, g1_ref, sh2_ref, sc2_ref, g2_ref, ng_ref, wout_ref,
                       wg_ref, wu_ref, wd_ref, fg_ref, o_ref, *, final_norm, tf):
    x1 = _mix_residual(x_ref, (m0_ref, m1_ref, m2_ref, m3_ref), g1_ref, wout_ref)
    h = _modulate(_rms(x1) * ng_ref[...], sc2_ref[0], sh2_ref[0]).astype(BF16)
    x2 = x1 + _gate(_swiglu(h, wg_ref, wu_ref, wd_ref, tf), g2_ref[0])
    if final_norm:
        x2 = _rms(x2) * fg_ref[...]
    o_ref[0] = x2


def post_dense(x, mixes, g1, sh2, sc2, g2, ng, wout, wg, wu, wd, fg, tm, tf, final_norm):
    s, r, d = x.shape
    rm = g1.shape[1]
    ff = wg.shape[1]
    mod = pl.BlockSpec((1, rm, d), lambda b_, i: (b_, 0, 0))
    const = lambda shape: pl.BlockSpec(shape, lambda b_, i: (0,) * len(shape))
    resident = lambda shape: pl.BlockSpec(shape, lambda b_, i: (0,) * len(shape), pipeline_mode=pl.Buffered(1))
    mixspec = pl.BlockSpec((tm, 256), lambda b_, i: (i, b_))
    return pl.pallas_call(
        functools.partial(_post_dense_kernel, final_norm=final_norm, tf=tf),
        out_shape=jax.ShapeDtypeStruct((s, r, d), F32),
        grid=(s, r // tm),
        in_specs=[pl.BlockSpec((1, tm, d), lambda b_, i: (b_, i, 0)),
                  mixspec, mixspec, mixspec, mixspec,
                  mod, mod, mod, mod,
                  const((1, d)), resident((d, d)),
                  resident((d, ff)), resident((d, ff)), resident((ff, d)),
                  const((1, d))],
        out_specs=pl.BlockSpec((1, tm, d), lambda b_, i: (b_, i, 0)),
        compiler_params=_cparams(("arbitrary", "arbitrary")),
        name="post_dense",
    )(x, *mixes, g1, sh2, sc2, g2, ng, wout, wg, wu, wd, fg)


ROW_TILE = 8


def _store_row_tiles(ref, x, lead=()):
    rows = x.shape[0]
    for s in range(ROW_TILE):
        ref[(*lead, pl.ds(s, rows, stride=ROW_TILE), slice(None))] = x[:, s * LANE:(s + 1) * LANE]


def _load_row_tiles(ref, rows, lead=()):
    return jnp.concatenate([ref[(*lead, pl.ds(s, rows, stride=ROW_TILE), slice(None))] for s in range(ROW_TILE)],
                           axis=1)


def _route_kernel(x_ref, m0_ref, m1_ref, m2_ref, m3_ref, g1_ref, sh2_ref, sc2_ref, ng_ref, wout_ref, router_ref,
                  *rest):
    x1_ref, h2_ref, route_ref = rest[-3:]
    x1 = _mix_residual(x_ref, (m0_ref, m1_ref, m2_ref, m3_ref), g1_ref, wout_ref)
    x1_ref[0] = x1
    h = _modulate(_rms(x1) * ng_ref[...], sc2_ref[0], sh2_ref[0])
    _store_row_tiles(h2_ref, h)
    h_hi = h.astype(BF16)
    h_lo = (h - h_hi.astype(F32)).astype(BF16)
    w = router_ref[...]
    w_hi = w.astype(BF16)
    w_lo = (w - w_hi.astype(F32)).astype(BF16)
    logits = (jnp.dot(h_hi, w_hi, preferred_element_type=F32) + jnp.dot(h_lo, w_hi, preferred_element_type=F32)
              + jnp.dot(h_hi, w_lo, preferred_element_type=F32))
    lane = lax.broadcasted_iota(jnp.int32, logits.shape, 1).astype(F32)
    neg = jnp.float32(-jnp.inf)
    logits = jnp.where(lane < N_EXPERTS, logits, neg)
    m1 = jnp.max(logits, axis=1, keepdims=True)
    i1 = jnp.min(jnp.where(logits == m1, lane, float(LANE)), axis=1, keepdims=True)
    others = jnp.where(lane == i1, neg, logits)
    m2 = jnp.max(others, axis=1, keepdims=True)
    i2 = jnp.min(jnp.where(others == m2, lane, float(LANE)), axis=1, keepdims=True)
    e2 = jnp.exp(m2 - m1)
    den = 1.0 + e2
    route_ref[...] = (jnp.where(lane == 0.0, i1, 0.0) + jnp.where(lane == 1.0, i2, 0.0)
                      + jnp.where(lane == 2.0, 1.0 / den, 0.0) + jnp.where(lane == 3.0, e2 / den, 0.0))


def moe_route(x, mixes, g1, sh2, sc2, ng, wout, router, tm, row0, shared):
    s, r, d = x.shape
    rm = g1.shape[1]
    nt = r // tm
    blk0 = row0 // tm
    n_total = shared[1].shape[0]
    mod = pl.BlockSpec((1, rm, d), lambda b_, i: (b_, 0, 0))
    const = lambda shape: pl.BlockSpec(shape, lambda b_, i: (0,) * len(shape))
    mixspec = pl.BlockSpec((tm, 256), lambda b_, i: (i, b_))
    in_specs = [pl.BlockSpec((1, tm, d), lambda b_, i: (b_, i, 0)),
                mixspec, mixspec, mixspec, mixspec, mod, mod, mod,
                const((1, d)), const((d, d)), const((d, LANE))]
    args = [x, *mixes, g1, sh2, sc2, ng, wout, router]
    in_specs += [pl.BlockSpec(memory_space=pl.ANY), pl.BlockSpec(memory_space=pl.ANY)]
    aliases = {len(args): 1, len(args) + 1: 2}
    args += list(shared)
    return pl.pallas_call(
        _route_kernel,
        out_shape=[jax.ShapeDtypeStruct((s, r, d), F32),
                   jax.ShapeDtypeStruct((n_total * ROW_TILE, LANE), F32),
                   jax.ShapeDtypeStruct((n_total, LANE), F32)],
        grid=(s, nt),
        in_specs=in_specs,
        out_specs=[pl.BlockSpec((1, tm, d), lambda b_, i: (b_, i, 0)),
                   pl.BlockSpec((tm * ROW_TILE, LANE), lambda b_, i: (blk0 + b_ * nt + i, 0)),
                   pl.BlockSpec((tm, LANE), lambda b_, i: (blk0 + b_ * nt + i, 0))],
        input_output_aliases=aliases,
        compiler_params=_cparams(("arbitrary", "arbitrary")),
        name="moe_route",
    )(*args)


def _route_tables(route, tg, n_tiles):
    n_total = route.shape[0]
    flat_e = route[:, 0:2].astype(jnp.int32).reshape(-1)
    keys = jnp.concatenate([flat_e, jnp.full((tg,), N_EXPERTS, jnp.int32)])
    order = jnp.argsort(keys, stable=True).astype(jnp.int32)
    counts = jnp.sum(flat_e[:, None] == jnp.arange(N_EXPERTS, dtype=jnp.int32)[None, :], axis=0).astype(jnp.int32)
    tiles_per = (counts + tg - 1) // tg
    tile_end = jnp.cumsum(tiles_per)
    n_used = tile_end[-1]
    tile_id = jnp.arange(n_tiles, dtype=jnp.int32)
    tile_ok = tile_id < n_used
    tile_e = jnp.sum(jnp.minimum(tile_id, n_used - 1)[:, None] >= tile_end[None, :], axis=1).astype(jnp.int32)
    sort_start = jnp.cumsum(counts) - counts
    done = (tile_id - (tile_end - tiles_per)[tile_e]) * tg
    n_valid = jnp.where(tile_ok, jnp.clip(counts[tile_e] - done, 0, tg), 0).astype(jnp.int32)
    tile_start = jnp.where(tile_ok, sort_start[tile_e] + done, 0).astype(jnp.int32)
    real = order < 2 * n_total
    src = jnp.where(real, (order >> 1) * ROW_TILE, 0)
    dst = jnp.where(real, ((order & 1) * n_total + (order >> 1)) * ROW_TILE, 0)
    return tile_e, n_valid, tile_start, src, dst


DMA_UNROLL = 8


def _experts_kernel(te_ref, nv_ref, ts_ref, src_ref, dst_ref, h2_hbm, wg_ref, wu_ref, wd_ref, out_hbm,
                    xbuf, obuf, gsem, ssem, *, tg, n_tiles, tf):
    j = pl.program_id(0)
    slot = lax.rem(j, 2)
    other = 1 - slot
    ok = nv_ref[j] > 0

    def row_tile(buf, s_, r):
        start = r * ROW_TILE if isinstance(r, int) else pl.multiple_of(r * ROW_TILE, ROW_TILE)
        return buf.at[s_, pl.ds(start, ROW_TILE), :]

    def gather_row(tile, s_, r, priority=0):
        row = pl.multiple_of(src_ref[ts_ref[tile] + r], ROW_TILE)
        pltpu.make_async_copy(h2_hbm.at[pl.ds(row, ROW_TILE), :], row_tile(xbuf, s_, r),
                              gsem.at[s_]).start(priority=priority)

    def scatter_row(tile, s_, r, priority=0):
        row = pl.multiple_of(dst_ref[ts_ref[tile] + r], ROW_TILE)
        pltpu.make_async_copy(row_tile(obuf, s_, r), out_hbm.at[pl.ds(row, ROW_TILE), :],
                              ssem.at[s_]).start(priority=priority)

    def full_tile(issue_row, tile, s_):
        def body(r8, c):
            for u in range(DMA_UNROLL):
                issue_row(tile, s_, r8 * DMA_UNROLL + u, priority=u % 2)
            return c

        lax.fori_loop(0, tg // DMA_UNROLL, body, 0)

    def gather(tile, s_):
        full_tile(gather_row, tile, s_)

    def scatter(tile, s_):
        n = nv_ref[tile]

        @pl.when(n == tg)
        def _():
            full_tile(scatter_row, tile, s_)

        @pl.when(n < tg)
        def _():
            def body(r, c):
                scatter_row(tile, s_, r)
                return c

            lax.fori_loop(0, n, body, 0)

    def wait_all(buf, sem, s_):
        pltpu.make_async_copy(buf.at[s_], buf.at[s_], sem.at[s_]).wait()

    def wait_scatter(tile, s_):
        n = nv_ref[tile]

        @pl.when(n == tg)
        def _():
            wait_all(obuf, ssem, s_)

        @pl.when(n < tg)
        def _():
            def body(r, c):
                pltpu.make_async_copy(obuf.at[s_, pl.ds(0, ROW_TILE), :], out_hbm.at[pl.ds(0, ROW_TILE), :],
                                      ssem.at[s_]).wait()
                return c

            lax.fori_loop(0, n, body, 0)

    @pl.when(jnp.logical_and(j == 0, ok))
    def _():
        gather(0, 0)

    @pl.when(jnp.logical_or(jnp.logical_and(j == 0, ok), nv_ref[jnp.maximum(j - 1, 0)] * jnp.minimum(j, 1) > 0))
    def _():
        wait_all(xbuf, gsem, slot)

    @pl.when(j >= 2)
    def _():
        wait_scatter(j - 2, slot)

    nxt = jnp.minimum(j + 1, n_tiles - 1)

    @pl.when(ok)
    def _():
        x = _load_row_tiles(xbuf, tg, lead=(slot,)).astype(BF16)

        def gather_some(i, n):
            for r in range(i * tg // n, (i + 1) * tg // n):
                gather_row(nxt, other, r, priority=r % 2)

        y = _swiglu(x, wg_ref, wu_ref, wd_ref, tf, lead=(0,), between=gather_some)
        _store_row_tiles(obuf, y, lead=(slot,))
        scatter(j, slot)

    @pl.when(j == n_tiles - 1)
    def _():
        @pl.when(ok)
        def _():
            wait_all(xbuf, gsem, other)

        wait_scatter(j - 1, other)
        wait_scatter(j, slot)


def moe_experts(h2, tables, wg, wu, wd, tg, n_tiles, tf):
    n_exp, d, ff = wg.shape
    assert d == ROW_TILE * LANE and n_tiles >= 2
    tile_e, n_valid, tile_start, src, dst = tables
    wspec = lambda shape: pl.BlockSpec(shape, lambda j, te, *_: (te[j], 0, 0))
    grid_spec = pltpu.PrefetchScalarGridSpec(
        num_scalar_prefetch=5,
        grid=(n_tiles,),
        in_specs=[pl.BlockSpec(memory_space=pl.ANY), wspec((1, d, ff)), wspec((1, d, ff)), wspec((1, ff, d))],
        out_specs=pl.BlockSpec(memory_space=pl.ANY),
        scratch_shapes=[pltpu.VMEM((2, tg * ROW_TILE, LANE), F32), pltpu.VMEM((2, tg * ROW_TILE, LANE), F32),
                        pltpu.SemaphoreType.DMA((2,)), pltpu.SemaphoreType.DMA((2,))])
    return pl.pallas_call(
        functools.partial(_experts_kernel, tg=tg, n_tiles=n_tiles, tf=tf),
        out_shape=jax.ShapeDtypeStruct((2 * h2.shape[0], LANE), F32),
        grid_spec=grid_spec,
        compiler_params=_cparams(("arbitrary",)),
        name="moe_experts",
    )(tile_e, n_valid, tile_start, src, dst, h2, wg, wu, wd)


def _combine_kernel(x1_ref, y0_ref, y1_ref, route_ref, g2_ref, fg_ref, o_ref, *, final_norm):
    r = route_ref[...]
    rows = r.shape[0]
    f = r[:, 2:3] * _load_row_tiles(y0_ref, rows, lead=(0,)) + r[:, 3:4] * _load_row_tiles(y1_ref, rows, lead=(0,))
    x2 = x1_ref[0] + _gate(f, g2_ref[0])
    if final_norm:
        x2 = _rms(x2) * fg_ref[...]
    o_ref[0] = x2


def moe_combine(x1, y, route, g2, fg, tm, row0, final_norm):
    s, r, d = x1.shape
    rm = g2.shape[1]
    nt = r // tm
    blk0 = row0 // tm
    return pl.pallas_call(
        functools.partial(_combine_kernel, final_norm=final_norm),
        out_shape=jax.ShapeDtypeStruct((s, r, d), F32),
        grid=(s, nt),
        in_specs=[pl.BlockSpec((1, tm, d), lambda b_, i: (b_, i, 0)),
                  pl.BlockSpec((1, tm * ROW_TILE, LANE), lambda b_, i: (0, blk0 + b_ * nt + i, 0)),
                  pl.BlockSpec((1, tm * ROW_TILE, LANE), lambda b_, i: (1, blk0 + b_ * nt + i, 0)),
                  pl.BlockSpec((tm, LANE), lambda b_, i: (blk0 + b_ * nt + i, 0)),
                  pl.BlockSpec((1, rm, d), lambda b_, i: (b_, 0, 0)),
                  pl.BlockSpec((1, d), lambda b_, i: (0, 0))],
        out_specs=pl.BlockSpec((1, tm, d), lambda b_, i: (b_, i, 0)),
        compiler_params=_cparams(("arbitrary", "arbitrary")),
        name="moe_combine",
    )(x1, y, y, route, g2, fg)


def _reorder_w_in(w_in, b_in):
    cut = PW_S5 + 128 + 128 + 256 + 256 + GLA_GATE_RANK
    pad = LANE - GLA_GATE_RANK
    w = jnp.concatenate([w_in[:, :cut], jnp.zeros((w_in.shape[0], pad), w_in.dtype), w_in[:, cut:]], axis=1)
    b = jnp.concatenate([b_in[:cut], jnp.zeros((pad,), b_in.dtype), b_in[cut:]])
    return w.astype(BF16), b.reshape(1, PW_TOTAL)


def _row(a):
    return a.reshape(1, -1)


class _Branch:
    def __init__(self, nseq, nb, t_len, seq_form, tm_pre, tm_post, tc, tt_seq, row0):
        self.nseq, self.nb, self.t_len, self.seq_form = nseq, nb, t_len, seq_form
        self.tm_pre, self.tm_post, self.tc, self.tt_seq, self.row0 = tm_pre, tm_post, tc, tt_seq, row0


def _layer_params(W, i):
    row = _row
    p = {}
    p['w_in'], p['b_in'] = _reorder_w_in(W['w_in'][i], W['b_in'][i])
    p['norm_g'] = row(W['norm_mix_g'][i])
    mg = W['merge_g'][i]
    p['mg'] = [row(mg[k * 256:(k + 1) * 256]) for k in range(4)]
    ab_re, ab_im, bb_re, bb_im = s5_discretise(W['s5_a_re'][i], W['s5_a_im'][i], W['s5_log_dt'][i],
                                               W['s5_b_re'][i], W['s5_b_im'][i])
    p['s5'] = (jnp.concatenate([_block_diag_in(bb_re), _block_diag_in(bb_im)], axis=1).astype(BF16),
               _block_diag_out(W['s5_c_re'][i]).astype(BF16), _block_diag_out(W['s5_c_im'][i]).astype(BF16),
               row(ab_re), row(ab_im), row(W['s5_d'][i]), W['s5_w_glu'][i].astype(BF16), row(W['s5_b_glu'][i]))
    wg2 = jnp.zeros((LANE, GLA_KEY_WIDTH), F32).at[:GLA_GATE_RANK].set(W['gla_w_gate2'][i]).astype(BF16)
    p['gla'] = (wg2, row(W['gla_b_gate2'][i]), row(W['gla_onorm_g'][i]))
    p['conv'] = (W['conv_w_dw'][i], row(W['conv_b_dw'][i]), row(W['conv_ln_g'][i]), row(W['conv_ln_b'][i]),
                 W['conv_w_pw'][i].astype(BF16), row(W['conv_b_pw'][i]))
    p['gmlp_ln'] = (row(W['gmlp_ln_g'][i]), row(W['gmlp_ln_b'][i]))
    p['gmlp_ws'], p['gmlp_bs'] = W['gmlp_w_s'][i], W['gmlp_b_s'][i]
    return p


def _mixers(x, mods, states, p, i, br, out):
    nseq, nb, t_len, seq_form = br.nseq, br.nb, br.t_len, br.seq_form
    s5_re0, s5_im0, gla0, conv0 = states
    new_re, new_im, new_gla, new_conv, new_v = out
    mg = p['mg']
    p_s5, p_gla, p_conv, p_mlp = pre_mixer(x, mods[0], mods[1], p['norm_g'], p['w_in'], p['b_in'], br.tm_pre)

    h0 = jnp.concatenate([s5_re0[i].reshape(nb, S5_LANES), s5_im0[i].reshape(nb, S5_LANES)], axis=1)
    o_s5, h_t = s5_mixer(p_s5, h0, *p['s5'], mg[0], nb, br.tc, nseq)
    new_re.append(h_t[:, :S5_LANES].reshape(nb, S5_GROUPS, S5_STATE))
    new_im.append(h_t[:, S5_LANES:].reshape(nb, S5_GROUPS, S5_STATE))

    if seq_form:
        eye = jnp.eye(GLA_HEADS, dtype=F32)
        s0 = jnp.einsum('bhkv,hg->bhvgk', gla0[i], eye).reshape(nseq, GLA_WIDTH, GLA_KEY_WIDTH)
        o_gla, s_t = gla_seq(p_gla, nseq, s0, *p['gla'], mg[1], br.tt_seq)
        s5d = s_t.reshape(nseq, GLA_HEADS, GLA_DV, GLA_HEADS, GLA_DK)
        new_gla.append(jnp.stack([jnp.swapaxes(s5d[:, h, :, h, :], 1, 2) for h in range(GLA_HEADS)], axis=1))
    else:
        ek, ev = _gla_expanders()
        o_gla, s_t = gla_recurrent(p_gla, gla0[i].reshape(nb, GLA_STATE_LANES), ek, ev, *p['gla'], mg[1], nb, t_len)
        new_gla.append(s_t.reshape(nb, GLA_HEADS, GLA_DK, GLA_DV))

    c0 = jnp.transpose(conv0[i], (1, 0, 2)).reshape(CONV_HIST * nb, CONV_DIM)
    o_conv, buf = conv_mixer(p_conv, c0, *p['conv'], mg[2], nb, br.tc, nseq)
    new_conv.append(jnp.transpose(buf.reshape(CONV_HIST, nb, CONV_DIM), (1, 0, 2)))

    ws, bs = p['gmlp_ws'], p['gmlp_bs']
    if seq_form:
        wcat = jnp.transpose(ws, (1, 0, 2)).reshape(GMLP_CHUNK, GMLP_HEADS * GMLP_CHUNK)
        bias = jnp.repeat(bs.T, GMLP_HEAD_DIM, axis=1)
        o_mlp = gmlp_seq(p_mlp, nseq, *p['gmlp_ln'], wcat, bias, mg[3], min(4 * br.tt_seq, t_len))
        new_v.append(None)
    else:
        tri = jnp.tril(jnp.ones((t_len, t_len), F32))
        wrow = jnp.repeat(jnp.transpose(ws[:, :t_len, :t_len] * tri[None], (1, 2, 0)).reshape(t_len * t_len, GMLP_HEADS),
                          GMLP_HEAD_DIM, axis=1)
        brow = jnp.repeat(bs[:, :t_len].T, GMLP_HEAD_DIM, axis=1)
        o_mlp, vn = gmlp_short(p_mlp, *p['gmlp_ln'], wrow, brow, mg[3], nb, t_len)
        new_v.append(vn)

    return [o_s5, o_gla, o_conv, o_mlp]


FF_TILE = 1408
EXPERT_FF_TILE = 256
EXPERT_ROWS = 512


def _channel_mixer(xs, mixes, mods, W, i, branches, last):
    ng, wout, fg = _row(W['norm_ffn_g'][i]), W['w_out'][i].astype(BF16), _row(W['final_norm_g'])
    j = i // 2
    if i % 2 == 0:
        wg, wu, wd = (W['ffn_w_gate'][j].astype(BF16), W['ffn_w_up'][j].astype(BF16),
                      W['ffn_w_down'][j].astype(BF16))
        return [post_dense(x, mx, m[2], m[3], m[4], m[5], ng, wout, wg, wu, wd, fg, br.tm_post, FF_TILE, last)
                for x, mx, m, br in zip(xs, mixes, mods, branches)]
    n_total = sum(x.shape[0] * x.shape[1] for x in xs)
    tg = EXPERT_ROWS
    n_tiles = 2 * n_total // tg + N_EXPERTS
    router = jnp.zeros((D_MODEL, LANE), F32).at[:, :N_EXPERTS].set(W['moe_router'][j])
    h2, route = jnp.zeros((n_total * ROW_TILE, LANE), F32), jnp.zeros((n_total, LANE), F32)
    x1s = []
    for x, mx, m, br in zip(xs, mixes, mods, branches):
        x1, h2, route = moe_route(x, mx, m[2], m[3], m[4], ng, wout, router, br.tm_post, br.row0, (h2, route))
        x1s.append(x1)
    tables = _route_tables(route, tg, n_tiles)
    y = moe_experts(h2, tables, W['moe_w_gate'][j].astype(BF16), W['moe_w_up'][j].astype(BF16),
                    W['moe_w_down'][j].astype(BF16), tg, n_tiles, EXPERT_FF_TILE)
    y = y.reshape(2, n_total * ROW_TILE, LANE)
    return [moe_combine(x1, y, route, m[5], fg, br.tm_post, br.row0, last)
            for x1, m, br in zip(x1s, mods, branches)]


def kernel(x_prompt, x_sample, c_prompt, c_sample, state_s5_re, state_s5_im, state_gla, cache_conv, ada_w, ada_b, norm_mix_g, norm_ffn_g, w_in, b_in, s5_a_re, s5_a_im, s5_log_dt, s5_b_re, s5_b_im, s5_c_re, s5_c_im, s5_d, s5_w_glu, s5_b_glu, gla_w_gate2, gla_b_gate2, gla_onorm_g, conv_w_dw, conv_b_dw, conv_ln_g, conv_ln_b, conv_w_pw, conv_b_pw, gmlp_ln_g, gmlp_ln_b, gmlp_w_s, gmlp_b_s, merge_g, w_out, ffn_w_gate, ffn_w_up, ffn_w_down, moe_router, moe_w_gate, moe_w_up, moe_w_down, final_norm_g):
    W = dict(norm_mix_g=norm_mix_g, norm_ffn_g=norm_ffn_g, w_in=w_in, b_in=b_in, s5_a_re=s5_a_re, s5_a_im=s5_a_im,
             s5_log_dt=s5_log_dt, s5_b_re=s5_b_re, s5_b_im=s5_b_im, s5_c_re=s5_c_re, s5_c_im=s5_c_im, s5_d=s5_d,
             s5_w_glu=s5_w_glu, s5_b_glu=s5_b_glu, gla_w_gate2=gla_w_gate2, gla_b_gate2=gla_b_gate2,
             gla_onorm_g=gla_onorm_g, conv_w_dw=conv_w_dw, conv_b_dw=conv_b_dw, conv_ln_g=conv_ln_g,
             conv_ln_b=conv_ln_b, conv_w_pw=conv_w_pw, conv_b_pw=conv_b_pw, gmlp_ln_g=gmlp_ln_g,
             gmlp_ln_b=gmlp_ln_b, gmlp_w_s=gmlp_w_s, gmlp_b_s=gmlp_b_s, merge_g=merge_g, w_out=w_out,
             ffn_w_gate=ffn_w_gate, ffn_w_up=ffn_w_up, ffn_w_down=ffn_w_down, moe_router=moe_router,
             moe_w_gate=moe_w_gate, moe_w_up=moe_w_up, moe_w_down=moe_w_down, final_norm_g=final_norm_g)
    depth = w_in.shape[0]
    bp, tp, d = x_prompt.shape
    bs, ts, _ = x_sample.shape

    m = ada_modulation(jnp.concatenate([c_prompt, c_sample], axis=0), ada_w, ada_b)
    mods_p = [[m[i, :bp, k * d:(k + 1) * d].reshape(bp, 1, d) for k in range(6)] for i in range(depth)]
    mods_s = [[m[i, bp:, k * d:(k + 1) * d].reshape(1, bs, d) for k in range(6)] for i in range(depth)]

    z_re = jnp.zeros((depth, bp, S5_GROUPS, S5_STATE), F32)
    z_gla = jnp.zeros((depth, bp, GLA_HEADS, GLA_DK, GLA_DV), F32)
    z_conv = jnp.zeros((depth, bp, CONV_HIST, CONV_DIM), x_prompt.dtype)
    states = [(z_re, z_re, z_gla, z_conv), (state_s5_re, state_s5_im, state_gla, cache_conv)]
    branches = [_Branch(nseq=bp, nb=bp, t_len=tp, seq_form=True, tm_pre=min(512, tp), tm_post=min(512, tp),
                        tc=min(256, tp), tt_seq=min(512, tp), row0=0),
                _Branch(nseq=1, nb=bs, t_len=ts, seq_form=False, tm_pre=ts * bs, tm_post=min(512, ts * bs),
                        tc=ts, tt_seq=None, row0=bp * tp)]
    xs = [x_prompt, jnp.transpose(x_sample, (1, 0, 2)).reshape(1, ts * bs, d)]
    outs = [([], [], [], [], []), ([], [], [], [], [])]
    for i in range(depth):
        mods = [mods_p[i], mods_s[i]]
        params = _layer_params(W, i)
        mixes = [_mixers(x, m, st_, params, i, br, o)
                 for x, m, st_, br, o in zip(xs, mods, states, branches, outs)]
        xs = _channel_mixer(xs, mixes, mods, W, i, branches, i == depth - 1)

    y_p = xs[0]
    y_s = jnp.transpose(xs[1].reshape(ts, bs, d), (1, 0, 2))
    p_re, p_im, p_gla, p_conv, _ = outs[0]
    s_re, s_im, s_gla, s_conv, s_v = outs[1]
    s_v = [jnp.transpose(v.reshape(ts, bs, GMLP_WIDTH), (1, 0, 2)) for v in s_v]
    st = jnp.stack
    return (y_p, y_s, st(p_re), st(p_im), st(p_gla), st(p_conv),
            st(s_re), st(s_im), st(s_gla), st(s_conv), st(s_v))
```

```python
import functools
import math

import jax
import jax.numpy as jnp
from jax import lax
from jax.experimental import pallas as pl
from jax.experimental.pallas import tpu as pltpu

D_MODEL = 1024
S5_WIDTH = 256
S5_GROUP = 16
S5_GROUPS = 16
S5_STATE = 64
S5_LANES = S5_GROUPS * S5_STATE
GLA_HEADS = 4
GLA_DV = 64
GLA_DK = 32
GLA_WIDTH = 256
GLA_KEY_WIDTH = 128
GLA_GATE_RANK = 16
GLA_TAU = 16.0
GLA_CHUNK = 64
GLA_STATE_LANES = GLA_HEADS * GLA_DK * GLA_DV
CONV_DIM = 256
CONV_WIDTH = 31
CONV_HIST = CONV_WIDTH - 1
GMLP_WIDTH = 256
GMLP_HEADS = 4
GMLP_HEAD_DIM = 64
GMLP_CHUNK = 128
D_FF = 2816
N_EXPERTS = 8
EPS = 1e-6

LANE = 128
PW_S5 = 256
PW_GLA = 128 + 128 + 256 + 256 + LANE
PW_CONV = 512
PW_MLP = 512
PW_TOTAL = PW_S5 + PW_GLA + PW_CONV + PW_MLP
VMEM_LIMIT = 56 * 1024 * 1024

F32 = jnp.float32
BF16 = jnp.bfloat16


def _cparams(sem):
    return pltpu.CompilerParams(dimension_semantics=sem, vmem_limit_bytes=VMEM_LIMIT)


def _rms(x):
    return x * lax.rsqrt(jnp.mean(x * x, axis=-1, keepdims=True) + EPS)


def _layernorm(x, g, b):
    mu = jnp.mean(x, axis=-1, keepdims=True)
    xc = x - mu
    var = jnp.mean(xc * xc, axis=-1, keepdims=True)
    return xc * lax.rsqrt(var + EPS) * g + b


def _silu(x):
    return x * jax.nn.sigmoid(x)


def _gelu_tanh(x):
    return 0.5 * x * (1.0 + jnp.tanh(math.sqrt(2.0 / math.pi) * (x + 0.044715 * (x * x * x))))


def _log_sigmoid(x):
    return jnp.minimum(x, 0.0) - jnp.log(1.0 + jnp.exp(-jnp.abs(x)))


def _same_block(shape, row_block, col_block):
    r = lax.broadcasted_iota(jnp.int32, shape, 0) >> (row_block.bit_length() - 1)
    c = lax.broadcasted_iota(jnp.int32, shape, 1) >> (col_block.bit_length() - 1)
    return r == c


def _modulate(y, sc, sh):
    rm = sc.shape[0]
    if rm == 1:
        return y * (1.0 + sc) + sh
    rows, d = y.shape
    y3 = y.reshape(rows // rm, rm, d)
    return (y3 * (1.0 + sc)[None] + sh[None]).reshape(rows, d)


def _gate(y, g):
    rm = g.shape[0]
    if rm == 1:
        return y * g
    rows, d = y.shape
    return (y.reshape(rows // rm, rm, d) * g[None]).reshape(rows, d)


def _ada_kernel(c_ref, w_ref, b_ref, o_ref):
    c = c_ref[...]
    s = _silu(c).astype(BF16)
    o_ref[0] = jnp.dot(s, w_ref[0].astype(BF16), preferred_element_type=F32) + b_ref[0]


def ada_modulation(c_all, ada_w, ada_b):
    depth, d, n6 = ada_w.shape
    rows = c_all.shape[0]
    tn = 1536
    return pl.pallas_call(
        _ada_kernel,
        out_shape=jax.ShapeDtypeStruct((depth, rows, n6), F32),
        grid=(depth, n6 // tn),
        in_specs=[pl.BlockSpec((rows, d), lambda l, j: (0, 0)),
                  pl.BlockSpec((1, d, tn), lambda l, j: (l, 0, j)),
                  pl.BlockSpec((1, 1, tn), lambda l, j: (l, 0, j))],
        out_specs=pl.BlockSpec((1, rows, tn), lambda l, j: (l, 0, j)),
        compiler_params=_cparams(("arbitrary", "arbitrary")),
        name="ada_modulation",
    )(c_all, ada_w, ada_b.reshape(depth, 1, n6))


def _pre_kernel(x_ref, sh_ref, sc_ref, g_ref, w_ref, b_ref, o_s5, o_gla, o_conv, o_mlp):
    x = x_ref[0]
    y = _modulate(_rms(x) * g_ref[...], sc_ref[0], sh_ref[0])
    p = jnp.dot(y.astype(BF16), w_ref[...], preferred_element_type=F32) + b_ref[...]
    o_s5[...] = p[:, 0:PW_S5]
    o_gla[...] = p[:, PW_S5:PW_S5 + PW_GLA]
    o_conv[...] = p[:, PW_S5 + PW_GLA:PW_S5 + PW_GLA + PW_CONV]
    o_mlp[...] = p[:, PW_S5 + PW_GLA + PW_CONV:PW_TOTAL]


def pre_mixer(x, sh, sc, g, w, b, tm):
    s, r, d = x.shape
    rm = sh.shape[1]
    widths = (PW_S5, PW_GLA, PW_CONV, PW_MLP)
    return pl.pallas_call(
        _pre_kernel,
        out_shape=[jax.ShapeDtypeStruct((r, s * w_), F32) for w_ in widths],
        grid=(s, r // tm),
        in_specs=[pl.BlockSpec((1, tm, d), lambda b_, i: (b_, i, 0)),
                  pl.BlockSpec((1, rm, d), lambda b_, i: (b_, 0, 0)),
                  pl.BlockSpec((1, rm, d), lambda b_, i: (b_, 0, 0)),
                  pl.BlockSpec((1, d), lambda b_, i: (0, 0)),
                  pl.BlockSpec((d, PW_TOTAL), lambda b_, i: (0, 0)),
                  pl.BlockSpec((1, PW_TOTAL), lambda b_, i: (0, 0))],
        out_specs=[pl.BlockSpec((tm, w_), lambda b_, i: (i, b_)) for w_ in widths],
        compiler_params=_cparams(("arbitrary", "arbitrary")),
        name="pre_mixer",
    )(x, sh, sc, g, w, b)


def _s5_disc_kernel(lr_ref, li_ref, ldt_ref, br_ref, bi_ref, abr_ref, abi_ref, bbr_ref, bbi_ref):
    lr = lr_ref[...]
    li = li_ref[...]
    dt = jnp.exp(ldt_ref[...])
    mag = jnp.exp(lr * dt)
    ang = li * dt
    ab_re = mag * jnp.cos(ang)
    ab_im = mag * jnp.sin(ang)
    den = lr * lr + li * li
    nr = ab_re - 1.0
    f_re = (nr * lr + ab_im * li) / den
    f_im = (ab_im * lr - nr * li) / den
    br = br_ref[...]
    bi = bi_ref[...]
    abr_ref[...] = ab_re
    abi_ref[...] = ab_im
    bbr_ref[...] = f_re * br - f_im * bi
    bbi_ref[...] = f_re * bi + f_im * br


def s5_discretise(a_re, a_im, log_dt, b_re, b_im):
    n = b_re.shape[-1]
    gp = a_re.size
    bc = lambda a: jnp.broadcast_to(a.reshape(gp, 1), (gp, n))
    ldt = jnp.broadcast_to(log_dt[:, None], a_re.shape)
    outs = pl.pallas_call(
        _s5_disc_kernel,
        out_shape=[jax.ShapeDtypeStruct((gp, n), F32)] * 4,
        name="s5_discretise",
    )(bc(a_re), bc(a_im), bc(ldt), b_re.reshape(gp, n), b_im.reshape(gp, n))
    ab_re, ab_im, bb_re, bb_im = outs
    return ab_re[:, 0], ab_im[:, 0], bb_re, bb_im


def _block_diag_in(bb):
    g, p, n = S5_GROUPS, S5_STATE, S5_GROUP
    b3 = bb.reshape(g, p, n)
    eye = jnp.eye(g, dtype=bb.dtype)
    return jnp.einsum('gpn,gh->gnhp', b3, eye).reshape(g * n, g * p)


def _block_diag_out(c):
    g, p, n = S5_GROUPS, S5_STATE, S5_GROUP
    eye = jnp.eye(g, dtype=c.dtype)
    return jnp.einsum('gnp,gh->gphn', c, eye).reshape(g * p, g * n)


def _to_time_major(x_ref, cols, tm_ref, row0, nsl):
    rt = x_ref.shape[0]
    w = x_ref.shape[1] // nsl
    start, width = cols
    for l in range(nsl):
        for h in range(width // LANE):
            c0 = l * w + start + h * LANE
            tm_ref[h, pl.ds(row0 + l, rt, stride=nsl), :] = x_ref[:, c0:c0 + LANE]


def _from_time_major(tm_ref, o_ref, nsl):
    rt = o_ref.shape[0]
    nh = tm_ref.shape[0]
    for l in range(nsl):
        piece = jnp.concatenate([tm_ref[h, pl.ds(l, rt, stride=nsl), :] for h in range(nh)], axis=1)
        o_ref[:, l * nh * LANE:(l + 1) * nh * LANE] = piece.astype(o_ref.dtype)


def _lane_tiles(tm_ref, rows=slice(None)):
    return jnp.concatenate([tm_ref[h, rows, :] for h in range(tm_ref.shape[0])], axis=1)


def _set_lane_tiles(tm_ref, x):
    for h in range(tm_ref.shape[0]):
        tm_ref[h] = x[:, h * LANE:(h + 1) * LANE]


def _s5_kernel(u_ref, h0_ref, bblk_ref, cre_ref, cim_ref, ar_ref, ai_ref, d_ref, wglu_ref, bglu_ref, mg_ref,
               o_ref, hT_ref, xs_ref, hs_ref, tm_ref, *, nb, tc, nsl):
    i = pl.program_id(0)

    @pl.when(i == 0)
    def _():
        hs_ref[...] = h0_ref[...]

    _to_time_major(u_ref, (0, S5_WIDTH), tm_ref, 0, nsl)
    u = _lane_tiles(tm_ref)
    xs_ref[...] = jnp.dot(u.astype(BF16), bblk_ref[...], preferred_element_type=F32)
    ar = jnp.broadcast_to(ar_ref[...], (nb, S5_LANES))
    ai = jnp.broadcast_to(ai_ref[...], (nb, S5_LANES))

    def step(t, carry):
        hr, hi = carry
        row = pl.multiple_of(t * nb, nb)
        xr = xs_ref[pl.ds(row, nb), 0:S5_LANES]
        xi = xs_ref[pl.ds(row, nb), S5_LANES:2 * S5_LANES]
        nr = ar * hr - ai * hi + xr
        ni = ar * hi + ai * hr + xi
        xs_ref[pl.ds(row, nb), 0:S5_LANES] = nr
        xs_ref[pl.ds(row, nb), S5_LANES:2 * S5_LANES] = ni
        return nr, ni

    hr, hi = lax.fori_loop(0, tc, step, (hs_ref[:, 0:S5_LANES], hs_ref[:, S5_LANES:2 * S5_LANES]),
                           unroll=True if tc <= 8 else 4)
    hs_ref[:, 0:S5_LANES] = hr
    hs_ref[:, S5_LANES:2 * S5_LANES] = hi

    y = (jnp.dot(xs_ref[:, 0:S5_LANES].astype(BF16), cre_ref[...], preferred_element_type=F32)
         - jnp.dot(xs_ref[:, S5_LANES:2 * S5_LANES].astype(BF16), cim_ref[...], preferred_element_type=F32))
    y = y + d_ref[...] * u
    y = _gelu_tanh(y)
    y = y * jax.nn.sigmoid(jnp.dot(y.astype(BF16), wglu_ref[...], preferred_element_type=F32) + bglu_ref[...])
    _set_lane_tiles(tm_ref, _rms(y) * mg_ref[...])
    _from_time_major(tm_ref, o_ref, nsl)

    @pl.when(i == pl.num_programs(0) - 1)
    def _():
        hT_ref[...] = hs_ref[...]


def s5_mixer(u, h0, bblk, cre, cim, ar, ai, d, wglu, bglu, mg, nb, tc, nsl):
    rows = u.shape[0]
    rc = nb * tc
    rt = rc // nsl
    full = lambda shape: pl.BlockSpec(shape, lambda i: (0,) * len(shape))
    return pl.pallas_call(
        functools.partial(_s5_kernel, nb=nb, tc=tc, nsl=nsl),
        out_shape=[jax.ShapeDtypeStruct((rows, nsl * S5_WIDTH), BF16),
                   jax.ShapeDtypeStruct((nb, 2 * S5_LANES), F32)],
        grid=(rows // rt,),
        in_specs=[pl.BlockSpec((rt, nsl * S5_WIDTH), lambda i: (i, 0)),
                  full((nb, 2 * S5_LANES)),
                  full((S5_WIDTH, 2 * S5_LANES)),
                  full((S5_LANES, S5_WIDTH)), full((S5_LANES, S5_WIDTH)),
                  full((1, S5_LANES)), full((1, S5_LANES)),
                  full((1, S5_WIDTH)), full((S5_WIDTH, S5_WIDTH)), full((1, S5_WIDTH)), full((1, S5_WIDTH))],
        out_specs=[pl.BlockSpec((rt, nsl * S5_WIDTH), lambda i: (i, 0)),
                   full((nb, 2 * S5_LANES))],
        scratch_shapes=[pltpu.VMEM((rc, 2 * S5_LANES), F32), pltpu.VMEM((nb, 2 * S5_LANES), F32),
                        pltpu.VMEM((S5_WIDTH // LANE, rc, LANE), F32)],
        compiler_params=_cparams(("arbitrary",)),
        name="s5_mixer",
    )(u, h0, bblk, cre, cim, ar, ai, d, wglu, bglu, mg)


CONV_ROWS = 64


def _conv_kernel(ag_ref, c0_ref, wdw_ref, bdw_ref, lng_ref, lnb_ref, wpw_ref, bpw_ref, mg_ref,
                 o_ref, buf_ref, zc_ref, y_ref, *, nb, tc, nsl):
    i = pl.program_id(0)
    hist = CONV_HIST * nb
    rc = nb * tc
    rt = rc // nsl
    n_lt = CONV_DIM // LANE

    @pl.when(i == 0)
    def _():
        for h in range(n_lt):
            zc_ref[h, 0:hist, :] = c0_ref[:, h * LANE:(h + 1) * LANE]

    @pl.when(i > 0)
    def _():
        for h in range(n_lt):
            zc_ref[h, 0:hist, :] = zc_ref[h, rc:rc + hist, :]

    w_seq = 2 * CONV_DIM
    for l in range(nsl):
        a = ag_ref[:, l * w_seq:l * w_seq + CONV_DIM]
        g = ag_ref[:, l * w_seq + CONV_DIM:(l + 1) * w_seq]
        z = a * jax.nn.sigmoid(g)
        for h in range(n_lt):
            zc_ref[h, pl.ds(hist + l, rt, stride=nsl), :] = z[:, h * LANE:(h + 1) * LANE]

    w = wdw_ref[...]

    def tile(j, carry):
        r0 = pl.multiple_of(j * CONV_ROWS, CONV_ROWS)
        for h in range(n_lt):
            acc = jnp.zeros((CONV_ROWS, LANE), F32)
            for k in range(CONV_WIDTH):
                acc = acc + w[k:k + 1, h * LANE:(h + 1) * LANE] * zc_ref[h, pl.ds(r0 + k * nb, CONV_ROWS), :]
            y_ref[h, pl.ds(r0, CONV_ROWS), :] = acc
        return carry

    lax.fori_loop(0, rc // CONV_ROWS, tile, 0)
    y = _lane_tiles(y_ref) + bdw_ref[...]
    y = _silu(_layernorm(y, lng_ref[...], lnb_ref[...]))
    y = jnp.dot(y.astype(BF16), wpw_ref[...], preferred_element_type=F32) + bpw_ref[...]
    _set_lane_tiles(y_ref, _rms(y) * mg_ref[...])
    _from_time_major(y_ref, o_ref, nsl)

    @pl.when(i == pl.num_programs(0) - 1)
    def _():
        buf_ref[...] = _lane_tiles(zc_ref, slice(rc, rc + hist))


def conv_mixer(ag, c0, wdw, bdw, lng, lnb, wpw, bpw, mg, nb, tc, nsl):
    rows = ag.shape[0]
    rc = nb * tc
    rt = rc // nsl
    hist = CONV_HIST * nb
    assert rows == rt or tc >= CONV_HIST
    n_lt = CONV_DIM // LANE
    full = lambda shape: pl.BlockSpec(shape, lambda i: (0,) * len(shape))
    return pl.pallas_call(
        functools.partial(_conv_kernel, nb=nb, tc=tc, nsl=nsl),
        out_shape=[jax.ShapeDtypeStruct((rows, nsl * CONV_DIM), BF16),
                   jax.ShapeDtypeStruct((hist, CONV_DIM), F32)],
        grid=(rows // rt,),
        in_specs=[pl.BlockSpec((rt, nsl * 2 * CONV_DIM), lambda i: (i, 0)),
                  full((hist, CONV_DIM)), full((CONV_WIDTH, CONV_DIM)),
                  full((1, CONV_DIM)), full((1, CONV_DIM)), full((1, CONV_DIM)),
                  full((CONV_DIM, CONV_DIM)), full((1, CONV_DIM)), full((1, CONV_DIM))],
        out_specs=[pl.BlockSpec((rt, nsl * CONV_DIM), lambda i: (i, 0)), full((hist, CONV_DIM))],
        scratch_shapes=[pltpu.VMEM((n_lt, hist + rc, LANE), F32), pltpu.VMEM((n_lt, rc, LANE), F32)],
        compiler_params=_cparams(("arbitrary",)),
        name="conv_mixer",
    )(ag, c0, wdw, bdw, lng, lnb, wpw, bpw, mg)


def _gmlp_seq_kernel(uv_ref, lng_ref, lnb_ref, wcat_ref, bias_ref, mg_ref, o_ref, *, tt):
    n_chunks = tt // GMLP_CHUNK
    kc = GMLP_HEADS * GMLP_CHUNK
    rowi = lax.broadcasted_iota(jnp.int32, (GMLP_CHUNK, kc), 0)
    coli = lax.broadcasted_iota(jnp.int32, (GMLP_CHUNK, kc), 1)
    wcat = jnp.where((coli & (GMLP_CHUNK - 1)) <= rowi, wcat_ref[...], 0.0).astype(BF16)
    sel = _same_block((kc, GMLP_WIDTH), GMLP_CHUNK, GMLP_HEAD_DIM)
    for c in range(n_chunks):
        rows = slice(c * GMLP_CHUNK, (c + 1) * GMLP_CHUNK)
        u = uv_ref[rows, 0:GMLP_WIDTH]
        v = uv_ref[rows, GMLP_WIDTH:2 * GMLP_WIDTH]
        vn = _layernorm(v, lng_ref[...], lnb_ref[...])
        vbd = jnp.where(sel, jnp.concatenate([vn] * GMLP_HEADS, axis=0), 0.0).astype(BF16)
        mixed = jnp.dot(wcat, vbd, preferred_element_type=F32) + bias_ref[...]
        o_ref[rows, :] = (_rms(u * mixed) * mg_ref[...]).astype(o_ref.dtype)


def gmlp_seq(uv, nseq, lng, lnb, wcat, bias, mg, tt):
    t = uv.shape[0]
    full = lambda shape: pl.BlockSpec(shape, lambda b_, i: (0,) * len(shape))
    return pl.pallas_call(
        functools.partial(_gmlp_seq_kernel, tt=tt),
        out_shape=jax.ShapeDtypeStruct((t, nseq * GMLP_WIDTH), BF16),
        grid=(nseq, t // tt),
        in_specs=[pl.BlockSpec((tt, 2 * GMLP_WIDTH), lambda b_, i: (i, b_)),
                  full((1, GMLP_WIDTH)), full((1, GMLP_WIDTH)),
                  full((GMLP_CHUNK, GMLP_HEADS * GMLP_CHUNK)), full((GMLP_CHUNK, GMLP_WIDTH)),
                  full((1, GMLP_WIDTH))],
        out_specs=pl.BlockSpec((tt, GMLP_WIDTH), lambda b_, i: (i, b_)),
        compiler_params=_cparams(("arbitrary", "arbitrary")),
        name="gmlp_seq",
    )(uv, lng, lnb, wcat, bias, mg)


def _gmlp_short_kernel(uv_ref, lng_ref, lnb_ref, wrow_ref, brow_ref, mg_ref, o_ref, vn_ref, *, nb, t_len):
    u = uv_ref[:, 0:GMLP_WIDTH]
    v = uv_ref[:, GMLP_WIDTH:2 * GMLP_WIDTH]
    vn = _layernorm(v, lng_ref[...], lnb_ref[...])
    vn_ref[...] = vn
    wrow = wrow_ref[...]
    brow = brow_ref[...]
    for t in range(t_len):
        mixed = jnp.zeros((nb, GMLP_WIDTH), F32) + brow[t:t + 1, :]
        for j in range(t + 1):
            mixed = mixed + wrow[t * t_len + j:t * t_len + j + 1, :] * vn[j * nb:(j + 1) * nb, :]
        o = u[t * nb:(t + 1) * nb, :] * mixed
        o_ref[t * nb:(t + 1) * nb, :] = (_rms(o) * mg_ref[...]).astype(o_ref.dtype)


def gmlp_short(uv, lng, lnb, wrow, brow, mg, nb, t_len):
    rows = uv.shape[0]
    return pl.pallas_call(
        functools.partial(_gmlp_short_kernel, nb=nb, t_len=t_len),
        out_shape=[jax.ShapeDtypeStruct((rows, GMLP_WIDTH), BF16),
                   jax.ShapeDtypeStruct((rows, GMLP_WIDTH), F32)],
        compiler_params=pltpu.CompilerParams(vmem_limit_bytes=VMEM_LIMIT),
        name="gmlp_short",
    )(uv, lng, lnb, wrow, brow, mg)


def _split3(x):
    a = x.astype(BF16)
    r1 = x - a.astype(F32)
    b = r1.astype(BF16)
    c = (r1 - b.astype(F32)).astype(BF16)
    return a, b, c


def _dot_exact_rhs(x, m):
    return sum(jnp.dot(t, m, preferred_element_type=F32) for t in _split3(x))


def _dot_exact_lhs(m, x):
    return sum(jnp.dot(m, t, preferred_element_type=F32) for t in _split3(x))


def _gla_tail(o, r, gmean, onorm, mg):
    ms = _dot_exact_rhs(o * o, gmean)
    o = o * lax.rsqrt(ms + EPS) * onorm
    o = o * _silu(r)
    return _rms(o) * mg


def _head_mean_matrix():
    return jnp.where(_same_block((GLA_WIDTH, GLA_WIDTH), GLA_DV, GLA_DV), 1.0 / GLA_DV, 0.0).astype(BF16)


GLA_SEQS = 4


def _gla_seq_kernel(x_ref, s0_ref, wg_ref, bg_ref, onorm_ref, mg_ref, o_ref, sT_ref,
                    s_ref, qt_ref, kt_ref, kd_ref, dl_ref, oacc_ref, *, tt):
    i = pl.program_id(1)
    L = GLA_CHUNK
    n_ch = tt // L
    kw, vw = GLA_KEY_WIDTH, GLA_WIDTH

    @pl.when(i == 0)
    def _():
        s_ref[...] = s0_ref[...]

    rows_i = lax.broadcasted_iota(jnp.int32, (tt, tt), 0)
    cols_i = lax.broadcasted_iota(jnp.int32, (tt, tt), 1)
    shift = L.bit_length() - 1
    same_chunk = (rows_i >> shift) == (cols_i >> shift)
    tri = jnp.logical_and(same_chunk, cols_i <= rows_i).astype(BF16)
    chunk_sum = same_chunk.astype(BF16)
    chunk_rows = (lax.broadcasted_iota(jnp.int32, (n_ch, tt), 0)
                  == (lax.broadcasted_iota(jnp.int32, (n_ch, tt), 1) >> shift)).astype(BF16)
    kbd_sel = _same_block((GLA_HEADS * L, kw), L, GLA_DK)
    vbd_sel = _same_block((GLA_HEADS * L, vw), L, GLA_DV)
    causal = ((lax.broadcasted_iota(jnp.int32, (L, GLA_HEADS * L), 1) & (L - 1))
              <= lax.broadcasted_iota(jnp.int32, (L, GLA_HEADS * L), 0))
    s_sel = _same_block((vw, kw), GLA_DV, GLA_DK)
    gmean = _head_mean_matrix()
    scale = GLA_DK ** -0.5
    nt_dims = (((1,), (1,)), ((), ()))
    tn_dims = (((0,), (0,)), ((), ()))
    zero = jnp.zeros((), BF16)

    for g in range(GLA_SEQS):
        x0 = g * PW_GLA
        q = x_ref[:, x0:x0 + kw] * scale
        k = x_ref[:, x0 + kw:x0 + 2 * kw]
        gl = x_ref[:, x0 + 2 * kw + 2 * vw:x0 + PW_GLA]
        la = _log_sigmoid(jnp.dot(gl.astype(BF16), wg_ref[...], preferred_element_type=F32) + bg_ref[...])
        la = la / GLA_TAU
        bc = _dot_exact_lhs(tri, la)
        b_end = _dot_exact_lhs(chunk_sum, la)
        qt_ref[g] = (q * jnp.exp(bc)).astype(BF16)
        kt_ref[g] = (k * jnp.exp(-bc)).astype(BF16)
        kd_ref[g] = (k * jnp.exp(b_end - bc)).astype(BF16)
        dl_ref[g] = jnp.exp(_dot_exact_lhs(chunk_rows, la))

    def chunk(c, carry):
        r0 = pl.multiple_of(c * L, L)
        for g in range(GLA_SEQS):
            x0 = g * PW_GLA
            qt = qt_ref[g, pl.ds(r0, L), :]
            kt = kt_ref[g, pl.ds(r0, L), :]
            kdec = kd_ref[g, pl.ds(r0, L), :]
            vb = x_ref[pl.ds(r0, L), x0 + 2 * kw:x0 + 2 * kw + vw].astype(BF16)
            kbd = jnp.where(kbd_sel, jnp.concatenate([kt] * GLA_HEADS, axis=0), zero)
            att = lax.dot_general(qt, kbd, nt_dims, preferred_element_type=F32)
            att = jnp.where(causal, att, 0.0).astype(BF16)
            vbd = jnp.where(vbd_sel, jnp.concatenate([vb] * GLA_HEADS, axis=0), zero)
            st = s_ref[g]
            oacc_ref[g, pl.ds(r0, L), :] = (jnp.dot(att, vbd, preferred_element_type=F32)
                                            + lax.dot_general(qt, st.astype(BF16), nt_dims,
                                                              preferred_element_type=F32))
            upd = lax.dot_general(vb, kdec, tn_dims, preferred_element_type=F32)
            s_ref[g] = st * dl_ref[g, pl.ds(c, 1), :] + jnp.where(s_sel, upd, 0.0)
        return carry

    lax.fori_loop(0, n_ch, chunk, 0)

    for g in range(GLA_SEQS):
        x0 = g * PW_GLA
        r = x_ref[:, x0 + 2 * kw + vw:x0 + 2 * kw + 2 * vw]
        o_ref[:, g * vw:(g + 1) * vw] = _gla_tail(oacc_ref[g], r, gmean, onorm_ref[...],
                                                  mg_ref[...]).astype(o_ref.dtype)

    @pl.when(i == pl.num_programs(1) - 1)
    def _():
        sT_ref[...] = s_ref[...]


def gla_seq(x, nseq, s0, wg, bg, onorm, mg, tt):
    t = x.shape[0]
    g = GLA_SEQS
    assert nseq % g == 0
    full = lambda shape: pl.BlockSpec(shape, lambda b_, i: (0,) * len(shape))
    state = pl.BlockSpec((g, GLA_WIDTH, GLA_KEY_WIDTH), lambda b_, i: (b_, 0, 0))
    return pl.pallas_call(
        functools.partial(_gla_seq_kernel, tt=tt),
        out_shape=[jax.ShapeDtypeStruct((t, nseq * GLA_WIDTH), BF16),
                   jax.ShapeDtypeStruct((nseq, GLA_WIDTH, GLA_KEY_WIDTH), F32)],
        grid=(nseq // g, t // tt),
        in_specs=[pl.BlockSpec((tt, g * PW_GLA), lambda b_, i: (i, b_)),
                  state,
                  full((LANE, GLA_KEY_WIDTH)), full((1, GLA_KEY_WIDTH)),
                  full((1, GLA_WIDTH)), full((1, GLA_WIDTH))],
        out_specs=[pl.BlockSpec((tt, g * GLA_WIDTH), lambda b_, i: (i, b_)), state],
        scratch_shapes=[pltpu.VMEM((g, GLA_WIDTH, GLA_KEY_WIDTH), F32),
                        pltpu.VMEM((g, tt, GLA_KEY_WIDTH), BF16), pltpu.VMEM((g, tt, GLA_KEY_WIDTH), BF16),
                        pltpu.VMEM((g, tt, GLA_KEY_WIDTH), BF16),
                        pltpu.VMEM((g, tt // GLA_CHUNK, GLA_KEY_WIDTH), F32),
                        pltpu.VMEM((g, tt, GLA_WIDTH), F32)],
        compiler_params=_cparams(("arbitrary", "arbitrary")),
        name="gla_seq",
    )(x, s0, wg, bg, onorm, mg)


def _gla_rec_kernel(x_ref, s0_ref, ek_ref, ev_ref, wg_ref, bg_ref, onorm_ref, mg_ref, o_ref, sT_ref,
                    *, nb, t_len):
    kw, vw = GLA_KEY_WIDTH, GLA_WIDTH
    hl = GLA_DK * GLA_DV
    sT_ref[...] = s0_ref[...]
    gmean = _head_mean_matrix()
    scale = GLA_DK ** -0.5

    def step(t, carry):
        r0 = pl.multiple_of(t * nb, nb)
        q = x_ref[pl.ds(r0, nb), 0:kw] * scale
        k = x_ref[pl.ds(r0, nb), kw:2 * kw]
        v = x_ref[pl.ds(r0, nb), 2 * kw:2 * kw + vw]
        r = x_ref[pl.ds(r0, nb), 2 * kw + vw:2 * kw + 2 * vw]
        gl = x_ref[pl.ds(r0, nb), 2 * kw + 2 * vw:2 * kw + 2 * vw + LANE]
        la = _log_sigmoid(jnp.dot(gl.astype(BF16), wg_ref[...], preferred_element_type=F32) + bg_ref[...])
        a = jnp.exp(la / GLA_TAU)
        a3 = _split3(a)
        qb = q.astype(BF16)
        kb = k.astype(BF16)
        vb = v.astype(BF16)
        outs = []
        for h in range(GLA_HEADS):
            lanes = slice(h * hl, (h + 1) * hl)
            ek = ek_ref[:, lanes]
            a_e = (jnp.dot(a3[0], ek, preferred_element_type=F32)
                   + jnp.dot(a3[1], ek, preferred_element_type=F32)
                   + jnp.dot(a3[2], ek, preferred_element_type=F32))
            k_e = jnp.dot(kb, ek, preferred_element_type=F32)
            q_e = jnp.dot(qb, ek, preferred_element_type=F32)
            v_e = jnp.dot(vb, ev_ref[:, lanes], preferred_element_type=F32)
            s_new = a_e * sT_ref[:, lanes] + k_e * v_e
            sT_ref[:, lanes] = s_new
            prod = q_e * s_new
            acc = prod[:, 0:LANE]
            for j in range(1, hl // LANE):
                acc = acc + prod[:, j * LANE:(j + 1) * LANE]
            outs.append(acc[:, 0:GLA_DV] + acc[:, GLA_DV:2 * GLA_DV])
        o = jnp.concatenate(outs, axis=1)
        o_ref[pl.ds(r0, nb), :] = _gla_tail(o, r, gmean, onorm_ref[...], mg_ref[...]).astype(o_ref.dtype)
        return carry

    lax.fori_loop(0, t_len, step, 0)


def gla_recurrent(x, s0, ek, ev, wg, bg, onorm, mg, nb, t_len):
    rows = x.shape[0]
    return pl.pallas_call(
        functools.partial(_gla_rec_kernel, nb=nb, t_len=t_len),
        out_shape=[jax.ShapeDtypeStruct((rows, GLA_WIDTH), BF16),
                   jax.ShapeDtypeStruct((nb, GLA_STATE_LANES), F32)],
        compiler_params=pltpu.CompilerParams(vmem_limit_bytes=VMEM_LIMIT),
        name="gla_recurrent",
    )(x, s0, ek, ev, wg, bg, onorm, mg)


def _gla_expanders():
    lane = jnp.arange(GLA_STATE_LANES)
    h = lane // (GLA_DK * GLA_DV)
    dk = (lane // GLA_DV) % GLA_DK
    dv = lane % GLA_DV
    ek = (jnp.arange(GLA_KEY_WIDTH)[:, None] == (h * GLA_DK + dk)[None, :]).astype(BF16)
    ev = (jnp.arange(GLA_WIDTH)[:, None] == (h * GLA_DV + dv)[None, :]).astype(BF16)
    return ek, ev


def _mix_residual(x_ref, m_refs, g1_ref, wout_ref):
    mix = jnp.concatenate([m[...] for m in m_refs], axis=1)
    proj = jnp.dot(mix, wout_ref[...], preferred_element_type=F32)
    return x_ref[0] + _gate(proj, g1_ref[0])


def _swiglu(h, wg_ref, wu_ref, wd_ref, tf, lead=(), between=None):
    ff = wg_ref.shape[-1]
    n_dots = 3 * (ff // tf)
    tick = (lambda i: between(i, n_dots)) if between is not None else (lambda i: None)
    acc = jnp.zeros((h.shape[0], wd_ref.shape[-1]), F32)
    for c in range(ff // tf):
        cols = slice(c * tf, (c + 1) * tf)
        gate = jnp.dot(h, wg_ref[(*lead, slice(None), cols)], preferred_element_type=F32)
        tick(3 * c)
        up = jnp.dot(h, wu_ref[(*lead, slice(None), cols)], preferred_element_type=F32)
        tick(3 * c + 1)
        acc = acc + jnp.dot((_silu(gate) * up).astype(BF16), wd_ref[(*lead, cols, slice(None))],
                            preferred_element_type=F32)
        tick(3 * c + 2)
    return acc


def _post_dense_kernel(x_ref, m0_ref, m1_ref, m2_ref, m3_ref, g1_ref, sh2_ref, sc2_ref, g2_ref, ng_ref, wout_ref,
                       wg_ref, wu_ref, wd_ref, fg_ref, o_ref, *, final_norm, tf):
    x1 = _mix_residual(x_ref, (m0_ref, m1_ref, m2_ref, m3_ref), g1_ref, wout_ref)
    h = _modulate(_rms(x1) * ng_ref[...], sc2_ref[0], sh2_ref[0]).astype(BF16)
    x2 = x1 + _gate(_swiglu(h, wg_ref, wu_ref, wd_ref, tf), g2_ref[0])
    if final_norm:
        x2 = _rms(x2) * fg_ref[...]
    o_ref[0] = x2


def post_dense(x, mixes, g1, sh2, sc2, g2, ng, wout, wg, wu, wd, fg, tm, tf, final_norm):
    s, r, d = x.shape
    rm = g1.shape[1]
    ff = wg.shape[1]
    mod = pl.BlockSpec((1, rm, d), lambda b_, i: (b_, 0, 0))
    const = lambda shape: pl.BlockSpec(shape, lambda b_, i: (0,) * len(shape))
    resident = lambda shape: pl.BlockSpec(shape, lambda b_, i: (0,) * len(shape), pipeline_mode=pl.Buffered(1))
    mixspec = pl.BlockSpec((tm, 256), lambda b_, i: (i, b_))
    return pl.pallas_call(
        functools.partial(_post_dense_kernel, final_norm=final_norm, tf=tf),
        out_shape=jax.ShapeDtypeStruct((s, r, d), F32),
        grid=(s, r // tm),
        in_specs=[pl.BlockSpec((1, tm, d), lambda b_, i: (b_, i, 0)),
                  mixspec, mixspec, mixspec, mixspec,
                  mod, mod, mod, mod,
                  const((1, d)), resident((d, d)),
                  resident((d, ff)), resident((d, ff)), resident((ff, d)),
                  const((1, d))],
        out_specs=pl.BlockSpec((1, tm, d), lambda b_, i: (b_, i, 0)),
        compiler_params=_cparams(("arbitrary", "arbitrary")),
        name="post_dense",
    )(x, *mixes, g1, sh2, sc2, g2, ng, wout, wg, wu, wd, fg)


ROW_TILE = 8


def _store_row_tiles(ref, x, lead=()):
    rows = x.shape[0]
    for s in range(ROW_TILE):
        ref[(*lead, pl.ds(s, rows, stride=ROW_TILE), slice(None))] = x[:, s * LANE:(s + 1) * LANE]


def _load_row_tiles(ref, rows, lead=()):
    return jnp.concatenate([ref[(*lead, pl.ds(s, rows, stride=ROW_TILE), slice(None))] for s in range(ROW_TILE)],
                           axis=1)


def _route_kernel(x_ref, m0_ref, m1_ref, m2_ref, m3_ref, g1_ref, sh2_ref, sc2_ref, ng_ref, wout_ref, router_ref,
                  *rest):
    x1_ref, h2_ref, route_ref = rest[-3:]
    x1 = _mix_residual(x_ref, (m0_ref, m1_ref, m2_ref, m3_ref), g1_ref, wout_ref)
    x1_ref[0] = x1
    h = _modulate(_rms(x1) * ng_ref[...], sc2_ref[0], sh2_ref[0])
    _store_row_tiles(h2_ref, h)
    h_hi = h.astype(BF16)
    h_lo = (h - h_hi.astype(F32)).astype(BF16)
    w = router_ref[...]
    w_hi = w.astype(BF16)
    w_lo = (w - w_hi.astype(F32)).astype(BF16)
    logits = (jnp.dot(h_hi, w_hi, preferred_element_type=F32) + jnp.dot(h_lo, w_hi, preferred_element_type=F32)
              + jnp.dot(h_hi, w_lo, preferred_element_type=F32))
    lane = lax.broadcasted_iota(jnp.int32, logits.shape, 1).astype(F32)
    neg = jnp.float32(-jnp.inf)
    logits = jnp.where(lane < N_EXPERTS, logits, neg)
    m1 = jnp.max(logits, axis=1, keepdims=True)
    i1 = jnp.min(jnp.where(logits == m1, lane, float(LANE)), axis=1, keepdims=True)
    others = jnp.where(lane == i1, neg, logits)
    m2 = jnp.max(others, axis=1, keepdims=True)
    i2 = jnp.min(jnp.where(others == m2, lane, float(LANE)), axis=1, keepdims=True)
    e2 = jnp.exp(m2 - m1)
    den = 1.0 + e2
    route_ref[...] = (jnp.where(lane == 0.0, i1, 0.0) + jnp.where(lane == 1.0, i2, 0.0)
                      + jnp.where(lane == 2.0, 1.0 / den, 0.0) + jnp.where(lane == 3.0, e2 / den, 0.0))


def moe_route(x, mixes, g1, sh2, sc2, ng, wout, router, tm, row0, shared):
    s, r, d = x.shape
    rm = g1.shape[1]
    nt = r // tm
    blk0 = row0 // tm
    n_total = shared[1].shape[0]
    mod = pl.BlockSpec((1, rm, d), lambda b_, i: (b_, 0, 0))
    const = lambda shape: pl.BlockSpec(shape, lambda b_, i: (0,) * len(shape))
    mixspec = pl.BlockSpec((tm, 256), lambda b_, i: (i, b_))
    in_specs = [pl.BlockSpec((1, tm, d), lambda b_, i: (b_, i, 0)),
                mixspec, mixspec, mixspec, mixspec, mod, mod, mod,
                const((1, d)), const((d, d)), const((d, LANE))]
    args = [x, *mixes, g1, sh2, sc2, ng, wout, router]
    in_specs += [pl.BlockSpec(memory_space=pl.ANY), pl.BlockSpec(memory_space=pl.ANY)]
    aliases = {len(args): 1, len(args) + 1: 2}
    args += list(shared)
    return pl.pallas_call(
        _route_kernel,
        out_shape=[jax.ShapeDtypeStruct((s, r, d), F32),
                   jax.ShapeDtypeStruct((n_total * ROW_TILE, LANE), F32),
                   jax.ShapeDtypeStruct((n_total, LANE), F32)],
        grid=(s, nt),
        in_specs=in_specs,
        out_specs=[pl.BlockSpec((1, tm, d), lambda b_, i: (b_, i, 0)),
                   pl.BlockSpec((tm * ROW_TILE, LANE), lambda b_, i: (blk0 + b_ * nt + i, 0)),
                   pl.BlockSpec((tm, LANE), lambda b_, i: (blk0 + b_ * nt + i, 0))],
        input_output_aliases=aliases,
        compiler_params=_cparams(("arbitrary", "arbitrary")),
        name="moe_route",
    )(*args)


def _route_tables(route, tg, n_tiles):
    n_total = route.shape[0]
    flat_e = route[:, 0:2].astype(jnp.int32).reshape(-1)
    keys = jnp.concatenate([flat_e, jnp.full((tg,), N_EXPERTS, jnp.int32)])
    order = jnp.argsort(keys, stable=True).astype(jnp.int32)
    counts = jnp.sum(flat_e[:, None] == jnp.arange(N_EXPERTS, dtype=jnp.int32)[None, :], axis=0).astype(jnp.int32)
    tiles_per = (counts + tg - 1) // tg
    tile_end = jnp.cumsum(tiles_per)
    n_used = tile_end[-1]
    tile_id = jnp.arange(n_tiles, dtype=jnp.int32)
    tile_ok = tile_id < n_used
    tile_e = jnp.sum(jnp.minimum(tile_id, n_used - 1)[:, None] >= tile_end[None, :], axis=1).astype(jnp.int32)
    sort_start = jnp.cumsum(counts) - counts
    done = (tile_id - (tile_end - tiles_per)[tile_e]) * tg
    n_valid = jnp.where(tile_ok, jnp.clip(counts[tile_e] - done, 0, tg), 0).astype(jnp.int32)
    tile_start = jnp.where(tile_ok, sort_start[tile_e] + done, 0).astype(jnp.int32)
    real = order < 2 * n_total
    src = jnp.where(real, (order >> 1) * ROW_TILE, 0)
    dst = jnp.where(real, ((order & 1) * n_total + (order >> 1)) * ROW_TILE, 0)
    return tile_e, n_valid, tile_start, src, dst


DMA_UNROLL = 8


def _experts_kernel(te_ref, nv_ref, ts_ref, src_ref, dst_ref, h2_hbm, wg_ref, wu_ref, wd_ref, out_hbm,
                    xbuf, obuf, gsem, ssem, *, tg, n_tiles, tf):
    j = pl.program_id(0)
    slot = lax.rem(j, 2)
    other = 1 - slot
    ok = nv_ref[j] > 0

    def row_tile(buf, s_, r):
        start = r * ROW_TILE if isinstance(r, int) else pl.multiple_of(r * ROW_TILE, ROW_TILE)
        return buf.at[s_, pl.ds(start, ROW_TILE), :]

    def gather_row(tile, s_, r):
        row = pl.multiple_of(src_ref[ts_ref[tile] + r], ROW_TILE)
        pltpu.make_async_copy(h2_hbm.at[pl.ds(row, ROW_TILE), :], row_tile(xbuf, s_, r), gsem.at[s_]).start()

    def scatter_row(tile, s_, r):
        row = pl.multiple_of(dst_ref[ts_ref[tile] + r], ROW_TILE)
        pltpu.make_async_copy(row_tile(obuf, s_, r), out_hbm.at[pl.ds(row, ROW_TILE), :], ssem.at[s_]).start()

    def full_tile(issue_row, tile, s_):
        def body(r, c):
            issue_row(tile, s_, r)
            return c

        lax.fori_loop(0, tg, body, 0, unroll=DMA_UNROLL)

    def gather(tile, s_):
        full_tile(gather_row, tile, s_)

    def scatter(tile, s_):
        n = nv_ref[tile]

        @pl.when(n == tg)
        def _():
            full_tile(scatter_row, tile, s_)

        @pl.when(n < tg)
        def _():
            def body(r, c):
                scatter_row(tile, s_, r)
                return c

            lax.fori_loop(0, n, body, 0)

    def wait_all(buf, sem, s_):
        pltpu.make_async_copy(buf.at[s_], buf.at[s_], sem.at[s_]).wait()

    def wait_scatter(tile, s_):
        n = nv_ref[tile]

        @pl.when(n == tg)
        def _():
            wait_all(obuf, ssem, s_)

        @pl.when(n < tg)
        def _():
            def body(r, c):
                pltpu.make_async_copy(obuf.at[s_, pl.ds(0, ROW_TILE), :], out_hbm.at[pl.ds(0, ROW_TILE), :],
                                      ssem.at[s_]).wait()
                return c

            lax.fori_loop(0, n, body, 0)

    @pl.when(jnp.logical_and(j == 0, ok))
    def _():
        gather(0, 0)

    @pl.when(jnp.logical_or(jnp.logical_and(j == 0, ok), nv_ref[jnp.maximum(j - 1, 0)] * jnp.minimum(j, 1) > 0))
    def _():
        wait_all(xbuf, gsem, slot)

    @pl.when(j >= 2)
    def _():
        wait_scatter(j - 2, slot)

    nxt = jnp.minimum(j + 1, n_tiles - 1)

    @pl.when(ok)
    def _():
        x = _load_row_tiles(xbuf, tg, lead=(slot,)).astype(BF16)

        def gather_some(i, n):
            for r in range(i * tg // n, (i + 1) * tg // n):
                gather_row(nxt, other, r)

        y = _swiglu(x, wg_ref, wu_ref, wd_ref, tf, lead=(0,), between=gather_some)
        _store_row_tiles(obuf, y, lead=(slot,))
        scatter(j, slot)

    @pl.when(j == n_tiles - 1)
    def _():
        @pl.when(ok)
        def _():
            wait_all(xbuf, gsem, other)

        wait_scatter(j - 1, other)
        wait_scatter(j, slot)


def moe_experts(h2, tables, wg, wu, wd, tg, n_tiles, tf):
    n_exp, d, ff = wg.shape
    assert d == ROW_TILE * LANE and n_tiles >= 2
    tile_e, n_valid, tile_start, src, dst = tables
    wspec = lambda shape: pl.BlockSpec(shape, lambda j, te, *_: (te[j], 0, 0))
    grid_spec = pltpu.PrefetchScalarGridSpec(
        num_scalar_prefetch=5,
        grid=(n_tiles,),
        in_specs=[pl.BlockSpec(memory_space=pl.ANY), wspec((1, d, ff)), wspec((1, d, ff)), wspec((1, ff, d))],
        out_specs=pl.BlockSpec(memory_space=pl.ANY),
        scratch_shapes=[pltpu.VMEM((2, tg * ROW_TILE, LANE), F32), pltpu.VMEM((2, tg * ROW_TILE, LANE), F32),
                        pltpu.SemaphoreType.DMA((2,)), pltpu.SemaphoreType.DMA((2,))])
    return pl.pallas_call(
        functools.partial(_experts_kernel, tg=tg, n_tiles=n_tiles, tf=tf),
        out_shape=jax.ShapeDtypeStruct((2 * h2.shape[0], LANE), F32),
        grid_spec=grid_spec,
        compiler_params=_cparams(("arbitrary",)),
        name="moe_experts",
    )(tile_e, n_valid, tile_start, src, dst, h2, wg, wu, wd)


def _combine_kernel(x1_ref, y0_ref, y1_ref, route_ref, g2_ref, fg_ref, o_ref, *, final_norm):
    r = route_ref[...]
    rows = r.shape[0]
    f = r[:, 2:3] * _load_row_tiles(y0_ref, rows, lead=(0,)) + r[:, 3:4] * _load_row_tiles(y1_ref, rows, lead=(0,))
    x2 = x1_ref[0] + _gate(f, g2_ref[0])
    if final_norm:
        x2 = _rms(x2) * fg_ref[...]
    o_ref[0] = x2


def moe_combine(x1, y, route, g2, fg, tm, row0, final_norm):
    s, r, d = x1.shape
    rm = g2.shape[1]
    nt = r // tm
    blk0 = row0 // tm
    return pl.pallas_call(
        functools.partial(_combine_kernel, final_norm=final_norm),
        out_shape=jax.ShapeDtypeStruct((s, r, d), F32),
        grid=(s, nt),
        in_specs=[pl.BlockSpec((1, tm, d), lambda b_, i: (b_, i, 0)),
                  pl.BlockSpec((1, tm * ROW_TILE, LANE), lambda b_, i: (0, blk0 + b_ * nt + i, 0)),
                  pl.BlockSpec((1, tm * ROW_TILE, LANE), lambda b_, i: (1, blk0 + b_ * nt + i, 0)),
                  pl.BlockSpec((tm, LANE), lambda b_, i: (blk0 + b_ * nt + i, 0)),
                  pl.BlockSpec((1, rm, d), lambda b_, i: (b_, 0, 0)),
                  pl.BlockSpec((1, d), lambda b_, i: (0, 0))],
        out_specs=pl.BlockSpec((1, tm, d), lambda b_, i: (b_, i, 0)),
        compiler_params=_cparams(("arbitrary", "arbitrary")),
        name="moe_combine",
    )(x1, y, y, route, g2, fg)


def _reorder_w_in(w_in, b_in):
    cut = PW_S5 + 128 + 128 + 256 + 256 + GLA_GATE_RANK
    pad = LANE - GLA_GATE_RANK
    w = jnp.concatenate([w_in[:, :cut], jnp.zeros((w_in.shape[0], pad), w_in.dtype), w_in[:, cut:]], axis=1)
    b = jnp.concatenate([b_in[:cut], jnp.zeros((pad,), b_in.dtype), b_in[cut:]])
    return w.astype(BF16), b.reshape(1, PW_TOTAL)


def _row(a):
    return a.reshape(1, -1)


class _Branch:
    def __init__(self, nseq, nb, t_len, seq_form, tm_pre, tm_post, tc, tt_seq, row0):
        self.nseq, self.nb, self.t_len, self.seq_form = nseq, nb, t_len, seq_form
        self.tm_pre, self.tm_post, self.tc, self.tt_seq, self.row0 = tm_pre, tm_post, tc, tt_seq, row0


def _layer_params(W, i):
    row = _row
    p = {}
    p['w_in'], p['b_in'] = _reorder_w_in(W['w_in'][i], W['b_in'][i])
    p['norm_g'] = row(W['norm_mix_g'][i])
    mg = W['merge_g'][i]
    p['mg'] = [row(mg[k * 256:(k + 1) * 256]) for k in range(4)]
    ab_re, ab_im, bb_re, bb_im = s5_discretise(W['s5_a_re'][i], W['s5_a_im'][i], W['s5_log_dt'][i],
                                               W['s5_b_re'][i], W['s5_b_im'][i])
    p['s5'] = (jnp.concatenate([_block_diag_in(bb_re), _block_diag_in(bb_im)], axis=1).astype(BF16),
               _block_diag_out(W['s5_c_re'][i]).astype(BF16), _block_diag_out(W['s5_c_im'][i]).astype(BF16),
               row(ab_re), row(ab_im), row(W['s5_d'][i]), W['s5_w_glu'][i].astype(BF16), row(W['s5_b_glu'][i]))
    wg2 = jnp.zeros((LANE, GLA_KEY_WIDTH), F32).at[:GLA_GATE_RANK].set(W['gla_w_gate2'][i]).astype(BF16)
    p['gla'] = (wg2, row(W['gla_b_gate2'][i]), row(W['gla_onorm_g'][i]))
    p['conv'] = (W['conv_w_dw'][i], row(W['conv_b_dw'][i]), row(W['conv_ln_g'][i]), row(W['conv_ln_b'][i]),
                 W['conv_w_pw'][i].astype(BF16), row(W['conv_b_pw'][i]))
    p['gmlp_ln'] = (row(W['gmlp_ln_g'][i]), row(W['gmlp_ln_b'][i]))
    p['gmlp_ws'], p['gmlp_bs'] = W['gmlp_w_s'][i], W['gmlp_b_s'][i]
    return p


def _mixers(x, mods, states, p, i, br, out):
    nseq, nb, t_len, seq_form = br.nseq, br.nb, br.t_len, br.seq_form
    s5_re0, s5_im0, gla0, conv0 = states
    new_re, new_im, new_gla, new_conv, new_v = out
    mg = p['mg']
    p_s5, p_gla, p_conv, p_mlp = pre_mixer(x, mods[0], mods[1], p['norm_g'], p['w_in'], p['b_in'], br.tm_pre)

    h0 = jnp.concatenate([s5_re0[i].reshape(nb, S5_LANES), s5_im0[i].reshape(nb, S5_LANES)], axis=1)
    o_s5, h_t = s5_mixer(p_s5, h0, *p['s5'], mg[0], nb, br.tc, nseq)
    new_re.append(h_t[:, :S5_LANES].reshape(nb, S5_GROUPS, S5_STATE))
    new_im.append(h_t[:, S5_LANES:].reshape(nb, S5_GROUPS, S5_STATE))

    if seq_form:
        eye = jnp.eye(GLA_HEADS, dtype=F32)
        s0 = jnp.einsum('bhkv,hg->bhvgk', gla0[i], eye).reshape(nseq, GLA_WIDTH, GLA_KEY_WIDTH)
        o_gla, s_t = gla_seq(p_gla, nseq, s0, *p['gla'], mg[1], br.tt_seq)
        s5d = s_t.reshape(nseq, GLA_HEADS, GLA_DV, GLA_HEADS, GLA_DK)
        new_gla.append(jnp.stack([jnp.swapaxes(s5d[:, h, :, h, :], 1, 2) for h in range(GLA_HEADS)], axis=1))
    else:
        ek, ev = _gla_expanders()
        o_gla, s_t = gla_recurrent(p_gla, gla0[i].reshape(nb, GLA_STATE_LANES), ek, ev, *p['gla'], mg[1], nb, t_len)
        new_gla.append(s_t.reshape(nb, GLA_HEADS, GLA_DK, GLA_DV))

    c0 = jnp.transpose(conv0[i], (1, 0, 2)).reshape(CONV_HIST * nb, CONV_DIM)
    o_conv, buf = conv_mixer(p_conv, c0, *p['conv'], mg[2], nb, br.tc, nseq)
    new_conv.append(jnp.transpose(buf.reshape(CONV_HIST, nb, CONV_DIM), (1, 0, 2)))

    ws, bs = p['gmlp_ws'], p['gmlp_bs']
    if seq_form:
        wcat = jnp.transpose(ws, (1, 0, 2)).reshape(GMLP_CHUNK, GMLP_HEADS * GMLP_CHUNK)
        bias = jnp.repeat(bs.T, GMLP_HEAD_DIM, axis=1)
        o_mlp = gmlp_seq(p_mlp, nseq, *p['gmlp_ln'], wcat, bias, mg[3], min(4 * br.tt_seq, t_len))
        new_v.append(None)
    else:
        tri = jnp.tril(jnp.ones((t_len, t_len), F32))
        wrow = jnp.repeat(jnp.transpose(ws[:, :t_len, :t_len] * tri[None], (1, 2, 0)).reshape(t_len * t_len, GMLP_HEADS),
                          GMLP_HEAD_DIM, axis=1)
        brow = jnp.repeat(bs[:, :t_len].T, GMLP_HEAD_DIM, axis=1)
        o_mlp, vn = gmlp_short(p_mlp, *p['gmlp_ln'], wrow, brow, mg[3], nb, t_len)
        new_v.append(vn)

    return [o_s5, o_gla, o_conv, o_mlp]


FF_TILE = 1408
EXPERT_FF_TILE = 256
EXPERT_ROWS = 512


def _channel_mixer(xs, mixes, mods, W, i, branches, last):
    ng, wout, fg = _row(W['norm_ffn_g'][i]), W['w_out'][i].astype(BF16), _row(W['final_norm_g'])
    j = i // 2
    if i % 2 == 0:
        wg, wu, wd = (W['ffn_w_gate'][j].astype(BF16), W['ffn_w_up'][j].astype(BF16),
                      W['ffn_w_down'][j].astype(BF16))
        return [post_dense(x, mx, m[2], m[3], m[4], m[5], ng, wout, wg, wu, wd, fg, br.tm_post, FF_TILE, last)
                for x, mx, m, br in zip(xs, mixes, mods, branches)]
    n_total = sum(x.shape[0] * x.shape[1] for x in xs)
    tg = EXPERT_ROWS
    n_tiles = 2 * n_total // tg + N_EXPERTS
    router = jnp.zeros((D_MODEL, LANE), F32).at[:, :N_EXPERTS].set(W['moe_router'][j])
    h2, route = jnp.zeros((n_total * ROW_TILE, LANE), F32), jnp.zeros((n_total, LANE), F32)
    x1s = []
    for x, mx, m, br in zip(xs, mixes, mods, branches):
        x1, h2, route = moe_route(x, mx, m[2], m[3], m[4], ng, wout, router, br.tm_post, br.row0, (h2, route))
        x1s.append(x1)
    tables = _route_tables(route, tg, n_tiles)
    y = moe_experts(h2, tables, W['moe_w_gate'][j].astype(BF16), W['moe_w_up'][j].astype(BF16),
                    W['moe_w_down'][j].astype(BF16), tg, n_tiles, EXPERT_FF_TILE)
    y = y.reshape(2, n_total * ROW_TILE, LANE)
    return [moe_combine(x1, y, route, m[5], fg, br.tm_post, br.row0, last)
            for x1, m, br in zip(x1s, mods, branches)]


def kernel(x_prompt, x_sample, c_prompt, c_sample, state_s5_re, state_s5_im, state_gla, cache_conv, ada_w, ada_b, norm_mix_g, norm_ffn_g, w_in, b_in, s5_a_re, s5_a_im, s5_log_dt, s5_b_re, s5_b_im, s5_c_re, s5_c_im, s5_d, s5_w_glu, s5_b_glu, gla_w_gate2, gla_b_gate2, gla_onorm_g, conv_w_dw, conv_b_dw, conv_ln_g, conv_ln_b, conv_w_pw, conv_b_pw, gmlp_ln_g, gmlp_ln_b, gmlp_w_s, gmlp_b_s, merge_g, w_out, ffn_w_gate, ffn_w_up, ffn_w_down, moe_router, moe_w_gate, moe_w_up, moe_w_down, final_norm_g):
    W = dict(norm_mix_g=norm_mix_g, norm_ffn_g=norm_ffn_g, w_in=w_in, b_in=b_in, s5_a_re=s5_a_re, s5_a_im=s5_a_im,
             s5_log_dt=s5_log_dt, s5_b_re=s5_b_re, s5_b_im=s5_b_im, s5_c_re=s5_c_re, s5_c_im=s5_c_im, s5_d=s5_d,
             s5_w_glu=s5_w_glu, s5_b_glu=s5_b_glu, gla_w_gate2=gla_w_gate2, gla_b_gate2=gla_b_gate2,
             gla_onorm_g=gla_onorm_g, conv_w_dw=conv_w_dw, conv_b_dw=conv_b_dw, conv_ln_g=conv_ln_g,
             conv_ln_b=conv_ln_b, conv_w_pw=conv_w_pw, conv_b_pw=conv_b_pw, gmlp_ln_g=gmlp_ln_g,
             gmlp_ln_b=gmlp_ln_b, gmlp_w_s=gmlp_w_s, gmlp_b_s=gmlp_b_s, merge_g=merge_g, w_out=w_out,
             ffn_w_gate=ffn_w_gate, ffn_w_up=ffn_w_up, ffn_w_down=ffn_w_down, moe_router=moe_router,
             moe_w_gate=moe_w_gate, moe_w_up=moe_w_up, moe_w_down=moe_w_down, final_norm_g=final_norm_g)
    depth = w_in.shape[0]
    bp, tp, d = x_prompt.shape
    bs, ts, _ = x_sample.shape

    m = ada_modulation(jnp.concatenate([c_prompt, c_sample], axis=0), ada_w, ada_b)
    mods_p = [[m[i, :bp, k * d:(k + 1) * d].reshape(bp, 1, d) for k in range(6)] for i in range(depth)]
    mods_s = [[m[i, bp:, k * d:(k + 1) * d].reshape(1, bs, d) for k in range(6)] for i in range(depth)]

    z_re = jnp.zeros((depth, bp, S5_GROUPS, S5_STATE), F32)
    z_gla = jnp.zeros((depth, bp, GLA_HEADS, GLA_DK, GLA_DV), F32)
    z_conv = jnp.zeros((depth, bp, CONV_HIST, CONV_DIM), x_prompt.dtype)
    states = [(z_re, z_re, z_gla, z_conv), (state_s5_re, state_s5_im, state_gla, cache_conv)]
    branches = [_Branch(nseq=bp, nb=bp, t_len=tp, seq_form=True, tm_pre=min(512, tp), tm_post=min(512, tp),
                        tc=min(256, tp), tt_seq=min(512, tp), row0=0),
                _Branch(nseq=1, nb=bs, t_len=ts, seq_form=False, tm_pre=ts * bs, tm_post=min(512, ts * bs),
                        tc=ts, tt_seq=None, row0=bp * tp)]
    xs = [x_prompt, jnp.transpose(x_sample, (1, 0, 2)).reshape(1, ts * bs, d)]
    outs = [([], [], [], [], []), ([], [], [], [], [])]
    for i in range(depth):
        mods = [mods_p[i], mods_s[i]]
        params = _layer_params(W, i)
        mixes = [_mixers(x, m, st_, params, i, br, o)
                 for x, m, st_, br, o in zip(xs, mods, states, branches, outs)]
        xs = _channel_mixer(xs, mixes, mods, W, i, branches, i == depth - 1)

    y_p = xs[0]
    y_s = jnp.transpose(xs[1].reshape(ts, bs, d), (1, 0, 2))
    p_re, p_im, p_gla, p_conv, _ = outs[0]
    s_re, s_im, s_gla, s_conv, s_v = outs[1]
    s_v = [jnp.transpose(v.reshape(ts, bs, GMLP_WIDTH), (1, 0, 2)) for v in s_v]
    st = jnp.stack
    return (y_p, y_s, st(p_re), st(p_im), st(p_gla), st(p_conv),
            st(s_re), st(s_im), st(s_gla), st(s_conv), st(s_v))
```

```python
import functools
import math

import jax
import jax.numpy as jnp
from jax import lax
from jax.experimental import pallas as pl
from jax.experimental.pallas import tpu as pltpu

D_MODEL = 1024
S5_WIDTH = 256
S5_GROUP = 16
S5_GROUPS = 16
S5_STATE = 64
S5_LANES = S5_GROUPS * S5_STATE
GLA_HEADS = 4
GLA_DV = 64
GLA_DK = 32
GLA_WIDTH = 256
GLA_KEY_WIDTH = 128
GLA_GATE_RANK = 16
GLA_TAU = 16.0
GLA_CHUNK = 64
GLA_STATE_LANES = GLA_HEADS * GLA_DK * GLA_DV
CONV_DIM = 256
CONV_WIDTH = 31
CONV_HIST = CONV_WIDTH - 1
GMLP_WIDTH = 256
GMLP_HEADS = 4
GMLP_HEAD_DIM = 64
GMLP_CHUNK = 128
D_FF = 2816
N_EXPERTS = 8
EPS = 1e-6

LANE = 128
PW_S5 = 256
PW_GLA = 128 + 128 + 256 + 256 + LANE
PW_CONV = 512
PW_MLP = 512
PW_TOTAL = PW_S5 + PW_GLA + PW_CONV + PW_MLP
VMEM_LIMIT = 56 * 1024 * 1024

F32 = jnp.float32
BF16 = jnp.bfloat16
HI = lax.Precision.HIGHEST


def _cparams(sem):
    return pltpu.CompilerParams(dimension_semantics=sem, vmem_limit_bytes=VMEM_LIMIT)


def _rms(x):
    return x * lax.rsqrt(jnp.mean(x * x, axis=-1, keepdims=True) + EPS)


def _layernorm(x, g, b):
    mu = jnp.mean(x, axis=-1, keepdims=True)
    xc = x - mu
    var = jnp.mean(xc * xc, axis=-1, keepdims=True)
    return xc * lax.rsqrt(var + EPS) * g + b


def _silu(x):
    return x * jax.nn.sigmoid(x)


def _gelu_tanh(x):
    return 0.5 * x * (1.0 + jnp.tanh(math.sqrt(2.0 / math.pi) * (x + 0.044715 * (x * x * x))))


def _log_sigmoid(x):
    return jnp.minimum(x, 0.0) - jnp.log(1.0 + jnp.exp(-jnp.abs(x)))


def _same_block(shape, row_block, col_block):
    r = lax.broadcasted_iota(jnp.int32, shape, 0) >> (row_block.bit_length() - 1)
    c = lax.broadcasted_iota(jnp.int32, shape, 1) >> (col_block.bit_length() - 1)
    return r == c


def _modulate(y, sc, sh):
    rm = sc.shape[0]
    if rm == 1:
        return y * (1.0 + sc) + sh
    rows, d = y.shape
    y3 = y.reshape(rows // rm, rm, d)
    return (y3 * (1.0 + sc)[None] + sh[None]).reshape(rows, d)


def _gate(y, g):
    rm = g.shape[0]
    if rm == 1:
        return y * g
    rows, d = y.shape
    return (y.reshape(rows // rm, rm, d) * g[None]).reshape(rows, d)


def _ada_kernel(c_ref, w_ref, b_ref, o_ref):
    c = c_ref[...]
    s = _silu(c).astype(BF16)
    o_ref[0] = jnp.dot(s, w_ref[0].astype(BF16), preferred_element_type=F32) + b_ref[0]


def ada_modulation(c_all, ada_w, ada_b):
    depth, d, n6 = ada_w.shape
    rows = c_all.shape[0]
    tn = 1536
    return pl.pallas_call(
        _ada_kernel,
        out_shape=jax.ShapeDtypeStruct((depth, rows, n6), F32),
        grid=(depth, n6 // tn),
        in_specs=[pl.BlockSpec((rows, d), lambda l, j: (0, 0)),
                  pl.BlockSpec((1, d, tn), lambda l, j: (l, 0, j)),
                  pl.BlockSpec((1, 1, tn), lambda l, j: (l, 0, j))],
        out_specs=pl.BlockSpec((1, rows, tn), lambda l, j: (l, 0, j)),
        compiler_params=_cparams(("arbitrary", "arbitrary")),
        name="ada_modulation",
    )(c_all, ada_w, ada_b.reshape(depth, 1, n6))


def _pre_kernel(x_ref, sh_ref, sc_ref, g_ref, w_ref, b_ref, o_s5, o_gla, o_conv, o_mlp):
    x = x_ref[0]
    y = _modulate(_rms(x) * g_ref[...], sc_ref[0], sh_ref[0])
    p = jnp.dot(y.astype(BF16), w_ref[...], preferred_element_type=F32) + b_ref[...]
    o_s5[...] = p[:, 0:PW_S5]
    o_gla[...] = p[:, PW_S5:PW_S5 + PW_GLA]
    o_conv[...] = p[:, PW_S5 + PW_GLA:PW_S5 + PW_GLA + PW_CONV]
    o_mlp[...] = p[:, PW_S5 + PW_GLA + PW_CONV:PW_TOTAL]


def pre_mixer(x, sh, sc, g, w, b, tm):
    s, r, d = x.shape
    rm = sh.shape[1]
    widths = (PW_S5, PW_GLA, PW_CONV, PW_MLP)
    return pl.pallas_call(
        _pre_kernel,
        out_shape=[jax.ShapeDtypeStruct((r, s * w_), F32) for w_ in widths],
        grid=(s, r // tm),
        in_specs=[pl.BlockSpec((1, tm, d), lambda b_, i: (b_, i, 0)),
                  pl.BlockSpec((1, rm, d), lambda b_, i: (b_, 0, 0)),
                  pl.BlockSpec((1, rm, d), lambda b_, i: (b_, 0, 0)),
                  pl.BlockSpec((1, d), lambda b_, i: (0, 0)),
                  pl.BlockSpec((d, PW_TOTAL), lambda b_, i: (0, 0)),
                  pl.BlockSpec((1, PW_TOTAL), lambda b_, i: (0, 0))],
        out_specs=[pl.BlockSpec((tm, w_), lambda b_, i: (i, b_)) for w_ in widths],
        compiler_params=_cparams(("arbitrary", "arbitrary")),
        name="pre_mixer",
    )(x, sh, sc, g, w, b)


def _s5_disc_kernel(lr_ref, li_ref, ldt_ref, br_ref, bi_ref, abr_ref, abi_ref, bbr_ref, bbi_ref):
    lr = lr_ref[...]
    li = li_ref[...]
    dt = jnp.exp(ldt_ref[...])
    mag = jnp.exp(lr * dt)
    ang = li * dt
    ab_re = mag * jnp.cos(ang)
    ab_im = mag * jnp.sin(ang)
    den = lr * lr + li * li
    nr = ab_re - 1.0
    f_re = (nr * lr + ab_im * li) / den
    f_im = (ab_im * lr - nr * li) / den
    br = br_ref[...]
    bi = bi_ref[...]
    abr_ref[...] = ab_re
    abi_ref[...] = ab_im
    bbr_ref[...] = f_re * br - f_im * bi
    bbi_ref[...] = f_re * bi + f_im * br


def s5_discretise(a_re, a_im, log_dt, b_re, b_im):
    n = b_re.shape[-1]
    gp = a_re.size
    bc = lambda a: jnp.broadcast_to(a.reshape(gp, 1), (gp, n))
    ldt = jnp.broadcast_to(log_dt[:, None], a_re.shape)
    outs = pl.pallas_call(
        _s5_disc_kernel,
        out_shape=[jax.ShapeDtypeStruct((gp, n), F32)] * 4,
        name="s5_discretise",
    )(bc(a_re), bc(a_im), bc(ldt), b_re.reshape(gp, n), b_im.reshape(gp, n))
    ab_re, ab_im, bb_re, bb_im = outs
    return ab_re[:, 0], ab_im[:, 0], bb_re, bb_im


def _block_diag_in(bb):
    g, p, n = S5_GROUPS, S5_STATE, S5_GROUP
    b3 = bb.reshape(g, p, n)
    eye = jnp.eye(g, dtype=bb.dtype)
    return jnp.einsum('gpn,gh->gnhp', b3, eye).reshape(g * n, g * p)


def _block_diag_out(c):
    g, p, n = S5_GROUPS, S5_STATE, S5_GROUP
    eye = jnp.eye(g, dtype=c.dtype)
    return jnp.einsum('gnp,gh->gphn', c, eye).reshape(g * p, g * n)


def _to_time_major(x_ref, cols, tm_ref, row0, nsl):
    rt = x_ref.shape[0]
    w = x_ref.shape[1] // nsl
    start, width = cols
    for l in range(nsl):
        for h in range(width // LANE):
            c0 = l * w + start + h * LANE
            tm_ref[h, pl.ds(row0 + l, rt, stride=nsl), :] = x_ref[:, c0:c0 + LANE]


def _from_time_major(tm_ref, o_ref, nsl):
    rt = o_ref.shape[0]
    nh = tm_ref.shape[0]
    for l in range(nsl):
        piece = jnp.concatenate([tm_ref[h, pl.ds(l, rt, stride=nsl), :] for h in range(nh)], axis=1)
        o_ref[:, l * nh * LANE:(l + 1) * nh * LANE] = piece.astype(o_ref.dtype)


def _lane_tiles(tm_ref, rows=slice(None)):
    return jnp.concatenate([tm_ref[h, rows, :] for h in range(tm_ref.shape[0])], axis=1)


def _set_lane_tiles(tm_ref, x):
    for h in range(tm_ref.shape[0]):
        tm_ref[h] = x[:, h * LANE:(h + 1) * LANE]


def _s5_kernel(u_ref, h0_ref, bblk_ref, cre_ref, cim_ref, ar_ref, ai_ref, d_ref, wglu_ref, bglu_ref, mg_ref,
               o_ref, hT_ref, xs_ref, hs_ref, tm_ref, *, nb, tc, nsl):
    i = pl.program_id(0)

    @pl.when(i == 0)
    def _():
        hs_ref[...] = h0_ref[...]

    _to_time_major(u_ref, (0, S5_WIDTH), tm_ref, 0, nsl)
    u = _lane_tiles(tm_ref)
    xs_ref[...] = jnp.dot(u.astype(BF16), bblk_ref[...], preferred_element_type=F32)
    ar = jnp.broadcast_to(ar_ref[...], (nb, S5_LANES))
    ai = jnp.broadcast_to(ai_ref[...], (nb, S5_LANES))

    def step(t, carry):
        hr, hi = carry
        row = pl.multiple_of(t * nb, nb)
        xr = xs_ref[pl.ds(row, nb), 0:S5_LANES]
        xi = xs_ref[pl.ds(row, nb), S5_LANES:2 * S5_LANES]
        nr = ar * hr - ai * hi + xr
        ni = ar * hi + ai * hr + xi
        xs_ref[pl.ds(row, nb), 0:S5_LANES] = nr
        xs_ref[pl.ds(row, nb), S5_LANES:2 * S5_LANES] = ni
        return nr, ni

    hr, hi = lax.fori_loop(0, tc, step, (hs_ref[:, 0:S5_LANES], hs_ref[:, S5_LANES:2 * S5_LANES]),
                           unroll=True if tc <= 8 else 4)
    hs_ref[:, 0:S5_LANES] = hr
    hs_ref[:, S5_LANES:2 * S5_LANES] = hi

    y = (jnp.dot(xs_ref[:, 0:S5_LANES].astype(BF16), cre_ref[...], preferred_element_type=F32)
         - jnp.dot(xs_ref[:, S5_LANES:2 * S5_LANES].astype(BF16), cim_ref[...], preferred_element_type=F32))
    y = y + d_ref[...] * u
    y = _gelu_tanh(y)
    y = y * jax.nn.sigmoid(jnp.dot(y.astype(BF16), wglu_ref[...], preferred_element_type=F32) + bglu_ref[...])
    _set_lane_tiles(tm_ref, _rms(y) * mg_ref[...])
    _from_time_major(tm_ref, o_ref, nsl)

    @pl.when(i == pl.num_programs(0) - 1)
    def _():
        hT_ref[...] = hs_ref[...]


def s5_mixer(u, h0, bblk, cre, cim, ar, ai, d, wglu, bglu, mg, nb, tc, nsl):
    rows = u.shape[0]
    rc = nb * tc
    rt = rc // nsl
    full = lambda shape: pl.BlockSpec(shape, lambda i: (0,) * len(shape))
    return pl.pallas_call(
        functools.partial(_s5_kernel, nb=nb, tc=tc, nsl=nsl),
        out_shape=[jax.ShapeDtypeStruct((rows, nsl * S5_WIDTH), BF16),
                   jax.ShapeDtypeStruct((nb, 2 * S5_LANES), F32)],
        grid=(rows // rt,),
        in_specs=[pl.BlockSpec((rt, nsl * S5_WIDTH), lambda i: (i, 0)),
                  full((nb, 2 * S5_LANES)),
                  full((S5_WIDTH, 2 * S5_LANES)),
                  full((S5_LANES, S5_WIDTH)), full((S5_LANES, S5_WIDTH)),
                  full((1, S5_LANES)), full((1, S5_LANES)),
                  full((1, S5_WIDTH)), full((S5_WIDTH, S5_WIDTH)), full((1, S5_WIDTH)), full((1, S5_WIDTH))],
        out_specs=[pl.BlockSpec((rt, nsl * S5_WIDTH), lambda i: (i, 0)),
                   full((nb, 2 * S5_LANES))],
        scratch_shapes=[pltpu.VMEM((rc, 2 * S5_LANES), F32), pltpu.VMEM((nb, 2 * S5_LANES), F32),
                        pltpu.VMEM((S5_WIDTH // LANE, rc, LANE), F32)],
        compiler_params=_cparams(("arbitrary",)),
        name="s5_mixer",
    )(u, h0, bblk, cre, cim, ar, ai, d, wglu, bglu, mg)


CONV_ROWS = 64


def _conv_kernel(ag_ref, c0_ref, wdw_ref, bdw_ref, lng_ref, lnb_ref, wpw_ref, bpw_ref, mg_ref,
                 o_ref, buf_ref, zc_ref, y_ref, *, nb, tc, nsl):
    i = pl.program_id(0)
    hist = CONV_HIST * nb
    rc = nb * tc
    rt = rc // nsl
    n_lt = CONV_DIM // LANE

    @pl.when(i == 0)
    def _():
        for h in range(n_lt):
            zc_ref[h, 0:hist, :] = c0_ref[:, h * LANE:(h + 1) * LANE]

    @pl.when(i > 0)
    def _():
        for h in range(n_lt):
            zc_ref[h, 0:hist, :] = zc_ref[h, rc:rc + hist, :]

    w_seq = 2 * CONV_DIM
    for l in range(nsl):
        a = ag_ref[:, l * w_seq:l * w_seq + CONV_DIM]
        g = ag_ref[:, l * w_seq + CONV_DIM:(l + 1) * w_seq]
        z = a * jax.nn.sigmoid(g)
        for h in range(n_lt):
            zc_ref[h, pl.ds(hist + l, rt, stride=nsl), :] = z[:, h * LANE:(h + 1) * LANE]

    w = wdw_ref[...]

    def tile(j, carry):
        r0 = pl.multiple_of(j * CONV_ROWS, CONV_ROWS)
        for h in range(n_lt):
            acc = jnp.zeros((CONV_ROWS, LANE), F32)
            for k in range(CONV_WIDTH):
                acc = acc + w[k:k + 1, h * LANE:(h + 1) * LANE] * zc_ref[h, pl.ds(r0 + k * nb, CONV_ROWS), :]
            y_ref[h, pl.ds(r0, CONV_ROWS), :] = acc
        return carry

    lax.fori_loop(0, rc // CONV_ROWS, tile, 0)
    y = _lane_tiles(y_ref) + bdw_ref[...]
    y = _silu(_layernorm(y, lng_ref[...], lnb_ref[...]))
    y = jnp.dot(y.astype(BF16), wpw_ref[...], preferred_element_type=F32) + bpw_ref[...]
    _set_lane_tiles(y_ref, _rms(y) * mg_ref[...])
    _from_time_major(y_ref, o_ref, nsl)

    @pl.when(i == pl.num_programs(0) - 1)
    def _():
        buf_ref[...] = _lane_tiles(zc_ref, slice(rc, rc + hist))


def conv_mixer(ag, c0, wdw, bdw, lng, lnb, wpw, bpw, mg, nb, tc, nsl):
    rows = ag.shape[0]
    rc = nb * tc
    rt = rc // nsl
    hist = CONV_HIST * nb
    assert rows == rt or tc >= CONV_HIST
    n_lt = CONV_DIM // LANE
    full = lambda shape: pl.BlockSpec(shape, lambda i: (0,) * len(shape))
    return pl.pallas_call(
        functools.partial(_conv_kernel, nb=nb, tc=tc, nsl=nsl),
        out_shape=[jax.ShapeDtypeStruct((rows, nsl * CONV_DIM), BF16),
                   jax.ShapeDtypeStruct((hist, CONV_DIM), F32)],
        grid=(rows // rt,),
        in_specs=[pl.BlockSpec((rt, nsl * 2 * CONV_DIM), lambda i: (i, 0)),
                  full((hist, CONV_DIM)), full((CONV_WIDTH, CONV_DIM)),
                  full((1, CONV_DIM)), full((1, CONV_DIM)), full((1, CONV_DIM)),
                  full((CONV_DIM, CONV_DIM)), full((1, CONV_DIM)), full((1, CONV_DIM))],
        out_specs=[pl.BlockSpec((rt, nsl * CONV_DIM), lambda i: (i, 0)), full((hist, CONV_DIM))],
        scratch_shapes=[pltpu.VMEM((n_lt, hist + rc, LANE), F32), pltpu.VMEM((n_lt, rc, LANE), F32)],
        compiler_params=_cparams(("arbitrary",)),
        name="conv_mixer",
    )(ag, c0, wdw, bdw, lng, lnb, wpw, bpw, mg)


def _gmlp_seq_kernel(uv_ref, lng_ref, lnb_ref, wcat_ref, bias_ref, mg_ref, o_ref, *, tt):
    n_chunks = tt // GMLP_CHUNK
    kc = GMLP_HEADS * GMLP_CHUNK
    rowi = lax.broadcasted_iota(jnp.int32, (GMLP_CHUNK, kc), 0)
    coli = lax.broadcasted_iota(jnp.int32, (GMLP_CHUNK, kc), 1)
    wcat = jnp.where((coli & (GMLP_CHUNK - 1)) <= rowi, wcat_ref[...], 0.0).astype(BF16)
    sel = _same_block((kc, GMLP_WIDTH), GMLP_CHUNK, GMLP_HEAD_DIM)
    for c in range(n_chunks):
        rows = slice(c * GMLP_CHUNK, (c + 1) * GMLP_CHUNK)
        u = uv_ref[rows, 0:GMLP_WIDTH]
        v = uv_ref[rows, GMLP_WIDTH:2 * GMLP_WIDTH]
        vn = _layernorm(v, lng_ref[...], lnb_ref[...])
        vbd = jnp.where(sel, jnp.concatenate([vn] * GMLP_HEADS, axis=0), 0.0).astype(BF16)
        mixed = jnp.dot(wcat, vbd, preferred_element_type=F32) + bias_ref[...]
        o_ref[rows, :] = (_rms(u * mixed) * mg_ref[...]).astype(o_ref.dtype)


def gmlp_seq(uv, nseq, lng, lnb, wcat, bias, mg, tt):
    t = uv.shape[0]
    full = lambda shape: pl.BlockSpec(shape, lambda b_, i: (0,) * len(shape))
    return pl.pallas_call(
        functools.partial(_gmlp_seq_kernel, tt=tt),
        out_shape=jax.ShapeDtypeStruct((t, nseq * GMLP_WIDTH), BF16),
        grid=(nseq, t // tt),
        in_specs=[pl.BlockSpec((tt, 2 * GMLP_WIDTH), lambda b_, i: (i, b_)),
                  full((1, GMLP_WIDTH)), full((1, GMLP_WIDTH)),
                  full((GMLP_CHUNK, GMLP_HEADS * GMLP_CHUNK)), full((GMLP_CHUNK, GMLP_WIDTH)),
                  full((1, GMLP_WIDTH))],
        out_specs=pl.BlockSpec((tt, GMLP_WIDTH), lambda b_, i: (i, b_)),
        compiler_params=_cparams(("arbitrary", "arbitrary")),
        name="gmlp_seq",
    )(uv, lng, lnb, wcat, bias, mg)


def _gmlp_short_kernel(uv_ref, lng_ref, lnb_ref, wrow_ref, brow_ref, mg_ref, o_ref, vn_ref, *, nb, t_len):
    u = uv_ref[:, 0:GMLP_WIDTH]
    v = uv_ref[:, GMLP_WIDTH:2 * GMLP_WIDTH]
    vn = _layernorm(v, lng_ref[...], lnb_ref[...])
    vn_ref[...] = vn
    wrow = wrow_ref[...]
    brow = brow_ref[...]
    for t in range(t_len):
        mixed = jnp.zeros((nb, GMLP_WIDTH), F32) + brow[t:t + 1, :]
        for j in range(t + 1):
            mixed = mixed + wrow[t * t_len + j:t * t_len + j + 1, :] * vn[j * nb:(j + 1) * nb, :]
        o = u[t * nb:(t + 1) * nb, :] * mixed
        o_ref[t * nb:(t + 1) * nb, :] = (_rms(o) * mg_ref[...]).astype(o_ref.dtype)


def gmlp_short(uv, lng, lnb, wrow, brow, mg, nb, t_len):
    rows = uv.shape[0]
    return pl.pallas_call(
        functools.partial(_gmlp_short_kernel, nb=nb, t_len=t_len),
        out_shape=[jax.ShapeDtypeStruct((rows, GMLP_WIDTH), BF16),
                   jax.ShapeDtypeStruct((rows, GMLP_WIDTH), F32)],
        compiler_params=pltpu.CompilerParams(vmem_limit_bytes=VMEM_LIMIT),
        name="gmlp_short",
    )(uv, lng, lnb, wrow, brow, mg)


def _split3(x):
    a = x.astype(BF16)
    r1 = x - a.astype(F32)
    b = r1.astype(BF16)
    c = (r1 - b.astype(F32)).astype(BF16)
    return a, b, c


def _dot_exact_rhs(x, m):
    return sum(jnp.dot(t, m, preferred_element_type=F32) for t in _split3(x))


def _dot_exact_lhs(m, x):
    return sum(jnp.dot(m, t, preferred_element_type=F32) for t in _split3(x))


def _gla_tail(o, r, gmean, onorm, mg):
    ms = _dot_exact_rhs(o * o, gmean)
    o = o * lax.rsqrt(ms + EPS) * onorm
    o = o * _silu(r)
    return _rms(o) * mg


def _head_mean_matrix():
    return jnp.where(_same_block((GLA_WIDTH, GLA_WIDTH), GLA_DV, GLA_DV), 1.0 / GLA_DV, 0.0).astype(BF16)


GLA_SEQS = 8


def _gla_seq_kernel(x_ref, s0_ref, wg_ref, bg_ref, onorm_ref, mg_ref, o_ref, sT_ref,
                    s_ref, qt_ref, kt_ref, kd_ref, dl_ref, oacc_ref, *, tt):
    i = pl.program_id(1)
    L = GLA_CHUNK
    n_ch = tt // L
    kw, vw = GLA_KEY_WIDTH, GLA_WIDTH

    @pl.when(i == 0)
    def _():
        s_ref[...] = s0_ref[...]

    rows_i = lax.broadcasted_iota(jnp.int32, (tt, tt), 0)
    cols_i = lax.broadcasted_iota(jnp.int32, (tt, tt), 1)
    shift = L.bit_length() - 1
    same_chunk = (rows_i >> shift) == (cols_i >> shift)
    tri = jnp.logical_and(same_chunk, cols_i <= rows_i).astype(BF16)
    chunk_sum = same_chunk.astype(BF16)
    chunk_rows = (lax.broadcasted_iota(jnp.int32, (n_ch, tt), 0)
                  == (lax.broadcasted_iota(jnp.int32, (n_ch, tt), 1) >> shift)).astype(BF16)
    kbd_sel = _same_block((GLA_HEADS * L, kw), L, GLA_DK)
    vbd_sel = _same_block((GLA_HEADS * L, vw), L, GLA_DV)
    causal = ((lax.broadcasted_iota(jnp.int32, (L, GLA_HEADS * L), 1) & (L - 1))
              <= lax.broadcasted_iota(jnp.int32, (L, GLA_HEADS * L), 0))
    s_sel = _same_block((vw, kw), GLA_DV, GLA_DK)
    gmean = _head_mean_matrix()
    scale = GLA_DK ** -0.5
    nt_dims = (((1,), (1,)), ((), ()))
    tn_dims = (((0,), (0,)), ((), ()))
    zero = jnp.zeros((), BF16)

    for g in range(GLA_SEQS):
        x0 = g * PW_GLA
        q = x_ref[:, x0:x0 + kw] * scale
        k = x_ref[:, x0 + kw:x0 + 2 * kw]
        gl = x_ref[:, x0 + 2 * kw + 2 * vw:x0 + PW_GLA]
        la = _log_sigmoid(jnp.dot(gl.astype(BF16), wg_ref[...], preferred_element_type=F32) + bg_ref[...])
        la = la / GLA_TAU
        bc = _dot_exact_lhs(tri, la)
        b_end = _dot_exact_lhs(chunk_sum, la)
        qt_ref[g] = (q * jnp.exp(bc)).astype(BF16)
        kt_ref[g] = (k * jnp.exp(-bc)).astype(BF16)
        kd_ref[g] = (k * jnp.exp(b_end - bc)).astype(BF16)
        dl_ref[g] = jnp.exp(_dot_exact_lhs(chunk_rows, la))

    def chunk(c, carry):
        r0 = pl.multiple_of(c * L, L)
        for g in range(GLA_SEQS):
            x0 = g * PW_GLA
            qt = qt_ref[g, pl.ds(r0, L), :]
            kt = kt_ref[g, pl.ds(r0, L), :]
            kdec = kd_ref[g, pl.ds(r0, L), :]
            vb = x_ref[pl.ds(r0, L), x0 + 2 * kw:x0 + 2 * kw + vw].astype(BF16)
            kbd = jnp.where(kbd_sel, jnp.concatenate([kt] * GLA_HEADS, axis=0), zero)
            att = lax.dot_general(qt, kbd, nt_dims, preferred_element_type=F32)
            att = jnp.where(causal, att, 0.0).astype(BF16)
            vbd = jnp.where(vbd_sel, jnp.concatenate([vb] * GLA_HEADS, axis=0), zero)
            st = s_ref[g]
            oacc_ref[g, pl.ds(r0, L), :] = (jnp.dot(att, vbd, preferred_element_type=F32)
                                            + lax.dot_general(qt, st.astype(BF16), nt_dims,
                                                              preferred_element_type=F32))
            upd = lax.dot_general(vb, kdec, tn_dims, preferred_element_type=F32)
            s_ref[g] = st * dl_ref[g, pl.ds(c, 1), :] + jnp.where(s_sel, upd, 0.0)
        return carry

    lax.fori_loop(0, n_ch, chunk, 0)

    for g in range(GLA_SEQS):
        x0 = g * PW_GLA
        r = x_ref[:, x0 + 2 * kw + vw:x0 + 2 * kw + 2 * vw]
        o_ref[:, g * vw:(g + 1) * vw] = _gla_tail(oacc_ref[g], r, gmean, onorm_ref[...],
                                                  mg_ref[...]).astype(o_ref.dtype)

    @pl.when(i == pl.num_programs(1) - 1)
    def _():
        sT_ref[...] = s_ref[...]


def gla_seq(x, nseq, s0, wg, bg, onorm, mg, tt):
    t = x.shape[0]
    g = GLA_SEQS
    assert nseq % g == 0
    full = lambda shape: pl.BlockSpec(shape, lambda b_, i: (0,) * len(shape))
    state = pl.BlockSpec((g, GLA_WIDTH, GLA_KEY_WIDTH), lambda b_, i: (b_, 0, 0))
    return pl.pallas_call(
        functools.partial(_gla_seq_kernel, tt=tt),
        out_shape=[jax.ShapeDtypeStruct((t, nseq * GLA_WIDTH), BF16),
                   jax.ShapeDtypeStruct((nseq, GLA_WIDTH, GLA_KEY_WIDTH), F32)],
        grid=(nseq // g, t // tt),
        in_specs=[pl.BlockSpec((tt, g * PW_GLA), lambda b_, i: (i, b_)),
                  state,
                  full((LANE, GLA_KEY_WIDTH)), full((1, GLA_KEY_WIDTH)),
                  full((1, GLA_WIDTH)), full((1, GLA_WIDTH))],
        out_specs=[pl.BlockSpec((tt, g * GLA_WIDTH), lambda b_, i: (i, b_)), state],
        scratch_shapes=[pltpu.VMEM((g, GLA_WIDTH, GLA_KEY_WIDTH), F32),
                        pltpu.VMEM((g, tt, GLA_KEY_WIDTH), BF16), pltpu.VMEM((g, tt, GLA_KEY_WIDTH), BF16),
                        pltpu.VMEM((g, tt, GLA_KEY_WIDTH), BF16),
                        pltpu.VMEM((g, tt // GLA_CHUNK, GLA_KEY_WIDTH), F32),
                        pltpu.VMEM((g, tt, GLA_WIDTH), F32)],
        compiler_params=_cparams(("arbitrary", "arbitrary")),
        name="gla_seq",
    )(x, s0, wg, bg, onorm, mg)


def _gla_rec_kernel(x_ref, s0_ref, ek_ref, ev_ref, wg_ref, bg_ref, onorm_ref, mg_ref, o_ref, sT_ref,
                    *, nb, t_len):
    kw, vw = GLA_KEY_WIDTH, GLA_WIDTH
    hl = GLA_DK * GLA_DV
    sT_ref[...] = s0_ref[...]
    gmean = _head_mean_matrix()
    scale = GLA_DK ** -0.5

    def step(t, carry):
        r0 = pl.multiple_of(t * nb, nb)
        q = x_ref[pl.ds(r0, nb), 0:kw] * scale
        k = x_ref[pl.ds(r0, nb), kw:2 * kw]
        v = x_ref[pl.ds(r0, nb), 2 * kw:2 * kw + vw]
        r = x_ref[pl.ds(r0, nb), 2 * kw + vw:2 * kw + 2 * vw]
        gl = x_ref[pl.ds(r0, nb), 2 * kw + 2 * vw:2 * kw + 2 * vw + LANE]
        la = _log_sigmoid(jnp.dot(gl.astype(BF16), wg_ref[...], preferred_element_type=F32) + bg_ref[...])
        a = jnp.exp(la / GLA_TAU)
        a3 = _split3(a)
        qb = q.astype(BF16)
        kb = k.astype(BF16)
        vb = v.astype(BF16)
        outs = []
        for h in range(GLA_HEADS):
            lanes = slice(h * hl, (h + 1) * hl)
            ek = ek_ref[:, lanes]
            a_e = (jnp.dot(a3[0], ek, preferred_element_type=F32)
                   + jnp.dot(a3[1], ek, preferred_element_type=F32)
                   + jnp.dot(a3[2], ek, preferred_element_type=F32))
            k_e = jnp.dot(kb, ek, preferred_element_type=F32)
            q_e = jnp.dot(qb, ek, preferred_element_type=F32)
            v_e = jnp.dot(vb, ev_ref[:, lanes], preferred_element_type=F32)
            s_new = a_e * sT_ref[:, lanes] + k_e * v_e
            sT_ref[:, lanes] = s_new
            prod = q_e * s_new
            acc = prod[:, 0:LANE]
            for j in range(1, hl // LANE):
                acc = acc + prod[:, j * LANE:(j + 1) * LANE]
            outs.append(acc[:, 0:GLA_DV] + acc[:, GLA_DV:2 * GLA_DV])
        o = jnp.concatenate(outs, axis=1)
        o_ref[pl.ds(r0, nb), :] = _gla_tail(o, r, gmean, onorm_ref[...], mg_ref[...]).astype(o_ref.dtype)
        return carry

    lax.fori_loop(0, t_len, step, 0)


def gla_recurrent(x, s0, ek, ev, wg, bg, onorm, mg, nb, t_len):
    rows = x.shape[0]
    return pl.pallas_call(
        functools.partial(_gla_rec_kernel, nb=nb, t_len=t_len),
        out_shape=[jax.ShapeDtypeStruct((rows, GLA_WIDTH), BF16),
                   jax.ShapeDtypeStruct((nb, GLA_STATE_LANES), F32)],
        compiler_params=pltpu.CompilerParams(vmem_limit_bytes=VMEM_LIMIT),
        name="gla_recurrent",
    )(x, s0, ek, ev, wg, bg, onorm, mg)


def _gla_expanders():
    lane = jnp.arange(GLA_STATE_LANES)
    h = lane // (GLA_DK * GLA_DV)
    dk = (lane // GLA_DV) % GLA_DK
    dv = lane % GLA_DV
    ek = (jnp.arange(GLA_KEY_WIDTH)[:, None] == (h * GLA_DK + dk)[None, :]).astype(BF16)
    ev = (jnp.arange(GLA_WIDTH)[:, None] == (h * GLA_DV + dv)[None, :]).astype(BF16)
    return ek, ev


def _mix_residual(x_ref, m_refs, g1_ref, wout_ref):
    mix = jnp.concatenate([m[...] for m in m_refs], axis=1)
    proj = jnp.dot(mix, wout_ref[...], preferred_element_type=F32)
    return x_ref[0] + _gate(proj, g1_ref[0])


def _swiglu(h, wg_ref, wu_ref, wd_ref, tf, lead=(), between=None):
    ff = wg_ref.shape[-1]
    n_dots = 3 * (ff // tf)
    tick = (lambda i: between(i, n_dots)) if between is not None else (lambda i: None)
    acc = jnp.zeros((h.shape[0], wd_ref.shape[-1]), F32)
    for c in range(ff // tf):
        cols = slice(c * tf, (c + 1) * tf)
        gate = jnp.dot(h, wg_ref[(*lead, slice(None), cols)], preferred_element_type=F32)
        tick(3 * c)
        up = jnp.dot(h, wu_ref[(*lead, slice(None), cols)], preferred_element_type=F32)
        tick(3 * c + 1)
        acc = acc + jnp.dot((_silu(gate) * up).astype(BF16), wd_ref[(*lead, cols, slice(None))],
                            preferred_element_type=F32)
        tick(3 * c + 2)
    return acc


def _post_dense_kernel(x_ref, m0_ref, m1_ref, m2_ref, m3_ref, g1_ref, sh2_ref, sc2_ref, g2_ref, ng_ref, wout_ref,
                       wg_ref, wu_ref, wd_ref, fg_ref, o_ref, *, final_norm, tf):
    x1 = _mix_residual(x_ref, (m0_ref, m1_ref, m2_ref, m3_ref), g1_ref, wout_ref)
    h = _modulate(_rms(x1) * ng_ref[...], sc2_ref[0], sh2_ref[0]).astype(BF16)
    x2 = x1 + _gate(_swiglu(h, wg_ref, wu_ref, wd_ref, tf), g2_ref[0])
    if final_norm:
        x2 = _rms(x2) * fg_ref[...]
    o_ref[0] = x2


def post_dense(x, mixes, g1, sh2, sc2, g2, ng, wout, wg, wu, wd, fg, tm, tf, final_norm):
    s, r, d = x.shape
    rm = g1.shape[1]
    ff = wg.shape[1]
    mod = pl.BlockSpec((1, rm, d), lambda b_, i: (b_, 0, 0))
    const = lambda shape: pl.BlockSpec(shape, lambda b_, i: (0,) * len(shape))
    resident = lambda shape: pl.BlockSpec(shape, lambda b_, i: (0,) * len(shape), pipeline_mode=pl.Buffered(1))
    mixspec = pl.BlockSpec((tm, 256), lambda b_, i: (i, b_))
    return pl.pallas_call(
        functools.partial(_post_dense_kernel, final_norm=final_norm, tf=tf),
        out_shape=jax.ShapeDtypeStruct((s, r, d), F32),
        grid=(s, r // tm),
        in_specs=[pl.BlockSpec((1, tm, d), lambda b_, i: (b_, i, 0)),
                  mixspec, mixspec, mixspec, mixspec,
                  mod, mod, mod, mod,
                  const((1, d)), resident((d, d)),
                  resident((d, ff)), resident((d, ff)), resident((ff, d)),
                  const((1, d))],
        out_specs=pl.BlockSpec((1, tm, d), lambda b_, i: (b_, i, 0)),
        compiler_params=_cparams(("arbitrary", "arbitrary")),
        name="post_dense",
    )(x, *mixes, g1, sh2, sc2, g2, ng, wout, wg, wu, wd, fg)


ROW_TILE = 8


def _store_row_tiles(ref, x, lead=()):
    rows = x.shape[0]
    for s in range(ROW_TILE):
        ref[(*lead, pl.ds(s, rows, stride=ROW_TILE), slice(None))] = x[:, s * LANE:(s + 1) * LANE]


def _load_row_tiles(ref, rows, lead=()):
    return jnp.concatenate([ref[(*lead, pl.ds(s, rows, stride=ROW_TILE), slice(None))] for s in range(ROW_TILE)],
                           axis=1)


def _route_kernel(x_ref, m0_ref, m1_ref, m2_ref, m3_ref, g1_ref, sh2_ref, sc2_ref, ng_ref, wout_ref, router_ref,
                  *rest):
    x1_ref, h2_ref, route_ref = rest[-3:]
    x1 = _mix_residual(x_ref, (m0_ref, m1_ref, m2_ref, m3_ref), g1_ref, wout_ref)
    x1_ref[0] = x1
    h = _modulate(_rms(x1) * ng_ref[...], sc2_ref[0], sh2_ref[0])
    _store_row_tiles(h2_ref, h)
    h_hi = h.astype(BF16)
    h_lo = (h - h_hi.astype(F32)).astype(BF16)
    w = router_ref[...]
    w_hi = w.astype(BF16)
    w_lo = (w - w_hi.astype(F32)).astype(BF16)
    logits = (jnp.dot(h_hi, w_hi, preferred_element_type=F32) + jnp.dot(h_lo, w_hi, preferred_element_type=F32)
              + jnp.dot(h_hi, w_lo, preferred_element_type=F32))
    lane = lax.broadcasted_iota(jnp.int32, logits.shape, 1).astype(F32)
    neg = jnp.float32(-jnp.inf)
    logits = jnp.where(lane < N_EXPERTS, logits, neg)
    m1 = jnp.max(logits, axis=1, keepdims=True)
    i1 = jnp.min(jnp.where(logits == m1, lane, float(LANE)), axis=1, keepdims=True)
    others = jnp.where(lane == i1, neg, logits)
    m2 = jnp.max(others, axis=1, keepdims=True)
    i2 = jnp.min(jnp.where(others == m2, lane, float(LANE)), axis=1, keepdims=True)
    e2 = jnp.exp(m2 - m1)
    den = 1.0 + e2
    route_ref[...] = (jnp.where(lane == 0.0, i1, 0.0) + jnp.where(lane == 1.0, i2, 0.0)
                      + jnp.where(lane == 2.0, 1.0 / den, 0.0) + jnp.where(lane == 3.0, e2 / den, 0.0))


def moe_route(x, mixes, g1, sh2, sc2, ng, wout, router, tm, row0, shared):
    s, r, d = x.shape
    rm = g1.shape[1]
    nt = r // tm
    blk0 = row0 // tm
    n_total = shared[1].shape[0]
    mod = pl.BlockSpec((1, rm, d), lambda b_, i: (b_, 0, 0))
    const = lambda shape: pl.BlockSpec(shape, lambda b_, i: (0,) * len(shape))
    mixspec = pl.BlockSpec((tm, 256), lambda b_, i: (i, b_))
    in_specs = [pl.BlockSpec((1, tm, d), lambda b_, i: (b_, i, 0)),
                mixspec, mixspec, mixspec, mixspec, mod, mod, mod,
                const((1, d)), const((d, d)), const((d, LANE))]
    args = [x, *mixes, g1, sh2, sc2, ng, wout, router]
    in_specs += [pl.BlockSpec(memory_space=pl.ANY), pl.BlockSpec(memory_space=pl.ANY)]
    aliases = {len(args): 1, len(args) + 1: 2}
    args += list(shared)
    return pl.pallas_call(
        _route_kernel,
        out_shape=[jax.ShapeDtypeStruct((s, r, d), F32),
                   jax.ShapeDtypeStruct((n_total * ROW_TILE, LANE), F32),
                   jax.ShapeDtypeStruct((n_total, LANE), F32)],
        grid=(s, nt),
        in_specs=in_specs,
        out_specs=[pl.BlockSpec((1, tm, d), lambda b_, i: (b_, i, 0)),
                   pl.BlockSpec((tm * ROW_TILE, LANE), lambda b_, i: (blk0 + b_ * nt + i, 0)),
                   pl.BlockSpec((tm, LANE), lambda b_, i: (blk0 + b_ * nt + i, 0))],
        input_output_aliases=aliases,
        compiler_params=_cparams(("arbitrary", "arbitrary")),
        name="moe_route",
    )(*args)


def _route_tables(route, tg, n_tiles):
    n_total = route.shape[0]
    flat_e = route[:, 0:2].astype(jnp.int32).reshape(-1)
    keys = jnp.concatenate([flat_e, jnp.full((tg,), N_EXPERTS, jnp.int32)])
    order = jnp.argsort(keys, stable=True).astype(jnp.int32)
    counts = jnp.sum(flat_e[:, None] == jnp.arange(N_EXPERTS, dtype=jnp.int32)[None, :], axis=0).astype(jnp.int32)
    tiles_per = (counts + tg - 1) // tg
    tile_end = jnp.cumsum(tiles_per)
    n_used = tile_end[-1]
    tile_id = jnp.arange(n_tiles, dtype=jnp.int32)
    tile_ok = tile_id < n_used
    tile_e = jnp.sum(jnp.minimum(tile_id, n_used - 1)[:, None] >= tile_end[None, :], axis=1).astype(jnp.int32)
    sort_start = jnp.cumsum(counts) - counts
    done = (tile_id - (tile_end - tiles_per)[tile_e]) * tg
    n_valid = jnp.where(tile_ok, jnp.clip(counts[tile_e] - done, 0, tg), 0).astype(jnp.int32)
    tile_start = jnp.where(tile_ok, sort_start[tile_e] + done, 0).astype(jnp.int32)
    real = order < 2 * n_total
    src = jnp.where(real, (order >> 1) * ROW_TILE, 0)
    dst = jnp.where(real, ((order & 1) * n_total + (order >> 1)) * ROW_TILE, 0)
    return tile_e, n_valid, tile_start, src, dst


DMA_UNROLL = 8


def _experts_kernel(te_ref, nv_ref, ts_ref, src_ref, dst_ref, h2_hbm, wg_ref, wu_ref, wd_ref, out_hbm,
                    xbuf, obuf, gsem, ssem, *, tg, n_tiles, tf):
    j = pl.program_id(0)
    slot = lax.rem(j, 2)
    other = 1 - slot
    ok = nv_ref[j] > 0

    def row_tile(buf, s_, r):
        start = r * ROW_TILE if isinstance(r, int) else pl.multiple_of(r * ROW_TILE, ROW_TILE)
        return buf.at[s_, pl.ds(start, ROW_TILE), :]

    def gather_row(tile, s_, r, priority=0):
        row = pl.multiple_of(src_ref[ts_ref[tile] + r], ROW_TILE)
        pltpu.make_async_copy(h2_hbm.at[pl.ds(row, ROW_TILE), :], row_tile(xbuf, s_, r),
                              gsem.at[s_]).start(priority=priority)

    def scatter_row(tile, s_, r, priority=0):
        row = pl.multiple_of(dst_ref[ts_ref[tile] + r], ROW_TILE)
        pltpu.make_async_copy(row_tile(obuf, s_, r), out_hbm.at[pl.ds(row, ROW_TILE), :],
                              ssem.at[s_]).start(priority=priority)

    def full_tile(issue_row, tile, s_):
        def body(r8, c):
            for u in range(DMA_UNROLL):
                issue_row(tile, s_, r8 * DMA_UNROLL + u, priority=u % 2)
            return c

        lax.fori_loop(0, tg // DMA_UNROLL, body, 0)

    def gather(tile, s_):
        full_tile(gather_row, tile, s_)

    def scatter(tile, s_):
        n = nv_ref[tile]

        @pl.when(n == tg)
        def _():
            full_tile(scatter_row, tile, s_)

        @pl.when(n < tg)
        def _():
            def body(r, c):
                scatter_row(tile, s_, r)
                return c

            lax.fori_loop(0, n, body, 0)

    def wait_all(buf, sem, s_):
        pltpu.make_async_copy(buf.at[s_], buf.at[s_], sem.at[s_]).wait()

    def wait_scatter(tile, s_):
        n = nv_ref[tile]

        @pl.when(n == tg)
        def _():
            wait_all(obuf, ssem, s_)

        @pl.when(n < tg)
        def _():
            def body(r, c):
                pltpu.make_async_copy(obuf.at[s_, pl.ds(0, ROW_TILE), :], out_hbm.at[pl.ds(0, ROW_TILE), :],
                                      ssem.at[s_]).wait()
                return c

            lax.fori_loop(0, n, body, 0)

    @pl.when(jnp.logical_and(j == 0, ok))
    def _():
        gather(0, 0)

    @pl.when(jnp.logical_or(jnp.logical_and(j == 0, ok), nv_ref[jnp.maximum(j - 1, 0)] * jnp.minimum(j, 1) > 0))
    def _():
        wait_all(xbuf, gsem, slot)

    @pl.when(j >= 2)
    def _():
        wait_scatter(j - 2, slot)

    nxt = jnp.minimum(j + 1, n_tiles - 1)

    @pl.when(ok)
    def _():
        x = _load_row_tiles(xbuf, tg, lead=(slot,)).astype(BF16)

        def gather_some(i, n):
            for r in range(i * tg // n, (i + 1) * tg // n):
                gather_row(nxt, other, r, priority=r % 2)

        y = _swiglu(x, wg_ref, wu_ref, wd_ref, tf, lead=(0,), between=gather_some)
        _store_row_tiles(obuf, y, lead=(slot,))
        scatter(j, slot)

    @pl.when(j == n_tiles - 1)
    def _():
        @pl.when(ok)
        def _():
            wait_all(xbuf, gsem, other)

        wait_scatter(j - 1, other)
        wait_scatter(j, slot)


def moe_experts(h2, tables, wg, wu, wd, tg, n_tiles, tf):
    n_exp, d, ff = wg.shape
    assert d == ROW_TILE * LANE and n_tiles >= 2
    tile_e, n_valid, tile_start, src, dst = tables
    wspec = lambda shape: pl.BlockSpec(shape, lambda j, te, *_: (te[j], 0, 0))
    grid_spec = pltpu.PrefetchScalarGridSpec(
        num_scalar_prefetch=5,
        grid=(n_tiles,),
        in_specs=[pl.BlockSpec(memory_space=pl.ANY), wspec((1, d, ff)), wspec((1, d, ff)), wspec((1, ff, d))],
        out_specs=pl.BlockSpec(memory_space=pl.ANY),
        scratch_shapes=[pltpu.VMEM((2, tg * ROW_TILE, LANE), F32), pltpu.VMEM((2, tg * ROW_TILE, LANE), F32),
                        pltpu.SemaphoreType.DMA((2,)), pltpu.SemaphoreType.DMA((2,))])
    return pl.pallas_call(
        functools.partial(_experts_kernel, tg=tg, n_tiles=n_tiles, tf=tf),
        out_shape=jax.ShapeDtypeStruct((2 * h2.shape[0], LANE), F32),
        grid_spec=grid_spec,
        compiler_params=_cparams(("arbitrary",)),
        name="moe_experts",
    )(tile_e, n_valid, tile_start, src, dst, h2, wg, wu, wd)


def _combine_kernel(x1_ref, y0_ref, y1_ref, route_ref, g2_ref, fg_ref, o_ref, *, final_norm):
    r = route_ref[...]
    rows = r.shape[0]
    f = r[:, 2:3] * _load_row_tiles(y0_ref, rows, lead=(0,)) + r[:, 3:4] * _load_row_tiles(y1_ref, rows, lead=(0,))
    x2 = x1_ref[0] + _gate(f, g2_ref[0])
    if final_norm:
        x2 = _rms(x2) * fg_ref[...]
    o_ref[0] = x2


def moe_combine(x1, y, route, g2, fg, tm, row0, final_norm):
    s, r, d = x1.shape
    rm = g2.shape[1]
    nt = r // tm
    blk0 = row0 // tm
    return pl.pallas_call(
        functools.partial(_combine_kernel, final_norm=final_norm),
        out_shape=jax.ShapeDtypeStruct((s, r, d), F32),
        grid=(s, nt),
        in_specs=[pl.BlockSpec((1, tm, d), lambda b_, i: (b_, i, 0)),
                  pl.BlockSpec((1, tm * ROW_TILE, LANE), lambda b_, i: (0, blk0 + b_ * nt + i, 0)),
                  pl.BlockSpec((1, tm * ROW_TILE, LANE), lambda b_, i: (1, blk0 + b_ * nt + i, 0)),
                  pl.BlockSpec((tm, LANE), lambda b_, i: (blk0 + b_ * nt + i, 0)),
                  pl.BlockSpec((1, rm, d), lambda b_, i: (b_, 0, 0)),
                  pl.BlockSpec((1, d), lambda b_, i: (0, 0))],
        out_specs=pl.BlockSpec((1, tm, d), lambda b_, i: (b_, i, 0)),
        compiler_params=_cparams(("arbitrary", "arbitrary")),
        name="moe_combine",
    )(x1, y, y, route, g2, fg)


def _reorder_w_in(w_in, b_in):
    cut = PW_S5 + 128 + 128 + 256 + 256 + GLA_GATE_RANK
    pad = LANE - GLA_GATE_RANK
    w = jnp.concatenate([w_in[:, :cut], jnp.zeros((w_in.shape[0], pad), w_in.dtype), w_in[:, cut:]], axis=1)
    b = jnp.concatenate([b_in[:cut], jnp.zeros((pad,), b_in.dtype), b_in[cut:]])
    return w.astype(BF16), b.reshape(1, PW_TOTAL)


def _row(a):
    return a.reshape(1, -1)


class _Branch:
    def __init__(self, nseq, nb, t_len, seq_form, tm_pre, tm_post, tc, tt_seq, row0):
        self.nseq, self.nb, self.t_len, self.seq_form = nseq, nb, t_len, seq_form
        self.tm_pre, self.tm_post, self.tc, self.tt_seq, self.row0 = tm_pre, tm_post, tc, tt_seq, row0


def _layer_params(W, i):
    row = _row
    p = {}
    p['w_in'], p['b_in'] = _reorder_w_in(W['w_in'][i], W['b_in'][i])
    p['norm_g'] = row(W['norm_mix_g'][i])
    mg = W['merge_g'][i]
    p['mg'] = [row(mg[k * 256:(k + 1) * 256]) for k in range(4)]
    ab_re, ab_im, bb_re, bb_im = s5_discretise(W['s5_a_re'][i], W['s5_a_im'][i], W['s5_log_dt'][i],
                                               W['s5_b_re'][i], W['s5_b_im'][i])
    p['s5'] = (jnp.concatenate([_block_diag_in(bb_re), _block_diag_in(bb_im)], axis=1).astype(BF16),
               _block_diag_out(W['s5_c_re'][i]).astype(BF16), _block_diag_out(W['s5_c_im'][i]).astype(BF16),
               row(ab_re), row(ab_im), row(W['s5_d'][i]), W['s5_w_glu'][i].astype(BF16), row(W['s5_b_glu'][i]))
    wg2 = jnp.zeros((LANE, GLA_KEY_WIDTH), F32).at[:GLA_GATE_RANK].set(W['gla_w_gate2'][i]).astype(BF16)
    p['gla'] = (wg2, row(W['gla_b_gate2'][i]), row(W['gla_onorm_g'][i]))
    p['conv'] = (W['conv_w_dw'][i], row(W['conv_b_dw'][i]), row(W['conv_ln_g'][i]), row(W['conv_ln_b'][i]),
                 W['conv_w_pw'][i].astype(BF16), row(W['conv_b_pw'][i]))
    p['gmlp_ln'] = (row(W['gmlp_ln_g'][i]), row(W['gmlp_ln_b'][i]))
    p['gmlp_ws'], p['gmlp_bs'] = W['gmlp_w_s'][i], W['gmlp_b_s'][i]
    return p


def _mixers(x, mods, states, p, i, br, out):
    nseq, nb, t_len, seq_form = br.nseq, br.nb, br.t_len, br.seq_form
    s5_re0, s5_im0, gla0, conv0 = states
    new_re, new_im, new_gla, new_conv, new_v = out
    mg = p['mg']
    p_s5, p_gla, p_conv, p_mlp = pre_mixer(x, mods[0], mods[1], p['norm_g'], p['w_in'], p['b_in'], br.tm_pre)

    h0 = jnp.concatenate([s5_re0[i].reshape(nb, S5_LANES), s5_im0[i].reshape(nb, S5_LANES)], axis=1)
    o_s5, h_t = s5_mixer(p_s5, h0, *p['s5'], mg[0], nb, br.tc, nseq)
    new_re.append(h_t[:, :S5_LANES].reshape(nb, S5_GROUPS, S5_STATE))
    new_im.append(h_t[:, S5_LANES:].reshape(nb, S5_GROUPS, S5_STATE))

    if seq_form:
        eye = jnp.eye(GLA_HEADS, dtype=F32)
        s0 = jnp.einsum('bhkv,hg->bhvgk', gla0[i], eye).reshape(nseq, GLA_WIDTH, GLA_KEY_WIDTH)
        o_gla, s_t = gla_seq(p_gla, nseq, s0, *p['gla'], mg[1], br.tt_seq)
        s5d = s_t.reshape(nseq, GLA_HEADS, GLA_DV, GLA_HEADS, GLA_DK)
        new_gla.append(jnp.stack([jnp.swapaxes(s5d[:, h, :, h, :], 1, 2) for h in range(GLA_HEADS)], axis=1))
    else:
        ek, ev = _gla_expanders()
        o_gla, s_t = gla_recurrent(p_gla, gla0[i].reshape(nb, GLA_STATE_LANES), ek, ev, *p['gla'], mg[1], nb, t_len)
        new_gla.append(s_t.reshape(nb, GLA_HEADS, GLA_DK, GLA_DV))

    c0 = jnp.transpose(conv0[i], (1, 0, 2)).reshape(CONV_HIST * nb, CONV_DIM)
    o_conv, buf = conv_mixer(p_conv, c0, *p['conv'], mg[2], nb, br.tc, nseq)
    new_conv.append(jnp.transpose(buf.reshape(CONV_HIST, nb, CONV_DIM), (1, 0, 2)))

    ws, bs = p['gmlp_ws'], p['gmlp_bs']
    if seq_form:
        wcat = jnp.transpose(ws, (1, 0, 2)).reshape(GMLP_CHUNK, GMLP_HEADS * GMLP_CHUNK)
        bias = jnp.repeat(bs.T, GMLP_HEAD_DIM, axis=1)
        o_mlp = gmlp_seq(p_mlp, nseq, *p['gmlp_ln'], wcat, bias, mg[3], min(4 * br.tt_seq, t_len))
        new_v.append(None)
    else:
        tri = jnp.tril(jnp.ones((t_len, t_len), F32))
        wrow = jnp.repeat(jnp.transpose(ws[:, :t_len, :t_len] * tri[None], (1, 2, 0)).reshape(t_len * t_len, GMLP_HEADS),
                          GMLP_HEAD_DIM, axis=1)
        brow = jnp.repeat(bs[:, :t_len].T, GMLP_HEAD_DIM, axis=1)
        o_mlp, vn = gmlp_short(p_mlp, *p['gmlp_ln'], wrow, brow, mg[3], nb, t_len)
        new_v.append(vn)

    return [o_s5, o_gla, o_conv, o_mlp]


FF_TILE = 1408
EXPERT_FF_TILE = 256
EXPERT_ROWS = 512


def _channel_mixer(xs, mixes, mods, W, i, branches, last):
    ng, wout, fg = _row(W['norm_ffn_g'][i]), W['w_out'][i].astype(BF16), _row(W['final_norm_g'])
    j = i // 2
    if i % 2 == 0:
        wg, wu, wd = (W['ffn_w_gate'][j].astype(BF16), W['ffn_w_up'][j].astype(BF16),
                      W['ffn_w_down'][j].astype(BF16))
        return [post_dense(x, mx, m[2], m[3], m[4], m[5], ng, wout, wg, wu, wd, fg, br.tm_post, FF_TILE, last)
                for x, mx, m, br in zip(xs, mixes, mods, branches)]
    n_total = sum(x.shape[0] * x.shape[1] for x in xs)
    tg = EXPERT_ROWS
    n_tiles = 2 * n_total // tg + N_EXPERTS
    router = jnp.zeros((D_MODEL, LANE), F32).at[:, :N_EXPERTS].set(W['moe_router'][j])
    h2, route = jnp.zeros((n_total * ROW_TILE, LANE), F32), jnp.zeros((n_total, LANE), F32)
    x1s = []
    for x, mx, m, br in zip(xs, mixes, mods, branches):
        x1, h2, route = moe_route(x, mx, m[2], m[3], m[4], ng, wout, router, br.tm_post, br.row0, (h2, route))
        x1s.append(x1)
    tables = _route_tables(route, tg, n_tiles)
    y = moe_experts(h2, tables, W['moe_w_gate'][j].astype(BF16), W['moe_w_up'][j].astype(BF16),
                    W['moe_w_down'][j].astype(BF16), tg, n_tiles, EXPERT_FF_TILE)
    y = y.reshape(2, n_total * ROW_TILE, LANE)
    return [moe_combine(x1, y, route, m[5], fg, br.tm_post, br.row0, last)
            for x1, m, br in zip(x1s, mods, branches)]


def kernel(x_prompt, x_sample, c_prompt, c_sample, state_s5_re, state_s5_im, state_gla, cache_conv, ada_w, ada_b, norm_mix_g, norm_ffn_g, w_in, b_in, s5_a_re, s5_a_im, s5_log_dt, s5_b_re, s5_b_im, s5_c_re, s5_c_im, s5_d, s5_w_glu, s5_b_glu, gla_w_gate2, gla_b_gate2, gla_onorm_g, conv_w_dw, conv_b_dw, conv_ln_g, conv_ln_b, conv_w_pw, conv_b_pw, gmlp_ln_g, gmlp_ln_b, gmlp_w_s, gmlp_b_s, merge_g, w_out, ffn_w_gate, ffn_w_up, ffn_w_down, moe_router, moe_w_gate, moe_w_up, moe_w_down, final_norm_g):
    W = dict(norm_mix_g=norm_mix_g, norm_ffn_g=norm_ffn_g, w_in=w_in, b_in=b_in, s5_a_re=s5_a_re, s5_a_im=s5_a_im,
             s5_log_dt=s5_log_dt, s5_b_re=s5_b_re, s5_b_im=s5_b_im, s5_c_re=s5_c_re, s5_c_im=s5_c_im, s5_d=s5_d,
             s5_w_glu=s5_w_glu, s5_b_glu=s5_b_glu, gla_w_gate2=gla_w_gate2, gla_b_gate2=gla_b_gate2,
             gla_onorm_g=gla_onorm_g, conv_w_dw=conv_w_dw, conv_b_dw=conv_b_dw, conv_ln_g=conv_ln_g,
             conv_ln_b=conv_ln_b, conv_w_pw=conv_w_pw, conv_b_pw=conv_b_pw, gmlp_ln_g=gmlp_ln_g,
             gmlp_ln_b=gmlp_ln_b, gmlp_w_s=gmlp_w_s, gmlp_b_s=gmlp_b_s, merge_g=merge_g, w_out=w_out,
             ffn_w_gate=ffn_w_gate, ffn_w_up=ffn_w_up, ffn_w_down=ffn_w_down, moe_router=moe_router,
             moe_w_gate=moe_w_gate, moe_w_up=moe_w_up, moe_w_down=moe_w_down, final_norm_g=final_norm_g)
    depth = w_in.shape[0]
    bp, tp, d = x_prompt.shape
    bs, ts, _ = x_sample.shape

    m = ada_modulation(jnp.concatenate([c_prompt, c_sample], axis=0), ada_w, ada_b)
    mods_p = [[m[i, :bp, k * d:(k + 1) * d].reshape(bp, 1, d) for k in range(6)] for i in range(depth)]
    mods_s = [[m[i, bp:, k * d:(k + 1) * d].reshape(1, bs, d) for k in range(6)] for i in range(depth)]

    z_re = jnp.zeros((depth, bp, S5_GROUPS, S5_STATE), F32)
    z_gla = jnp.zeros((depth, bp, GLA_HEADS, GLA_DK, GLA_DV), F32)
    z_conv = jnp.zeros((depth, bp, CONV_HIST, CONV_DIM), x_prompt.dtype)
    states = [(z_re, z_re, z_gla, z_conv), (state_s5_re, state_s5_im, state_gla, cache_conv)]
    branches = [_Branch(nseq=bp, nb=bp, t_len=tp, seq_form=True, tm_pre=min(512, tp), tm_post=min(512, tp),
                        tc=min(256, tp), tt_seq=min(512, tp), row0=0),
                _Branch(nseq=1, nb=bs, t_len=ts, seq_form=False, tm_pre=ts * bs, tm_post=min(512, ts * bs),
                        tc=ts, tt_seq=None, row0=bp * tp)]
    xs = [x_prompt, jnp.transpose(x_sample, (1, 0, 2)).reshape(1, ts * bs, d)]
    outs = [([], [], [], [], []), ([], [], [], [], [])]
    for i in range(depth):
        mods = [mods_p[i], mods_s[i]]
        params = _layer_params(W, i)
        mixes = [_mixers(x, m, st_, params, i, br, o)
                 for x, m, st_, br, o in zip(xs, mods, states, branches, outs)]
        xs = _channel_mixer(xs, mixes, mods, W, i, branches, i == depth - 1)

    y_p = xs[0]
    y_s = jnp.transpose(xs[1].reshape(ts, bs, d), (1, 0, 2))
    p_re, p_im, p_gla, p_conv, _ = outs[0]
    s_re, s_im, s_gla, s_conv, s_v = outs[1]
    s_v = [jnp.transpose(v.reshape(ts, bs, GMLP_WIDTH), (1, 0, 2)) for v in s_v]
    st = jnp.stack
    return (y_p, y_s, st(p_re), st(p_im), st(p_gla), st(p_conv),
            st(s_re), st(s_im), st(s_gla), st(s_conv), st(s_v))
```

```python
import functools
import math

import jax
import jax.numpy as jnp
from jax import lax
from jax.experimental import pallas as pl
from jax.experimental.pallas import tpu as pltpu

D_MODEL = 1024
S5_WIDTH = 256
S5_GROUP = 16
S5_GROUPS = 16
S5_STATE = 64
S5_LANES = S5_GROUPS * S5_STATE
GLA_HEADS = 4
GLA_DV = 64
GLA_DK = 32
GLA_WIDTH = 256
GLA_KEY_WIDTH = 128
GLA_GATE_RANK = 16
GLA_TAU = 16.0
GLA_CHUNK = 64
GLA_STATE_LANES = GLA_HEADS * GLA_DK * GLA_DV
CONV_DIM = 256
CONV_WIDTH = 31
CONV_HIST = CONV_WIDTH - 1
GMLP_WIDTH = 256
GMLP_HEADS = 4
GMLP_HEAD_DIM = 64
GMLP_CHUNK = 128
D_FF = 2816
N_EXPERTS = 8
EPS = 1e-6

LANE = 128
PW_S5 = 256
PW_GLA = 128 + 128 + 256 + 256 + LANE
PW_CONV = 512
PW_MLP = 512
PW_TOTAL = PW_S5 + PW_GLA + PW_CONV + PW_MLP
VMEM_LIMIT = 56 * 1024 * 1024

F32 = jnp.float32
BF16 = jnp.bfloat16
HI = lax.Precision.HIGHEST


def _cparams(sem):
    return pltpu.CompilerParams(dimension_semantics=sem, vmem_limit_bytes=VMEM_LIMIT)


def _rms(x):
    return x * lax.rsqrt(jnp.mean(x * x, axis=-1, keepdims=True) + EPS)


def _layernorm(x, g, b):
    mu = jnp.mean(x, axis=-1, keepdims=True)
    xc = x - mu
    var = jnp.mean(xc * xc, axis=-1, keepdims=True)
    return xc * lax.rsqrt(var + EPS) * g + b


def _silu(x):
    return x * jax.nn.sigmoid(x)


def _gelu_tanh(x):
    return 0.5 * x * (1.0 + jnp.tanh(math.sqrt(2.0 / math.pi) * (x + 0.044715 * (x * x * x))))


def _log_sigmoid(x):
    return jnp.minimum(x, 0.0) - jnp.log(1.0 + jnp.exp(-jnp.abs(x)))


def _same_block(shape, row_block, col_block):
    r = lax.broadcasted_iota(jnp.int32, shape, 0) >> (row_block.bit_length() - 1)
    c = lax.broadcasted_iota(jnp.int32, shape, 1) >> (col_block.bit_length() - 1)
    return r == c


def _modulate(y, sc, sh):
    rm = sc.shape[0]
    if rm == 1:
        return y * (1.0 + sc) + sh
    rows, d = y.shape
    y3 = y.reshape(rows // rm, rm, d)
    return (y3 * (1.0 + sc)[None] + sh[None]).reshape(rows, d)


def _gate(y, g):
    rm = g.shape[0]
    if rm == 1:
        return y * g
    rows, d = y.shape
    return (y.reshape(rows // rm, rm, d) * g[None]).reshape(rows, d)


def _ada_kernel(c_ref, w_ref, b_ref, o_ref):
    c = c_ref[...]
    s = _silu(c).astype(BF16)
    o_ref[0] = jnp.dot(s, w_ref[0].astype(BF16), preferred_element_type=F32) + b_ref[0]


def ada_modulation(c_all, ada_w, ada_b):
    depth, d, n6 = ada_w.shape
    rows = c_all.shape[0]
    tn = 1536
    return pl.pallas_call(
        _ada_kernel,
        out_shape=jax.ShapeDtypeStruct((depth, rows, n6), F32),
        grid=(depth, n6 // tn),
        in_specs=[pl.BlockSpec((rows, d), lambda l, j: (0, 0)),
                  pl.BlockSpec((1, d, tn), lambda l, j: (l, 0, j)),
                  pl.BlockSpec((1, 1, tn), lambda l, j: (l, 0, j))],
        out_specs=pl.BlockSpec((1, rows, tn), lambda l, j: (l, 0, j)),
        compiler_params=_cparams(("arbitrary", "arbitrary")),
        name="ada_modulation",
    )(c_all, ada_w, ada_b.reshape(depth, 1, n6))


def _pre_kernel(x_ref, sh_ref, sc_ref, g_ref, w_ref, b_ref, o_s5, o_gla, o_conv, o_mlp):
    x = x_ref[0]
    y = _modulate(_rms(x) * g_ref[...], sc_ref[0], sh_ref[0])
    p = jnp.dot(y.astype(BF16), w_ref[...], preferred_element_type=F32) + b_ref[...]
    o_s5[...] = p[:, 0:PW_S5]
    o_gla[...] = p[:, PW_S5:PW_S5 + PW_GLA]
    o_conv[...] = p[:, PW_S5 + PW_GLA:PW_S5 + PW_GLA + PW_CONV]
    o_mlp[...] = p[:, PW_S5 + PW_GLA + PW_CONV:PW_TOTAL]


def pre_mixer(x, sh, sc, g, w, b, tm):
    s, r, d = x.shape
    rm = sh.shape[1]
    widths = (PW_S5, PW_GLA, PW_CONV, PW_MLP)
    return pl.pallas_call(
        _pre_kernel,
        out_shape=[jax.ShapeDtypeStruct((r, s * w_), F32) for w_ in widths],
        grid=(s, r // tm),
        in_specs=[pl.BlockSpec((1, tm, d), lambda b_, i: (b_, i, 0)),
                  pl.BlockSpec((1, rm, d), lambda b_, i: (b_, 0, 0)),
                  pl.BlockSpec((1, rm, d), lambda b_, i: (b_, 0, 0)),
                  pl.BlockSpec((1, d), lambda b_, i: (0, 0)),
                  pl.BlockSpec((d, PW_TOTAL), lambda b_, i: (0, 0)),
                  pl.BlockSpec((1, PW_TOTAL), lambda b_, i: (0, 0))],
        out_specs=[pl.BlockSpec((tm, w_), lambda b_, i: (i, b_)) for w_ in widths],
        compiler_params=_cparams(("arbitrary", "arbitrary")),
        name="pre_mixer",
    )(x, sh, sc, g, w, b)


def _s5_disc_kernel(lr_ref, li_ref, ldt_ref, br_ref, bi_ref, abr_ref, abi_ref, bbr_ref, bbi_ref):
    lr = lr_ref[...]
    li = li_ref[...]
    dt = jnp.exp(ldt_ref[...])
    mag = jnp.exp(lr * dt)
    ang = li * dt
    ab_re = mag * jnp.cos(ang)
    ab_im = mag * jnp.sin(ang)
    den = lr * lr + li * li
    nr = ab_re - 1.0
    f_re = (nr * lr + ab_im * li) / den
    f_im = (ab_im * lr - nr * li) / den
    br = br_ref[...]
    bi = bi_ref[...]
    abr_ref[...] = ab_re
    abi_ref[...] = ab_im
    bbr_ref[...] = f_re * br - f_im * bi
    bbi_ref[...] = f_re * bi + f_im * br


def s5_discretise(a_re, a_im, log_dt, b_re, b_im):
    n = b_re.shape[-1]
    gp = a_re.size
    bc = lambda a: jnp.broadcast_to(a.reshape(gp, 1), (gp, n))
    ldt = jnp.broadcast_to(log_dt[:, None], a_re.shape)
    outs = pl.pallas_call(
        _s5_disc_kernel,
        out_shape=[jax.ShapeDtypeStruct((gp, n), F32)] * 4,
        name="s5_discretise",
    )(bc(a_re), bc(a_im), bc(ldt), b_re.reshape(gp, n), b_im.reshape(gp, n))
    ab_re, ab_im, bb_re, bb_im = outs
    return ab_re[:, 0], ab_im[:, 0], bb_re, bb_im


def _block_diag_in(bb):
    g, p, n = S5_GROUPS, S5_STATE, S5_GROUP
    b3 = bb.reshape(g, p, n)
    eye = jnp.eye(g, dtype=bb.dtype)
    return jnp.einsum('gpn,gh->gnhp', b3, eye).reshape(g * n, g * p)


def _block_diag_out(c):
    g, p, n = S5_GROUPS, S5_STATE, S5_GROUP
    eye = jnp.eye(g, dtype=c.dtype)
    return jnp.einsum('gnp,gh->gphn', c, eye).reshape(g * p, g * n)


def _to_time_major(x_ref, cols, tm_ref, row0, nsl):
    rt = x_ref.shape[0]
    w = x_ref.shape[1] // nsl
    start, width = cols
    for l in range(nsl):
        for h in range(width // LANE):
            c0 = l * w + start + h * LANE
            tm_ref[h, pl.ds(row0 + l, rt, stride=nsl), :] = x_ref[:, c0:c0 + LANE]


def _from_time_major(tm_ref, o_ref, nsl):
    rt = o_ref.shape[0]
    nh = tm_ref.shape[0]
    for l in range(nsl):
        piece = jnp.concatenate([tm_ref[h, pl.ds(l, rt, stride=nsl), :] for h in range(nh)], axis=1)
        o_ref[:, l * nh * LANE:(l + 1) * nh * LANE] = piece.astype(o_ref.dtype)


def _lane_tiles(tm_ref, rows=slice(None)):
    return jnp.concatenate([tm_ref[h, rows, :] for h in range(tm_ref.shape[0])], axis=1)


def _set_lane_tiles(tm_ref, x):
    for h in range(tm_ref.shape[0]):
        tm_ref[h] = x[:, h * LANE:(h + 1) * LANE]


def _s5_kernel(u_ref, h0_ref, bblk_ref, cre_ref, cim_ref, ar_ref, ai_ref, d_ref, wglu_ref, bglu_ref, mg_ref,
               o_ref, hT_ref, xs_ref, hs_ref, tm_ref, *, nb, tc, nsl):
    i = pl.program_id(0)

    @pl.when(i == 0)
    def _():
        hs_ref[...] = h0_ref[...]

    _to_time_major(u_ref, (0, S5_WIDTH), tm_ref, 0, nsl)
    u = _lane_tiles(tm_ref)
    xs_ref[...] = jnp.dot(u.astype(BF16), bblk_ref[...], preferred_element_type=F32)
    ar = jnp.broadcast_to(ar_ref[...], (nb, S5_LANES))
    ai = jnp.broadcast_to(ai_ref[...], (nb, S5_LANES))

    def step(t, carry):
        hr, hi = carry
        row = pl.multiple_of(t * nb, nb)
        xr = xs_ref[pl.ds(row, nb), 0:S5_LANES]
        xi = xs_ref[pl.ds(row, nb), S5_LANES:2 * S5_LANES]
        nr = ar * hr - ai * hi + xr
        ni = ar * hi + ai * hr + xi
        xs_ref[pl.ds(row, nb), 0:S5_LANES] = nr
        xs_ref[pl.ds(row, nb), S5_LANES:2 * S5_LANES] = ni
        return nr, ni

    hr, hi = lax.fori_loop(0, tc, step, (hs_ref[:, 0:S5_LANES], hs_ref[:, S5_LANES:2 * S5_LANES]),
                           unroll=True if tc <= 8 else 4)
    hs_ref[:, 0:S5_LANES] = hr
    hs_ref[:, S5_LANES:2 * S5_LANES] = hi

    y = (jnp.dot(xs_ref[:, 0:S5_LANES].astype(BF16), cre_ref[...], preferred_element_type=F32)
         - jnp.dot(xs_ref[:, S5_LANES:2 * S5_LANES].astype(BF16), cim_ref[...], preferred_element_type=F32))
    y = y + d_ref[...] * u
    y = _gelu_tanh(y)
    y = y * jax.nn.sigmoid(jnp.dot(y.astype(BF16), wglu_ref[...], preferred_element_type=F32) + bglu_ref[...])
    _set_lane_tiles(tm_ref, _rms(y) * mg_ref[...])
    _from_time_major(tm_ref, o_ref, nsl)

    @pl.when(i == pl.num_programs(0) - 1)
    def _():
        hT_ref[...] = hs_ref[...]


def s5_mixer(u, h0, bblk, cre, cim, ar, ai, d, wglu, bglu, mg, nb, tc, nsl):
    rows = u.shape[0]
    rc = nb * tc
    rt = rc // nsl
    full = lambda shape: pl.BlockSpec(shape, lambda i: (0,) * len(shape))
    return pl.pallas_call(
        functools.partial(_s5_kernel, nb=nb, tc=tc, nsl=nsl),
        out_shape=[jax.ShapeDtypeStruct((rows, nsl * S5_WIDTH), BF16),
                   jax.ShapeDtypeStruct((nb, 2 * S5_LANES), F32)],
        grid=(rows // rt,),
        in_specs=[pl.BlockSpec((rt, nsl * S5_WIDTH), lambda i: (i, 0)),
                  full((nb, 2 * S5_LANES)),
                  full((S5_WIDTH, 2 * S5_LANES)),
                  full((S5_LANES, S5_WIDTH)), full((S5_LANES, S5_WIDTH)),
                  full((1, S5_LANES)), full((1, S5_LANES)),
                  full((1, S5_WIDTH)), full((S5_WIDTH, S5_WIDTH)), full((1, S5_WIDTH)), full((1, S5_WIDTH))],
        out_specs=[pl.BlockSpec((rt, nsl * S5_WIDTH), lambda i: (i, 0)),
                   full((nb, 2 * S5_LANES))],
        scratch_shapes=[pltpu.VMEM((rc, 2 * S5_LANES), F32), pltpu.VMEM((nb, 2 * S5_LANES), F32),
                        pltpu.VMEM((S5_WIDTH // LANE, rc, LANE), F32)],
        compiler_params=_cparams(("arbitrary",)),
        name="s5_mixer",
    )(u, h0, bblk, cre, cim, ar, ai, d, wglu, bglu, mg)


CONV_ROWS = 64


def _conv_kernel(ag_ref, c0_ref, wdw_ref, bdw_ref, lng_ref, lnb_ref, wpw_ref, bpw_ref, mg_ref,
                 o_ref, buf_ref, zc_ref, y_ref, *, nb, tc, nsl):
    i = pl.program_id(0)
    hist = CONV_HIST * nb
    rc = nb * tc
    rt = rc // nsl
    n_lt = CONV_DIM // LANE

    @pl.when(i == 0)
    def _():
        for h in range(n_lt):
            zc_ref[h, 0:hist, :] = c0_ref[:, h * LANE:(h + 1) * LANE]

    @pl.when(i > 0)
    def _():
        for h in range(n_lt):
            zc_ref[h, 0:hist, :] = zc_ref[h, rc:rc + hist, :]

    w_seq = 2 * CONV_DIM
    for l in range(nsl):
        a = ag_ref[:, l * w_seq:l * w_seq + CONV_DIM]
        g = ag_ref[:, l * w_seq + CONV_DIM:(l + 1) * w_seq]
        z = a * jax.nn.sigmoid(g)
        for h in range(n_lt):
            zc_ref[h, pl.ds(hist + l, rt, stride=nsl), :] = z[:, h * LANE:(h + 1) * LANE]

    w = wdw_ref[...]

    def tile(j, carry):
        r0 = pl.multiple_of(j * CONV_ROWS, CONV_ROWS)
        for h in range(n_lt):
            acc = jnp.zeros((CONV_ROWS, LANE), F32)
            for k in range(CONV_WIDTH):
                acc = acc + w[k:k + 1, h * LANE:(h + 1) * LANE] * zc_ref[h, pl.ds(r0 + k * nb, CONV_ROWS), :]
            y_ref[h, pl.ds(r0, CONV_ROWS), :] = acc
        return carry

    lax.fori_loop(0, rc // CONV_ROWS, tile, 0)
    y = _lane_tiles(y_ref) + bdw_ref[...]
    y = _silu(_layernorm(y, lng_ref[...], lnb_ref[...]))
    y = jnp.dot(y.astype(BF16), wpw_ref[...], preferred_element_type=F32) + bpw_ref[...]
    _set_lane_tiles(y_ref, _rms(y) * mg_ref[...])
    _from_time_major(y_ref, o_ref, nsl)

    @pl.when(i == pl.num_programs(0) - 1)
    def _():
        buf_ref[...] = _lane_tiles(zc_ref, slice(rc, rc + hist))


def conv_mixer(ag, c0, wdw, bdw, lng, lnb, wpw, bpw, mg, nb, tc, nsl):
    rows = ag.shape[0]
    rc = nb * tc
    rt = rc // nsl
    hist = CONV_HIST * nb
    assert rows == rt or tc >= CONV_HIST
    n_lt = CONV_DIM // LANE
    full = lambda shape: pl.BlockSpec(shape, lambda i: (0,) * len(shape))
    return pl.pallas_call(
        functools.partial(_conv_kernel, nb=nb, tc=tc, nsl=nsl),
        out_shape=[jax.ShapeDtypeStruct((rows, nsl * CONV_DIM), BF16),
                   jax.ShapeDtypeStruct((hist, CONV_DIM), F32)],
        grid=(rows // rt,),
        in_specs=[pl.BlockSpec((rt, nsl * 2 * CONV_DIM), lambda i: (i, 0)),
                  full((hist, CONV_DIM)), full((CONV_WIDTH, CONV_DIM)),
                  full((1, CONV_DIM)), full((1, CONV_DIM)), full((1, CONV_DIM)),
                  full((CONV_DIM, CONV_DIM)), full((1, CONV_DIM)), full((1, CONV_DIM))],
        out_specs=[pl.BlockSpec((rt, nsl * CONV_DIM), lambda i: (i, 0)), full((hist, CONV_DIM))],
        scratch_shapes=[pltpu.VMEM((n_lt, hist + rc, LANE), F32), pltpu.VMEM((n_lt, rc, LANE), F32)],
        compiler_params=_cparams(("arbitrary",)),
        name="conv_mixer",
    )(ag, c0, wdw, bdw, lng, lnb, wpw, bpw, mg)


def _gmlp_seq_kernel(uv_ref, lng_ref, lnb_ref, wcat_ref, bias_ref, mg_ref, o_ref, *, tt):
    n_chunks = tt // GMLP_CHUNK
    kc = GMLP_HEADS * GMLP_CHUNK
    rowi = lax.broadcasted_iota(jnp.int32, (GMLP_CHUNK, kc), 0)
    coli = lax.broadcasted_iota(jnp.int32, (GMLP_CHUNK, kc), 1)
    wcat = jnp.where((coli & (GMLP_CHUNK - 1)) <= rowi, wcat_ref[...], 0.0).astype(BF16)
    sel = _same_block((kc, GMLP_WIDTH), GMLP_CHUNK, GMLP_HEAD_DIM)
    for c in range(n_chunks):
        rows = slice(c * GMLP_CHUNK, (c + 1) * GMLP_CHUNK)
        u = uv_ref[rows, 0:GMLP_WIDTH]
        v = uv_ref[rows, GMLP_WIDTH:2 * GMLP_WIDTH]
        vn = _layernorm(v, lng_ref[...], lnb_ref[...])
        vbd = jnp.where(sel, jnp.concatenate([vn] * GMLP_HEADS, axis=0), 0.0).astype(BF16)
        mixed = jnp.dot(wcat, vbd, preferred_element_type=F32) + bias_ref[...]
        o_ref[rows, :] = (_rms(u * mixed) * mg_ref[...]).astype(o_ref.dtype)


def gmlp_seq(uv, nseq, lng, lnb, wcat, bias, mg, tt):
    t = uv.shape[0]
    full = lambda shape: pl.BlockSpec(shape, lambda b_, i: (0,) * len(shape))
    return pl.pallas_call(
        functools.partial(_gmlp_seq_kernel, tt=tt),
        out_shape=jax.ShapeDtypeStruct((t, nseq * GMLP_WIDTH), BF16),
        grid=(nseq, t // tt),
        in_specs=[pl.BlockSpec((tt, 2 * GMLP_WIDTH), lambda b_, i: (i, b_)),
                  full((1, GMLP_WIDTH)), full((1, GMLP_WIDTH)),
                  full((GMLP_CHUNK, GMLP_HEADS * GMLP_CHUNK)), full((GMLP_CHUNK, GMLP_WIDTH)),
                  full((1, GMLP_WIDTH))],
        out_specs=pl.BlockSpec((tt, GMLP_WIDTH), lambda b_, i: (i, b_)),
        compiler_params=_cparams(("arbitrary", "arbitrary")),
        name="gmlp_seq",
    )(uv, lng, lnb, wcat, bias, mg)


def _gmlp_short_kernel(uv_ref, lng_ref, lnb_ref, wrow_ref, brow_ref, mg_ref, o_ref, vn_ref, *, nb, t_len):
    u = uv_ref[:, 0:GMLP_WIDTH]
    v = uv_ref[:, GMLP_WIDTH:2 * GMLP_WIDTH]
    vn = _layernorm(v, lng_ref[...], lnb_ref[...])
    vn_ref[...] = vn
    wrow = wrow_ref[...]
    brow = brow_ref[...]
    for t in range(t_len):
        mixed = jnp.zeros((nb, GMLP_WIDTH), F32) + brow[t:t + 1, :]
        for j in range(t + 1):
            mixed = mixed + wrow[t * t_len + j:t * t_len + j + 1, :] * vn[j * nb:(j + 1) * nb, :]
        o = u[t * nb:(t + 1) * nb, :] * mixed
        o_ref[t * nb:(t + 1) * nb, :] = (_rms(o) * mg_ref[...]).astype(o_ref.dtype)


def gmlp_short(uv, lng, lnb, wrow, brow, mg, nb, t_len):
    rows = uv.shape[0]
    return pl.pallas_call(
        functools.partial(_gmlp_short_kernel, nb=nb, t_len=t_len),
        out_shape=[jax.ShapeDtypeStruct((rows, GMLP_WIDTH), BF16),
                   jax.ShapeDtypeStruct((rows, GMLP_WIDTH), F32)],
        compiler_params=pltpu.CompilerParams(vmem_limit_bytes=VMEM_LIMIT),
        name="gmlp_short",
    )(uv, lng, lnb, wrow, brow, mg)


def _split3(x):
    a = x.astype(BF16)
    r1 = x - a.astype(F32)
    b = r1.astype(BF16)
    c = (r1 - b.astype(F32)).astype(BF16)
    return a, b, c


def _dot_exact_rhs(x, m):
    return sum(jnp.dot(t, m, preferred_element_type=F32) for t in _split3(x))


def _dot_exact_lhs(m, x):
    return sum(jnp.dot(m, t, preferred_element_type=F32) for t in _split3(x))


def _gla_tail(o, r, gmean, onorm, mg):
    ms = _dot_exact_rhs(o * o, gmean)
    o = o * lax.rsqrt(ms + EPS) * onorm
    o = o * _silu(r)
    return _rms(o) * mg


def _head_mean_matrix():
    return jnp.where(_same_block((GLA_WIDTH, GLA_WIDTH), GLA_DV, GLA_DV), 1.0 / GLA_DV, 0.0).astype(BF16)


GLA_SEQS = 8


def _gla_seq_kernel(x_ref, s0_ref, wg_ref, bg_ref, onorm_ref, mg_ref, o_ref, sT_ref,
                    s_ref, qt_ref, kt_ref, kd_ref, dl_ref, oacc_ref, *, tt):
    i = pl.program_id(1)
    L = GLA_CHUNK
    n_ch = tt // L
    kw, vw = GLA_KEY_WIDTH, GLA_WIDTH

    @pl.when(i == 0)
    def _():
        s_ref[...] = s0_ref[...]

    rows_i = lax.broadcasted_iota(jnp.int32, (tt, tt), 0)
    cols_i = lax.broadcasted_iota(jnp.int32, (tt, tt), 1)
    shift = L.bit_length() - 1
    same_chunk = (rows_i >> shift) == (cols_i >> shift)
    tri = jnp.logical_and(same_chunk, cols_i <= rows_i).astype(BF16)
    chunk_sum = same_chunk.astype(BF16)
    chunk_rows = (lax.broadcasted_iota(jnp.int32, (n_ch, tt), 0)
                  == (lax.broadcasted_iota(jnp.int32, (n_ch, tt), 1) >> shift)).astype(BF16)
    kbd_sel = _same_block((GLA_HEADS * L, kw), L, GLA_DK)
    vbd_sel = _same_block((GLA_HEADS * L, vw), L, GLA_DV)
    causal = ((lax.broadcasted_iota(jnp.int32, (L, GLA_HEADS * L), 1) & (L - 1))
              <= lax.broadcasted_iota(jnp.int32, (L, GLA_HEADS * L), 0))
    s_sel = _same_block((vw, kw), GLA_DV, GLA_DK)
    gmean = _head_mean_matrix()
    scale = GLA_DK ** -0.5
    nt_dims = (((1,), (1,)), ((), ()))
    tn_dims = (((0,), (0,)), ((), ()))
    zero = jnp.zeros((), BF16)

    for g in range(GLA_SEQS):
        x0 = g * PW_GLA
        q = x_ref[:, x0:x0 + kw] * scale
        k = x_ref[:, x0 + kw:x0 + 2 * kw]
        gl = x_ref[:, x0 + 2 * kw + 2 * vw:x0 + PW_GLA]
        la = _log_sigmoid(jnp.dot(gl.astype(BF16), wg_ref[...], preferred_element_type=F32) + bg_ref[...])
        la = la / GLA_TAU
        bc = _dot_exact_lhs(tri, la)
        b_end = _dot_exact_lhs(chunk_sum, la)
        qt_ref[g] = (q * jnp.exp(bc)).astype(BF16)
        kt_ref[g] = (k * jnp.exp(-bc)).astype(BF16)
        kd_ref[g] = (k * jnp.exp(b_end - bc)).astype(BF16)
        dl_ref[g] = jnp.exp(_dot_exact_lhs(chunk_rows, la))

    def chunk(c, carry):
        r0 = pl.multiple_of(c * L, L)
        for g in range(GLA_SEQS):
            x0 = g * PW_GLA
            qt = qt_ref[g, pl.ds(r0, L), :]
            kt = kt_ref[g, pl.ds(r0, L), :]
            kdec = kd_ref[g, pl.ds(r0, L), :]
            vb = x_ref[pl.ds(r0, L), x0 + 2 * kw:x0 + 2 * kw + vw].astype(BF16)
            kbd = jnp.where(kbd_sel, jnp.concatenate([kt] * GLA_HEADS, axis=0), zero)
            att = lax.dot_general(qt, kbd, nt_dims, preferred_element_type=F32)
            att = jnp.where(causal, att, 0.0).astype(BF16)
            vbd = jnp.where(vbd_sel, jnp.concatenate([vb] * GLA_HEADS, axis=0), zero)
            st = s_ref[g]
            oacc_ref[g, pl.ds(r0, L), :] = (jnp.dot(att, vbd, preferred_element_type=F32)
                                            + lax.dot_general(qt, st.astype(BF16), nt_dims,
                                                              preferred_element_type=F32))
            upd = lax.dot_general(vb, kdec, tn_dims, preferred_element_type=F32)
            s_ref[g] = st * dl_ref[g, pl.ds(c, 1), :] + jnp.where(s_sel, upd, 0.0)
        return carry

    lax.fori_loop(0, n_ch, chunk, 0)

    for g in range(GLA_SEQS):
        x0 = g * PW_GLA
        r = x_ref[:, x0 + 2 * kw + vw:x0 + 2 * kw + 2 * vw]
        o_ref[:, g * vw:(g + 1) * vw] = _gla_tail(oacc_ref[g], r, gmean, onorm_ref[...],
                                                  mg_ref[...]).astype(o_ref.dtype)

    @pl.when(i == pl.num_programs(1) - 1)
    def _():
        sT_ref[...] = s_ref[...]


def gla_seq(x, nseq, s0, wg, bg, onorm, mg, tt):
    t = x.shape[0]
    g = GLA_SEQS
    assert nseq % g == 0
    full = lambda shape: pl.BlockSpec(shape, lambda b_, i: (0,) * len(shape))
    state = pl.BlockSpec((g, GLA_WIDTH, GLA_KEY_WIDTH), lambda b_, i: (b_, 0, 0))
    return pl.pallas_call(
        functools.partial(_gla_seq_kernel, tt=tt),
        out_shape=[jax.ShapeDtypeStruct((t, nseq * GLA_WIDTH), BF16),
                   jax.ShapeDtypeStruct((nseq, GLA_WIDTH, GLA_KEY_WIDTH), F32)],
        grid=(nseq // g, t // tt),
        in_specs=[pl.BlockSpec((tt, g * PW_GLA), lambda b_, i: (i, b_)),
                  state,
                  full((LANE, GLA_KEY_WIDTH)), full((1, GLA_KEY_WIDTH)),
                  full((1, GLA_WIDTH)), full((1, GLA_WIDTH))],
        out_specs=[pl.BlockSpec((tt, g * GLA_WIDTH), lambda b_, i: (i, b_)), state],
        scratch_shapes=[pltpu.VMEM((g, GLA_WIDTH, GLA_KEY_WIDTH), F32),
                        pltpu.VMEM((g, tt, GLA_KEY_WIDTH), BF16), pltpu.VMEM((g, tt, GLA_KEY_WIDTH), BF16),
                        pltpu.VMEM((g, tt, GLA_KEY_WIDTH), BF16),
                        pltpu.VMEM((g, tt // GLA_CHUNK, GLA_KEY_WIDTH), F32),
                        pltpu.VMEM((g, tt, GLA_WIDTH), F32)],
        compiler_params=_cparams(("arbitrary", "arbitrary")),
        name="gla_seq",
    )(x, s0, wg, bg, onorm, mg)


def _gla_rec_kernel(x_ref, s0_ref, ek_ref, ev_ref, wg_ref, bg_ref, onorm_ref, mg_ref, o_ref, sT_ref,
                    *, nb, t_len):
    kw, vw = GLA_KEY_WIDTH, GLA_WIDTH
    hl = GLA_DK * GLA_DV
    sT_ref[...] = s0_ref[...]
    gmean = _head_mean_matrix()
    scale = GLA_DK ** -0.5

    def step(t, carry):
        r0 = pl.multiple_of(t * nb, nb)
        q = x_ref[pl.ds(r0, nb), 0:kw] * scale
        k = x_ref[pl.ds(r0, nb), kw:2 * kw]
        v = x_ref[pl.ds(r0, nb), 2 * kw:2 * kw + vw]
        r = x_ref[pl.ds(r0, nb), 2 * kw + vw:2 * kw + 2 * vw]
        gl = x_ref[pl.ds(r0, nb), 2 * kw + 2 * vw:2 * kw + 2 * vw + LANE]
        la = _log_sigmoid(jnp.dot(gl.astype(BF16), wg_ref[...], preferred_element_type=F32) + bg_ref[...])
        a = jnp.exp(la / GLA_TAU)
        a3 = _split3(a)
        qb = q.astype(BF16)
        kb = k.astype(BF16)
        vb = v.astype(BF16)
        outs = []
        for h in range(GLA_HEADS):
            lanes = slice(h * hl, (h + 1) * hl)
            ek = ek_ref[:, lanes]
            a_e = (jnp.dot(a3[0], ek, preferred_element_type=F32)
                   + jnp.dot(a3[1], ek, preferred_element_type=F32)
                   + jnp.dot(a3[2], ek, preferred_element_type=F32))
            k_e = jnp.dot(kb, ek, preferred_element_type=F32)
            q_e = jnp.dot(qb, ek, preferred_element_type=F32)
            v_e = jnp.dot(vb, ev_ref[:, lanes], preferred_element_type=F32)
            s_new = a_e * sT_ref[:, lanes] + k_e * v_e
            sT_ref[:, lanes] = s_new
            prod = q_e * s_new
            acc = prod[:, 0:LANE]
            for j in range(1, hl // LANE):
                acc = acc + prod[:, j * LANE:(j + 1) * LANE]
            outs.append(acc[:, 0:GLA_DV] + acc[:, GLA_DV:2 * GLA_DV])
        o = jnp.concatenate(outs, axis=1)
        o_ref[pl.ds(r0, nb), :] = _gla_tail(o, r, gmean, onorm_ref[...], mg_ref[...]).astype(o_ref.dtype)
        return carry

    lax.fori_loop(0, t_len, step, 0)


def gla_recurrent(x, s0, ek, ev, wg, bg, onorm, mg, nb, t_len):
    rows = x.shape[0]
    return pl.pallas_call(
        functools.partial(_gla_rec_kernel, nb=nb, t_len=t_len),
        out_shape=[jax.ShapeDtypeStruct((rows, GLA_WIDTH), BF16),
                   jax.ShapeDtypeStruct((nb, GLA_STATE_LANES), F32)],
        compiler_params=pltpu.CompilerParams(vmem_limit_bytes=VMEM_LIMIT),
        name="gla_recurrent",
    )(x, s0, ek, ev, wg, bg, onorm, mg)


def _gla_expanders():
    lane = jnp.arange(GLA_STATE_LANES)
    h = lane // (GLA_DK * GLA_DV)
    dk = (lane // GLA_DV) % GLA_DK
    dv = lane % GLA_DV
    ek = (jnp.arange(GLA_KEY_WIDTH)[:, None] == (h * GLA_DK + dk)[None, :]).astype(BF16)
    ev = (jnp.arange(GLA_WIDTH)[:, None] == (h * GLA_DV + dv)[None, :]).astype(BF16)
    return ek, ev


def _mix_residual(x_ref, m_refs, g1_ref, wout_ref):
    mix = jnp.concatenate([m[...] for m in m_refs], axis=1)
    proj = jnp.dot(mix, wout_ref[...], preferred_element_type=F32)
    return x_ref[0] + _gate(proj, g1_ref[0])


def _swiglu(h, wg_ref, wu_ref, wd_ref, tf, lead=(), between=None):
    ff = wg_ref.shape[-1]
    n_dots = 3 * (ff // tf)
    tick = (lambda i: between(i, n_dots)) if between is not None else (lambda i: None)
    acc = jnp.zeros((h.shape[0], wd_ref.shape[-1]), F32)
    for c in range(ff // tf):
        cols = slice(c * tf, (c + 1) * tf)
        gate = jnp.dot(h, wg_ref[(*lead, slice(None), cols)], preferred_element_type=F32)
        tick(3 * c)
        up = jnp.dot(h, wu_ref[(*lead, slice(None), cols)], preferred_element_type=F32)
        tick(3 * c + 1)
        acc = acc + jnp.dot((_silu(gate) * up).astype(BF16), wd_ref[(*lead, cols, slice(None))],
                            preferred_element_type=F32)
        tick(3 * c + 2)
    return acc


def _post_dense_kernel(x_ref, m0_ref, m1_ref, m2_ref, m3_ref, g1_ref, sh2_ref, sc2_ref, g2_ref, ng_ref, wout_ref,
                       wg_ref, wu_ref, wd_ref, fg_ref, o_ref, *, final_norm, tf):
    x1 = _mix_residual(x_ref, (m0_ref, m1_ref, m2_ref, m3_ref), g1_ref, wout_ref)
    h = _modulate(_rms(x1) * ng_ref[...], sc2_ref[0], sh2_ref[0]).astype(BF16)
    x2 = x1 + _gate(_swiglu(h, wg_ref, wu_ref, wd_ref, tf), g2_ref[0])
    if final_norm:
        x2 = _rms(x2) * fg_ref[...]
    o_ref[0] = x2


def post_dense(x, mixes, g1, sh2, sc2, g2, ng, wout, wg, wu, wd, fg, tm, tf, final_norm):
    s, r, d = x.shape
    rm = g1.shape[1]
    ff = wg.shape[1]
    mod = pl.BlockSpec((1, rm, d), lambda b_, i: (b_, 0, 0))
    const = lambda shape: pl.BlockSpec(shape, lambda b_, i: (0,) * len(shape))
    resident = lambda shape: pl.BlockSpec(shape, lambda b_, i: (0,) * len(shape), pipeline_mode=pl.Buffered(1))
    mixspec = pl.BlockSpec((tm, 256), lambda b_, i: (i, b_))
    return pl.pallas_call(
        functools.partial(_post_dense_kernel, final_norm=final_norm, tf=tf),
        out_shape=jax.ShapeDtypeStruct((s, r, d), F32),
        grid=(s, r // tm),
        in_specs=[pl.BlockSpec((1, tm, d), lambda b_, i: (b_, i, 0)),
                  mixspec, mixspec, mixspec, mixspec,
                  mod, mod, mod, mod,
                  const((1, d)), resident((d, d)),
                  resident((d, ff)), resident((d, ff)), resident((ff, d)),
                  const((1, d))],
        out_specs=pl.BlockSpec((1, tm, d), lambda b_, i: (b_, i, 0)),
        compiler_params=_cparams(("arbitrary", "arbitrary")),
        name="post_dense",
    )(x, *mixes, g1, sh2, sc2, g2, ng, wout, wg, wu, wd, fg)


ROW_TILE = 8


def _store_row_tiles(ref, x, lead=()):
    rows = x.shape[0]
    for s in range(ROW_TILE):
        ref[(*lead, pl.ds(s, rows, stride=ROW_TILE), slice(None))] = x[:, s * LANE:(s + 1) * LANE]


def _load_row_tiles(ref, rows, lead=()):
    return jnp.concatenate([ref[(*lead, pl.ds(s, rows, stride=ROW_TILE), slice(None))] for s in range(ROW_TILE)],
                           axis=1)


def _route_kernel(x_ref, m0_ref, m1_ref, m2_ref, m3_ref, g1_ref, sh2_ref, sc2_ref, ng_ref, wout_ref, router_ref,
                  *rest):
    x1_ref, h2_ref, route_ref = rest[-3:]
    x1 = _mix_residual(x_ref, (m0_ref, m1_ref, m2_ref, m3_ref), g1_ref, wout_ref)
    x1_ref[0] = x1
    h = _modulate(_rms(x1) * ng_ref[...], sc2_ref[0], sh2_ref[0])
    _store_row_tiles(h2_ref, h)
    h_hi = h.astype(BF16)
    h_lo = (h - h_hi.astype(F32)).astype(BF16)
    w = router_ref[...]
    w_hi = w.astype(BF16)
    w_lo = (w - w_hi.astype(F32)).astype(BF16)
    logits = (jnp.dot(h_hi, w_hi, preferred_element_type=F32) + jnp.dot(h_lo, w_hi, preferred_element_type=F32)
              + jnp.dot(h_hi, w_lo, preferred_element_type=F32))
    lane = lax.broadcasted_iota(jnp.int32, logits.shape, 1).astype(F32)
    neg = jnp.float32(-jnp.inf)
    logits = jnp.where(lane < N_EXPERTS, logits, neg)
    m1 = jnp.max(logits, axis=1, keepdims=True)
    i1 = jnp.min(jnp.where(logits == m1, lane, float(LANE)), axis=1, keepdims=True)
    others = jnp.where(lane == i1, neg, logits)
    m2 = jnp.max(others, axis=1, keepdims=True)
    i2 = jnp.min(jnp.where(others == m2, lane, float(LANE)), axis=1, keepdims=True)
    e2 = jnp.exp(m2 - m1)
    den = 1.0 + e2
    route_ref[...] = (jnp.where(lane == 0.0, i1, 0.0) + jnp.where(lane == 1.0, i2, 0.0)
                      + jnp.where(lane == 2.0, 1.0 / den, 0.0) + jnp.where(lane == 3.0, e2 / den, 0.0))


def moe_route(x, mixes, g1, sh2, sc2, ng, wout, router, tm, row0, shared):
    s, r, d = x.shape
    rm = g1.shape[1]
    nt = r // tm
    blk0 = row0 // tm
    n_total = shared[1].shape[0]
    mod = pl.BlockSpec((1, rm, d), lambda b_, i: (b_, 0, 0))
    const = lambda shape: pl.BlockSpec(shape, lambda b_, i: (0,) * len(shape))
    mixspec = pl.BlockSpec((tm, 256), lambda b_, i: (i, b_))
    in_specs = [pl.BlockSpec((1, tm, d), lambda b_, i: (b_, i, 0)),
                mixspec, mixspec, mixspec, mixspec, mod, mod, mod,
                const((1, d)), const((d, d)), const((d, LANE))]
    args = [x, *mixes, g1, sh2, sc2, ng, wout, router]
    in_specs += [pl.BlockSpec(memory_space=pl.ANY), pl.BlockSpec(memory_space=pl.ANY)]
    aliases = {len(args): 1, len(args) + 1: 2}
    args += list(shared)
    return pl.pallas_call(
        _route_kernel,
        out_shape=[jax.ShapeDtypeStruct((s, r, d), F32),
                   jax.ShapeDtypeStruct((n_total * ROW_TILE, LANE), F32),
                   jax.ShapeDtypeStruct((n_total, LANE), F32)],
        grid=(s, nt),
        in_specs=in_specs,
        out_specs=[pl.BlockSpec((1, tm, d), lambda b_, i: (b_, i, 0)),
                   pl.BlockSpec((tm * ROW_TILE, LANE), lambda b_, i: (blk0 + b_ * nt + i, 0)),
                   pl.BlockSpec((tm, LANE), lambda b_, i: (blk0 + b_ * nt + i, 0))],
        input_output_aliases=aliases,
        compiler_params=_cparams(("arbitrary", "arbitrary")),
        name="moe_route",
    )(*args)


def _route_tables(route, tg, n_tiles):
    n_total = route.shape[0]
    flat_e = route[:, 0:2].astype(jnp.int32).reshape(-1)
    keys = jnp.concatenate([flat_e, jnp.full((tg,), N_EXPERTS, jnp.int32)])
    order = jnp.argsort(keys, stable=True).astype(jnp.int32)
    counts = jnp.sum(flat_e[:, None] == jnp.arange(N_EXPERTS, dtype=jnp.int32)[None, :], axis=0).astype(jnp.int32)
    tiles_per = (counts + tg - 1) // tg
    tile_end = jnp.cumsum(tiles_per)
    n_used = tile_end[-1]
    tile_id = jnp.arange(n_tiles, dtype=jnp.int32)
    tile_ok = tile_id < n_used
    tile_e = jnp.sum(jnp.minimum(tile_id, n_used - 1)[:, None] >= tile_end[None, :], axis=1).astype(jnp.int32)
    sort_start = jnp.cumsum(counts) - counts
    done = (tile_id - (tile_end - tiles_per)[tile_e]) * tg
    n_valid = jnp.where(tile_ok, jnp.clip(counts[tile_e] - done, 0, tg), 0).astype(jnp.int32)
    tile_start = jnp.where(tile_ok, sort_start[tile_e] + done, 0).astype(jnp.int32)
    real = order < 2 * n_total
    src = jnp.where(real, (order >> 1) * ROW_TILE, 0)
    dst = jnp.where(real, ((order & 1) * n_total + (order >> 1)) * ROW_TILE, 0)
    return tile_e, n_valid, tile_start, src, dst


DMA_UNROLL = 8


def _experts_kernel(te_ref, nv_ref, ts_ref, src_ref, dst_ref, h2_hbm, wg_ref, wu_ref, wd_ref, out_hbm,
                    xbuf, obuf, gsem, ssem, *, tg, n_tiles, tf):
    j = pl.program_id(0)
    slot = lax.rem(j, 2)
    other = 1 - slot
    ok = nv_ref[j] > 0

    def row_tile(buf, s_, r):
        start = r * ROW_TILE if isinstance(r, int) else pl.multiple_of(r * ROW_TILE, ROW_TILE)
        return buf.at[s_, pl.ds(start, ROW_TILE), :]

    def gather_row(tile, s_, r, priority=0):
        row = pl.multiple_of(src_ref[ts_ref[tile] + r], ROW_TILE)
        pltpu.make_async_copy(h2_hbm.at[pl.ds(row, ROW_TILE), :], row_tile(xbuf, s_, r),
                              gsem.at[s_]).start(priority=priority)

    def scatter_row(tile, s_, r, priority=0):
        row = pl.multiple_of(dst_ref[ts_ref[tile] + r], ROW_TILE)
        pltpu.make_async_copy(row_tile(obuf, s_, r), out_hbm.at[pl.ds(row, ROW_TILE), :],
                              ssem.at[s_]).start(priority=priority)

    def full_tile(issue_row, tile, s_):
        def body(r8, c):
            for u in range(DMA_UNROLL):
                issue_row(tile, s_, r8 * DMA_UNROLL + u, priority=u % 2)
            return c

        lax.fori_loop(0, tg // DMA_UNROLL, body, 0)

    def gather(tile, s_):
        full_tile(gather_row, tile, s_)

    def scatter(tile, s_):
        n = nv_ref[tile]

        @pl.when(n == tg)
        def _():
            full_tile(scatter_row, tile, s_)

        @pl.when(n < tg)
        def _():
            def body(r, c):
                scatter_row(tile, s_, r)
                return c

            lax.fori_loop(0, n, body, 0)

    def wait_all(buf, sem, s_):
        pltpu.make_async_copy(buf.at[s_], buf.at[s_], sem.at[s_]).wait()

    def wait_scatter(tile, s_):
        n = nv_ref[tile]

        @pl.when(n == tg)
        def _():
            wait_all(obuf, ssem, s_)

        @pl.when(n < tg)
        def _():
            def body(r, c):
                pltpu.make_async_copy(obuf.at[s_, pl.ds(0, ROW_TILE), :], out_hbm.at[pl.ds(0, ROW_TILE), :],
                                      ssem.at[s_]).wait()
                return c

            lax.fori_loop(0, n, body, 0)

    @pl.when(jnp.logical_and(j == 0, ok))
    def _():
        gather(0, 0)

    @pl.when(jnp.logical_or(jnp.logical_and(j == 0, ok), nv_ref[jnp.maximum(j - 1, 0)] * jnp.minimum(j, 1) > 0))
    def _():
        wait_all(xbuf, gsem, slot)

    @pl.when(j >= 2)
    def _():
        wait_scatter(j - 2, slot)

    nxt = jnp.minimum(j + 1, n_tiles - 1)

    @pl.when(ok)
    def _():
        x = _load_row_tiles(xbuf, tg, lead=(slot,)).astype(BF16)

        def gather_some(i, n):
            for r in range(i * tg // n, (i + 1) * tg // n):
                gather_row(nxt, other, r, priority=r % 2)

        y = _swiglu(x, wg_ref, wu_ref, wd_ref, tf, lead=(0,), between=gather_some)
        _store_row_tiles(obuf, y, lead=(slot,))
        scatter(j, slot)

    @pl.when(j == n_tiles - 1)
    def _():
        @pl.when(ok)
        def _():
            wait_all(xbuf, gsem, other)

        wait_scatter(j - 1, other)
        wait_scatter(j, slot)


def moe_experts(h2, tables, wg, wu, wd, tg, n_tiles, tf):
    n_exp, d, ff = wg.shape
    assert d == ROW_TILE * LANE and n_tiles >= 2
    tile_e, n_valid, tile_start, src, dst = tables
    wspec = lambda shape: pl.BlockSpec(shape, lambda j, te, *_: (te[j], 0, 0))
    grid_spec = pltpu.PrefetchScalarGridSpec(
        num_scalar_prefetch=5,
        grid=(n_tiles,),
        in_specs=[pl.BlockSpec(memory_space=pl.ANY), wspec((1, d, ff)), wspec((1, d, ff)), wspec((1, ff, d))],
        out_specs=pl.BlockSpec(memory_space=pl.ANY),
        scratch_shapes=[pltpu.VMEM((2, tg * ROW_TILE, LANE), F32), pltpu.VMEM((2, tg * ROW_TILE, LANE), F32),
                        pltpu.SemaphoreType.DMA((2,)), pltpu.SemaphoreType.DMA((2,))])
    return pl.pallas_call(
        functools.partial(_experts_kernel, tg=tg, n_tiles=n_tiles, tf=tf),
        out_shape=jax.ShapeDtypeStruct((2 * h2.shape[0], LANE), F32),
        grid_spec=grid_spec,
        compiler_params=_cparams(("arbitrary",)),
        name="moe_experts",
    )(tile_e, n_valid, tile_start, src, dst, h2, wg, wu, wd)


def _combine_kernel(x1_ref, y0_ref, y1_ref, route_ref, g2_ref, fg_ref, o_ref, *, final_norm):
    r = route_ref[...]
    rows = r.shape[0]
    f = r[:, 2:3] * _load_row_tiles(y0_ref, rows, lead=(0,)) + r[:, 3:4] * _load_row_tiles(y1_ref, rows, lead=(0,))
    x2 = x1_ref[0] + _gate(f, g2_ref[0])
    if final_norm:
        x2 = _rms(x2) * fg_ref[...]
    o_ref[0] = x2


def moe_combine(x1, y, route, g2, fg, tm, row0, final_norm):
    s, r, d = x1.shape
    rm = g2.shape[1]
    nt = r // tm
    blk0 = row0 // tm
    return pl.pallas_call(
        functools.partial(_combine_kernel, final_norm=final_norm),
        out_shape=jax.ShapeDtypeStruct((s, r, d), F32),
        grid=(s, nt),
        in_specs=[pl.BlockSpec((1, tm, d), lambda b_, i: (b_, i, 0)),
                  pl.BlockSpec((1, tm * ROW_TILE, LANE), lambda b_, i: (0, blk0 + b_ * nt + i, 0)),
                  pl.BlockSpec((1, tm * ROW_TILE, LANE), lambda b_, i: (1, blk0 + b_ * nt + i, 0)),
                  pl.BlockSpec((tm, LANE), lambda b_, i: (blk0 + b_ * nt + i, 0)),
                  pl.BlockSpec((1, rm, d), lambda b_, i: (b_, 0, 0)),
                  pl.BlockSpec((1, d), lambda b_, i: (0, 0))],
        out_specs=pl.BlockSpec((1, tm, d), lambda b_, i: (b_, i, 0)),
        compiler_params=_cparams(("arbitrary", "arbitrary")),
        name="moe_combine",
    )(x1, y, y, route, g2, fg)


def _reorder_w_in(w_in, b_in):
    cut = PW_S5 + 128 + 128 + 256 + 256 + GLA_GATE_RANK
    pad = LANE - GLA_GATE_RANK
    w = jnp.concatenate([w_in[:, :cut], jnp.zeros((w_in.shape[0], pad), w_in.dtype), w_in[:, cut:]], axis=1)
    b = jnp.concatenate([b_in[:cut], jnp.zeros((pad,), b_in.dtype), b_in[cut:]])
    return w.astype(BF16), b.reshape(1, PW_TOTAL)


def _row(a):
    return a.reshape(1, -1)


class _Branch:
    def __init__(self, nseq, nb, t_len, seq_form, tm_pre, tm_post, tc, tt_seq, row0):
        self.nseq, self.nb, self.t_len, self.seq_form = nseq, nb, t_len, seq_form
        self.tm_pre, self.tm_post, self.tc, self.tt_seq, self.row0 = tm_pre, tm_post, tc, tt_seq, row0


def _layer_params(W, i):
    row = _row
    p = {}
    p['w_in'], p['b_in'] = _reorder_w_in(W['w_in'][i], W['b_in'][i])
    p['norm_g'] = row(W['norm_mix_g'][i])
    mg = W['merge_g'][i]
    p['mg'] = [row(mg[k * 256:(k + 1) * 256]) for k in range(4)]
    ab_re, ab_im, bb_re, bb_im = s5_discretise(W['s5_a_re'][i], W['s5_a_im'][i], W['s5_log_dt'][i],
                                               W['s5_b_re'][i], W['s5_b_im'][i])
    p['s5'] = (jnp.concatenate([_block_diag_in(bb_re), _block_diag_in(bb_im)], axis=1).astype(BF16),
               _block_diag_out(W['s5_c_re'][i]).astype(BF16), _block_diag_out(W['s5_c_im'][i]).astype(BF16),
               row(ab_re), row(ab_im), row(W['s5_d'][i]), W['s5_w_glu'][i].astype(BF16), row(W['s5_b_glu'][i]))
    wg2 = jnp.zeros((LANE, GLA_KEY_WIDTH), F32).at[:GLA_GATE_RANK].set(W['gla_w_gate2'][i]).astype(BF16)
    p['gla'] = (wg2, row(W['gla_b_gate2'][i]), row(W['gla_onorm_g'][i]))
    p['conv'] = (W['conv_w_dw'][i], row(W['conv_b_dw'][i]), row(W['conv_ln_g'][i]), row(W['conv_ln_b'][i]),
                 W['conv_w_pw'][i].astype(BF16), row(W['conv_b_pw'][i]))
    p['gmlp_ln'] = (row(W['gmlp_ln_g'][i]), row(W['gmlp_ln_b'][i]))
    p['gmlp_ws'], p['gmlp_bs'] = W['gmlp_w_s'][i], W['gmlp_b_s'][i]
    return p


def _mixers(x, mods, states, p, i, br, out):
    nseq, nb, t_len, seq_form = br.nseq, br.nb, br.t_len, br.seq_form
    s5_re0, s5_im0, gla0, conv0 = states
    new_re, new_im, new_gla, new_conv, new_v = out
    mg = p['mg']
    p_s5, p_gla, p_conv, p_mlp = pre_mixer(x, mods[0], mods[1], p['norm_g'], p['w_in'], p['b_in'], br.tm_pre)

    h0 = jnp.concatenate([s5_re0[i].reshape(nb, S5_LANES), s5_im0[i].reshape(nb, S5_LANES)], axis=1)
    o_s5, h_t = s5_mixer(p_s5, h0, *p['s5'], mg[0], nb, br.tc, nseq)
    new_re.append(h_t[:, :S5_LANES].reshape(nb, S5_GROUPS, S5_STATE))
    new_im.append(h_t[:, S5_LANES:].reshape(nb, S5_GROUPS, S5_STATE))

    if seq_form:
        eye = jnp.eye(GLA_HEADS, dtype=F32)
        s0 = jnp.einsum('bhkv,hg->bhvgk', gla0[i], eye).reshape(nseq, GLA_WIDTH, GLA_KEY_WIDTH)
        o_gla, s_t = gla_seq(p_gla, nseq, s0, *p['gla'], mg[1], br.tt_seq)
        s5d = s_t.reshape(nseq, GLA_HEADS, GLA_DV, GLA_HEADS, GLA_DK)
        new_gla.append(jnp.stack([jnp.swapaxes(s5d[:, h, :, h, :], 1, 2) for h in range(GLA_HEADS)], axis=1))
    else:
        ek, ev = _gla_expanders()
        o_gla, s_t = gla_recurrent(p_gla, gla0[i].reshape(nb, GLA_STATE_LANES), ek, ev, *p['gla'], mg[1], nb, t_len)
        new_gla.append(s_t.reshape(nb, GLA_HEADS, GLA_DK, GLA_DV))

    c0 = jnp.transpose(conv0[i], (1, 0, 2)).reshape(CONV_HIST * nb, CONV_DIM)
    o_conv, buf = conv_mixer(p_conv, c0, *p['conv'], mg[2], nb, br.tc, nseq)
    new_conv.append(jnp.transpose(buf.reshape(CONV_HIST, nb, CONV_DIM), (1, 0, 2)))

    ws, bs = p['gmlp_ws'], p['gmlp_bs']
    if seq_form:
        wcat = jnp.transpose(ws, (1, 0, 2)).reshape(GMLP_CHUNK, GMLP_HEADS * GMLP_CHUNK)
        bias = jnp.repeat(bs.T, GMLP_HEAD_DIM, axis=1)
        o_mlp = gmlp_seq(p_mlp, nseq, *p['gmlp_ln'], wcat, bias, mg[3], min(4 * br.tt_seq, t_len))
        new_v.append(None)
    else:
        tri = jnp.tril(jnp.ones((t_len, t_len), F32))
        wrow = jnp.repeat(jnp.transpose(ws[:, :t_len, :t_len] * tri[None], (1, 2, 0)).reshape(t_len * t_len, GMLP_HEADS),
                          GMLP_HEAD_DIM, axis=1)
        brow = jnp.repeat(bs[:, :t_len].T, GMLP_HEAD_DIM, axis=1)
        o_mlp, vn = gmlp_short(p_mlp, *p['gmlp_ln'], wrow, brow, mg[3], nb, t_len)
        new_v.append(vn)

    return [o_s5, o_gla, o_conv, o_mlp]


FF_TILE = 1408
EXPERT_FF_TILE = 256
EXPERT_ROWS = 512


def _channel_mixer(xs, mixes, mods, W, i, branches, last):
    ng, wout, fg = _row(W['norm_ffn_g'][i]), W['w_out'][i].astype(BF16), _row(W['final_norm_g'])
    j = i // 2
    if i % 2 == 0:
        wg, wu, wd = (W['ffn_w_gate'][j].astype(BF16), W['ffn_w_up'][j].astype(BF16),
                      W['ffn_w_down'][j].astype(BF16))
        return [post_dense(x, mx, m[2], m[3], m[4], m[5], ng, wout, wg, wu, wd, fg, br.tm_post, FF_TILE, last)
                for x, mx, m, br in zip(xs, mixes, mods, branches)]
    n_total = sum(x.shape[0] * x.shape[1] for x in xs)
    tg = EXPERT_ROWS
    n_tiles = 2 * n_total // tg + N_EXPERTS
    router = jnp.zeros((D_MODEL, LANE), F32).at[:, :N_EXPERTS].set(W['moe_router'][j])
    h2, route = jnp.zeros((n_total * ROW_TILE, LANE), F32), jnp.zeros((n_total, LANE), F32)
    x1s = []
    for x, mx, m, br in zip(xs, mixes, mods, branches):
        x1, h2, route = moe_route(x, mx, m[2], m[3], m[4], ng, wout, router, min(2 * br.tm_post, x.shape[1]),
                                  br.row0, (h2, route))
        x1s.append(x1)
    tables = _route_tables(route, tg, n_tiles)
    y = moe_experts(h2, tables, W['moe_w_gate'][j].astype(BF16), W['moe_w_up'][j].astype(BF16),
                    W['moe_w_down'][j].astype(BF16), tg, n_tiles, EXPERT_FF_TILE)
    y = y.reshape(2, n_total * ROW_TILE, LANE)
    return [moe_combine(x1, y, route, m[5], fg, min(2 * br.tm_post, x1.shape[1]), br.row0, last)
            for x1, m, br in zip(x1s, mods, branches)]


def kernel(x_prompt, x_sample, c_prompt, c_sample, state_s5_re, state_s5_im, state_gla, cache_conv, ada_w, ada_b, norm_mix_g, norm_ffn_g, w_in, b_in, s5_a_re, s5_a_im, s5_log_dt, s5_b_re, s5_b_im, s5_c_re, s5_c_im, s5_d, s5_w_glu, s5_b_glu, gla_w_gate2, gla_b_gate2, gla_onorm_g, conv_w_dw, conv_b_dw, conv_ln_g, conv_ln_b, conv_w_pw, conv_b_pw, gmlp_ln_g, gmlp_ln_b, gmlp_w_s, gmlp_b_s, merge_g, w_out, ffn_w_gate, ffn_w_up, ffn_w_down, moe_router, moe_w_gate, moe_w_up, moe_w_down, final_norm_g):
    W = dict(norm_mix_g=norm_mix_g, norm_ffn_g=norm_ffn_g, w_in=w_in, b_in=b_in, s5_a_re=s5_a_re, s5_a_im=s5_a_im,
             s5_log_dt=s5_log_dt, s5_b_re=s5_b_re, s5_b_im=s5_b_im, s5_c_re=s5_c_re, s5_c_im=s5_c_im, s5_d=s5_d,
             s5_w_glu=s5_w_glu, s5_b_glu=s5_b_glu, gla_w_gate2=gla_w_gate2, gla_b_gate2=gla_b_gate2,
             gla_onorm_g=gla_onorm_g, conv_w_dw=conv_w_dw, conv_b_dw=conv_b_dw, conv_ln_g=conv_ln_g,
             conv_ln_b=conv_ln_b, conv_w_pw=conv_w_pw, conv_b_pw=conv_b_pw, gmlp_ln_g=gmlp_ln_g,
             gmlp_ln_b=gmlp_ln_b, gmlp_w_s=gmlp_w_s, gmlp_b_s=gmlp_b_s, merge_g=merge_g, w_out=w_out,
             ffn_w_gate=ffn_w_gate, ffn_w_up=ffn_w_up, ffn_w_down=ffn_w_down, moe_router=moe_router,
             moe_w_gate=moe_w_gate, moe_w_up=moe_w_up, moe_w_down=moe_w_down, final_norm_g=final_norm_g)
    depth = w_in.shape[0]
    bp, tp, d = x_prompt.shape
    bs, ts, _ = x_sample.shape

    m = ada_modulation(jnp.concatenate([c_prompt, c_sample], axis=0), ada_w, ada_b)
    mods_p = [[m[i, :bp, k * d:(k + 1) * d].reshape(bp, 1, d) for k in range(6)] for i in range(depth)]
    mods_s = [[m[i, bp:, k * d:(k + 1) * d].reshape(1, bs, d) for k in range(6)] for i in range(depth)]

    z_re = jnp.zeros((depth, bp, S5_GROUPS, S5_STATE), F32)
    z_gla = jnp.zeros((depth, bp, GLA_HEADS, GLA_DK, GLA_DV), F32)
    z_conv = jnp.zeros((depth, bp, CONV_HIST, CONV_DIM), x_prompt.dtype)
    states = [(z_re, z_re, z_gla, z_conv), (state_s5_re, state_s5_im, state_gla, cache_conv)]
    branches = [_Branch(nseq=bp, nb=bp, t_len=tp, seq_form=True, tm_pre=min(512, tp), tm_post=min(512, tp),
                        tc=min(256, tp), tt_seq=min(512, tp), row0=0),
                _Branch(nseq=1, nb=bs, t_len=ts, seq_form=False, tm_pre=ts * bs, tm_post=min(512, ts * bs),
                        tc=ts, tt_seq=None, row0=bp * tp)]
    xs = [x_prompt, jnp.transpose(x_sample, (1, 0, 2)).reshape(1, ts * bs, d)]
    outs = [([], [], [], [], []), ([], [], [], [], [])]
    for i in range(depth):
        mods = [mods_p[i], mods_s[i]]
        params = _layer_params(W, i)
        mixes = [_mixers(x, m, st_, params, i, br, o)
                 for x, m, st_, br, o in zip(xs, mods, states, branches, outs)]
        xs = _channel_mixer(xs, mixes, mods, W, i, branches, i == depth - 1)

    y_p = xs[0]
    y_s = jnp.transpose(xs[1].reshape(ts, bs, d), (1, 0, 2))
    p_re, p_im, p_gla, p_conv, _ = outs[0]
    s_re, s_im, s_gla, s_conv, s_v = outs[1]
    s_v = [jnp.transpose(v.reshape(ts, bs, GMLP_WIDTH), (1, 0, 2)) for v in s_v]
    st = jnp.stack
    return (y_p, y_s, st(p_re), st(p_im), st(p_gla), st(p_conv),
            st(s_re), st(s_im), st(s_gla), st(s_conv), st(s_v))
```
